```python
import math
import jax, jax.numpy as jnp
from jax import lax
import numpy as np

D_MODEL = 2048
BATCH = 8
SEQ = 4096
DEPTH = 4

HEAD_DIM = 64
N_HEADS_A = D_MODEL // HEAD_DIM
N_KV_A = N_HEADS_A // 8
GROUP_A = N_HEADS_A // N_KV_A
N_HEADS_B = D_MODEL // HEAD_DIM
WINDOW = 128
BLOCK = 128
D_FF = 4 * D_MODEL
N_MIXERS = 2
N_A_LAYERS = (DEPTH + 1) // 2
N_B_LAYERS = DEPTH // 2
RMS_EPS = 1e-5

kernel_name = "hybrid_swa_sink_alibi_stickbreaking_sqrelu"


def rmsnorm(x, gain):
    xf = x.astype(jnp.float32)
    y = xf * lax.rsqrt(jnp.mean(xf * xf, axis=-1, keepdims=True) + RMS_EPS)
    return (y * gain.astype(jnp.float32)).astype(x.dtype)


def alibi_slopes(n_heads):
    return jnp.power(2.0, -8.0 * (jnp.arange(n_heads, dtype=jnp.float32) + 1.0) / n_heads)


def sliding_window_sink_attention(xn, w_qkv, w_o, sinks):
    b, s, _ = xn.shape
    nb = s // BLOCK
    qkv = xn @ w_qkv
    q_w, k_w = N_HEADS_A * HEAD_DIM, N_KV_A * HEAD_DIM
    q = qkv[..., :q_w].reshape(b, nb, BLOCK, N_KV_A, GROUP_A, HEAD_DIM)
    k = qkv[..., q_w:q_w + k_w].reshape(b, nb, BLOCK, N_KV_A, HEAD_DIM)
    v = qkv[..., q_w + k_w:].reshape(b, nb, BLOCK, N_KV_A, HEAD_DIM)
    pad = ((0, 0), (1, 0), (0, 0), (0, 0), (0, 0))
    k_win = jnp.concatenate([jnp.pad(k, pad)[:, :-1], k], axis=2)
    v_win = jnp.concatenate([jnp.pad(v, pad)[:, :-1], v], axis=2)
    scale = 1.0 / math.sqrt(HEAD_DIM)
    scores = jnp.einsum('bnqhgd,bnkhd->bnhgqk', q, k_win).astype(jnp.float32) * scale
    dist = (jnp.arange(BLOCK)[:, None] + BLOCK) - jnp.arange(2 * BLOCK)[None, :]
    key_abs = (jnp.arange(nb)[:, None] - 1) * BLOCK + jnp.arange(2 * BLOCK)[None, :]
    valid = ((dist >= 0) & (dist < WINDOW))[None, :, :] & (key_abs >= 0)[:, None, :]
    slopes = alibi_slopes(N_HEADS_A).reshape(N_KV_A, GROUP_A)
    scores = scores - slopes[None, None, :, :, None, None] * dist.astype(jnp.float32)
    scores = jnp.where(valid[None, :, None, None], scores, -jnp.inf)
    sink = sinks.astype(jnp.float32).reshape(N_KV_A, GROUP_A)[None, None, :, :, None, None]
    m = jnp.maximum(jnp.max(scores, axis=-1, keepdims=True), sink)
    p = jnp.exp(scores - m)
    p = p / (jnp.sum(p, axis=-1, keepdims=True) + jnp.exp(sink - m))
    out = jnp.einsum('bnhgqk,bnkhd->bnqhgd', p.astype(v.dtype), v_win)
    return out.reshape(b, s, N_HEADS_A * HEAD_DIM) @ w_o


def stick_breaking_attention(xn, w_qkv, w_o):
    b, s, _ = xn.shape
    nb = s // BLOCK
    qkv = xn @ w_qkv
    w = N_HEADS_B * HEAD_DIM
    q = qkv[..., :w].reshape(b, nb, BLOCK, N_HEADS_B, HEAD_DIM)
    k = qkv[..., w:2 * w].reshape(b, s, N_HEADS_B, HEAD_DIM)
    v = qkv[..., 2 * w:].reshape(b, s, N_HEADS_B, HEAD_DIM)
    q_blocks = jnp.moveaxis(q, 1, 0)
    starts = jnp.arange(nb, dtype=jnp.int32) * BLOCK
    scale = 1.0 / math.sqrt(HEAD_DIM)
    key_pos = jnp.arange(s, dtype=jnp.int32)

    def one_block(args):
        qb, start = args
        z = jnp.einsum('bqhd,bkhd->bhqk', qb, k).astype(jnp.float32) * scale
        t = start + jnp.arange(BLOCK, dtype=jnp.int32)
        before = (key_pos[None, :] < t[:, None])[None, None]
        log_beta = jax.nn.log_sigmoid(z)
        log_1m_beta = jnp.where(before, jax.nn.log_sigmoid(-z), 0.0)
        suffix = lax.cumsum(log_1m_beta, axis=3, reverse=True) - log_1m_beta
        a = jnp.where(before, jnp.exp(log_beta + suffix), 0.0)
        return jnp.einsum('bhqk,bkhd->bqhd', a.astype(v.dtype), v)

    out = lax.map(one_block, (q_blocks, starts))
    out = jnp.moveaxis(out, 0, 1).reshape(b, s, N_HEADS_B * HEAD_DIM)
    return out @ w_o


def squared_relu_mlp(xn, w_in, w_out):
    h = jax.nn.relu(xn @ w_in)
    return (h * h) @ w_out


def _fwd_setup_inputs(seed: int = 0) -> dict:
    key = jax.random.key(seed)
    ks = jax.random.split(key, 12)
    d = D_MODEL
    qkv_a = (N_HEADS_A + 2 * N_KV_A) * HEAD_DIM
    qkv_b = 3 * N_HEADS_B * HEAD_DIM
    x = jax.random.normal(ks[0], (BATCH, SEQ, d), jnp.float32)
    a_w_qkv = jax.random.normal(ks[1], (N_A_LAYERS, d, qkv_a), jnp.float32) * d ** -0.5
    a_w_o = jax.random.normal(ks[2], (N_A_LAYERS, N_HEADS_A * HEAD_DIM, d), jnp.float32) * (N_HEADS_A * HEAD_DIM) ** -0.5
    a_sinks = jax.random.normal(ks[3], (N_A_LAYERS, N_HEADS_A), jnp.float32) * 0.5
    b_w_qkv = jax.random.normal(ks[4], (N_B_LAYERS, d, qkv_b), jnp.float32) * d ** -0.5
    b_w_o = jax.random.normal(ks[5], (N_B_LAYERS, N_HEADS_B * HEAD_DIM, d), jnp.float32) * (N_HEADS_B * HEAD_DIM) ** -0.5
    norm_mix = 1.0 + 0.02 * jax.random.normal(ks[6], (DEPTH, d), jnp.float32)
    norm_mlp = 1.0 + 0.02 * jax.random.normal(ks[7], (DEPTH, d), jnp.float32)
    mlp_w_in = jax.random.normal(ks[8], (DEPTH, d, D_FF), jnp.float32) * d ** -0.5
    mlp_w_out = jax.random.normal(ks[9], (DEPTH, D_FF, d), jnp.float32) * D_FF ** -0.5
    final_norm = 1.0 + 0.02 * jax.random.normal(ks[10], (d,), jnp.float32)
    return {"x": x, "a_w_qkv": a_w_qkv, "a_w_o": a_w_o, "a_sinks": a_sinks,
            "b_w_qkv": b_w_qkv, "b_w_o": b_w_o, "norm_mix": norm_mix,
            "norm_mlp": norm_mlp, "mlp_w_in": mlp_w_in, "mlp_w_out": mlp_w_out,
            "final_norm": final_norm}


def _fwd_reference(x, a_w_qkv, a_w_o, a_sinks, b_w_qkv, b_w_o, norm_mix, norm_mlp,
              mlp_w_in, mlp_w_out, final_norm):
    for i in range(DEPTH):
        h = rmsnorm(x, norm_mix[i])
        j = i // N_MIXERS
        if i % N_MIXERS == 0:
            x = x + sliding_window_sink_attention(h, a_w_qkv[j], a_w_o[j], a_sinks[j])
        else:
            x = x + stick_breaking_attention(h, b_w_qkv[j], b_w_o[j])
        h = rmsnorm(x, norm_mlp[i])
        x = x + squared_relu_mlp(h, mlp_w_in[i], mlp_w_out[i])
    return rmsnorm(x, final_norm)


import jax as _jax
import jax.numpy as _jnp

TWIN_FORMAT = 'train_step'
FWD_PARAMS = ['x', 'a_w_qkv', 'a_w_o', 'a_sinks', 'b_w_qkv', 'b_w_o', 'norm_mix', 'norm_mlp', 'mlp_w_in', 'mlp_w_out', 'final_norm']
TWIN_WEIGHTS = ['a_w_qkv', 'a_w_o', 'a_sinks', 'b_w_qkv', 'b_w_o', 'norm_mix', 'norm_mlp', 'mlp_w_in', 'mlp_w_out', 'final_norm']
TWIN_DIFF_INPUT = 'x'
TWIN_INPUTS = ['x', 'a_w_qkv', 'a_w_o', 'a_sinks', 'b_w_qkv', 'b_w_o', 'norm_mix', 'norm_mlp', 'mlp_w_in', 'mlp_w_out', 'final_norm', 'loss_target', 'm_a_w_qkv', 'm_a_w_o', 'm_a_sinks', 'm_b_w_qkv', 'm_b_w_o', 'm_norm_mix', 'm_norm_mlp', 'm_mlp_w_in', 'm_mlp_w_out', 'm_final_norm', 'v_a_w_qkv', 'v_a_w_o', 'v_a_sinks', 'v_b_w_qkv', 'v_b_w_o', 'v_norm_mix', 'v_norm_mlp', 'v_mlp_w_in', 'v_mlp_w_out', 'v_final_norm']
TWIN_OUTPUTS = ['loss', 'grad_x', 'grad_a_w_qkv', 'grad_a_w_o', 'grad_a_sinks', 'grad_b_w_qkv', 'grad_b_w_o', 'grad_norm_mix', 'grad_norm_mlp', 'grad_mlp_w_in', 'grad_mlp_w_out', 'grad_final_norm', 'delta_a_w_qkv', 'delta_a_w_o', 'delta_a_sinks', 'delta_b_w_qkv', 'delta_b_w_o', 'delta_norm_mix', 'delta_norm_mlp', 'delta_mlp_w_in', 'delta_mlp_w_out', 'delta_final_norm', 'new_m_a_w_qkv', 'new_m_a_w_o', 'new_m_a_sinks', 'new_m_b_w_qkv', 'new_m_b_w_o', 'new_m_norm_mix', 'new_m_norm_mlp', 'new_m_mlp_w_in', 'new_m_mlp_w_out', 'new_m_final_norm', 'new_v_a_w_qkv', 'new_v_a_w_o', 'new_v_a_sinks', 'new_v_b_w_qkv', 'new_v_b_w_o', 'new_v_norm_mix', 'new_v_norm_mlp', 'new_v_mlp_w_in', 'new_v_mlp_w_out', 'new_v_final_norm']
TWIN_LEAF_KINDS = {'loss': 'loss', 'grad_x': 'grad_x', 'grad_a_w_qkv': 'grad_w', 'grad_a_w_o': 'grad_w', 'grad_a_sinks': 'grad_w', 'grad_b_w_qkv': 'grad_w', 'grad_b_w_o': 'grad_w', 'grad_norm_mix': 'grad_w', 'grad_norm_mlp': 'grad_w', 'grad_mlp_w_in': 'grad_w', 'grad_mlp_w_out': 'grad_w', 'grad_final_norm': 'grad_w', 'delta_a_w_qkv': 'delta_w', 'delta_a_w_o': 'delta_w', 'delta_a_sinks': 'delta_w', 'delta_b_w_qkv': 'delta_w', 'delta_b_w_o': 'delta_w', 'delta_norm_mix': 'delta_w', 'delta_norm_mlp': 'delta_w', 'delta_mlp_w_in': 'delta_w', 'delta_mlp_w_out': 'delta_w', 'delta_final_norm': 'delta_w', 'new_m_a_w_qkv': 'new_m', 'new_m_a_w_o': 'new_m', 'new_m_a_sinks': 'new_m', 'new_m_b_w_qkv': 'new_m', 'new_m_b_w_o': 'new_m', 'new_m_norm_mix': 'new_m', 'new_m_norm_mlp': 'new_m', 'new_m_mlp_w_in': 'new_m', 'new_m_mlp_w_out': 'new_m', 'new_m_final_norm': 'new_m', 'new_v_a_w_qkv': 'new_v', 'new_v_a_w_o': 'new_v', 'new_v_a_sinks': 'new_v', 'new_v_b_w_qkv': 'new_v', 'new_v_b_w_o': 'new_v', 'new_v_norm_mix': 'new_v', 'new_v_norm_mlp': 'new_v', 'new_v_mlp_w_in': 'new_v', 'new_v_mlp_w_out': 'new_v', 'new_v_final_norm': 'new_v'}


def _forward(args):
    return _fwd_reference(*[args[k] for k in FWD_PARAMS])


def _output_shape():
    def fwd():
        inp = _fwd_setup_inputs(0)
        return _fwd_reference(*[inp[k] for k in FWD_PARAMS])
    out = _jax.eval_shape(fwd)
    return out.shape, out.dtype

N_MICROBATCH = 1
ADAM_LR = 0.001
ADAM_B1 = 0.9
ADAM_B2 = 0.999
ADAM_EPS = 1e-08
ADAM_WD = 0.01
ADAM_STEP = 10
PER_EXAMPLE_BATCH_AXIS = {'x': 0, 'loss_target': 0}
SHARED_INPUTS = []
_WEIGHT_DTYPES = {'a_w_qkv': _jnp.float32, 'a_w_o': _jnp.float32, 'a_sinks': _jnp.float32, 'b_w_qkv': _jnp.float32, 'b_w_o': _jnp.float32, 'norm_mix': _jnp.float32, 'norm_mlp': _jnp.float32, 'mlp_w_in': _jnp.float32, 'mlp_w_out': _jnp.float32, 'final_norm': _jnp.float32}
MOMENT_SCALE = {'a_w_qkv': 3.783831e-02, 'a_w_o': 3.377494e-02, 'a_sinks': 4.971787e-02, 'b_w_qkv': 2.977175e-02, 'b_w_o': 4.753587e-02, 'norm_mix': 4.589741e-02, 'norm_mlp': 6.912142e-02, 'mlp_w_in': 3.420001e-02, 'mlp_w_out': 7.969328e-02, 'final_norm': 1.664164e+01}


def _to_microbatches(a, axis):
    t = _jnp.moveaxis(a, axis, 0)
    t = t.reshape((N_MICROBATCH, t.shape[0] // N_MICROBATCH) + t.shape[1:])
    return _jnp.moveaxis(t, 1, axis + 1)


def setup_inputs(seed: int = 0) -> dict:
    inp = _fwd_setup_inputs(seed)
    key = _jax.random.fold_in(_jax.random.key(seed), 7919)
    shape, _ = _output_shape()
    out = dict(inp)
    out["loss_target"] = _jax.random.normal(_jax.random.fold_in(key, 0), shape, _jnp.float32)
    for i, name in enumerate(TWIN_WEIGHTS):
        w = inp[name].astype(_jnp.float32)
        if MOMENT_SCALE is None:
            s = _jnp.sqrt(_jnp.mean(_jnp.square(w)) + 1e-30)
        else:
            s = MOMENT_SCALE[name]
        km, kv = _jax.random.split(_jax.random.fold_in(key, i + 1))
        out[name] = w
        out["m_" + name] = s * _jax.random.normal(km, w.shape, _jnp.float32)
        out["v_" + name] = (s * s) * _jax.random.uniform(kv, w.shape, _jnp.float32, 0.5, 1.5)
    if N_MICROBATCH > 1:
        for name, axis in PER_EXAMPLE_BATCH_AXIS.items():
            out[name] = _to_microbatches(out[name], axis)
    return {'x': out['x'], 'a_w_qkv': out['a_w_qkv'], 'a_w_o': out['a_w_o'], 'a_sinks': out['a_sinks'], 'b_w_qkv': out['b_w_qkv'], 'b_w_o': out['b_w_o'], 'norm_mix': out['norm_mix'], 'norm_mlp': out['norm_mlp'], 'mlp_w_in': out['mlp_w_in'], 'mlp_w_out': out['mlp_w_out'], 'final_norm': out['final_norm'], 'loss_target': out['loss_target'], 'm_a_w_qkv': out['m_a_w_qkv'], 'm_a_w_o': out['m_a_w_o'], 'm_a_sinks': out['m_a_sinks'], 'm_b_w_qkv': out['m_b_w_qkv'], 'm_b_w_o': out['m_b_w_o'], 'm_norm_mix': out['m_norm_mix'], 'm_norm_mlp': out['m_norm_mlp'], 'm_mlp_w_in': out['m_mlp_w_in'], 'm_mlp_w_out': out['m_mlp_w_out'], 'm_final_norm': out['m_final_norm'], 'v_a_w_qkv': out['v_a_w_qkv'], 'v_a_w_o': out['v_a_w_o'], 'v_a_sinks': out['v_a_sinks'], 'v_b_w_qkv': out['v_b_w_qkv'], 'v_b_w_o': out['v_b_w_o'], 'v_norm_mix': out['v_norm_mix'], 'v_norm_mlp': out['v_norm_mlp'], 'v_mlp_w_in': out['v_mlp_w_in'], 'v_mlp_w_out': out['v_mlp_w_out'], 'v_final_norm': out['v_final_norm']}


def _loss(weights, diff, rest, loss_target):
    with _jax.named_scope("forward"):
        args = {**rest, TWIN_DIFF_INPUT: diff, **{k: w.astype(_WEIGHT_DTYPES[k]) for k, w in weights.items()}}
        y = _forward(args)
    with _jax.named_scope("loss_head"):
        err = _jnp.square(y.astype(_jnp.float32) - loss_target)
        return 0.5 * _jnp.sum(_jnp.mean(err, axis=-1)) if err.ndim else 0.5 * err


def _adamw(w, g, m, v):
    m = ADAM_B1 * m + (1.0 - ADAM_B1) * g
    v = ADAM_B2 * v + (1.0 - ADAM_B2) * _jnp.square(g)
    m_hat = m / (1.0 - ADAM_B1 ** ADAM_STEP)
    v_hat = v / (1.0 - ADAM_B2 ** ADAM_STEP)
    delta = -ADAM_LR * (m_hat / (_jnp.sqrt(v_hat) + ADAM_EPS) + ADAM_WD * w)
    return delta, m, v


def reference(x, a_w_qkv, a_w_o, a_sinks, b_w_qkv, b_w_o, norm_mix, norm_mlp, mlp_w_in, mlp_w_out, final_norm, loss_target, m_a_w_qkv, m_a_w_o, m_a_sinks, m_b_w_qkv, m_b_w_o, m_norm_mix, m_norm_mlp, m_mlp_w_in, m_mlp_w_out, m_final_norm, v_a_w_qkv, v_a_w_o, v_a_sinks, v_b_w_qkv, v_b_w_o, v_norm_mix, v_norm_mlp, v_mlp_w_in, v_mlp_w_out, v_final_norm):
    given = dict(x=x, a_w_qkv=a_w_qkv, a_w_o=a_w_o, a_sinks=a_sinks, b_w_qkv=b_w_qkv, b_w_o=b_w_o, norm_mix=norm_mix, norm_mlp=norm_mlp, mlp_w_in=mlp_w_in, mlp_w_out=mlp_w_out, final_norm=final_norm, loss_target=loss_target, m_a_w_qkv=m_a_w_qkv, m_a_w_o=m_a_w_o, m_a_sinks=m_a_sinks, m_b_w_qkv=m_b_w_qkv, m_b_w_o=m_b_w_o, m_norm_mix=m_norm_mix, m_norm_mlp=m_norm_mlp, m_mlp_w_in=m_mlp_w_in, m_mlp_w_out=m_mlp_w_out, m_final_norm=m_final_norm, v_a_w_qkv=v_a_w_qkv, v_a_w_o=v_a_w_o, v_a_sinks=v_a_sinks, v_b_w_qkv=v_b_w_qkv, v_b_w_o=v_b_w_o, v_norm_mix=v_norm_mix, v_norm_mlp=v_norm_mlp, v_mlp_w_in=v_mlp_w_in, v_mlp_w_out=v_mlp_w_out, v_final_norm=v_final_norm)
    weights = {n: given[n] for n in TWIN_WEIGHTS}
    shared = {n: given[n] for n in SHARED_INPUTS}
    per_example = {n: given[n] for n in ['x']}
    grad_fn = _jax.value_and_grad(_loss, argnums=(0, 1))

    def one_microbatch(ex, loss_target):
        ex = dict(ex)
        diff = ex.pop(TWIN_DIFF_INPUT)
        return grad_fn(weights, diff, {**shared, **ex}, loss_target)

    if N_MICROBATCH == 1:
        loss, (grad_w, grad_x) = one_microbatch(per_example, given["loss_target"])
    else:
        def body(carry, xs):
            loss_sum, grad_sum = carry
            l_k, (gw_k, gx_k) = one_microbatch(xs[0], xs[1])
            with _jax.named_scope("update"):
                return (loss_sum + l_k, _jax.tree.map(_jnp.add, grad_sum, gw_k)), gx_k

        init = (_jnp.zeros((), _jnp.float32), _jax.tree.map(_jnp.zeros_like, weights))
        (loss, grad_w), grad_x = _jax.lax.scan(body, init, (per_example, given["loss_target"]))
    with _jax.named_scope("update"):
        delta_w, new_m, new_v = {}, {}, {}
        for n in TWIN_WEIGHTS:
            delta_w[n], new_m[n], new_v[n] = _adamw(weights[n], grad_w[n], given["m_" + n], given["v_" + n])
    return (loss, grad_x, *[grad_w[n] for n in TWIN_WEIGHTS], *[delta_w[n] for n in TWIN_WEIGHTS],
            *[new_m[n] for n in TWIN_WEIGHTS], *[new_v[n] for n in TWIN_WEIGHTS])
```

```python
import functools
import math

import jax
import jax.numpy as jnp
from jax import lax
from jax.experimental import pallas as pl
from jax.experimental.pallas import tpu as pltpu

F32 = jnp.float32
BF16 = jnp.bfloat16
MESH = pl.DeviceIdType.MESH

N_CHIPS = 4
HEAD_DIM = 64
LANES = 128
N_HEADS = 32
N_KV_A = 4
WINDOW = 128
RMS_EPS = 1e-5
ATTN_SCALE = 1.0 / math.sqrt(HEAD_DIM)
ADAM_LR, ADAM_B1, ADAM_B2, ADAM_EPS, ADAM_WD, ADAM_STEP = 0.001, 0.9, 0.999, 1e-08, 0.01, 10
NEG_BIG = -1e30
VMEM_LIMIT = 56 * 1024 * 1024

_DN = {"nn": (((1,), (0,)), ((), ())), "nt": (((1,), (1,)), ((), ())), "tn": (((0,), (0,)), ((), ()))}


def _dot(a, b, mode="nn"):
    return lax.dot_general(a, b, _DN[mode], preferred_element_type=F32)


def _params(*sem):
    return pltpu.CompilerParams(dimension_semantics=sem, vmem_limit_bytes=VMEM_LIMIT)


def _pick(n, prefs):
    for t in prefs:
        if n % t == 0:
            return t
    return n


def _mm(name, mode, a, b, *, grid, a_spec, b_spec, extras=(), extra_specs=(), out_shapes, out_specs, nk,
        acc_shape, epilogue):
    n_ex, n_out = len(extras), len(out_shapes)

    def body(*refs):
        a_ref, b_ref = refs[0], refs[1]
        ex = refs[2:2 + n_ex]
        outs = refs[2 + n_ex:2 + n_ex + n_out]
        part = _dot(a_ref[...], b_ref[...], mode)

        def finish(acc):
            res = epilogue(acc, *[e[...] for e in ex])
            for o, r in zip(outs, res):
                o[...] = r.astype(o.dtype)

        if nk == 1:
            finish(part)
        else:
            acc_ref = refs[-1]
            k = pl.program_id(len(grid) - 1)

            @pl.when(k == 0)
            def _():
                acc_ref[...] = part

            @pl.when(k > 0)
            def _():
                acc_ref[...] += part

            @pl.when(k == nk - 1)
            def _():
                finish(acc_ref[...])

    sem = ("parallel",) * (len(grid) - 1) + ("arbitrary" if nk > 1 else "parallel",)
    return pl.pallas_call(
        body, name=name, grid=grid,
        in_specs=[a_spec, b_spec, *extra_specs],
        out_specs=list(out_specs), out_shape=list(out_shapes),
        scratch_shapes=[] if nk == 1 else [pltpu.VMEM(acc_shape, F32)],
        compiler_params=_params(*sem),
    )(a, b, *extras)


def mm_cols(name, a, wg, epilogue, out_dtypes, tm=1024):
    m, k = a.shape
    c = wg.shape[2]
    tm = min(tm, m)
    tn = _pick(c, (512, 640, 256, 128))
    nj = c // tn
    o_spec = pl.BlockSpec((tm, tn), lambda i, j: (i, j))
    return _mm(name, "nn", a, wg, grid=(m // tm, N_CHIPS * nj),
               a_spec=pl.BlockSpec((tm, k), lambda i, j: (i, 0)),
               b_spec=pl.BlockSpec((None, k, tn), lambda i, j: (j // nj, 0, j % nj)),
               out_shapes=[jax.ShapeDtypeStruct((m, N_CHIPS * c), d) for d in out_dtypes],
               out_specs=[o_spec] * len(out_dtypes), nk=1, acc_shape=None, epilogue=epilogue)


def mm_res(name, a, w, res, tm=1024, tn=512, tk=2048):
    m, k = a.shape
    n = w.shape[1]
    tm, tn, tk = min(tm, m), min(tn, n), min(tk, k)
    nk = k // tk
    return _mm(name, "nn", a, w, grid=(m // tm, n // tn, nk),
               a_spec=pl.BlockSpec((tm, tk), lambda i, j, kk: (i, kk)),
               b_spec=pl.BlockSpec((tk, tn), lambda i, j, kk: (kk, j)),
               extras=(res,), extra_specs=(pl.BlockSpec((tm, tn), lambda i, j, kk: (i, j)),),
               out_shapes=[jax.ShapeDtypeStruct((m, n), F32)],
               out_specs=[pl.BlockSpec((tm, tn), lambda i, j, kk: (i, j))], nk=nk, acc_shape=(tm, tn),
               epilogue=lambda acc, r: (acc + r,))[0]


def mm_nt(name, a, w, epilogue, extras=(), tm=1024, tn=512):
    m, k = a.shape
    n = w.shape[0]
    tm, tn = min(tm, m), min(tn, n)
    o_spec = pl.BlockSpec((tm, tn), lambda i, j: (i, j))
    return _mm(name, "nt", a, w, grid=(m // tm, n // tn),
               a_spec=pl.BlockSpec((tm, k), lambda i, j: (i, 0)),
               b_spec=pl.BlockSpec((tn, k), lambda i, j: (j, 0)),
               extras=tuple(extras), extra_specs=(o_spec,) * len(extras),
               out_shapes=[jax.ShapeDtypeStruct((m, n), BF16)], out_specs=[o_spec], nk=1, acc_shape=None,
               epilogue=epilogue)[0]


def mm_nt_cols(name, dy, wg, tm=1024, tn=512):
    m = dy.shape[0]
    _, d, c = wg.shape
    tm, tn = min(tm, m), min(tn, d)
    tk = _pick(c, (2048, 1536, 640, 512, 128))
    nkk = c // tk
    nk = N_CHIPS * nkk
    return _mm(name, "nt", dy, wg, grid=(m // tm, d // tn, nk),
               a_spec=pl.BlockSpec((tm, tk), lambda i, j, kk: (i, kk)),
               b_spec=pl.BlockSpec((None, tn, tk), lambda i, j, kk: (kk // nkk, j, kk % nkk)),
               out_shapes=[jax.ShapeDtypeStruct((m, d), F32)],
               out_specs=[pl.BlockSpec((tm, tn), lambda i, j, kk: (i, j))], nk=nk, acc_shape=(tm, tn),
               epilogue=lambda acc: (acc,))[0]


def mm_tn(name, a, b, tm=1024, tn=1024, tk=1024):
    m, p = a.shape
    q = b.shape[1]
    tm, tn, tk = min(tm, p), min(tn, q), min(tk, m)
    nk = m // tk
    return _mm(name, "tn", a, b, grid=(p // tm, q // tn, nk),
               a_spec=pl.BlockSpec((tk, tm), lambda i, j, kk: (kk, i)),
               b_spec=pl.BlockSpec((tk, tn), lambda i, j, kk: (kk, j)),
               out_shapes=[jax.ShapeDtypeStruct((p, q), BF16)],
               out_specs=[pl.BlockSpec((tm, tn), lambda i, j, kk: (i, j))], nk=nk, acc_shape=(tm, tn),
               epilogue=lambda acc: (acc,))[0]


def mm_tn_cols(name, a, dy, tm=1024, tk=1024):
    m, d = a.shape
    c = dy.shape[1] // N_CHIPS
    tm, tk = min(tm, d), min(tk, m)
    tn = _pick(c, (1024, 768, 640, 512, 128))
    nj = c // tn
    nk = m // tk
    return _mm(name, "tn", a, dy, grid=(d // tm, N_CHIPS * nj, nk),
               a_spec=pl.BlockSpec((tk, tm), lambda i, j, kk: (kk, i)),
               b_spec=pl.BlockSpec((tk, tn), lambda i, j, kk: (kk, j)),
               out_shapes=[jax.ShapeDtypeStruct((N_CHIPS, d, c), BF16)],
               out_specs=[pl.BlockSpec((None, tm, tn), lambda i, j, kk: (j // nj, i, j % nj))], nk=nk,
               acc_shape=(tm, tn), epilogue=lambda acc: (acc,))[0]


def _rows_to_8(v):
    tm, d = v.shape
    return jnp.sum(v.reshape(tm // 8, 8, d), axis=0)


def rms_fwd(name, x, gain, tm=512):
    s, d = x.shape
    tm = min(tm, s)

    def body(x_ref, g_ref, h_ref):
        xv = x_ref[...]
        r = lax.rsqrt(jnp.mean(xv * xv, axis=-1, keepdims=True) + RMS_EPS)
        h_ref[...] = (xv * r * g_ref[...]).astype(BF16)

    row = pl.BlockSpec((tm, d), lambda i: (i, 0))
    return pl.pallas_call(
        body, name=name, grid=(s // tm,),
        in_specs=[row, pl.BlockSpec((1, d), lambda i: (0, 0))], out_specs=row,
        out_shape=jax.ShapeDtypeStruct((s, d), BF16), compiler_params=_params("parallel"),
    )(x, gain.reshape(1, d))


def rms_bwd(name, x, gain, dh, dres, tm=512):
    s, d = x.shape
    tm = min(tm, s)

    def body(x_ref, g_ref, dh_ref, dres_ref, dx_ref, dxb_ref, dg_ref):
        xv = x_ref[...]
        r = lax.rsqrt(jnp.mean(xv * xv, axis=-1, keepdims=True) + RMS_EPS)
        xhat = xv * r
        dhv = dh_ref[...]
        dxhat = dhv * g_ref[...]
        dx = dres_ref[...] + r * (dxhat - xhat * jnp.mean(dxhat * xhat, axis=-1, keepdims=True))
        dx_ref[...] = dx
        dxb_ref[...] = dx.astype(BF16)

        @pl.when(pl.program_id(0) == 0)
        def _():
            dg_ref[...] = jnp.zeros_like(dg_ref)

        dg_ref[...] += _rows_to_8(dhv * xhat)

    row = pl.BlockSpec((tm, d), lambda i: (i, 0))
    return pl.pallas_call(
        body, name=name, grid=(s // tm,),
        in_specs=[row, pl.BlockSpec((1, d), lambda i: (0, 0)), row, row],
        out_specs=[row, row, pl.BlockSpec((8, d), lambda i: (0, 0))],
        out_shape=[jax.ShapeDtypeStruct((s, d), F32), jax.ShapeDtypeStruct((s, d), BF16),
                   jax.ShapeDtypeStruct((8, d), F32)],
        compiler_params=_params("arbitrary"),
    )(x, gain.reshape(1, d), dh, dres)


def loss_head(name, x, gain, target, tm=512):
    s, d = x.shape
    tm = min(tm, s)

    def body(x_ref, g_ref, t_ref, loss_ref, dx_ref, dxb_ref, dg_ref):
        xv = x_ref[...]
        g = g_ref[...]
        r = lax.rsqrt(jnp.mean(xv * xv, axis=-1, keepdims=True) + RMS_EPS)
        xhat = xv * r
        err = xhat * g - t_ref[...]
        dy = err * (1.0 / d)
        dxhat = dy * g
        dx = r * (dxhat - xhat * jnp.mean(dxhat * xhat, axis=-1, keepdims=True))
        dx_ref[...] = dx
        dxb_ref[...] = dx.astype(BF16)

        @pl.when(pl.program_id(0) == 0)
        def _():
            dg_ref[...] = jnp.zeros_like(dg_ref)
            loss_ref[...] = jnp.zeros_like(loss_ref)

        dg_ref[...] += _rows_to_8(dy * xhat)
        loss_ref[...] += _rows_to_8(err * err) * (0.5 / d)

    row = pl.BlockSpec((tm, d), lambda i: (i, 0))
    vec = pl.BlockSpec((8, d), lambda i: (0, 0))
    return pl.pallas_call(
        body, name=name, grid=(s // tm,),
        in_specs=[row, pl.BlockSpec((1, d), lambda i: (0, 0)), row],
        out_specs=[vec, row, row, vec],
        out_shape=[jax.ShapeDtypeStruct((8, d), F32), jax.ShapeDtypeStruct((s, d), F32),
                   jax.ShapeDtypeStruct((s, d), BF16), jax.ShapeDtypeStruct((8, d), F32)],
        compiler_params=_params("arbitrary"),
    )(x, gain.reshape(1, d), target)


def _half_masks(dtype):
    lane = lax.broadcasted_iota(jnp.int32, (1, LANES), 1)
    lo = (lane < HEAD_DIM).astype(dtype)
    return lo, (1 - lo).astype(dtype)


def _swap_halves(v):
    return pltpu.roll(v, HEAD_DIM, axis=1)


def _swa_tile(q, k2, v2, sink, slope, n, hq_mask_b):
    t = q.shape[0]
    s = _dot(q * hq_mask_b, k2, "nt")
    row = lax.broadcasted_iota(jnp.int32, (t, 2 * WINDOW), 0)
    col = lax.broadcasted_iota(jnp.int32, (t, 2 * WINDOW), 1)
    dist = row + WINDOW - col
    valid = (dist >= 0) & (dist < WINDOW) & ((col >= WINDOW) | (n > 0))
    s = jnp.where(valid, s - slope * dist.astype(F32), NEG_BIG)
    m = jnp.maximum(jnp.max(s, axis=-1, keepdims=True), sink)
    e = jnp.exp(s - m)
    e_sink = jnp.exp(sink - m)
    inv = 1.0 / (jnp.sum(e, axis=-1, keepdims=True) + e_sink)
    return e * inv, e_sink * inv


def _swa_kv(kp_ref, kc_ref, vp_ref, vc_ref, kv_half):
    lo_b, hi_b = _half_masks(F32)
    sel = jnp.where(kv_half == 0, lo_b, hi_b)
    k2 = jnp.concatenate([kp_ref[...], kc_ref[...]], axis=0).astype(F32) * sel
    v2 = jnp.concatenate([vp_ref[...], vc_ref[...]], axis=0).astype(F32) * sel
    k2 = (k2 + _swap_halves(k2)).astype(BF16)
    v2 = (v2 + _swap_halves(v2)).astype(BF16)
    return k2, v2, sel


def _swa_specs(t):
    q_spec = pl.BlockSpec((t, LANES), lambda j, n, *_: (n, j))
    q_blocks = N_HEADS // 2

    def kv(off, prev):
        if prev:
            return pl.BlockSpec((t, LANES), lambda j, n, *_: (jnp.maximum(n - 1, 0), q_blocks + off + j // 8))
        return pl.BlockSpec((t, LANES), lambda j, n, *_: (n, q_blocks + off + j // 8))

    kv_blocks = N_KV_A // 2
    return q_spec, [kv(0, True), kv(0, False), kv(kv_blocks, True), kv(kv_blocks, False)]


def swa_fwd(name, qkv, sinks, slopes):
    s = qkv.shape[0]
    t = WINDOW

    def body(sink_ref, slope_ref, q_ref, kp_ref, kc_ref, vp_ref, vc_ref, o_ref):
        j, n = pl.program_id(0), pl.program_id(1)
        k2, v2, _ = _swa_kv(kp_ref, kc_ref, vp_ref, vc_ref, (j // 4) % 2)
        q = q_ref[...] * ATTN_SCALE
        masks = _half_masks(BF16)
        acc = jnp.zeros((t, LANES), F32)
        for hq in range(2):
            p, _ = _swa_tile(q, k2, v2, sink_ref[2 * j + hq], slope_ref[2 * j + hq], n, masks[hq])
            acc += _dot(p.astype(BF16), v2 * masks[hq])
        o_ref[...] = acc.astype(BF16)

    q_spec, kv_specs = _swa_specs(t)
    return pl.pallas_call(
        body, name=name,
        grid_spec=pltpu.PrefetchScalarGridSpec(
            num_scalar_prefetch=2, grid=(N_HEADS // 2, s // t),
            in_specs=[q_spec, *kv_specs], out_specs=q_spec),
        out_shape=jax.ShapeDtypeStruct((s, N_HEADS * HEAD_DIM), BF16),
        compiler_params=_params("parallel", "parallel"),
    )(sinks, slopes, qkv, qkv, qkv, qkv, qkv)


def swa_bwd(name, qkv, do, sinks, slopes):
    s = qkv.shape[0]
    t = WINDOW

    def body(sink_ref, slope_ref, q_ref, kp_ref, kc_ref, vp_ref, vc_ref, do_ref, dq_ref, dk_ref, dv_ref, ds_ref):
        j, n = pl.program_id(0), pl.program_id(1)
        k2, v2, sel = _swa_kv(kp_ref, kc_ref, vp_ref, vc_ref, (j // 4) % 2)
        q = q_ref[...] * ATTN_SCALE
        do_v = do_ref[...]
        masks = _half_masks(BF16)

        @pl.when((j % 8 == 0) & (n == 0))
        def _():
            dk_ref[...] = jnp.zeros_like(dk_ref)
            dv_ref[...] = jnp.zeros_like(dv_ref)

        @pl.when(n == 0)
        def _():
            ds_ref[...] = jnp.zeros_like(ds_ref)

        dq = jnp.zeros((t, LANES), F32)
        dk = jnp.zeros((2 * t, LANES), F32)
        dv = jnp.zeros((2 * t, LANES), F32)
        for hq in range(2):
            sink = sink_ref[2 * j + hq]
            p, p_sink = _swa_tile(q, k2, v2, sink, slope_ref[2 * j + hq], n, masks[hq])
            qm = q * masks[hq]
            dom = do_v * masks[hq]
            dp = _dot(dom, v2, "nt")
            delta = jnp.sum(p * dp, axis=-1, keepdims=True)
            dsc = (p * (dp - delta)).astype(BF16)
            dq += _dot(dsc, k2 * masks[hq])
            dk += _dot(dsc, qm, "tn")
            dv += _dot(p.astype(BF16), dom, "tn")
            ds_ref[hq:hq + 1, :] += jnp.zeros((1, LANES), F32) - jnp.sum(p_sink * delta)
        dq_ref[...] = (dq * ATTN_SCALE).astype(BF16)
        dk = (dk + _swap_halves(dk)) * sel
        dv = (dv + _swap_halves(dv)) * sel

        @pl.when(n == 0)
        def _():
            dk_ref[pl.ds(0, t), :] += dk[t:]
            dv_ref[pl.ds(0, t), :] += dv[t:]

        @pl.when(n > 0)
        def _():
            start = pl.multiple_of((n - 1) * t, t)
            dk_ref[pl.ds(start, 2 * t), :] += dk
            dv_ref[pl.ds(start, 2 * t), :] += dv

    q_spec, kv_specs = _swa_specs(t)
    kv_out = pl.BlockSpec((s, LANES), lambda j, n, *_: (0, j // 8))
    return pl.pallas_call(
        body, name=name,
        grid_spec=pltpu.PrefetchScalarGridSpec(
            num_scalar_prefetch=2, grid=(N_HEADS // 2, s // t),
            in_specs=[q_spec, *kv_specs, q_spec],
            out_specs=[q_spec, kv_out, kv_out, pl.BlockSpec((None, 8, LANES), lambda j, n, *_: (j, 0, 0))]),
        out_shape=[jax.ShapeDtypeStruct((s, N_HEADS * HEAD_DIM), BF16),
                   jax.ShapeDtypeStruct((s, N_KV_A * HEAD_DIM), F32),
                   jax.ShapeDtypeStruct((s, N_KV_A * HEAD_DIM), F32),
                   jax.ShapeDtypeStruct((N_HEADS // 2, 8, LANES), F32)],
        compiler_params=_params("arbitrary", "arbitrary"),
    )(sinks, slopes, qkv, qkv, qkv, qkv, qkv, do)


SB_TILE = 256


def _split2(v):
    hi = v.astype(BF16)
    lo = (v - hi.astype(F32)).astype(BF16)
    return jnp.concatenate([hi, lo], axis=0)


def _tri(t, inclusive):
    r = lax.broadcasted_iota(jnp.int32, (t, t), 0)
    c = lax.broadcasted_iota(jnp.int32, (t, t), 1)
    return ((r >= c) if inclusive else (r > c)).astype(BF16)


def _sb_probs(qm, kj, c_lm, tri_x, before):
    t = qm.shape[0]
    z = _dot(qm, kj, "nt")
    l = jnp.log(1.0 + jnp.exp(-jnp.abs(z)))
    lb = jnp.minimum(z, 0.0) - l
    lm = lb - z
    if before is not None:
        lm = jnp.where(before, lm, 0.0)
    sfx = _dot(_split2(lm), tri_x)
    e = lb + (sfx[:t] + sfx[t:]) + c_lm
    a = jnp.exp(e)
    if before is not None:
        a = jnp.where(before, a, 0.0)
    return a, lb, jnp.sum(lm, axis=-1, keepdims=True)


def _sb_specs(s, t):
    hp = N_HEADS // 2
    q_spec = pl.BlockSpec((t, LANES), lambda h, i: (i, h))
    k_spec = pl.BlockSpec((s, LANES), lambda h, i: (0, hp + h))
    v_spec = pl.BlockSpec((s, LANES), lambda h, i: (0, 2 * hp + h))
    return q_spec, k_spec, v_spec


def sb_fwd(name, qkv, t=SB_TILE):
    s = qkv.shape[0]
    t = min(t, s)

    def body(q_ref, k_ref, v_ref, ob_ref, of_ref):
        i = pl.program_id(1)
        masks = _half_masks(BF16)
        q = q_ref[...] * ATTN_SCALE
        qm = [q * masks[0], q * masks[1]]
        tri_x = _tri(t, False)
        r = lax.broadcasted_iota(jnp.int32, (t, t), 0)
        c = lax.broadcasted_iota(jnp.int32, (t, t), 1)
        diag = c < r

        def tile(j, carry, before):
            c0, c1, acc = carry
            start = pl.multiple_of(j * t, t)
            kj = k_ref[pl.ds(start, t), :]
            vj = v_ref[pl.ds(start, t), :]
            cs = [c0, c1]
            for h in range(2):
                a, _, lm_sum = _sb_probs(qm[h], kj, cs[h], tri_x, before)
                acc += _dot(a.astype(BF16), vj * masks[h])
                cs[h] = cs[h] + lm_sum
            return cs[0], cs[1], acc

        zero = jnp.zeros((t, 1), F32)
        carry = tile(i, (zero, zero, jnp.zeros((t, LANES), F32)), diag)
        carry = lax.fori_loop(0, i, lambda step, cr: tile(i - 1 - step, cr, None), carry)
        ob_ref[...] = carry[2].astype(BF16)
        of_ref[...] = carry[2]

    q_spec, k_spec, v_spec = _sb_specs(s, t)
    return pl.pallas_call(
        body, name=name, grid=(N_HEADS // 2, s // t),
        in_specs=[q_spec, k_spec, v_spec], out_specs=[q_spec, q_spec],
        out_shape=[jax.ShapeDtypeStruct((s, N_HEADS * HEAD_DIM), BF16),
                   jax.ShapeDtypeStruct((s, N_HEADS * HEAD_DIM), F32)],
        compiler_params=_params("parallel", "parallel"),
    )(qkv, qkv, qkv)


def sb_bwd(name, qkv, o_f32, do, t=SB_TILE):
    s = qkv.shape[0]
    t = min(t, s)

    def body(q_ref, k_ref, v_ref, o_ref, do_ref, dq_ref, dk_ref, dv_ref):
        i = pl.program_id(1)
        masks = _half_masks(BF16)
        fmasks = _half_masks(F32)
        q = q_ref[...] * ATTN_SCALE
        do_v = do_ref[...]
        qm = [q * masks[0], q * masks[1]]
        dom = [do_v * masks[0], do_v * masks[1]]
        prod = do_v.astype(F32) * o_ref[...]
        delta = [jnp.sum(prod * fmasks[h], axis=-1, keepdims=True) for h in range(2)]
        tri_x = _tri(t, False)
        tri_i = _tri(t, True)
        r = lax.broadcasted_iota(jnp.int32, (t, t), 0)
        c = lax.broadcasted_iota(jnp.int32, (t, t), 1)
        diag = c < r

        @pl.when(i == 0)
        def _():
            dk_ref[...] = jnp.zeros_like(dk_ref)
            dv_ref[...] = jnp.zeros_like(dv_ref)

        def tile(j, carry, before):
            cl0, cl1, cd0, cd1, dq = carry
            start = pl.multiple_of(j * t, t)
            kj = k_ref[pl.ds(start, t), :]
            vj = v_ref[pl.ds(start, t), :]
            cl, cd = [cl0, cl1], [cd0, cd1]
            dk = jnp.zeros((t, LANES), F32)
            dv = jnp.zeros((t, LANES), F32)
            for h in range(2):
                a, lb, lm_sum = _sb_probs(qm[h], kj, cl[h], tri_x, before)
                ab = a.astype(BF16)
                de = _dot(dom[h], vj, "nt") * ab.astype(F32)
                sfx = _dot(_split2(de), tri_i)
                farther = delta[h] - cd[h] - (sfx[:t] + sfx[t:])
                sig = jnp.exp(lb)
                dz = de * (1.0 - sig) - farther * sig
                if before is not None:
                    dz = jnp.where(before, dz, 0.0)
                dzb = dz.astype(BF16)
                dq += _dot(dzb, kj * masks[h])
                dk += _dot(dzb, qm[h], "tn")
                dv += _dot(ab, dom[h], "tn")
                cl[h] = cl[h] + lm_sum
                cd[h] = cd[h] + jnp.sum(de, axis=-1, keepdims=True)
            dk_ref[pl.ds(start, t), :] += dk
            dv_ref[pl.ds(start, t), :] += dv
            return cl[0], cl[1], cd[0], cd[1], dq

        zero = jnp.zeros((t, 1), F32)
        carry = tile(i, (zero, zero, zero, zero, jnp.zeros((t, LANES), F32)), diag)
        carry = lax.fori_loop(0, i, lambda step, cr: tile(i - 1 - step, cr, None), carry)
        dq_ref[...] = (carry[4] * ATTN_SCALE).astype(BF16)

    q_spec, k_spec, v_spec = _sb_specs(s, t)
    kv_out = pl.BlockSpec((s, LANES), lambda h, i: (0, h))
    width = N_HEADS * HEAD_DIM
    return pl.pallas_call(
        body, name=name, grid=(N_HEADS // 2, s // t),
        in_specs=[q_spec, k_spec, v_spec, q_spec, q_spec], out_specs=[q_spec, kv_out, kv_out],
        out_shape=[jax.ShapeDtypeStruct((s, width), BF16), jax.ShapeDtypeStruct((s, width), F32),
                   jax.ShapeDtypeStruct((s, width), F32)],
        compiler_params=_params("parallel", "arbitrary"),
    )(qkv, qkv, qkv, o_f32, do)


def adamw(name, w, g, m, v, tm=256):
    shape = w.shape
    c = shape[-1]
    rows = math.prod(shape[:-1])
    tm = min(tm, rows)

    def body(w_ref, g_ref, m_ref, v_ref, d_ref, mo_ref, vo_ref):
        gv = g_ref[...]
        m2 = ADAM_B1 * m_ref[...] + (1.0 - ADAM_B1) * gv
        v2 = ADAM_B2 * v_ref[...] + (1.0 - ADAM_B2) * (gv * gv)
        m_hat = m2 / (1.0 - ADAM_B1 ** ADAM_STEP)
        v_hat = v2 / (1.0 - ADAM_B2 ** ADAM_STEP)
        d_ref[...] = -ADAM_LR * (m_hat / (jnp.sqrt(v_hat) + ADAM_EPS) + ADAM_WD * w_ref[...])
        mo_ref[...] = m2
        vo_ref[...] = v2

    blk = pl.BlockSpec((tm, c), lambda i: (i, 0))
    outs = pl.pallas_call(
        body, name=name, grid=(rows // tm,), in_specs=[blk] * 4, out_specs=[blk] * 3,
        out_shape=[jax.ShapeDtypeStruct((rows, c), F32)] * 3, compiler_params=_params("parallel"),
    )(*[t.reshape(rows, c) for t in (w, g, m, v)])
    return [o.reshape(shape) for o in outs]


HBM = pl.BlockSpec(memory_space=pltpu.HBM)


def _place():
    x, y, c = lax.axis_index("x"), lax.axis_index("y"), lax.axis_index("c")
    return x, y, c, [(1 - x, y), (x, 1 - y), (1 - x, 1 - y)]


def _remote(src, dst, send, recv, dev):
    return pltpu.make_async_remote_copy(src_ref=src, dst_ref=dst, send_sem=send, recv_sem=recv, device_id=dev,
                                        device_id_type=MESH)


def gather_shards(name, ws):
    n = len(ws)

    def body(*refs):
        ins, outs = refs[:n], refs[n:2 * n]
        send, recv, loc = refs[2 * n:]
        x, y, c, chips = _place()
        me = 2 * x + y
        local = [pltpu.make_async_copy(ins[t], outs[t].at[me], loc.at[t]) for t in range(n)]
        sends = [_remote(ins[t], outs[t].at[me], send.at[t, k], recv.at[t, k], (px, py, c))
                 for t in range(n) for k, (px, py) in enumerate(chips)]
        for cp in local + sends:
            cp.start()
        for t in range(n):
            for k, (px, py) in enumerate(chips):
                _remote(ins[t], outs[t].at[2 * px + py], send.at[t, k], recv.at[t, k], (px, py, c)).wait_recv()
        for cp in sends:
            cp.wait_send()
        for cp in local:
            cp.wait()

    return pl.pallas_call(
        body, name=name, in_specs=[HBM] * n, out_specs=[HBM] * n,
        out_shape=[jax.ShapeDtypeStruct((N_CHIPS,) + w.shape, w.dtype) for w in ws],
        scratch_shapes=[pltpu.SemaphoreType.DMA((n, 3)), pltpu.SemaphoreType.DMA((n, 3)),
                        pltpu.SemaphoreType.DMA((n,))],
    )(*ws)


def swap_halves_with_sibling(name, gs):
    n = len(gs)

    def body(*refs):
        ins, outs = refs[:n], refs[n:2 * n]
        send, recv = refs[2 * n:]
        x, y, c, _ = _place()
        sib = (x, y, 1 - c)
        sends = [_remote(ins[t].at[s, 1 - c], outs[t].at[s], send.at[t, s], recv.at[t, s], sib)
                 for t in range(n) for s in range(N_CHIPS)]
        for cp in sends:
            cp.start()
        for cp in sends:
            cp.wait_recv()
        for cp in sends:
            cp.wait_send()

    return pl.pallas_call(
        body, name=name, in_specs=[HBM] * n, out_specs=[HBM] * n,
        out_shape=[jax.ShapeDtypeStruct((N_CHIPS,) + g.shape[2:], g.dtype) for g in gs],
        scratch_shapes=[pltpu.SemaphoreType.DMA((n, N_CHIPS)), pltpu.SemaphoreType.DMA((n, N_CHIPS))],
    )(*gs)


def scatter_to_chips(name, ps):
    n = len(ps)

    def body(*refs):
        ins, outs = refs[:n], refs[n:2 * n]
        send, recv, loc = refs[2 * n:]
        x, y, c, chips = _place()
        me = 2 * x + y
        local = [pltpu.make_async_copy(ins[t].at[me], outs[t].at[me], loc.at[t]) for t in range(n)]
        sends = [_remote(ins[t].at[2 * px + py], outs[t].at[me], send.at[t, k], recv.at[t, k], (px, py, c))
                 for t in range(n) for k, (px, py) in enumerate(chips)]
        for cp in local + sends:
            cp.start()
        for t in range(n):
            for k, (px, py) in enumerate(chips):
                _remote(ins[t].at[me], outs[t].at[2 * px + py], send.at[t, k], recv.at[t, k], (px, py, c)).wait_recv()
        for cp in sends:
            cp.wait_send()
        for cp in local:
            cp.wait()

    return pl.pallas_call(
        body, name=name, in_specs=[HBM] * n, out_specs=[HBM] * n,
        out_shape=[jax.ShapeDtypeStruct(p.shape, p.dtype) for p in ps],
        scratch_shapes=[pltpu.SemaphoreType.DMA((n, 3)), pltpu.SemaphoreType.DMA((n, 3)),
                        pltpu.SemaphoreType.DMA((n,))],
    )(*ps)


def join_halves_with_sibling(name, rs):
    n = len(rs)

    def body(*refs):
        ins, out = refs[:n], refs[n]
        send, recv, loc = refs[n + 1:]
        x, y, c, _ = _place()
        sib = (x, y, 1 - c)
        local = [pltpu.make_async_copy(ins[l], out.at[l, c], loc.at[l]) for l in range(n)]
        sends = [_remote(ins[l], out.at[l, c], send.at[l], recv.at[l], sib) for l in range(n)]
        for cp in local + sends:
            cp.start()
        for l in range(n):
            _remote(ins[l], out.at[l, 1 - c], send.at[l], recv.at[l], sib).wait_recv()
        for cp in sends:
            cp.wait_send()
        for cp in local:
            cp.wait()

    return pl.pallas_call(
        body, name=name, in_specs=[HBM] * n, out_specs=HBM,
        out_shape=jax.ShapeDtypeStruct((n, 2) + rs[0].shape, rs[0].dtype),
        scratch_shapes=[pltpu.SemaphoreType.DMA((n,)), pltpu.SemaphoreType.DMA((n,)), pltpu.SemaphoreType.DMA((n,))],
    )(*rs)


def add_sibling_half(name, g, a, core, tm=256):
    _, _, rh, c = g.shape
    tm = min(tm, rh)

    def body(core_ref, g_ref, a_ref, o_ref):
        o_ref[...] = (g_ref[...].astype(F32) + a_ref[...].astype(F32)).astype(BF16)

    return pl.pallas_call(
        body, name=name,
        grid_spec=pltpu.PrefetchScalarGridSpec(
            num_scalar_prefetch=1, grid=(N_CHIPS, rh // tm),
            in_specs=[pl.BlockSpec((None, None, tm, c), lambda s, i, core_ref: (s, core_ref[0], i, 0)),
                      pl.BlockSpec((None, tm, c), lambda s, i, core_ref: (s, i, 0))],
            out_specs=pl.BlockSpec((None, tm, c), lambda s, i, core_ref: (s, i, 0))),
        out_shape=jax.ShapeDtypeStruct(a.shape, BF16), compiler_params=_params("parallel", "parallel"),
    )(core, g, a)


def sum_chips(name, b, tm=256):
    _, rh, c = b.shape
    tm = min(tm, rh)

    def body(b_ref, o_ref):
        acc = b_ref[0].astype(F32)
        for s in range(1, N_CHIPS):
            acc = acc + b_ref[s].astype(F32)
        o_ref[...] = acc

    return pl.pallas_call(
        body, name=name, grid=(rh // tm,),
        in_specs=[pl.BlockSpec((N_CHIPS, tm, c), lambda i: (0, i, 0))],
        out_specs=pl.BlockSpec((tm, c), lambda i: (i, 0)),
        out_shape=jax.ShapeDtypeStruct((rh, c), F32), compiler_params=_params("parallel"),
    )(b)


N_DEV = 8


def allreduce_small(name, parts):
    p, _, d = parts.shape
    m_per = p * 8

    def body(x_ref, out_ref, all_ref, send_sems, recv_sems, local_sem):
        x, y, c, chips = _place()
        me, sibling = (x, y, c), (x, y, 1 - c)

        def rows(px, py, pc):
            return all_ref.at[pl.ds((4 * px + 2 * py + pc) * m_per, m_per), :]

        def copy(k, block, to, src=None):
            return _remote(rows(*block) if src is None else src, rows(*block), send_sems.at[k], recv_sems.at[k], to)

        mine = pltpu.make_async_copy(x_ref, rows(*me), local_sem)
        mine.start()
        first = [copy(0, me, sibling, src=x_ref)]
        first += [copy(1 + j, me, (*chip, c), src=x_ref) for j, chip in enumerate(chips)]
        for cp in first:
            cp.start()
        passed = [copy(4 + j, (*chip, c), sibling) for j, chip in enumerate(chips)]
        for j, chip in enumerate(chips):
            copy(1 + j, (*chip, c), me).wait_recv()
            passed[j].start()
        copy(0, sibling, me).wait_recv()
        for j, chip in enumerate(chips):
            copy(4 + j, (*chip, 1 - c), me).wait_recv()
        for cp in first + passed:
            cp.wait_send()
        mine.wait()
        acc = all_ref[pl.ds(0, m_per), :]
        for dev in range(1, N_DEV):
            acc = acc + all_ref[pl.ds(dev * m_per, m_per), :]
        out_ref[...] = jnp.sum(acc.reshape(p, 8, d), axis=1)

    vmem = pl.BlockSpec(memory_space=pltpu.VMEM)
    return pl.pallas_call(
        body, name=name, in_specs=[vmem], out_specs=vmem,
        out_shape=jax.ShapeDtypeStruct((p, d), F32),
        scratch_shapes=[pltpu.VMEM((N_DEV * m_per, d), F32), pltpu.SemaphoreType.DMA((7,)),
                        pltpu.SemaphoreType.DMA((7,)), pltpu.SemaphoreType.DMA],
        compiler_params=pltpu.CompilerParams(vmem_limit_bytes=VMEM_LIMIT),
    )(parts.reshape(m_per, d))


def _gather_layers(kind, w):
    return gather_shards(f"gather_{kind}", [w[l].astype(BF16) for l in range(w.shape[0])])


def _reduce_kind(kind, gs, core):
    n = len(gs)
    _, r, c = gs[0].shape
    g4 = [g.reshape(N_CHIPS, 2, r // 2, c) for g in gs]
    a = swap_halves_with_sibling(f"swap_{kind}", g4)
    p = [add_sibling_half(f"add_{kind}{l}", g4[l], a[l], core) for l in range(n)]
    b = scatter_to_chips(f"scatter_{kind}", p)
    red = [sum_chips(f"sum_{kind}{l}", b[l]) for l in range(n)]
    return join_halves_with_sibling(f"join_{kind}", red).reshape(n, r, c)


def _relu2(acc):
    r = jnp.maximum(acc, 0.0)
    return acc, r * r


def _relu2_bwd(acc, u):
    return (acc * (2.0 * jnp.maximum(u.astype(F32), 0.0)),)


def _same(acc):
    return (acc,)


def kernel(x, a_w_qkv, a_w_o, a_sinks, b_w_qkv, b_w_o, norm_mix, norm_mlp, mlp_w_in, mlp_w_out, final_norm, loss_target, m_a_w_qkv, m_a_w_o, m_a_sinks, m_b_w_qkv, m_b_w_o, m_norm_mix, m_norm_mlp, m_mlp_w_in, m_mlp_w_out, m_final_norm, v_a_w_qkv, v_a_w_o, v_a_sinks, v_b_w_qkv, v_b_w_o, v_norm_mix, v_norm_mlp, v_mlp_w_in, v_mlp_w_out, v_final_norm):
    _, s, d = x.shape
    depth = norm_mix.shape[0]
    width = N_HEADS * HEAD_DIM
    core = lax.axis_index("c").astype(jnp.int32).reshape(1)
    slopes = jnp.power(2.0, -8.0 * (jnp.arange(N_HEADS, dtype=F32) + 1.0) / N_HEADS)

    w_qkv = {0: _gather_layers("a_qkv", a_w_qkv), 1: _gather_layers("b_qkv", b_w_qkv)}
    w_o = {0: [w.reshape(width, d) for w in _gather_layers("a_o", a_w_o)],
           1: [w.reshape(width, d) for w in _gather_layers("b_o", b_w_o)]}
    w_in = _gather_layers("mlp_in", mlp_w_in)
    w_out = [w.reshape(-1, d) for w in _gather_layers("mlp_out", mlp_w_out)]

    saved = []
    xc = x[0]
    for i in range(depth):
        mixer, j = i % 2, i // 2
        h = rms_fwd(f"l{i}_norm_mix", xc, norm_mix[i])
        qkv = mm_cols(f"l{i}_qkv", h, w_qkv[mixer][j], _same, (BF16,))[0]
        if mixer == 0:
            attn, attn_f32 = swa_fwd(f"l{i}_swa", qkv, a_sinks[j], slopes), None
        else:
            attn, attn_f32 = sb_fwd(f"l{i}_sb", qkv)
        xm = mm_res(f"l{i}_o", attn, w_o[mixer][j], xc)
        h2 = rms_fwd(f"l{i}_norm_mlp", xm, norm_mlp[i])
        u, hh = mm_cols(f"l{i}_in", h2, w_in[i], _relu2, (BF16, BF16))
        xn = mm_res(f"l{i}_out", hh, w_out[i], xm)
        saved.append((xc, h, qkv, attn, attn_f32, xm, h2, u, hh))
        xc = xn
    loss_rows, dx, dxb, dg_final = loss_head("loss_head", xc, final_norm, loss_target[0])
    loss = lax.psum(jnp.sum(loss_rows), ("x", "y", "c"))

    g_qkv, g_o, g_in, g_out = {0: [], 1: []}, {0: [], 1: []}, [], []
    dg_mix, dg_mlp, dsinks = [], [], []
    for i in reversed(range(depth)):
        mixer, j = i % 2, i // 2
        xin, h, qkv, attn, attn_f32, xm, h2, u, hh = saved[i]
        du = mm_nt(f"l{i}_d_hidden", dxb, w_out[i], _relu2_bwd, (u,))
        g_out.append(mm_tn(f"l{i}_g_out", hh, dxb).reshape(N_CHIPS, -1, d))
        dh2 = mm_nt_cols(f"l{i}_d_h2", du, w_in[i])
        g_in.append(mm_tn_cols(f"l{i}_g_in", h2, du))
        dxm, dxmb, dg = rms_bwd(f"l{i}_norm_mlp_bwd", xm, norm_mlp[i], dh2, dx)
        dg_mlp.append(dg)
        dattn = mm_nt(f"l{i}_d_attn", dxmb, w_o[mixer][j], _same)
        g_o[mixer].append(mm_tn(f"l{i}_g_o", attn, dxmb).reshape(N_CHIPS, -1, d))
        if mixer == 0:
            dq, dk, dv, dsk = swa_bwd(f"l{i}_swa_bwd", qkv, dattn, a_sinks[j], slopes)
            dsinks.append(dsk[:, :2, 0].reshape(N_HEADS))
        else:
            dq, dk, dv = sb_bwd(f"l{i}_sb_bwd", qkv, attn_f32, dattn)
        dqkv = jnp.concatenate([dq, dk.astype(BF16), dv.astype(BF16)], axis=1)
        dh = mm_nt_cols(f"l{i}_d_h", dqkv, w_qkv[mixer][j])
        g_qkv[mixer].append(mm_tn_cols(f"l{i}_g_qkv", h, dqkv))
        dx, dxb, dg = rms_bwd(f"l{i}_norm_mix_bwd", xin, norm_mix[i], dh, dxm)
        dg_mix.append(dg)
    for lst in (g_qkv[0], g_qkv[1], g_o[0], g_o[1], g_in, g_out, dg_mix, dg_mlp, dsinks):
        lst.reverse()

    big = [("a_qkv", a_w_qkv, m_a_w_qkv, v_a_w_qkv, g_qkv[0]), ("a_o", a_w_o, m_a_w_o, v_a_w_o, g_o[0]),
           ("b_qkv", b_w_qkv, m_b_w_qkv, v_b_w_qkv, g_qkv[1]), ("b_o", b_w_o, m_b_w_o, v_b_w_o, g_o[1]),
           ("mlp_in", mlp_w_in, m_mlp_w_in, v_mlp_w_in, g_in), ("mlp_out", mlp_w_out, m_mlp_w_out, v_mlp_w_out, g_out)]
    res = {}
    for kind, w, m, v, gs in big:
        g = _reduce_kind(kind, gs, core)
        res[kind] = (g, *adamw(f"adamw_{kind}", w, g, m, v))

    n_sink = a_sinks.size
    sink_rows = jnp.zeros((1, 8, d), F32).at[0, 0, :n_sink].set(jnp.concatenate(dsinks))
    parts = jnp.concatenate([jnp.stack(dg_mix), jnp.stack(dg_mlp), dg_final[None], sink_rows], axis=0)
    g_small = allreduce_small("allreduce_small", parts)

    def pack(mix, mlp, fin, snk):
        snk_row = jnp.zeros((1, d), F32).at[0, :n_sink].set(snk.reshape(-1))
        return jnp.concatenate([mix, mlp, fin[None], snk_row], axis=0)

    def unpack(t):
        return t[:depth], t[depth:2 * depth], t[2 * depth], t[2 * depth + 1, :n_sink].reshape(a_sinks.shape)

    small = adamw("adamw_small", pack(norm_mix, norm_mlp, final_norm, a_sinks), g_small,
                  pack(m_norm_mix, m_norm_mlp, m_final_norm, m_a_sinks),
                  pack(v_norm_mix, v_norm_mlp, v_final_norm, v_a_sinks))
    outs = []
    for idx in range(4):
        mix, mlp, fin, snk = unpack(g_small if idx == 0 else small[idx - 1])
        outs += [res["a_qkv"][idx], res["a_o"][idx], snk, res["b_qkv"][idx], res["b_o"][idx], mix, mlp,
                 res["mlp_in"][idx], res["mlp_out"][idx], fin]
    return (loss, dx.reshape(x.shape), *outs)
```

```python
import functools
import math

import jax
import jax.numpy as jnp
from jax import lax
from jax.experimental import pallas as pl
from jax.experimental.pallas import tpu as pltpu

F32 = jnp.float32
BF16 = jnp.bfloat16
MESH = pl.DeviceIdType.MESH

N_CHIPS = 4
HEAD_DIM = 64
LANES = 128
N_HEADS = 32
N_KV_A = 4
WINDOW = 128
RMS_EPS = 1e-5
ATTN_SCALE = 1.0 / math.sqrt(HEAD_DIM)
ADAM_LR, ADAM_B1, ADAM_B2, ADAM_EPS, ADAM_WD, ADAM_STEP = 0.001, 0.9, 0.999, 1e-08, 0.01, 10
NEG_BIG = -1e30
VMEM_LIMIT = 56 * 1024 * 1024

_DN = {"nn": (((1,), (0,)), ((), ())), "nt": (((1,), (1,)), ((), ())), "tn": (((0,), (0,)), ((), ()))}


def _dot(a, b, mode="nn"):
    return lax.dot_general(a, b, _DN[mode], preferred_element_type=F32)


def _params(*sem):
    return pltpu.CompilerParams(dimension_semantics=sem, vmem_limit_bytes=VMEM_LIMIT)


def _pick(n, prefs):
    for t in prefs:
        if n % t == 0:
            return t
    return n


def _mm(name, mode, a, b, *, grid, a_spec, b_spec, extras=(), extra_specs=(), out_shapes, out_specs, nk,
        acc_shape, epilogue):
    n_ex, n_out = len(extras), len(out_shapes)

    def body(*refs):
        a_ref, b_ref = refs[0], refs[1]
        ex = refs[2:2 + n_ex]
        outs = refs[2 + n_ex:2 + n_ex + n_out]
        part = _dot(a_ref[...], b_ref[...], mode)

        def finish(acc):
            res = epilogue(acc, *[e[...] for e in ex])
            for o, r in zip(outs, res):
                o[...] = r.astype(o.dtype)

        if nk == 1:
            finish(part)
        else:
            acc_ref = refs[-1]
            k = pl.program_id(len(grid) - 1)

            @pl.when(k == 0)
            def _():
                acc_ref[...] = part

            @pl.when(k > 0)
            def _():
                acc_ref[...] += part

            @pl.when(k == nk - 1)
            def _():
                finish(acc_ref[...])

    sem = ("parallel",) * (len(grid) - 1) + ("arbitrary" if nk > 1 else "parallel",)
    return pl.pallas_call(
        body, name=name, grid=grid,
        in_specs=[a_spec, b_spec, *extra_specs],
        out_specs=list(out_specs), out_shape=list(out_shapes),
        scratch_shapes=[] if nk == 1 else [pltpu.VMEM(acc_shape, F32)],
        compiler_params=_params(*sem),
    )(a, b, *extras)


def mm_cols(name, a, wg, epilogue, out_dtypes, tm=1024):
    m, k = a.shape
    c = wg.shape[2]
    tm = min(tm, m)
    tn = _pick(c, (512, 640, 256, 128))
    nj = c // tn
    o_spec = pl.BlockSpec((tm, tn), lambda i, j: (i, j))
    return _mm(name, "nn", a, wg, grid=(m // tm, N_CHIPS * nj),
               a_spec=pl.BlockSpec((tm, k), lambda i, j: (i, 0)),
               b_spec=pl.BlockSpec((None, k, tn), lambda i, j: (j // nj, 0, j % nj)),
               out_shapes=[jax.ShapeDtypeStruct((m, N_CHIPS * c), d) for d in out_dtypes],
               out_specs=[o_spec] * len(out_dtypes), nk=1, acc_shape=None, epilogue=epilogue)


def mm_res(name, a, w, res, tm=1024, tn=512, tk=2048):
    m, k = a.shape
    n = w.shape[1]
    tm, tn, tk = min(tm, m), min(tn, n), min(tk, k)
    nk = k // tk
    return _mm(name, "nn", a, w, grid=(m // tm, n // tn, nk),
               a_spec=pl.BlockSpec((tm, tk), lambda i, j, kk: (i, kk)),
               b_spec=pl.BlockSpec((tk, tn), lambda i, j, kk: (kk, j)),
               extras=(res,), extra_specs=(pl.BlockSpec((tm, tn), lambda i, j, kk: (i, j)),),
               out_shapes=[jax.ShapeDtypeStruct((m, n), F32)],
               out_specs=[pl.BlockSpec((tm, tn), lambda i, j, kk: (i, j))], nk=nk, acc_shape=(tm, tn),
               epilogue=lambda acc, r: (acc + r,))[0]


def mm_nt(name, a, w, epilogue, extras=(), tm=1024, tn=512):
    m, k = a.shape
    n = w.shape[0]
    tm, tn = min(tm, m), min(tn, n)
    o_spec = pl.BlockSpec((tm, tn), lambda i, j: (i, j))
    return _mm(name, "nt", a, w, grid=(m // tm, n // tn),
               a_spec=pl.BlockSpec((tm, k), lambda i, j: (i, 0)),
               b_spec=pl.BlockSpec((tn, k), lambda i, j: (j, 0)),
               extras=tuple(extras), extra_specs=(o_spec,) * len(extras),
               out_shapes=[jax.ShapeDtypeStruct((m, n), BF16)], out_specs=[o_spec], nk=1, acc_shape=None,
               epilogue=epilogue)[0]


def mm_nt_cols(name, dy, wg, tm=1024, tn=512):
    m = dy.shape[0]
    _, d, c = wg.shape
    tm, tn = min(tm, m), min(tn, d)
    tk = _pick(c, (2048, 1536, 640, 512, 128))
    nkk = c // tk
    nk = N_CHIPS * nkk
    return _mm(name, "nt", dy, wg, grid=(m // tm, d // tn, nk),
               a_spec=pl.BlockSpec((tm, tk), lambda i, j, kk: (i, kk)),
               b_spec=pl.BlockSpec((None, tn, tk), lambda i, j, kk: (kk // nkk, j, kk % nkk)),
               out_shapes=[jax.ShapeDtypeStruct((m, d), F32)],
               out_specs=[pl.BlockSpec((tm, tn), lambda i, j, kk: (i, j))], nk=nk, acc_shape=(tm, tn),
               epilogue=lambda acc: (acc,))[0]


def mm_tn(name, a, b, tm=1024, tn=1024, tk=1024):
    m, p = a.shape
    q = b.shape[1]
    tm, tn, tk = min(tm, p), min(tn, q), min(tk, m)
    nk = m // tk
    return _mm(name, "tn", a, b, grid=(p // tm, q // tn, nk),
               a_spec=pl.BlockSpec((tk, tm), lambda i, j, kk: (kk, i)),
               b_spec=pl.BlockSpec((tk, tn), lambda i, j, kk: (kk, j)),
               out_shapes=[jax.ShapeDtypeStruct((p, q), BF16)],
               out_specs=[pl.BlockSpec((tm, tn), lambda i, j, kk: (i, j))], nk=nk, acc_shape=(tm, tn),
               epilogue=lambda acc: (acc,))[0]


def mm_tn_cols(name, a, dy, tm=1024, tk=1024):
    m, d = a.shape
    c = dy.shape[1] // N_CHIPS
    tm, tk = min(tm, d), min(tk, m)
    tn = _pick(c, (1024, 768, 640, 512, 128))
    nj = c // tn
    nk = m // tk
    return _mm(name, "tn", a, dy, grid=(d // tm, N_CHIPS * nj, nk),
               a_spec=pl.BlockSpec((tk, tm), lambda i, j, kk: (kk, i)),
               b_spec=pl.BlockSpec((tk, tn), lambda i, j, kk: (kk, j)),
               out_shapes=[jax.ShapeDtypeStruct((N_CHIPS, d, c), BF16)],
               out_specs=[pl.BlockSpec((None, tm, tn), lambda i, j, kk: (j // nj, i, j % nj))], nk=nk,
               acc_shape=(tm, tn), epilogue=lambda acc: (acc,))[0]


def _rows_to_8(v):
    tm, d = v.shape
    return jnp.sum(v.reshape(tm // 8, 8, d), axis=0)


def rms_fwd(name, x, gain, tm=512):
    s, d = x.shape
    tm = min(tm, s)

    def body(x_ref, g_ref, h_ref):
        xv = x_ref[...]
        r = lax.rsqrt(jnp.mean(xv * xv, axis=-1, keepdims=True) + RMS_EPS)
        h_ref[...] = (xv * r * g_ref[...]).astype(BF16)

    row = pl.BlockSpec((tm, d), lambda i: (i, 0))
    return pl.pallas_call(
        body, name=name, grid=(s // tm,),
        in_specs=[row, pl.BlockSpec((1, d), lambda i: (0, 0))], out_specs=row,
        out_shape=jax.ShapeDtypeStruct((s, d), BF16), compiler_params=_params("parallel"),
    )(x, gain.reshape(1, d))


def rms_bwd(name, x, gain, dh, dres, tm=512):
    s, d = x.shape
    tm = min(tm, s)

    def body(x_ref, g_ref, dh_ref, dres_ref, dx_ref, dxb_ref, dg_ref):
        xv = x_ref[...]
        r = lax.rsqrt(jnp.mean(xv * xv, axis=-1, keepdims=True) + RMS_EPS)
        xhat = xv * r
        dhv = dh_ref[...]
        dxhat = dhv * g_ref[...]
        dx = dres_ref[...] + r * (dxhat - xhat * jnp.mean(dxhat * xhat, axis=-1, keepdims=True))
        dx_ref[...] = dx
        dxb_ref[...] = dx.astype(BF16)

        @pl.when(pl.program_id(0) == 0)
        def _():
            dg_ref[...] = jnp.zeros_like(dg_ref)

        dg_ref[...] += _rows_to_8(dhv * xhat)

    row = pl.BlockSpec((tm, d), lambda i: (i, 0))
    return pl.pallas_call(
        body, name=name, grid=(s // tm,),
        in_specs=[row, pl.BlockSpec((1, d), lambda i: (0, 0)), row, row],
        out_specs=[row, row, pl.BlockSpec((8, d), lambda i: (0, 0))],
        out_shape=[jax.ShapeDtypeStruct((s, d), F32), jax.ShapeDtypeStruct((s, d), BF16),
                   jax.ShapeDtypeStruct((8, d), F32)],
        compiler_params=_params("arbitrary"),
    )(x, gain.reshape(1, d), dh, dres)


def loss_head(name, x, gain, target, tm=512):
    s, d = x.shape
    tm = min(tm, s)

    def body(x_ref, g_ref, t_ref, loss_ref, dx_ref, dxb_ref, dg_ref):
        xv = x_ref[...]
        g = g_ref[...]
        r = lax.rsqrt(jnp.mean(xv * xv, axis=-1, keepdims=True) + RMS_EPS)
        xhat = xv * r
        err = xhat * g - t_ref[...]
        dy = err * (1.0 / d)
        dxhat = dy * g
        dx = r * (dxhat - xhat * jnp.mean(dxhat * xhat, axis=-1, keepdims=True))
        dx_ref[...] = dx
        dxb_ref[...] = dx.astype(BF16)

        @pl.when(pl.program_id(0) == 0)
        def _():
            dg_ref[...] = jnp.zeros_like(dg_ref)
            loss_ref[...] = jnp.zeros_like(loss_ref)

        dg_ref[...] += _rows_to_8(dy * xhat)
        loss_ref[...] += _rows_to_8(err * err) * (0.5 / d)

    row = pl.BlockSpec((tm, d), lambda i: (i, 0))
    vec = pl.BlockSpec((8, d), lambda i: (0, 0))
    return pl.pallas_call(
        body, name=name, grid=(s // tm,),
        in_specs=[row, pl.BlockSpec((1, d), lambda i: (0, 0)), row],
        out_specs=[vec, row, row, vec],
        out_shape=[jax.ShapeDtypeStruct((8, d), F32), jax.ShapeDtypeStruct((s, d), F32),
                   jax.ShapeDtypeStruct((s, d), BF16), jax.ShapeDtypeStruct((8, d), F32)],
        compiler_params=_params("arbitrary"),
    )(x, gain.reshape(1, d), target)


def _half_masks(dtype):
    lane = lax.broadcasted_iota(jnp.int32, (1, LANES), 1)
    lo = (lane < HEAD_DIM).astype(dtype)
    return lo, (1 - lo).astype(dtype)


def _swap_halves(v):
    return pltpu.roll(v, HEAD_DIM, axis=1)


def _swa_tile(q, k2, v2, sink, slope, n, hq_mask_b):
    t = q.shape[0]
    s = _dot(q * hq_mask_b, k2, "nt")
    row = lax.broadcasted_iota(jnp.int32, (t, 2 * WINDOW), 0)
    col = lax.broadcasted_iota(jnp.int32, (t, 2 * WINDOW), 1)
    dist = row + WINDOW - col
    valid = (dist >= 0) & (dist < WINDOW) & ((col >= WINDOW) | (n > 0))
    s = jnp.where(valid, s - slope * dist.astype(F32), NEG_BIG)
    m = jnp.maximum(jnp.max(s, axis=-1, keepdims=True), sink)
    e = jnp.exp(s - m)
    e_sink = jnp.exp(sink - m)
    inv = 1.0 / (jnp.sum(e, axis=-1, keepdims=True) + e_sink)
    return e * inv, e_sink * inv


def _swa_kv(kp_ref, kc_ref, vp_ref, vc_ref, kv_half):
    lo_b, hi_b = _half_masks(F32)
    sel = jnp.where(kv_half == 0, lo_b, hi_b)
    k2 = jnp.concatenate([kp_ref[...], kc_ref[...]], axis=0).astype(F32) * sel
    v2 = jnp.concatenate([vp_ref[...], vc_ref[...]], axis=0).astype(F32) * sel
    k2 = (k2 + _swap_halves(k2)).astype(BF16)
    v2 = (v2 + _swap_halves(v2)).astype(BF16)
    return k2, v2, sel


def _swa_specs(t):
    q_spec = pl.BlockSpec((t, LANES), lambda j, n, *_: (n, j))
    q_blocks = N_HEADS // 2

    def kv(off, prev):
        if prev:
            return pl.BlockSpec((t, LANES), lambda j, n, *_: (jnp.maximum(n - 1, 0), q_blocks + off + j // 8))
        return pl.BlockSpec((t, LANES), lambda j, n, *_: (n, q_blocks + off + j // 8))

    kv_blocks = N_KV_A // 2
    return q_spec, [kv(0, True), kv(0, False), kv(kv_blocks, True), kv(kv_blocks, False)]


def swa_fwd(name, qkv, sinks, slopes):
    s = qkv.shape[0]
    t = WINDOW

    def body(sink_ref, slope_ref, q_ref, kp_ref, kc_ref, vp_ref, vc_ref, o_ref):
        j, n = pl.program_id(0), pl.program_id(1)
        k2, v2, _ = _swa_kv(kp_ref, kc_ref, vp_ref, vc_ref, (j // 4) % 2)
        q = q_ref[...] * ATTN_SCALE
        masks = _half_masks(BF16)
        acc = jnp.zeros((t, LANES), F32)
        for hq in range(2):
            p, _ = _swa_tile(q, k2, v2, sink_ref[2 * j + hq], slope_ref[2 * j + hq], n, masks[hq])
            acc += _dot(p.astype(BF16), v2 * masks[hq])
        o_ref[...] = acc.astype(BF16)

    q_spec, kv_specs = _swa_specs(t)
    return pl.pallas_call(
        body, name=name,
        grid_spec=pltpu.PrefetchScalarGridSpec(
            num_scalar_prefetch=2, grid=(N_HEADS // 2, s // t),
            in_specs=[q_spec, *kv_specs], out_specs=q_spec),
        out_shape=jax.ShapeDtypeStruct((s, N_HEADS * HEAD_DIM), BF16),
        compiler_params=_params("parallel", "parallel"),
    )(sinks, slopes, qkv, qkv, qkv, qkv, qkv)


def swa_bwd(name, qkv, do, sinks, slopes):
    s = qkv.shape[0]
    t = WINDOW

    def body(sink_ref, slope_ref, q_ref, kp_ref, kc_ref, vp_ref, vc_ref, do_ref, dq_ref, dk_ref, dv_ref, ds_ref):
        j, n = pl.program_id(0), pl.program_id(1)
        k2, v2, sel = _swa_kv(kp_ref, kc_ref, vp_ref, vc_ref, (j // 4) % 2)
        q = q_ref[...] * ATTN_SCALE
        do_v = do_ref[...]
        masks = _half_masks(BF16)

        @pl.when((j % 8 == 0) & (n == 0))
        def _():
            dk_ref[...] = jnp.zeros_like(dk_ref)
            dv_ref[...] = jnp.zeros_like(dv_ref)

        @pl.when(n == 0)
        def _():
            ds_ref[...] = jnp.zeros_like(ds_ref)

        dq = jnp.zeros((t, LANES), F32)
        dk = jnp.zeros((2 * t, LANES), F32)
        dv = jnp.zeros((2 * t, LANES), F32)
        for hq in range(2):
            sink = sink_ref[2 * j + hq]
            p, p_sink = _swa_tile(q, k2, v2, sink, slope_ref[2 * j + hq], n, masks[hq])
            qm = q * masks[hq]
            dom = do_v * masks[hq]
            dp = _dot(dom, v2, "nt")
            delta = jnp.sum(p * dp, axis=-1, keepdims=True)
            dsc = (p * (dp - delta)).astype(BF16)
            dq += _dot(dsc, k2 * masks[hq])
            dk += _dot(dsc, qm, "tn")
            dv += _dot(p.astype(BF16), dom, "tn")
            ds_ref[hq:hq + 1, :] += jnp.zeros((1, LANES), F32) - jnp.sum(p_sink * delta)
        dq_ref[...] = (dq * ATTN_SCALE).astype(BF16)
        dk = (dk + _swap_halves(dk)) * sel
        dv = (dv + _swap_halves(dv)) * sel

        @pl.when(n == 0)
        def _():
            dk_ref[pl.ds(0, t), :] += dk[t:]
            dv_ref[pl.ds(0, t), :] += dv[t:]

        @pl.when(n > 0)
        def _():
            start = pl.multiple_of((n - 1) * t, t)
            dk_ref[pl.ds(start, 2 * t), :] += dk
            dv_ref[pl.ds(start, 2 * t), :] += dv

    q_spec, kv_specs = _swa_specs(t)
    kv_out = pl.BlockSpec((s, LANES), lambda j, n, *_: (0, j // 8))
    return pl.pallas_call(
        body, name=name,
        grid_spec=pltpu.PrefetchScalarGridSpec(
            num_scalar_prefetch=2, grid=(N_HEADS // 2, s // t),
            in_specs=[q_spec, *kv_specs, q_spec],
            out_specs=[q_spec, kv_out, kv_out, pl.BlockSpec((None, 8, LANES), lambda j, n, *_: (j, 0, 0))]),
        out_shape=[jax.ShapeDtypeStruct((s, N_HEADS * HEAD_DIM), BF16),
                   jax.ShapeDtypeStruct((s, N_KV_A * HEAD_DIM), F32),
                   jax.ShapeDtypeStruct((s, N_KV_A * HEAD_DIM), F32),
                   jax.ShapeDtypeStruct((N_HEADS // 2, 8, LANES), F32)],
        compiler_params=_params("arbitrary", "arbitrary"),
    )(sinks, slopes, qkv, qkv, qkv, qkv, qkv, do)


SB_TILE = 256


def _split2(v):
    hi = v.astype(BF16)
    lo = (v - hi.astype(F32)).astype(BF16)
    return jnp.concatenate([hi, lo], axis=0)


def _tri(t, inclusive):
    r = lax.broadcasted_iota(jnp.int32, (t, t), 0)
    c = lax.broadcasted_iota(jnp.int32, (t, t), 1)
    return ((r >= c) if inclusive else (r > c)).astype(BF16)


def _sb_probs(qm, kj, c_lm, tri_x, before):
    t = qm.shape[0]
    z = _dot(qm, kj, "nt")
    l = jnp.log(1.0 + jnp.exp(-jnp.abs(z)))
    lb = jnp.minimum(z, 0.0) - l
    lm = lb - z
    if before is not None:
        lm = jnp.where(before, lm, 0.0)
    sfx = _dot(_split2(lm), tri_x)
    e = lb + (sfx[:t] + sfx[t:]) + c_lm
    a = jnp.exp(e)
    if before is not None:
        a = jnp.where(before, a, 0.0)
    return a, lb, jnp.sum(lm, axis=-1, keepdims=True)


def _sb_specs(s, t):
    hp = N_HEADS // 2
    q_spec = pl.BlockSpec((t, LANES), lambda h, i: (i, h))
    k_spec = pl.BlockSpec((s, LANES), lambda h, i: (0, hp + h))
    v_spec = pl.BlockSpec((s, LANES), lambda h, i: (0, 2 * hp + h))
    return q_spec, k_spec, v_spec


def sb_fwd(name, qkv, t=SB_TILE):
    s = qkv.shape[0]
    t = min(t, s)

    def body(q_ref, k_ref, v_ref, ob_ref, of_ref):
        i = pl.program_id(1)
        masks = _half_masks(BF16)
        q = q_ref[...] * ATTN_SCALE
        qm = [q * masks[0], q * masks[1]]
        tri_x = _tri(t, False)
        r = lax.broadcasted_iota(jnp.int32, (t, t), 0)
        c = lax.broadcasted_iota(jnp.int32, (t, t), 1)
        diag = c < r

        def tile(j, carry, before):
            c0, c1, acc = carry
            start = pl.multiple_of(j * t, t)
            kj = k_ref[pl.ds(start, t), :]
            vj = v_ref[pl.ds(start, t), :]
            cs = [c0, c1]
            for h in range(2):
                a, _, lm_sum = _sb_probs(qm[h], kj, cs[h], tri_x, before)
                acc += _dot(a.astype(BF16), vj * masks[h])
                cs[h] = cs[h] + lm_sum
            return cs[0], cs[1], acc

        zero = jnp.zeros((t, 1), F32)
        carry = tile(i, (zero, zero, jnp.zeros((t, LANES), F32)), diag)
        carry = lax.fori_loop(0, i, lambda step, cr: tile(i - 1 - step, cr, None), carry)
        ob_ref[...] = carry[2].astype(BF16)
        of_ref[...] = carry[2]

    q_spec, k_spec, v_spec = _sb_specs(s, t)
    return pl.pallas_call(
        body, name=name, grid=(N_HEADS // 2, s // t),
        in_specs=[q_spec, k_spec, v_spec], out_specs=[q_spec, q_spec],
        out_shape=[jax.ShapeDtypeStruct((s, N_HEADS * HEAD_DIM), BF16),
                   jax.ShapeDtypeStruct((s, N_HEADS * HEAD_DIM), F32)],
        compiler_params=_params("parallel", "parallel"),
    )(qkv, qkv, qkv)


def sb_bwd(name, qkv, o_f32, do, t=SB_TILE):
    s = qkv.shape[0]
    t = min(t, s)

    def body(q_ref, k_ref, v_ref, o_ref, do_ref, dq_ref, dk_ref, dv_ref):
        i = pl.program_id(1)
        masks = _half_masks(BF16)
        fmasks = _half_masks(F32)
        q = q_ref[...] * ATTN_SCALE
        do_v = do_ref[...]
        qm = [q * masks[0], q * masks[1]]
        dom = [do_v * masks[0], do_v * masks[1]]
        prod = do_v.astype(F32) * o_ref[...]
        delta = [jnp.sum(prod * fmasks[h], axis=-1, keepdims=True) for h in range(2)]
        tri_x = _tri(t, False)
        tri_i = _tri(t, True)
        r = lax.broadcasted_iota(jnp.int32, (t, t), 0)
        c = lax.broadcasted_iota(jnp.int32, (t, t), 1)
        diag = c < r

        @pl.when(i == 0)
        def _():
            dk_ref[...] = jnp.zeros_like(dk_ref)
            dv_ref[...] = jnp.zeros_like(dv_ref)

        def tile(j, carry, before):
            cl0, cl1, cd0, cd1, dq = carry
            start = pl.multiple_of(j * t, t)
            kj = k_ref[pl.ds(start, t), :]
            vj = v_ref[pl.ds(start, t), :]
            cl, cd = [cl0, cl1], [cd0, cd1]
            dk = jnp.zeros((t, LANES), F32)
            dv = jnp.zeros((t, LANES), F32)
            for h in range(2):
                a, lb, lm_sum = _sb_probs(qm[h], kj, cl[h], tri_x, before)
                ab = a.astype(BF16)
                de = _dot(dom[h], vj, "nt") * ab.astype(F32)
                sfx = _dot(_split2(de), tri_i)
                farther = delta[h] - cd[h] - (sfx[:t] + sfx[t:])
                sig = jnp.exp(lb)
                dz = de * (1.0 - sig) - farther * sig
                if before is not None:
                    dz = jnp.where(before, dz, 0.0)
                dzb = dz.astype(BF16)
                dq += _dot(dzb, kj * masks[h])
                dk += _dot(dzb, qm[h], "tn")
                dv += _dot(ab, dom[h], "tn")
                cl[h] = cl[h] + lm_sum
                cd[h] = cd[h] + jnp.sum(de, axis=-1, keepdims=True)
            dk_ref[pl.ds(start, t), :] += dk
            dv_ref[pl.ds(start, t), :] += dv
            return cl[0], cl[1], cd[0], cd[1], dq

        zero = jnp.zeros((t, 1), F32)
        carry = tile(i, (zero, zero, zero, zero, jnp.zeros((t, LANES), F32)), diag)
        carry = lax.fori_loop(0, i, lambda step, cr: tile(i - 1 - step, cr, None), carry)
        dq_ref[...] = (carry[4] * ATTN_SCALE).astype(BF16)

    q_spec, k_spec, v_spec = _sb_specs(s, t)
    kv_out = pl.BlockSpec((s, LANES), lambda h, i: (0, h))
    width = N_HEADS * HEAD_DIM
    return pl.pallas_call(
        body, name=name, grid=(N_HEADS // 2, s // t),
        in_specs=[q_spec, k_spec, v_spec, q_spec, q_spec], out_specs=[q_spec, kv_out, kv_out],
        out_shape=[jax.ShapeDtypeStruct((s, width), BF16), jax.ShapeDtypeStruct((s, width), F32),
                   jax.ShapeDtypeStruct((s, width), F32)],
        compiler_params=_params("parallel", "arbitrary"),
    )(qkv, qkv, qkv, o_f32, do)


def adamw(name, w, g, m, v, tm=256):
    shape = w.shape
    c = shape[-1]
    rows = math.prod(shape[:-1])
    tm = min(tm, rows)

    def body(w_ref, g_ref, m_ref, v_ref, d_ref, mo_ref, vo_ref):
        gv = g_ref[...]
        m2 = ADAM_B1 * m_ref[...] + (1.0 - ADAM_B1) * gv
        v2 = ADAM_B2 * v_ref[...] + (1.0 - ADAM_B2) * (gv * gv)
        m_hat = m2 / (1.0 - ADAM_B1 ** ADAM_STEP)
        v_hat = v2 / (1.0 - ADAM_B2 ** ADAM_STEP)
        d_ref[...] = -ADAM_LR * (m_hat / (jnp.sqrt(v_hat) + ADAM_EPS) + ADAM_WD * w_ref[...])
        mo_ref[...] = m2
        vo_ref[...] = v2

    blk = pl.BlockSpec((tm, c), lambda i: (i, 0))
    outs = pl.pallas_call(
        body, name=name, grid=(rows // tm,), in_specs=[blk] * 4, out_specs=[blk] * 3,
        out_shape=[jax.ShapeDtypeStruct((rows, c), F32)] * 3, compiler_params=_params("parallel"),
    )(*[t.reshape(rows, c) for t in (w, g, m, v)])
    return [o.reshape(shape) for o in outs]


HBM = pl.BlockSpec(memory_space=pltpu.HBM)


def _place():
    x, y, c = lax.axis_index("x"), lax.axis_index("y"), lax.axis_index("c")
    return x, y, c, [(1 - x, y), (x, 1 - y), (1 - x, 1 - y)]


def _remote(src, dst, send, recv, dev):
    return pltpu.make_async_remote_copy(src_ref=src, dst_ref=dst, send_sem=send, recv_sem=recv, device_id=dev,
                                        device_id_type=MESH)


SEM = pl.BlockSpec(memory_space=pltpu.SEMAPHORE)
ANY = pl.BlockSpec(memory_space=pl.ANY)
DATAFLOW = pltpu.SideEffectType.DATAFLOW_SIDE_EFFECTING


def _in_hbm(v):
    return pltpu.with_memory_space_constraint(v, pltpu.HBM)


def split_start(name, bufs, n_copies, sends, after=()):
    nb, na = len(bufs), len(after)

    def body(*refs):
        send, recv = refs[nb + na], refs[nb + na + 1]
        for cp in sends(refs[:nb], send, recv):
            cp.start()
        refs[-1][...] = jnp.zeros_like(refs[-1])

    outs = pl.pallas_call(
        body, name=name,
        in_specs=[HBM] * nb + [ANY] * na,
        out_shape=(pltpu.SemaphoreType.DMA((n_copies,)), pltpu.SemaphoreType.DMA((n_copies,)),
                   *[pltpu.HBM(b.shape, b.dtype) for b in bufs], jax.ShapeDtypeStruct((8, LANES), F32)),
        out_specs=(SEM, SEM, *[HBM] * nb, pl.BlockSpec(memory_space=pltpu.VMEM)),
        input_output_aliases={i: 2 + i for i in range(nb)},
        compiler_params=pltpu.CompilerParams(has_side_effects=DATAFLOW),
    )(*[_in_hbm(b) for b in bufs], *after)
    return outs[0], outs[1], list(outs[2:2 + nb]), outs[-1]


def split_wait(name, send_sems, recv_sems, bufs, sends, arrivals, after):
    nb, na = len(bufs), len(after)

    def body(*refs):
        send, recv = refs[nb], refs[nb + 1]
        for cp in sends(refs[:nb], send, recv):
            cp.wait_send()
        for cp in arrivals(refs[:nb], send, recv):
            cp.wait_recv()

    outs = pl.pallas_call(
        body, name=name,
        in_specs=[HBM] * nb + [SEM, SEM] + [ANY] * na,
        out_shape=tuple(pltpu.HBM(b.shape, b.dtype) for b in bufs), out_specs=tuple([HBM] * nb),
        input_output_aliases={i: i for i in range(nb)},
        compiler_params=pltpu.CompilerParams(has_side_effects=DATAFLOW),
    )(*bufs, send_sems, recv_sems, *after)
    return list(outs)


def _gather_plan(n):
    def sends(refs, send, recv):
        x, y, c, chips = _place()
        me = 2 * x + y
        return [_remote(refs[t].at[me], refs[t].at[me], send.at[3 * t + k], recv.at[3 * t + k], (px, py, c))
                for t in range(n) for k, (px, py) in enumerate(chips)]

    def arrivals(refs, send, recv):
        x, y, c, chips = _place()
        return [_remote(refs[t].at[2 * px + py], refs[t].at[2 * px + py], send.at[3 * t + k], recv.at[3 * t + k],
                        (px, py, c)) for t in range(n) for k, (px, py) in enumerate(chips)]

    return 3 * n, sends, arrivals


def _swap_plan(n):
    def copies(refs, send, recv):
        x, y, c, _ = _place()
        return [_remote(refs[t].at[s, 1 - c], refs[n + t].at[s], send.at[N_CHIPS * t + s], recv.at[N_CHIPS * t + s],
                        (x, y, 1 - c)) for t in range(n) for s in range(N_CHIPS)]

    return N_CHIPS * n, copies, copies


def _scatter_plan(n):
    def sends(refs, send, recv):
        x, y, c, chips = _place()
        me = 2 * x + y
        return [_remote(refs[t].at[2 * px + py], refs[n + t].at[me], send.at[3 * t + k], recv.at[3 * t + k],
                        (px, py, c)) for t in range(n) for k, (px, py) in enumerate(chips)]

    def arrivals(refs, send, recv):
        x, y, c, chips = _place()
        return [_remote(refs[t].at[2 * px + py], refs[n + t].at[2 * px + py], send.at[3 * t + k], recv.at[3 * t + k],
                        (px, py, c)) for t in range(n) for k, (px, py) in enumerate(chips)]

    return 3 * n, sends, arrivals


def cast_into_slot(name, w, chip, tm=256):
    r, c = w.shape
    tm = min(tm, r)

    def body(chip_ref, w_ref, o_ref):
        o_ref[...] = w_ref[...].astype(BF16)

    return pl.pallas_call(
        body, name=name,
        grid_spec=pltpu.PrefetchScalarGridSpec(
            num_scalar_prefetch=1, grid=(r // tm,),
            in_specs=[pl.BlockSpec((tm, c), lambda i, chip_ref: (i, 0))],
            out_specs=pl.BlockSpec((None, tm, c), lambda i, chip_ref: (chip_ref[0], i, 0))),
        out_shape=jax.ShapeDtypeStruct((N_CHIPS, r, c), BF16), compiler_params=_params("parallel"),
    )(chip, w)


def join_halves(name, f):
    n = f.shape[0]

    def body(f_ref, o_ref, send, recv):
        x, y, c, _ = _place()
        sib = (x, y, 1 - c)
        sends = [_remote(f_ref.at[l, c], o_ref.at[l, c], send.at[l], recv.at[l], sib) for l in range(n)]
        for cp in sends:
            cp.start()
        for l in range(n):
            _remote(f_ref.at[l, 1 - c], o_ref.at[l, 1 - c], send.at[l], recv.at[l], sib).wait_recv()
        for cp in sends:
            cp.wait_send()

    return pl.pallas_call(
        body, name=name, in_specs=[HBM], out_specs=HBM, out_shape=jax.ShapeDtypeStruct(f.shape, f.dtype),
        input_output_aliases={0: 0},
        scratch_shapes=[pltpu.SemaphoreType.DMA((n,)), pltpu.SemaphoreType.DMA((n,))],
    )(f)


def add_sibling_half(name, g, a, core, tm=256):
    _, _, rh, c = g.shape
    tm = min(tm, rh)

    def body(core_ref, g_ref, a_ref, o_ref):
        o_ref[...] = (g_ref[...].astype(F32) + a_ref[...].astype(F32)).astype(BF16)

    return pl.pallas_call(
        body, name=name,
        grid_spec=pltpu.PrefetchScalarGridSpec(
            num_scalar_prefetch=1, grid=(N_CHIPS, rh // tm),
            in_specs=[pl.BlockSpec((None, None, tm, c), lambda s, i, core_ref: (s, core_ref[0], i, 0)),
                      pl.BlockSpec((None, tm, c), lambda s, i, core_ref: (s, i, 0))],
            out_specs=pl.BlockSpec((None, tm, c), lambda s, i, core_ref: (s, i, 0))),
        out_shape=jax.ShapeDtypeStruct(a.shape, BF16), compiler_params=_params("parallel", "parallel"),
    )(core, g, a)


def sum_chips_into(name, p, b, f, layer, chip, core, tm=256):
    _, rh, c = b.shape
    tm = min(tm, rh)

    def body(chip_ref, core_ref, p_ref, b_ref, f_ref, o_ref):
        acc = jnp.zeros((tm, c), F32)
        for s in range(N_CHIPS):
            acc = acc + jnp.where(chip_ref[0] == s, p_ref[s].astype(F32), b_ref[s].astype(F32))
        o_ref[...] = acc

    slots = pl.BlockSpec((N_CHIPS, tm, c), lambda i, chip_ref, core_ref: (0, i, 0))
    return pl.pallas_call(
        body, name=name,
        grid_spec=pltpu.PrefetchScalarGridSpec(
            num_scalar_prefetch=2, grid=(rh // tm,), in_specs=[slots, slots, ANY],
            out_specs=pl.BlockSpec((None, None, tm, c), lambda i, chip_ref, core_ref: (layer, core_ref[0], i, 0))),
        out_shape=jax.ShapeDtypeStruct(f.shape, F32), input_output_aliases={4: 0},
        compiler_params=_params("parallel"),
    )(chip, core, p, b, f)


N_DEV = 8


def allreduce_small(name, parts):
    p, _, d = parts.shape
    m_per = p * 8

    def body(x_ref, out_ref, all_ref, send_sems, recv_sems, local_sem):
        x, y, c, chips = _place()
        me, sibling = (x, y, c), (x, y, 1 - c)

        def rows(px, py, pc):
            return all_ref.at[pl.ds((4 * px + 2 * py + pc) * m_per, m_per), :]

        def copy(k, block, to, src=None):
            return _remote(rows(*block) if src is None else src, rows(*block), send_sems.at[k], recv_sems.at[k], to)

        mine = pltpu.make_async_copy(x_ref, rows(*me), local_sem)
        mine.start()
        first = [copy(0, me, sibling, src=x_ref)]
        first += [copy(1 + j, me, (*chip, c), src=x_ref) for j, chip in enumerate(chips)]
        for cp in first:
            cp.start()
        passed = [copy(4 + j, (*chip, c), sibling) for j, chip in enumerate(chips)]
        for j, chip in enumerate(chips):
            copy(1 + j, (*chip, c), me).wait_recv()
            passed[j].start()
        copy(0, sibling, me).wait_recv()
        for j, chip in enumerate(chips):
            copy(4 + j, (*chip, 1 - c), me).wait_recv()
        for cp in first + passed:
            cp.wait_send()
        mine.wait()
        acc = all_ref[pl.ds(0, m_per), :]
        for dev in range(1, N_DEV):
            acc = acc + all_ref[pl.ds(dev * m_per, m_per), :]
        out_ref[...] = jnp.sum(acc.reshape(p, 8, d), axis=1)

    vmem = pl.BlockSpec(memory_space=pltpu.VMEM)
    return pl.pallas_call(
        body, name=name, in_specs=[vmem], out_specs=vmem,
        out_shape=jax.ShapeDtypeStruct((p, d), F32),
        scratch_shapes=[pltpu.VMEM((N_DEV * m_per, d), F32), pltpu.SemaphoreType.DMA((7,)),
                        pltpu.SemaphoreType.DMA((7,)), pltpu.SemaphoreType.DMA],
        compiler_params=pltpu.CompilerParams(vmem_limit_bytes=VMEM_LIMIT),
    )(parts.reshape(m_per, d))


def _tie(v, token):
    return lax.optimization_barrier((v, token))[0]


def _empty(shape, dtype):
    return _in_hbm(lax.empty(shape, dtype))


class _GradExchange:
    def __init__(self, layer, kinds, grads, chip, core):
        self.layer, self.kinds, self.chip, self.core = layer, kinds, chip, core
        self.g4 = [g.reshape(N_CHIPS, 2, g.shape[1] // 2, g.shape[2]) for g in grads]
        self.n = len(grads)

    def start_swap(self, after):
        n_copies, self.swap_sends, self.swap_arrivals = _swap_plan(self.n)
        lands = [_empty((N_CHIPS,) + g.shape[2:], BF16) for g in self.g4]
        self.swap = split_start(f"swap_start_l{self.layer}", self.g4 + lands, n_copies, self.swap_sends, after)
        return self.swap[3]

    def swap_to_scatter(self, after):
        send, recv, bufs, _ = self.swap
        bufs = split_wait(f"swap_wait_l{self.layer}", send, recv, bufs, self.swap_sends, self.swap_arrivals, after)
        g4, lands = bufs[:self.n], bufs[self.n:]
        self.p = [add_sibling_half(f"add_l{self.layer}_{k}", g4[t], lands[t], self.core)
                  for t, (k, _) in enumerate(self.kinds)]
        n_copies, self.sc_sends, self.sc_arrivals = _scatter_plan(self.n)
        lands = [_empty(p.shape, BF16) for p in self.p]
        self.scatter = split_start(f"scatter_start_l{self.layer}", self.p + lands, n_copies, self.sc_sends)
        return self.scatter[3]

    def finish(self, f, after):
        send, recv, bufs, _ = self.scatter
        bufs = split_wait(f"scatter_wait_l{self.layer}", send, recv, bufs, self.sc_sends, self.sc_arrivals, after)
        p, lands = bufs[:self.n], bufs[self.n:]
        for t, (kind, l) in enumerate(self.kinds):
            f[kind] = sum_chips_into(f"sum_l{self.layer}_{kind}", p[t], lands[t], f[kind], l, self.chip, self.core)


def _relu2(acc):
    r = jnp.maximum(acc, 0.0)
    return acc, r * r


def _relu2_bwd(acc, u):
    return (acc * (2.0 * jnp.maximum(u.astype(F32), 0.0)),)


def _same(acc):
    return (acc,)


def kernel(x, a_w_qkv, a_w_o, a_sinks, b_w_qkv, b_w_o, norm_mix, norm_mlp, mlp_w_in, mlp_w_out, final_norm, loss_target, m_a_w_qkv, m_a_w_o, m_a_sinks, m_b_w_qkv, m_b_w_o, m_norm_mix, m_norm_mlp, m_mlp_w_in, m_mlp_w_out, m_final_norm, v_a_w_qkv, v_a_w_o, v_a_sinks, v_b_w_qkv, v_b_w_o, v_norm_mix, v_norm_mlp, v_mlp_w_in, v_mlp_w_out, v_final_norm):
    _, s, d = x.shape
    depth = norm_mix.shape[0]
    width = N_HEADS * HEAD_DIM
    core = lax.axis_index("c").astype(jnp.int32).reshape(1)
    chip = (2 * lax.axis_index("x") + lax.axis_index("y")).astype(jnp.int32).reshape(1)
    slopes = jnp.power(2.0, -8.0 * (jnp.arange(N_HEADS, dtype=F32) + 1.0) / N_HEADS)
    qkv_of = {0: ("a_qkv", a_w_qkv), 1: ("b_qkv", b_w_qkv)}
    o_of = {0: ("a_o", a_w_o), 1: ("b_o", b_w_o)}

    def layer_kinds(i):
        return [(qkv_of[i % 2][0], i // 2), (o_of[i % 2][0], i // 2), ("mlp_in", i), ("mlp_out", i)]

    shards = [[qkv_of[i % 2][1][i // 2], o_of[i % 2][1][i // 2], mlp_w_in[i], mlp_w_out[i]] for i in range(depth)]
    slots = [[cast_into_slot(f"cast_l{i}_{k}", w, chip) for w, (k, _) in zip(shards[i], layer_kinds(i))]
             for i in range(depth)]
    n_gather, gather_sends, gather_arrivals = _gather_plan(4)

    saved, weights = [], []
    xc = x[0]
    flight = split_start("gather_start_l0", slots[0], n_gather, gather_sends)
    for i in range(depth):
        mixer, j = i % 2, i // 2
        got = split_wait(f"gather_wait_l{i}", flight[0], flight[1], flight[2], gather_sends, gather_arrivals, (xc,))
        if i + 1 < depth:
            flight = split_start(f"gather_start_l{i + 1}", slots[i + 1], n_gather, gather_sends, (got[0],))
            xc = _tie(xc, flight[3])
        w_qkv, w_o, w_in, w_out = got[0], got[1].reshape(width, d), got[2], got[3].reshape(-1, d)
        weights.append((w_qkv, w_o, w_in, w_out))
        h = rms_fwd(f"l{i}_norm_mix", xc, norm_mix[i])
        qkv = mm_cols(f"l{i}_qkv", h, w_qkv, _same, (BF16,))[0]
        if mixer == 0:
            attn, attn_f32 = swa_fwd(f"l{i}_swa", qkv, a_sinks[j], slopes), None
        else:
            attn, attn_f32 = sb_fwd(f"l{i}_sb", qkv)
        xm = mm_res(f"l{i}_o", attn, w_o, xc)
        h2 = rms_fwd(f"l{i}_norm_mlp", xm, norm_mlp[i])
        u, hh = mm_cols(f"l{i}_in", h2, w_in, _relu2, (BF16, BF16))
        xn = mm_res(f"l{i}_out", hh, w_out, xm)
        saved.append((xc, h, qkv, attn, attn_f32, xm, h2, u, hh))
        xc = xn
    loss_rows, dx, dxb, dg_final = loss_head("loss_head", xc, final_norm, loss_target[0])
    loss = lax.psum(jnp.sum(loss_rows), ("x", "y", "c"))

    big = {"a_qkv": (a_w_qkv, m_a_w_qkv, v_a_w_qkv), "a_o": (a_w_o, m_a_w_o, v_a_w_o),
           "b_qkv": (b_w_qkv, m_b_w_qkv, v_b_w_qkv), "b_o": (b_w_o, m_b_w_o, v_b_w_o),
           "mlp_in": (mlp_w_in, m_mlp_w_in, v_mlp_w_in), "mlp_out": (mlp_w_out, m_mlp_w_out, v_mlp_w_out)}
    f = {k: _empty((w.shape[0], 2, w.shape[1] // 2, w.shape[2]), F32) for k, (w, _, _) in big.items()}
    dg_mix, dg_mlp, dsinks = [], [], []
    prev = None
    for i in reversed(range(depth)):
        mixer, j = i % 2, i // 2
        xin, h, qkv, attn, attn_f32, xm, h2, u, hh = saved[i]
        w_qkv, w_o, w_in, w_out = weights[i]
        du = mm_nt(f"l{i}_d_hidden", dxb, w_out, _relu2_bwd, (u,))
        g_out = mm_tn(f"l{i}_g_out", hh, dxb).reshape(N_CHIPS, -1, d)
        if prev is not None:
            du = _tie(du, prev.swap_to_scatter((g_out,)))
        dh2 = mm_nt_cols(f"l{i}_d_h2", du, w_in)
        g_in = mm_tn_cols(f"l{i}_g_in", h2, du)
        dxm, dxmb, dg = rms_bwd(f"l{i}_norm_mlp_bwd", xm, norm_mlp[i], dh2, dx)
        dg_mlp.append(dg)
        dattn = mm_nt(f"l{i}_d_attn", dxmb, w_o, _same)
        g_o = mm_tn(f"l{i}_g_o", attn, dxmb).reshape(N_CHIPS, -1, d)
        if mixer == 0:
            dq, dk, dv, dsk = swa_bwd(f"l{i}_swa_bwd", qkv, dattn, a_sinks[j], slopes)
            dsinks.append(dsk[:, :2, 0].reshape(N_HEADS))
        else:
            dq, dk, dv = sb_bwd(f"l{i}_sb_bwd", qkv, attn_f32, dattn)
        dqkv = jnp.concatenate([dq, dk.astype(BF16), dv.astype(BF16)], axis=1)
        dh = mm_nt_cols(f"l{i}_d_h", dqkv, w_qkv)
        g_qkv = mm_tn_cols(f"l{i}_g_qkv", h, dqkv)
        dx, dxb, dg = rms_bwd(f"l{i}_norm_mix_bwd", xin, norm_mix[i], dh, dxm)
        dg_mix.append(dg)
        if prev is not None:
            prev.finish(f, (dx,))
        prev = _GradExchange(i, layer_kinds(i), [g_qkv, g_o, g_in, g_out], chip, core)
        dxb = _tie(dxb, prev.start_swap((dx,)))
    prev.swap_to_scatter((dx,))
    prev.finish(f, (dx,))
    for lst in (dg_mix, dg_mlp, dsinks):
        lst.reverse()

    res = {}
    for kind, (w, m, v) in big.items():
        g = join_halves(f"join_{kind}", f[kind]).reshape(w.shape)
        res[kind] = (g, *adamw(f"adamw_{kind}", w, g, m, v))

    n_sink = a_sinks.size
    sink_rows = jnp.zeros((1, 8, d), F32).at[0, 0, :n_sink].set(jnp.concatenate(dsinks))
    parts = jnp.concatenate([jnp.stack(dg_mix), jnp.stack(dg_mlp), dg_final[None], sink_rows], axis=0)
    g_small = allreduce_small("allreduce_small", parts)

    def pack(mix, mlp, fin, snk):
        snk_row = jnp.zeros((1, d), F32).at[0, :n_sink].set(snk.reshape(-1))
        return jnp.concatenate([mix, mlp, fin[None], snk_row], axis=0)

    def unpack(t):
        return t[:depth], t[depth:2 * depth], t[2 * depth], t[2 * depth + 1, :n_sink].reshape(a_sinks.shape)

    small = adamw("adamw_small", pack(norm_mix, norm_mlp, final_norm, a_sinks), g_small,
                  pack(m_norm_mix, m_norm_mlp, m_final_norm, m_a_sinks),
                  pack(v_norm_mix, v_norm_mlp, v_final_norm, v_a_sinks))
    outs = []
    for idx in range(4):
        mix, mlp, fin, snk = unpack(g_small if idx == 0 else small[idx - 1])
        outs += [res["a_qkv"][idx], res["a_o"][idx], snk, res["b_qkv"][idx], res["b_o"][idx], mix, mlp,
                 res["mlp_in"][idx], res["mlp_out"][idx], fin]
    return (loss, dx.reshape(x.shape), *outs)
```

```python
import functools
import math

import jax
import jax.numpy as jnp
from jax import lax
from jax.experimental import pallas as pl
from jax.experimental.pallas import tpu as pltpu

F32 = jnp.float32
BF16 = jnp.bfloat16
MESH = pl.DeviceIdType.MESH

N_CHIPS = 4
HEAD_DIM = 64
LANES = 128
N_HEADS = 32
N_KV_A = 4
WINDOW = 128
RMS_EPS = 1e-5
ATTN_SCALE = 1.0 / math.sqrt(HEAD_DIM)
ADAM_LR, ADAM_B1, ADAM_B2, ADAM_EPS, ADAM_WD, ADAM_STEP = 0.001, 0.9, 0.999, 1e-08, 0.01, 10
NEG_BIG = -1e30
VMEM_LIMIT = 56 * 1024 * 1024

_DN = {"nn": (((1,), (0,)), ((), ())), "nt": (((1,), (1,)), ((), ())), "tn": (((0,), (0,)), ((), ()))}


def _dot(a, b, mode="nn"):
    return lax.dot_general(a, b, _DN[mode], preferred_element_type=F32)


def _params(*sem):
    return pltpu.CompilerParams(dimension_semantics=sem, vmem_limit_bytes=VMEM_LIMIT)


def _pick(n, prefs):
    for t in prefs:
        if n % t == 0:
            return t
    return n


def _mm(name, mode, a, b, *, grid, a_spec, b_spec, extras=(), extra_specs=(), out_shapes, out_specs, nk,
        acc_shape, epilogue, deps=()):
    n_ex, n_out = len(extras), len(out_shapes)
    first_out = 2 + n_ex + len(deps)

    def body(*refs):
        a_ref, b_ref = refs[0], refs[1]
        ex = refs[2:2 + n_ex]
        outs = refs[first_out:first_out + n_out]
        part = _dot(a_ref[...], b_ref[...], mode)

        def finish(acc):
            res = epilogue(acc, *[e[...] for e in ex])
            for o, r in zip(outs, res):
                o[...] = r.astype(o.dtype)

        if nk == 1:
            finish(part)
        else:
            acc_ref = refs[-1]
            k = pl.program_id(len(grid) - 1)

            @pl.when(k == 0)
            def _():
                acc_ref[...] = part

            @pl.when(k > 0)
            def _():
                acc_ref[...] += part

            @pl.when(k == nk - 1)
            def _():
                finish(acc_ref[...])

    sem = ("parallel",) * (len(grid) - 1) + ("arbitrary" if nk > 1 else "parallel",)
    return pl.pallas_call(
        body, name=name, grid=grid,
        in_specs=[a_spec, b_spec, *extra_specs, *[ANY] * len(deps)],
        out_specs=list(out_specs), out_shape=list(out_shapes),
        scratch_shapes=[] if nk == 1 else [pltpu.VMEM(acc_shape, F32)],
        compiler_params=_params(*sem),
    )(a, b, *extras, *deps)


def mm_cols(name, a, wg, epilogue, out_dtypes, tm=1024):
    m, k = a.shape
    c = wg.shape[2]
    tm = min(tm, m)
    tn = _pick(c, (512, 640, 256, 128))
    nj = c // tn
    o_spec = pl.BlockSpec((tm, tn), lambda i, j: (i, j))
    return _mm(name, "nn", a, wg, grid=(m // tm, N_CHIPS * nj),
               a_spec=pl.BlockSpec((tm, k), lambda i, j: (i, 0)),
               b_spec=pl.BlockSpec((None, k, tn), lambda i, j: (j // nj, 0, j % nj)),
               out_shapes=[jax.ShapeDtypeStruct((m, N_CHIPS * c), d) for d in out_dtypes],
               out_specs=[o_spec] * len(out_dtypes), nk=1, acc_shape=None, epilogue=epilogue)


def mm_res(name, a, w, res, tm=1024, tn=512, tk=2048):
    m, k = a.shape
    n = w.shape[1]
    tm, tn, tk = min(tm, m), min(tn, n), min(tk, k)
    nk = k // tk
    return _mm(name, "nn", a, w, grid=(m // tm, n // tn, nk),
               a_spec=pl.BlockSpec((tm, tk), lambda i, j, kk: (i, kk)),
               b_spec=pl.BlockSpec((tk, tn), lambda i, j, kk: (kk, j)),
               extras=(res,), extra_specs=(pl.BlockSpec((tm, tn), lambda i, j, kk: (i, j)),),
               out_shapes=[jax.ShapeDtypeStruct((m, n), F32)],
               out_specs=[pl.BlockSpec((tm, tn), lambda i, j, kk: (i, j))], nk=nk, acc_shape=(tm, tn),
               epilogue=lambda acc, r: (acc + r,))[0]


def mm_nt(name, a, w, epilogue, extras=(), deps=(), tm=1024, tn=512):
    m, k = a.shape
    n = w.shape[0]
    tm, tn = min(tm, m), min(tn, n)
    o_spec = pl.BlockSpec((tm, tn), lambda i, j: (i, j))
    return _mm(name, "nt", a, w, grid=(m // tm, n // tn),
               a_spec=pl.BlockSpec((tm, k), lambda i, j: (i, 0)),
               b_spec=pl.BlockSpec((tn, k), lambda i, j: (j, 0)),
               extras=tuple(extras), extra_specs=(o_spec,) * len(extras),
               out_shapes=[jax.ShapeDtypeStruct((m, n), BF16)], out_specs=[o_spec], nk=1, acc_shape=None,
               epilogue=epilogue, deps=deps)[0]


def mm_nt_cols(name, dy, wg, deps=(), tm=1024, tn=512):
    m = dy.shape[0]
    _, d, c = wg.shape
    tm, tn = min(tm, m), min(tn, d)
    tk = _pick(c, (2048, 1536, 640, 512, 128))
    nkk = c // tk
    nk = N_CHIPS * nkk
    return _mm(name, "nt", dy, wg, grid=(m // tm, d // tn, nk),
               a_spec=pl.BlockSpec((tm, tk), lambda i, j, kk: (i, kk)),
               b_spec=pl.BlockSpec((None, tn, tk), lambda i, j, kk: (kk // nkk, j, kk % nkk)),
               out_shapes=[jax.ShapeDtypeStruct((m, d), F32)],
               out_specs=[pl.BlockSpec((tm, tn), lambda i, j, kk: (i, j))], nk=nk, acc_shape=(tm, tn),
               epilogue=lambda acc: (acc,), deps=deps)[0]


def mm_tn(name, a, b, tm=1024, tn=1024, tk=1024):
    m, p = a.shape
    q = b.shape[1]
    tm, tn, tk = min(tm, p), min(tn, q), min(tk, m)
    nk = m // tk
    return _mm(name, "tn", a, b, grid=(p // tm, q // tn, nk),
               a_spec=pl.BlockSpec((tk, tm), lambda i, j, kk: (kk, i)),
               b_spec=pl.BlockSpec((tk, tn), lambda i, j, kk: (kk, j)),
               out_shapes=[jax.ShapeDtypeStruct((p, q), BF16)],
               out_specs=[pl.BlockSpec((tm, tn), lambda i, j, kk: (i, j))], nk=nk, acc_shape=(tm, tn),
               epilogue=lambda acc: (acc,))[0]


def mm_tn_cols(name, a, dy, tm=1024, tk=1024):
    m, d = a.shape
    c = dy.shape[1] // N_CHIPS
    tm, tk = min(tm, d), min(tk, m)
    tn = _pick(c, (1024, 768, 640, 512, 128))
    nj = c // tn
    nk = m // tk
    return _mm(name, "tn", a, dy, grid=(d // tm, N_CHIPS * nj, nk),
               a_spec=pl.BlockSpec((tk, tm), lambda i, j, kk: (kk, i)),
               b_spec=pl.BlockSpec((tk, tn), lambda i, j, kk: (kk, j)),
               out_shapes=[jax.ShapeDtypeStruct((N_CHIPS, d, c), BF16)],
               out_specs=[pl.BlockSpec((None, tm, tn), lambda i, j, kk: (j // nj, i, j % nj))], nk=nk,
               acc_shape=(tm, tn), epilogue=lambda acc: (acc,))[0]


def _rows_to_8(v):
    tm, d = v.shape
    return jnp.sum(v.reshape(tm // 8, 8, d), axis=0)


def rms_fwd(name, x, gain, deps=(), tm=512):
    s, d = x.shape
    tm = min(tm, s)

    def body(x_ref, g_ref, *rest):
        h_ref = rest[-1]
        xv = x_ref[...]
        r = lax.rsqrt(jnp.mean(xv * xv, axis=-1, keepdims=True) + RMS_EPS)
        h_ref[...] = (xv * r * g_ref[...]).astype(BF16)

    row = pl.BlockSpec((tm, d), lambda i: (i, 0))
    return pl.pallas_call(
        body, name=name, grid=(s // tm,),
        in_specs=[row, pl.BlockSpec((1, d), lambda i: (0, 0)), *[ANY] * len(deps)], out_specs=row,
        out_shape=jax.ShapeDtypeStruct((s, d), BF16), compiler_params=_params("parallel"),
    )(x, gain.reshape(1, d), *deps)


def rms_bwd(name, x, gain, dh, dres, deps=(), tm=512):
    s, d = x.shape
    tm = min(tm, s)

    def body(x_ref, g_ref, dh_ref, dres_ref, *rest):
        dx_ref, dxb_ref, dg_ref = rest[-3:]
        xv = x_ref[...]
        r = lax.rsqrt(jnp.mean(xv * xv, axis=-1, keepdims=True) + RMS_EPS)
        xhat = xv * r
        dhv = dh_ref[...]
        dxhat = dhv * g_ref[...]
        dx = dres_ref[...] + r * (dxhat - xhat * jnp.mean(dxhat * xhat, axis=-1, keepdims=True))
        dx_ref[...] = dx
        dxb_ref[...] = dx.astype(BF16)

        @pl.when(pl.program_id(0) == 0)
        def _():
            dg_ref[...] = jnp.zeros_like(dg_ref)

        dg_ref[...] += _rows_to_8(dhv * xhat)

    row = pl.BlockSpec((tm, d), lambda i: (i, 0))
    return pl.pallas_call(
        body, name=name, grid=(s // tm,),
        in_specs=[row, pl.BlockSpec((1, d), lambda i: (0, 0)), row, row, *[ANY] * len(deps)],
        out_specs=[row, row, pl.BlockSpec((8, d), lambda i: (0, 0))],
        out_shape=[jax.ShapeDtypeStruct((s, d), F32), jax.ShapeDtypeStruct((s, d), BF16),
                   jax.ShapeDtypeStruct((8, d), F32)],
        compiler_params=_params("arbitrary"),
    )(x, gain.reshape(1, d), dh, dres, *deps)


def loss_head(name, x, gain, target, tm=512):
    s, d = x.shape
    tm = min(tm, s)

    def body(x_ref, g_ref, t_ref, loss_ref, dx_ref, dxb_ref, dg_ref):
        xv = x_ref[...]
        g = g_ref[...]
        r = lax.rsqrt(jnp.mean(xv * xv, axis=-1, keepdims=True) + RMS_EPS)
        xhat = xv * r
        err = xhat * g - t_ref[...]
        dy = err * (1.0 / d)
        dxhat = dy * g
        dx = r * (dxhat - xhat * jnp.mean(dxhat * xhat, axis=-1, keepdims=True))
        dx_ref[...] = dx
        dxb_ref[...] = dx.astype(BF16)

        @pl.when(pl.program_id(0) == 0)
        def _():
            dg_ref[...] = jnp.zeros_like(dg_ref)
            loss_ref[...] = jnp.zeros_like(loss_ref)

        dg_ref[...] += _rows_to_8(dy * xhat)
        loss_ref[...] += _rows_to_8(err * err) * (0.5 / d)

    row = pl.BlockSpec((tm, d), lambda i: (i, 0))
    vec = pl.BlockSpec((8, d), lambda i: (0, 0))
    return pl.pallas_call(
        body, name=name, grid=(s // tm,),
        in_specs=[row, pl.BlockSpec((1, d), lambda i: (0, 0)), row],
        out_specs=[vec, row, row, vec],
        out_shape=[jax.ShapeDtypeStruct((8, d), F32), jax.ShapeDtypeStruct((s, d), F32),
                   jax.ShapeDtypeStruct((s, d), BF16), jax.ShapeDtypeStruct((8, d), F32)],
        compiler_params=_params("arbitrary"),
    )(x, gain.reshape(1, d), target)


def _half_masks(dtype):
    lane = lax.broadcasted_iota(jnp.int32, (1, LANES), 1)
    lo = (lane < HEAD_DIM).astype(dtype)
    return lo, (1 - lo).astype(dtype)


def _swap_halves(v):
    return pltpu.roll(v, HEAD_DIM, axis=1)


def _swa_tile(q, k2, v2, sink, slope, n, hq_mask_b):
    t = q.shape[0]
    s = _dot(q * hq_mask_b, k2, "nt")
    row = lax.broadcasted_iota(jnp.int32, (t, 2 * WINDOW), 0)
    col = lax.broadcasted_iota(jnp.int32, (t, 2 * WINDOW), 1)
    dist = row + WINDOW - col
    valid = (dist >= 0) & (dist < WINDOW) & ((col >= WINDOW) | (n > 0))
    s = jnp.where(valid, s - slope * dist.astype(F32), NEG_BIG)
    m = jnp.maximum(jnp.max(s, axis=-1, keepdims=True), sink)
    e = jnp.exp(s - m)
    e_sink = jnp.exp(sink - m)
    inv = 1.0 / (jnp.sum(e, axis=-1, keepdims=True) + e_sink)
    return e * inv, e_sink * inv


def _swa_kv(kp_ref, kc_ref, vp_ref, vc_ref, kv_half):
    lo_b, hi_b = _half_masks(F32)
    sel = jnp.where(kv_half == 0, lo_b, hi_b)
    k2 = jnp.concatenate([kp_ref[...], kc_ref[...]], axis=0).astype(F32) * sel
    v2 = jnp.concatenate([vp_ref[...], vc_ref[...]], axis=0).astype(F32) * sel
    k2 = (k2 + _swap_halves(k2)).astype(BF16)
    v2 = (v2 + _swap_halves(v2)).astype(BF16)
    return k2, v2, sel


def _swa_specs(t):
    q_spec = pl.BlockSpec((t, LANES), lambda j, n, *_: (n, j))
    q_blocks = N_HEADS // 2

    def kv(off, prev):
        if prev:
            return pl.BlockSpec((t, LANES), lambda j, n, *_: (jnp.maximum(n - 1, 0), q_blocks + off + j // 8))
        return pl.BlockSpec((t, LANES), lambda j, n, *_: (n, q_blocks + off + j // 8))

    kv_blocks = N_KV_A // 2
    return q_spec, [kv(0, True), kv(0, False), kv(kv_blocks, True), kv(kv_blocks, False)]


def swa_fwd(name, qkv, sinks, slopes):
    s = qkv.shape[0]
    t = WINDOW

    def body(sink_ref, slope_ref, q_ref, kp_ref, kc_ref, vp_ref, vc_ref, o_ref):
        j, n = pl.program_id(0), pl.program_id(1)
        k2, v2, _ = _swa_kv(kp_ref, kc_ref, vp_ref, vc_ref, (j // 4) % 2)
        q = q_ref[...] * ATTN_SCALE
        masks = _half_masks(BF16)
        acc = jnp.zeros((t, LANES), F32)
        for hq in range(2):
            p, _ = _swa_tile(q, k2, v2, sink_ref[2 * j + hq], slope_ref[2 * j + hq], n, masks[hq])
            acc += _dot(p.astype(BF16), v2 * masks[hq])
        o_ref[...] = acc.astype(BF16)

    q_spec, kv_specs = _swa_specs(t)
    return pl.pallas_call(
        body, name=name,
        grid_spec=pltpu.PrefetchScalarGridSpec(
            num_scalar_prefetch=2, grid=(N_HEADS // 2, s // t),
            in_specs=[q_spec, *kv_specs], out_specs=q_spec),
        out_shape=jax.ShapeDtypeStruct((s, N_HEADS * HEAD_DIM), BF16),
        compiler_params=_params("parallel", "parallel"),
    )(sinks, slopes, qkv, qkv, qkv, qkv, qkv)


def swa_bwd(name, qkv, do, sinks, slopes):
    s = qkv.shape[0]
    t = WINDOW

    def body(sink_ref, slope_ref, q_ref, kp_ref, kc_ref, vp_ref, vc_ref, do_ref, dq_ref, dk_ref, dv_ref, ds_ref):
        j, n = pl.program_id(0), pl.program_id(1)
        k2, v2, sel = _swa_kv(kp_ref, kc_ref, vp_ref, vc_ref, (j // 4) % 2)
        q = q_ref[...] * ATTN_SCALE
        do_v = do_ref[...]
        masks = _half_masks(BF16)

        @pl.when((j % 8 == 0) & (n == 0))
        def _():
            dk_ref[...] = jnp.zeros_like(dk_ref)
            dv_ref[...] = jnp.zeros_like(dv_ref)

        @pl.when(n == 0)
        def _():
            ds_ref[...] = jnp.zeros_like(ds_ref)

        dq = jnp.zeros((t, LANES), F32)
        dk = jnp.zeros((2 * t, LANES), F32)
        dv = jnp.zeros((2 * t, LANES), F32)
        for hq in range(2):
            sink = sink_ref[2 * j + hq]
            p, p_sink = _swa_tile(q, k2, v2, sink, slope_ref[2 * j + hq], n, masks[hq])
            qm = q * masks[hq]
            dom = do_v * masks[hq]
            dp = _dot(dom, v2, "nt")
            delta = jnp.sum(p * dp, axis=-1, keepdims=True)
            dsc = (p * (dp - delta)).astype(BF16)
            dq += _dot(dsc, k2 * masks[hq])
            dk += _dot(dsc, qm, "tn")
            dv += _dot(p.astype(BF16), dom, "tn")
            ds_ref[hq:hq + 1, :] += jnp.zeros((1, LANES), F32) - jnp.sum(p_sink * delta)
        dq_ref[...] = (dq * ATTN_SCALE).astype(BF16)
        dk = (dk + _swap_halves(dk)) * sel
        dv = (dv + _swap_halves(dv)) * sel

        @pl.when(n == 0)
        def _():
            dk_ref[pl.ds(0, t), :] += dk[t:]
            dv_ref[pl.ds(0, t), :] += dv[t:]

        @pl.when(n > 0)
        def _():
            start = pl.multiple_of((n - 1) * t, t)
            dk_ref[pl.ds(start, 2 * t), :] += dk
            dv_ref[pl.ds(start, 2 * t), :] += dv

    q_spec, kv_specs = _swa_specs(t)
    kv_out = pl.BlockSpec((s, LANES), lambda j, n, *_: (0, j // 8))
    return pl.pallas_call(
        body, name=name,
        grid_spec=pltpu.PrefetchScalarGridSpec(
            num_scalar_prefetch=2, grid=(N_HEADS // 2, s // t),
            in_specs=[q_spec, *kv_specs, q_spec],
            out_specs=[q_spec, kv_out, kv_out, pl.BlockSpec((None, 8, LANES), lambda j, n, *_: (j, 0, 0))]),
        out_shape=[jax.ShapeDtypeStruct((s, N_HEADS * HEAD_DIM), BF16),
                   jax.ShapeDtypeStruct((s, N_KV_A * HEAD_DIM), F32),
                   jax.ShapeDtypeStruct((s, N_KV_A * HEAD_DIM), F32),
                   jax.ShapeDtypeStruct((N_HEADS // 2, 8, LANES), F32)],
        compiler_params=_params("arbitrary", "arbitrary"),
    )(sinks, slopes, qkv, qkv, qkv, qkv, qkv, do)


SB_TILE = 256


LOG2E = math.log2(math.e)


def _split_k(v):
    hi = v.astype(BF16)
    lo = (v - hi.astype(F32)).astype(BF16)
    return jnp.concatenate([hi, lo], axis=1)


def _tri2(t, inclusive):
    r = lax.broadcasted_iota(jnp.int32, (2 * t, t), 0)
    c = lax.broadcasted_iota(jnp.int32, (2 * t, t), 1)
    r = jnp.where(r >= t, r - t, r)
    return ((r >= c) if inclusive else (r > c)).astype(BF16)


def _sb_logs(z, before):
    z2 = z * LOG2E
    l2 = jnp.log2(1.0 + jnp.exp2(-jnp.abs(z2)))
    lb = jnp.minimum(z2, 0.0) - l2
    lm = lb - z2
    if before is not None:
        lm = jnp.where(before, lm, 0.0)
    return lb, lm


def _sb_weights(lb, sfx, c_lm, before):
    a = jnp.exp2(lb + sfx + c_lm)
    if before is not None:
        a = jnp.where(before, a, 0.0)
    return a


def _sb_sweep(tile, i, init, diag):
    carry = lax.cond(i > 0, lambda: tile([i, i - 1], init, diag), lambda: tile([i], init, diag))
    rest = jnp.maximum(i - 1, 0)
    odd = rest % 2
    carry = lax.cond(odd == 1, lambda: tile([i - 2], carry, None), lambda: carry)
    base = i - 2 - odd
    return lax.fori_loop(0, rest // 2, lambda n, cr: tile([base - 2 * n, base - 2 * n - 1], cr, None), carry)


def _sb_specs(s, t):
    hp = N_HEADS // 2
    q_spec = pl.BlockSpec((t, LANES), lambda h, i: (i, h))
    k_spec = pl.BlockSpec((s, LANES), lambda h, i: (0, hp + h))
    v_spec = pl.BlockSpec((s, LANES), lambda h, i: (0, 2 * hp + h))
    return q_spec, k_spec, v_spec


def sb_fwd(name, qkv, t=SB_TILE):
    s = qkv.shape[0]
    t = min(t, s)

    def body(q_ref, k_ref, v_ref, ob_ref, of_ref):
        i = pl.program_id(1)
        masks = _half_masks(BF16)
        q = q_ref[...] * ATTN_SCALE
        qm = [q * masks[0], q * masks[1]]
        tri_x = _tri2(t, False)
        r = lax.broadcasted_iota(jnp.int32, (t, t), 0)
        c = lax.broadcasted_iota(jnp.int32, (t, t), 1)
        diag = c < r

        def tile(js, carry, before):
            c0, c1, acc = carry
            cs = [c0, c1]
            kj = [k_ref[pl.ds(pl.multiple_of(j * t, t), t), :] for j in js]
            vj = [v_ref[pl.ds(pl.multiple_of(j * t, t), t), :] for j in js]
            chains = [(h, b) for b in range(len(js)) for h in range(2)]
            z = {(h, b): _dot(qm[h], kj[b], "nt") for h, b in chains}
            lb, sfx, c_at = {}, {}, {}
            for ch in chains:
                h, b = ch
                lb[ch], lm = _sb_logs(z[ch], before if b == 0 else None)
                sfx[ch] = _dot(_split_k(lm), tri_x)
                c_at[ch] = cs[h]
                cs[h] = cs[h] + jnp.sum(lm, axis=-1, keepdims=True)
            for ch in chains:
                h, b = ch
                a = _sb_weights(lb[ch], sfx[ch], c_at[ch], before if b == 0 else None)
                acc = acc + _dot(a.astype(BF16), vj[b] * masks[h])
            return cs[0], cs[1], acc

        zero = jnp.zeros((t, 1), F32)
        carry = _sb_sweep(tile, i, (zero, zero, jnp.zeros((t, LANES), F32)), diag)
        ob_ref[...] = carry[2].astype(BF16)
        of_ref[...] = carry[2]

    q_spec, k_spec, v_spec = _sb_specs(s, t)
    return pl.pallas_call(
        body, name=name, grid=(N_HEADS // 2, s // t),
        in_specs=[q_spec, k_spec, v_spec], out_specs=[q_spec, q_spec],
        out_shape=[jax.ShapeDtypeStruct((s, N_HEADS * HEAD_DIM), BF16),
                   jax.ShapeDtypeStruct((s, N_HEADS * HEAD_DIM), F32)],
        compiler_params=_params("parallel", "parallel"),
    )(qkv, qkv, qkv)


def sb_bwd(name, qkv, o_f32, do, t=SB_TILE):
    s = qkv.shape[0]
    t = min(t, s)

    def body(q_ref, k_ref, v_ref, o_ref, do_ref, dq_ref, dk_ref, dv_ref):
        i = pl.program_id(1)
        masks = _half_masks(BF16)
        fmasks = _half_masks(F32)
        q = q_ref[...] * ATTN_SCALE
        do_v = do_ref[...]
        qm = [q * masks[0], q * masks[1]]
        dom = [do_v * masks[0], do_v * masks[1]]
        prod = do_v.astype(F32) * o_ref[...]
        delta = [jnp.sum(prod * fmasks[h], axis=-1, keepdims=True) for h in range(2)]
        tri_x = _tri2(t, False)
        tri_i = _tri2(t, True)
        r = lax.broadcasted_iota(jnp.int32, (t, t), 0)
        c = lax.broadcasted_iota(jnp.int32, (t, t), 1)
        diag = c < r

        @pl.when(i == 0)
        def _():
            dk_ref[...] = jnp.zeros_like(dk_ref)
            dv_ref[...] = jnp.zeros_like(dv_ref)

        def tile(js, carry, before):
            cl0, cl1, cd0, cd1, dq = carry
            cl, cd = [cl0, cl1], [cd0, cd1]
            starts = [pl.multiple_of(j * t, t) for j in js]
            kj = [k_ref[pl.ds(st, t), :] for st in starts]
            vj = [v_ref[pl.ds(st, t), :] for st in starts]
            chains = [(h, b) for b in range(len(js)) for h in range(2)]
            z = {(h, b): _dot(qm[h], kj[b], "nt") for h, b in chains}
            da = {(h, b): _dot(dom[h], vj[b], "nt") for h, b in chains}
            lb, sfx, cl_at, cd_at, ab, de, dsfx = {}, {}, {}, {}, {}, {}, {}
            for ch in chains:
                h, b = ch
                lb[ch], lm = _sb_logs(z[ch], before if b == 0 else None)
                sfx[ch] = _dot(_split_k(lm), tri_x)
                cl_at[ch] = cl[h]
                cl[h] = cl[h] + jnp.sum(lm, axis=-1, keepdims=True)
            dv = [jnp.zeros((t, LANES), F32) for _ in js]
            dk = [jnp.zeros((t, LANES), F32) for _ in js]
            for ch in chains:
                h, b = ch
                a = _sb_weights(lb[ch], sfx[ch], cl_at[ch], before if b == 0 else None)
                ab[ch] = a.astype(BF16)
                de[ch] = da[ch] * ab[ch].astype(F32)
                dsfx[ch] = _dot(_split_k(de[ch]), tri_i)
                dv[b] = dv[b] + _dot(ab[ch], dom[h], "tn")
                cd_at[ch] = cd[h]
                cd[h] = cd[h] + jnp.sum(de[ch], axis=-1, keepdims=True)
            for ch in chains:
                h, b = ch
                farther = delta[h] - cd_at[ch] - dsfx[ch]
                sig = jnp.exp2(lb[ch])
                dz = de[ch] * (1.0 - sig) - farther * sig
                if before is not None and b == 0:
                    dz = jnp.where(before, dz, 0.0)
                dzb = dz.astype(BF16)
                dq = dq + _dot(dzb, kj[b] * masks[h])
                dk[b] = dk[b] + _dot(dzb, qm[h], "tn")
            for b, st in enumerate(starts):
                dk_ref[pl.ds(st, t), :] += dk[b]
                dv_ref[pl.ds(st, t), :] += dv[b]
            return cl[0], cl[1], cd[0], cd[1], dq

        zero = jnp.zeros((t, 1), F32)
        carry = _sb_sweep(tile, i, (zero, zero, zero, zero, jnp.zeros((t, LANES), F32)), diag)
        dq_ref[...] = (carry[4] * ATTN_SCALE).astype(BF16)

    q_spec, k_spec, v_spec = _sb_specs(s, t)
    kv_out = pl.BlockSpec((s, LANES), lambda h, i: (0, h))
    width = N_HEADS * HEAD_DIM
    return pl.pallas_call(
        body, name=name, grid=(N_HEADS // 2, s // t),
        in_specs=[q_spec, k_spec, v_spec, q_spec, q_spec], out_specs=[q_spec, kv_out, kv_out],
        out_shape=[jax.ShapeDtypeStruct((s, width), BF16), jax.ShapeDtypeStruct((s, width), F32),
                   jax.ShapeDtypeStruct((s, width), F32)],
        compiler_params=_params("parallel", "arbitrary"),
    )(qkv, qkv, qkv, o_f32, do)


def adamw(name, w, g, m, v, tm=256):
    shape = w.shape
    c = shape[-1]
    rows = math.prod(shape[:-1])
    tm = min(tm, rows)

    def body(w_ref, g_ref, m_ref, v_ref, d_ref, mo_ref, vo_ref):
        gv = g_ref[...]
        m2 = ADAM_B1 * m_ref[...] + (1.0 - ADAM_B1) * gv
        v2 = ADAM_B2 * v_ref[...] + (1.0 - ADAM_B2) * (gv * gv)
        m_hat = m2 / (1.0 - ADAM_B1 ** ADAM_STEP)
        v_hat = v2 / (1.0 - ADAM_B2 ** ADAM_STEP)
        d_ref[...] = -ADAM_LR * (m_hat / (jnp.sqrt(v_hat) + ADAM_EPS) + ADAM_WD * w_ref[...])
        mo_ref[...] = m2
        vo_ref[...] = v2

    blk = pl.BlockSpec((tm, c), lambda i: (i, 0))
    outs = pl.pallas_call(
        body, name=name, grid=(rows // tm,), in_specs=[blk] * 4, out_specs=[blk] * 3,
        out_shape=[jax.ShapeDtypeStruct((rows, c), F32)] * 3, compiler_params=_params("parallel"),
    )(*[t.reshape(rows, c) for t in (w, g, m, v)])
    return [o.reshape(shape) for o in outs]


HBM = pl.BlockSpec(memory_space=pltpu.HBM)


def _place():
    x, y, c = lax.axis_index("x"), lax.axis_index("y"), lax.axis_index("c")
    return x, y, c, [(1 - x, y), (x, 1 - y), (1 - x, 1 - y)]


def _remote(src, dst, send, recv, dev):
    return pltpu.make_async_remote_copy(src_ref=src, dst_ref=dst, send_sem=send, recv_sem=recv, device_id=dev,
                                        device_id_type=MESH)


SEM = pl.BlockSpec(memory_space=pltpu.SEMAPHORE)
ANY = pl.BlockSpec(memory_space=pl.ANY)
DATAFLOW = pltpu.SideEffectType.DATAFLOW_SIDE_EFFECTING


def _in_hbm(v):
    return pltpu.with_memory_space_constraint(v, pltpu.HBM)


def split_start(name, bufs, n_copies, sends, after=()):
    nb, na = len(bufs), len(after)

    def body(*refs):
        send, recv = refs[nb + na], refs[nb + na + 1]
        for cp in sends(refs[:nb], send, recv):
            cp.start()
        refs[-1][...] = jnp.zeros_like(refs[-1])

    outs = pl.pallas_call(
        body, name=name,
        in_specs=[HBM] * nb + [ANY] * na,
        out_shape=(pltpu.SemaphoreType.DMA((n_copies,)), pltpu.SemaphoreType.DMA((n_copies,)),
                   *[pltpu.HBM(b.shape, b.dtype) for b in bufs], jax.ShapeDtypeStruct((8, LANES), F32)),
        out_specs=(SEM, SEM, *[HBM] * nb, pl.BlockSpec(memory_space=pltpu.VMEM)),
        input_output_aliases={i: 2 + i for i in range(nb)},
        compiler_params=pltpu.CompilerParams(has_side_effects=DATAFLOW),
    )(*[_in_hbm(b) for b in bufs], *after)
    return outs[0], outs[1], list(outs[2:2 + nb]), outs[-1]


def split_wait(name, send_sems, recv_sems, bufs, sends, arrivals, after):
    nb, na = len(bufs), len(after)

    def body(*refs):
        send, recv = refs[nb], refs[nb + 1]
        for cp in sends(refs[:nb], send, recv):
            cp.wait_send()
        for cp in arrivals(refs[:nb], send, recv):
            cp.wait_recv()

    outs = pl.pallas_call(
        body, name=name,
        in_specs=[HBM] * nb + [SEM, SEM] + [ANY] * na,
        out_shape=tuple(pltpu.HBM(b.shape, b.dtype) for b in bufs), out_specs=tuple([HBM] * nb),
        input_output_aliases={i: i for i in range(nb)},
        compiler_params=pltpu.CompilerParams(has_side_effects=DATAFLOW),
    )(*bufs, send_sems, recv_sems, *after)
    return list(outs)


def _gather_plan(n):
    def sends(refs, send, recv):
        x, y, c, chips = _place()
        me = 2 * x + y
        return [_remote(refs[t].at[me], refs[t].at[me], send.at[3 * t + k], recv.at[3 * t + k], (px, py, c))
                for t in range(n) for k, (px, py) in enumerate(chips)]

    def arrivals(refs, send, recv):
        x, y, c, chips = _place()
        return [_remote(refs[t].at[2 * px + py], refs[t].at[2 * px + py], send.at[3 * t + k], recv.at[3 * t + k],
                        (px, py, c)) for t in range(n) for k, (px, py) in enumerate(chips)]

    return 3 * n, sends, arrivals


def _swap_plan(n):
    def copies(refs, send, recv):
        x, y, c, _ = _place()
        return [_remote(refs[t].at[s, 1 - c], refs[n + t].at[s], send.at[N_CHIPS * t + s], recv.at[N_CHIPS * t + s],
                        (x, y, 1 - c)) for t in range(n) for s in range(N_CHIPS)]

    return N_CHIPS * n, copies, copies


def _scatter_plan(n):
    def sends(refs, send, recv):
        x, y, c, chips = _place()
        me = 2 * x + y
        return [_remote(refs[t].at[2 * px + py], refs[n + t].at[me], send.at[3 * t + k], recv.at[3 * t + k],
                        (px, py, c)) for t in range(n) for k, (px, py) in enumerate(chips)]

    def arrivals(refs, send, recv):
        x, y, c, chips = _place()
        return [_remote(refs[t].at[2 * px + py], refs[n + t].at[2 * px + py], send.at[3 * t + k], recv.at[3 * t + k],
                        (px, py, c)) for t in range(n) for k, (px, py) in enumerate(chips)]

    return 3 * n, sends, arrivals


def cast_into_slot(name, w, chip, tm=256):
    r, c = w.shape
    tm = min(tm, r)

    def body(chip_ref, w_ref, o_ref):
        o_ref[...] = w_ref[...].astype(BF16)

    return pl.pallas_call(
        body, name=name,
        grid_spec=pltpu.PrefetchScalarGridSpec(
            num_scalar_prefetch=1, grid=(r // tm,),
            in_specs=[pl.BlockSpec((tm, c), lambda i, chip_ref: (i, 0))],
            out_specs=pl.BlockSpec((None, tm, c), lambda i, chip_ref: (chip_ref[0], i, 0))),
        out_shape=jax.ShapeDtypeStruct((N_CHIPS, r, c), BF16), compiler_params=_params("parallel"),
    )(chip, w)


def join_halves(name, f):
    n = f.shape[0]

    def body(f_ref, o_ref, send, recv):
        x, y, c, _ = _place()
        sib = (x, y, 1 - c)
        sends = [_remote(f_ref.at[l, c], o_ref.at[l, c], send.at[l], recv.at[l], sib) for l in range(n)]
        for cp in sends:
            cp.start()
        for l in range(n):
            _remote(f_ref.at[l, 1 - c], o_ref.at[l, 1 - c], send.at[l], recv.at[l], sib).wait_recv()
        for cp in sends:
            cp.wait_send()

    return pl.pallas_call(
        body, name=name, in_specs=[HBM], out_specs=HBM, out_shape=jax.ShapeDtypeStruct(f.shape, f.dtype),
        input_output_aliases={0: 0},
        scratch_shapes=[pltpu.SemaphoreType.DMA((n,)), pltpu.SemaphoreType.DMA((n,))],
    )(f)


def add_sibling_half(name, g, a, core, tm=256):
    _, _, rh, c = g.shape
    tm = min(tm, rh)

    def body(core_ref, g_ref, a_ref, o_ref):
        o_ref[...] = (g_ref[...].astype(F32) + a_ref[...].astype(F32)).astype(BF16)

    return pl.pallas_call(
        body, name=name,
        grid_spec=pltpu.PrefetchScalarGridSpec(
            num_scalar_prefetch=1, grid=(N_CHIPS, rh // tm),
            in_specs=[pl.BlockSpec((None, None, tm, c), lambda s, i, core_ref: (s, core_ref[0], i, 0)),
                      pl.BlockSpec((None, tm, c), lambda s, i, core_ref: (s, i, 0))],
            out_specs=pl.BlockSpec((None, tm, c), lambda s, i, core_ref: (s, i, 0))),
        out_shape=jax.ShapeDtypeStruct(a.shape, BF16), compiler_params=_params("parallel", "parallel"),
    )(core, g, a)


def sum_chips_into(name, p, b, f, layer, chip, core, tm=256):
    _, rh, c = b.shape
    tm = min(tm, rh)

    def body(chip_ref, core_ref, p_ref, b_ref, f_ref, o_ref):
        acc = jnp.zeros((tm, c), F32)
        for s in range(N_CHIPS):
            acc = acc + jnp.where(chip_ref[0] == s, p_ref[s].astype(F32), b_ref[s].astype(F32))
        o_ref[...] = acc

    slots = pl.BlockSpec((N_CHIPS, tm, c), lambda i, chip_ref, core_ref: (0, i, 0))
    return pl.pallas_call(
        body, name=name,
        grid_spec=pltpu.PrefetchScalarGridSpec(
            num_scalar_prefetch=2, grid=(rh // tm,), in_specs=[slots, slots, ANY],
            out_specs=pl.BlockSpec((None, None, tm, c), lambda i, chip_ref, core_ref: (layer, core_ref[0], i, 0))),
        out_shape=jax.ShapeDtypeStruct(f.shape, F32), input_output_aliases={4: 0},
        compiler_params=_params("parallel"),
    )(chip, core, p, b, f)


N_DEV = 8


def allreduce_small(name, parts):
    p, _, d = parts.shape
    m_per = p * 8

    def body(x_ref, out_ref, all_ref, send_sems, recv_sems, local_sem):
        x, y, c, chips = _place()
        me, sibling = (x, y, c), (x, y, 1 - c)

        def rows(px, py, pc):
            return all_ref.at[pl.ds((4 * px + 2 * py + pc) * m_per, m_per), :]

        def copy(k, block, to, src=None):
            return _remote(rows(*block) if src is None else src, rows(*block), send_sems.at[k], recv_sems.at[k], to)

        mine = pltpu.make_async_copy(x_ref, rows(*me), local_sem)
        mine.start()
        first = [copy(0, me, sibling, src=x_ref)]
        first += [copy(1 + j, me, (*chip, c), src=x_ref) for j, chip in enumerate(chips)]
        for cp in first:
            cp.start()
        passed = [copy(4 + j, (*chip, c), sibling) for j, chip in enumerate(chips)]
        for j, chip in enumerate(chips):
            copy(1 + j, (*chip, c), me).wait_recv()
            passed[j].start()
        copy(0, sibling, me).wait_recv()
        for j, chip in enumerate(chips):
            copy(4 + j, (*chip, 1 - c), me).wait_recv()
        for cp in first + passed:
            cp.wait_send()
        mine.wait()
        acc = all_ref[pl.ds(0, m_per), :]
        for dev in range(1, N_DEV):
            acc = acc + all_ref[pl.ds(dev * m_per, m_per), :]
        out_ref[...] = jnp.sum(acc.reshape(p, 8, d), axis=1)

    vmem = pl.BlockSpec(memory_space=pltpu.VMEM)
    return pl.pallas_call(
        body, name=name, in_specs=[vmem], out_specs=vmem,
        out_shape=jax.ShapeDtypeStruct((p, d), F32),
        scratch_shapes=[pltpu.VMEM((N_DEV * m_per, d), F32), pltpu.SemaphoreType.DMA((7,)),
                        pltpu.SemaphoreType.DMA((7,)), pltpu.SemaphoreType.DMA],
        compiler_params=pltpu.CompilerParams(vmem_limit_bytes=VMEM_LIMIT),
    )(parts.reshape(m_per, d))


def _empty(shape, dtype):
    return _in_hbm(lax.empty(shape, dtype))


class _GradExchange:
    def __init__(self, layer, kinds, grads, chip, core):
        self.layer, self.kinds, self.chip, self.core = layer, kinds, chip, core
        self.g4 = [g.reshape(N_CHIPS, 2, g.shape[1] // 2, g.shape[2]) for g in grads]
        self.n = len(grads)

    def start_swap(self, after):
        n_copies, self.swap_sends, self.swap_arrivals = _swap_plan(self.n)
        lands = [_empty((N_CHIPS,) + g.shape[2:], BF16) for g in self.g4]
        self.swap = split_start(f"swap_start_l{self.layer}", self.g4 + lands, n_copies, self.swap_sends, after)
        return self.swap[3]

    def swap_to_scatter(self, after):
        send, recv, bufs, _ = self.swap
        bufs = split_wait(f"swap_wait_l{self.layer}", send, recv, bufs, self.swap_sends, self.swap_arrivals, after)
        g4, lands = bufs[:self.n], bufs[self.n:]
        self.p = [add_sibling_half(f"add_l{self.layer}_{k}", g4[t], lands[t], self.core)
                  for t, (k, _) in enumerate(self.kinds)]
        n_copies, self.sc_sends, self.sc_arrivals = _scatter_plan(self.n)
        lands = [_empty(p.shape, BF16) for p in self.p]
        self.scatter = split_start(f"scatter_start_l{self.layer}", self.p + lands, n_copies, self.sc_sends)
        return self.scatter[3]

    def finish(self, f, after):
        send, recv, bufs, _ = self.scatter
        bufs = split_wait(f"scatter_wait_l{self.layer}", send, recv, bufs, self.sc_sends, self.sc_arrivals, after)
        p, lands = bufs[:self.n], bufs[self.n:]
        for t, (kind, l) in enumerate(self.kinds):
            f[kind] = sum_chips_into(f"sum_l{self.layer}_{kind}", p[t], lands[t], f[kind], l, self.chip, self.core)


def _relu2(acc):
    r = jnp.maximum(acc, 0.0)
    return acc, r * r


def _relu2_bwd(acc, u):
    return (acc * (2.0 * jnp.maximum(u.astype(F32), 0.0)),)


def _same(acc):
    return (acc,)


def kernel(x, a_w_qkv, a_w_o, a_sinks, b_w_qkv, b_w_o, norm_mix, norm_mlp, mlp_w_in, mlp_w_out, final_norm, loss_target, m_a_w_qkv, m_a_w_o, m_a_sinks, m_b_w_qkv, m_b_w_o, m_norm_mix, m_norm_mlp, m_mlp_w_in, m_mlp_w_out, m_final_norm, v_a_w_qkv, v_a_w_o, v_a_sinks, v_b_w_qkv, v_b_w_o, v_norm_mix, v_norm_mlp, v_mlp_w_in, v_mlp_w_out, v_final_norm):
    _, s, d = x.shape
    depth = norm_mix.shape[0]
    width = N_HEADS * HEAD_DIM
    core = lax.axis_index("c").astype(jnp.int32).reshape(1)
    chip = (2 * lax.axis_index("x") + lax.axis_index("y")).astype(jnp.int32).reshape(1)
    slopes = jnp.power(2.0, -8.0 * (jnp.arange(N_HEADS, dtype=F32) + 1.0) / N_HEADS)
    qkv_of = {0: ("a_qkv", a_w_qkv), 1: ("b_qkv", b_w_qkv)}
    o_of = {0: ("a_o", a_w_o), 1: ("b_o", b_w_o)}

    def layer_kinds(i):
        return [(qkv_of[i % 2][0], i // 2), (o_of[i % 2][0], i // 2), ("mlp_in", i), ("mlp_out", i)]

    shards = [[qkv_of[i % 2][1][i // 2], o_of[i % 2][1][i // 2], mlp_w_in[i], mlp_w_out[i]] for i in range(depth)]
    slots = [[cast_into_slot(f"cast_l{i}_{k}", w, chip) for w, (k, _) in zip(shards[i], layer_kinds(i))]
             for i in range(depth)]
    n_gather, gather_sends, gather_arrivals = _gather_plan(4)

    saved, weights = [], []
    xc = x[0]
    flight = split_start("gather_start_l0", slots[0], n_gather, gather_sends)
    for i in range(depth):
        mixer, j = i % 2, i // 2
        got = split_wait(f"gather_wait_l{i}", flight[0], flight[1], flight[2], gather_sends, gather_arrivals, (xc,))
        started = ()
        if i + 1 < depth:
            flight = split_start(f"gather_start_l{i + 1}", slots[i + 1], n_gather, gather_sends, (got[0],))
            started = (flight[3],)
        w_qkv, w_o, w_in, w_out = got[0], got[1].reshape(width, d), got[2], got[3].reshape(-1, d)
        weights.append((w_qkv, w_o, w_in, w_out))
        h = rms_fwd(f"l{i}_norm_mix", xc, norm_mix[i], deps=started)
        qkv = mm_cols(f"l{i}_qkv", h, w_qkv, _same, (BF16,))[0]
        if mixer == 0:
            attn, attn_f32 = swa_fwd(f"l{i}_swa", qkv, a_sinks[j], slopes), None
        else:
            attn, attn_f32 = sb_fwd(f"l{i}_sb", qkv)
        xm = mm_res(f"l{i}_o", attn, w_o, xc)
        h2 = rms_fwd(f"l{i}_norm_mlp", xm, norm_mlp[i])
        u, hh = mm_cols(f"l{i}_in", h2, w_in, _relu2, (BF16, BF16))
        xn = mm_res(f"l{i}_out", hh, w_out, xm)
        saved.append((xc, h, qkv, attn, attn_f32, xm, h2, u, hh))
        xc = xn
    loss_rows, dx, dxb, dg_final = loss_head("loss_head", xc, final_norm, loss_target[0])
    loss = lax.psum(jnp.sum(loss_rows), ("x", "y", "c"))

    big = {"a_qkv": (a_w_qkv, m_a_w_qkv, v_a_w_qkv), "a_o": (a_w_o, m_a_w_o, v_a_w_o),
           "b_qkv": (b_w_qkv, m_b_w_qkv, v_b_w_qkv), "b_o": (b_w_o, m_b_w_o, v_b_w_o),
           "mlp_in": (mlp_w_in, m_mlp_w_in, v_mlp_w_in), "mlp_out": (mlp_w_out, m_mlp_w_out, v_mlp_w_out)}
    f = {k: _empty((w.shape[0], 2, w.shape[1] // 2, w.shape[2]), F32) for k, (w, _, _) in big.items()}
    dg_mix, dg_mlp, dsinks = [], [], []
    prev, started = None, ()
    for i in reversed(range(depth)):
        mixer, j = i % 2, i // 2
        xin, h, qkv, attn, attn_f32, xm, h2, u, hh = saved[i]
        w_qkv, w_o, w_in, w_out = weights[i]
        du = mm_nt(f"l{i}_d_hidden", dxb, w_out, _relu2_bwd, (u,), deps=started)
        g_out = mm_tn(f"l{i}_g_out", hh, dxb).reshape(N_CHIPS, -1, d)
        started = (g_out,) if prev is None else (g_out, prev.swap_to_scatter((g_out,)))
        dh2 = mm_nt_cols(f"l{i}_d_h2", du, w_in, deps=started)
        g_in = mm_tn_cols(f"l{i}_g_in", h2, du)
        dxm, dxmb, dg = rms_bwd(f"l{i}_norm_mlp_bwd", xm, norm_mlp[i], dh2, dx, deps=(g_in,))
        dg_mlp.append(dg)
        dattn = mm_nt(f"l{i}_d_attn", dxmb, w_o, _same)
        g_o = mm_tn(f"l{i}_g_o", attn, dxmb).reshape(N_CHIPS, -1, d)
        if mixer == 0:
            dq, dk, dv, dsk = swa_bwd(f"l{i}_swa_bwd", qkv, dattn, a_sinks[j], slopes)
            dsinks.append(dsk[:, :2, 0].reshape(N_HEADS))
        else:
            dq, dk, dv = sb_bwd(f"l{i}_sb_bwd", qkv, attn_f32, dattn)
        dqkv = jnp.concatenate([dq, dk.astype(BF16), dv.astype(BF16)], axis=1)
        dh = mm_nt_cols(f"l{i}_d_h", dqkv, w_qkv, deps=(g_o,))
        g_qkv = mm_tn_cols(f"l{i}_g_qkv", h, dqkv)
        dx, dxb, dg = rms_bwd(f"l{i}_norm_mix_bwd", xin, norm_mix[i], dh, dxm, deps=(g_qkv,))
        dg_mix.append(dg)
        if prev is not None:
            prev.finish(f, (dx,))
        prev = _GradExchange(i, layer_kinds(i), [g_qkv, g_o, g_in, g_out], chip, core)
        started = (prev.start_swap((dx,)),)
    prev.swap_to_scatter((dx,))
    prev.finish(f, (dx,))
    for lst in (dg_mix, dg_mlp, dsinks):
        lst.reverse()

    res = {}
    for kind, (w, m, v) in big.items():
        g = join_halves(f"join_{kind}", f[kind]).reshape(w.shape)
        res[kind] = (g, *adamw(f"adamw_{kind}", w, g, m, v))

    n_sink = a_sinks.size
    sink_rows = jnp.zeros((1, 8, d), F32).at[0, 0, :n_sink].set(jnp.concatenate(dsinks))
    parts = jnp.concatenate([jnp.stack(dg_mix), jnp.stack(dg_mlp), dg_final[None], sink_rows], axis=0)
    g_small = allreduce_small("allreduce_small", parts)

    def pack(mix, mlp, fin, snk):
        snk_row = jnp.zeros((1, d), F32).at[0, :n_sink].set(snk.reshape(-1))
        return jnp.concatenate([mix, mlp, fin[None], snk_row], axis=0)

    def unpack(t):
        return t[:depth], t[depth:2 * depth], t[2 * depth], t[2 * depth + 1, :n_sink].reshape(a_sinks.shape)

    small = adamw("adamw_small", pack(norm_mix, norm_mlp, final_norm, a_sinks), g_small,
                  pack(m_norm_mix, m_norm_mlp, m_final_norm, m_a_sinks),
                  pack(v_norm_mix, v_norm_mlp, v_final_norm, v_a_sinks))
    outs = []
    for idx in range(4):
        mix, mlp, fin, snk = unpack(g_small if idx == 0 else small[idx - 1])
        outs += [res["a_qkv"][idx], res["a_o"][idx], snk, res["b_qkv"][idx], res["b_o"][idx], mix, mlp,
                 res["mlp_in"][idx], res["mlp_out"][idx], fin]
    return (loss, dx.reshape(x.shape), *outs)
```

```python
import functools
import math

import jax
import jax.numpy as jnp
from jax import lax
from jax.experimental import pallas as pl
from jax.experimental.pallas import tpu as pltpu

F32 = jnp.float32
BF16 = jnp.bfloat16
MESH = pl.DeviceIdType.MESH

N_CHIPS = 4
HEAD_DIM = 64
LANES = 128
N_HEADS = 32
N_KV_A = 4
WINDOW = 128
RMS_EPS = 1e-5
ATTN_SCALE = 1.0 / math.sqrt(HEAD_DIM)
ADAM_LR, ADAM_B1, ADAM_B2, ADAM_EPS, ADAM_WD, ADAM_STEP = 0.001, 0.9, 0.999, 1e-08, 0.01, 10
NEG_BIG = -1e30
VMEM_LIMIT = 56 * 1024 * 1024

_DN = {"nn": (((1,), (0,)), ((), ())), "nt": (((1,), (1,)), ((), ())), "tn": (((0,), (0,)), ((), ()))}


def _dot(a, b, mode="nn"):
    return lax.dot_general(a, b, _DN[mode], preferred_element_type=F32)


def _params(*sem):
    return pltpu.CompilerParams(dimension_semantics=sem, vmem_limit_bytes=VMEM_LIMIT)


def _pick(n, prefs):
    for t in prefs:
        if n % t == 0:
            return t
    return n


def _mm(name, mode, a, b, *, grid, a_spec, b_spec, extras=(), extra_specs=(), out_shapes, out_specs, nk,
        acc_shape, epilogue, deps=()):
    n_ex, n_out = len(extras), len(out_shapes)
    first_out = 2 + n_ex + len(deps)

    def body(*refs):
        a_ref, b_ref = refs[0], refs[1]
        ex = refs[2:2 + n_ex]
        outs = refs[first_out:first_out + n_out]
        part = _dot(a_ref[...], b_ref[...], mode)

        def finish(acc):
            res = epilogue(acc, *[e[...] for e in ex])
            for o, r in zip(outs, res):
                o[...] = r.astype(o.dtype)

        if nk == 1:
            finish(part)
        else:
            acc_ref = refs[-1]
            k = pl.program_id(len(grid) - 1)

            @pl.when(k == 0)
            def _():
                acc_ref[...] = part

            @pl.when(k > 0)
            def _():
                acc_ref[...] += part

            @pl.when(k == nk - 1)
            def _():
                finish(acc_ref[...])

    sem = ("parallel",) * (len(grid) - 1) + ("arbitrary" if nk > 1 else "parallel",)
    return pl.pallas_call(
        body, name=name, grid=grid,
        in_specs=[a_spec, b_spec, *extra_specs, *[ANY] * len(deps)],
        out_specs=list(out_specs), out_shape=list(out_shapes),
        scratch_shapes=[] if nk == 1 else [pltpu.VMEM(acc_shape, F32)],
        compiler_params=_params(*sem),
    )(a, b, *extras, *deps)


def mm_cols(name, a, wg, epilogue, out_dtypes, tm=2048):
    m, k = a.shape
    c = wg.shape[2]
    tm = min(tm, m)
    tn = _pick(c, (512, 640, 256, 128))
    nj = c // tn
    o_spec = pl.BlockSpec((tm, tn), lambda i, j: (i, j))
    return _mm(name, "nn", a, wg, grid=(m // tm, N_CHIPS * nj),
               a_spec=pl.BlockSpec((tm, k), lambda i, j: (i, 0)),
               b_spec=pl.BlockSpec((None, k, tn), lambda i, j: (j // nj, 0, j % nj)),
               out_shapes=[jax.ShapeDtypeStruct((m, N_CHIPS * c), d) for d in out_dtypes],
               out_specs=[o_spec] * len(out_dtypes), nk=1, acc_shape=None, epilogue=epilogue)


def mm_res(name, a, w, res, tm=1024, tn=1024, tk=2048):
    m, k = a.shape
    n = w.shape[1]
    tm, tn, tk = min(tm, m), min(tn, n), min(tk, k)
    nk = k // tk
    return _mm(name, "nn", a, w, grid=(m // tm, n // tn, nk),
               a_spec=pl.BlockSpec((tm, tk), lambda i, j, kk: (i, kk)),
               b_spec=pl.BlockSpec((tk, tn), lambda i, j, kk: (kk, j)),
               extras=(res,), extra_specs=(pl.BlockSpec((tm, tn), lambda i, j, kk: (i, j)),),
               out_shapes=[jax.ShapeDtypeStruct((m, n), F32)],
               out_specs=[pl.BlockSpec((tm, tn), lambda i, j, kk: (i, j))], nk=nk, acc_shape=(tm, tn),
               epilogue=lambda acc, r: (acc + r,))[0]


def mm_nt(name, a, w, epilogue, extras=(), deps=(), tm=2048, tn=512):
    m, k = a.shape
    n = w.shape[0]
    tm, tn = min(tm, m), min(tn, n)
    o_spec = pl.BlockSpec((tm, tn), lambda i, j: (i, j))
    return _mm(name, "nt", a, w, grid=(m // tm, n // tn),
               a_spec=pl.BlockSpec((tm, k), lambda i, j: (i, 0)),
               b_spec=pl.BlockSpec((tn, k), lambda i, j: (j, 0)),
               extras=tuple(extras), extra_specs=(o_spec,) * len(extras),
               out_shapes=[jax.ShapeDtypeStruct((m, n), BF16)], out_specs=[o_spec], nk=1, acc_shape=None,
               epilogue=epilogue, deps=deps)[0]


def mm_nt_cols(name, dy, wg, deps=(), tm=1024, tn=2048):
    m = dy.shape[0]
    _, d, c = wg.shape
    tm, tn = min(tm, m), min(tn, d)
    tk = _pick(c, (1024, 768, 640, 512, 128))
    nkk = c // tk
    nk = N_CHIPS * nkk
    return _mm(name, "nt", dy, wg, grid=(m // tm, d // tn, nk),
               a_spec=pl.BlockSpec((tm, tk), lambda i, j, kk: (i, kk)),
               b_spec=pl.BlockSpec((None, tn, tk), lambda i, j, kk: (kk // nkk, j, kk % nkk)),
               out_shapes=[jax.ShapeDtypeStruct((m, d), F32)],
               out_specs=[pl.BlockSpec((tm, tn), lambda i, j, kk: (i, j))], nk=nk, acc_shape=(tm, tn),
               epilogue=lambda acc: (acc,), deps=deps)[0]


def mm_tn(name, a, b, tm=1024, tn=2048, tk=1024):
    m, p = a.shape
    q = b.shape[1]
    tm, tn, tk = min(tm, p), min(tn, q), min(tk, m)
    nk = m // tk
    return _mm(name, "tn", a, b, grid=(p // tm, q // tn, nk),
               a_spec=pl.BlockSpec((tk, tm), lambda i, j, kk: (kk, i)),
               b_spec=pl.BlockSpec((tk, tn), lambda i, j, kk: (kk, j)),
               out_shapes=[jax.ShapeDtypeStruct((p, q), BF16)],
               out_specs=[pl.BlockSpec((tm, tn), lambda i, j, kk: (i, j))], nk=nk, acc_shape=(tm, tn),
               epilogue=lambda acc: (acc,))[0]


def mm_tn_cols(name, a, dy, tm=2048, tk=1024):
    m, d = a.shape
    c = dy.shape[1] // N_CHIPS
    tm, tk = min(tm, d), min(tk, m)
    tn = _pick(c, (1024, 768, 640, 512, 128))
    nj = c // tn
    nk = m // tk
    return _mm(name, "tn", a, dy, grid=(d // tm, N_CHIPS * nj, nk),
               a_spec=pl.BlockSpec((tk, tm), lambda i, j, kk: (kk, i)),
               b_spec=pl.BlockSpec((tk, tn), lambda i, j, kk: (kk, j)),
               out_shapes=[jax.ShapeDtypeStruct((N_CHIPS, d, c), BF16)],
               out_specs=[pl.BlockSpec((None, tm, tn), lambda i, j, kk: (j // nj, i, j % nj))], nk=nk,
               acc_shape=(tm, tn), epilogue=lambda acc: (acc,))[0]


def _rows_to_8(v):
    tm, d = v.shape
    return jnp.sum(v.reshape(tm // 8, 8, d), axis=0)


def rms_fwd(name, x, gain, deps=(), tm=512):
    s, d = x.shape
    tm = min(tm, s)

    def body(x_ref, g_ref, *rest):
        h_ref = rest[-1]
        xv = x_ref[...]
        r = lax.rsqrt(jnp.mean(xv * xv, axis=-1, keepdims=True) + RMS_EPS)
        h_ref[...] = (xv * r * g_ref[...]).astype(BF16)

    row = pl.BlockSpec((tm, d), lambda i: (i, 0))
    return pl.pallas_call(
        body, name=name, grid=(s // tm,),
        in_specs=[row, pl.BlockSpec((1, d), lambda i: (0, 0)), *[ANY] * len(deps)], out_specs=row,
        out_shape=jax.ShapeDtypeStruct((s, d), BF16), compiler_params=_params("parallel"),
    )(x, gain.reshape(1, d), *deps)


def rms_bwd(name, x, gain, dh, dres, deps=(), tm=512):
    s, d = x.shape
    tm = min(tm, s)

    def body(x_ref, g_ref, dh_ref, dres_ref, *rest):
        dx_ref, dxb_ref, dg_ref = rest[-3:]
        xv = x_ref[...]
        r = lax.rsqrt(jnp.mean(xv * xv, axis=-1, keepdims=True) + RMS_EPS)
        xhat = xv * r
        dhv = dh_ref[...]
        dxhat = dhv * g_ref[...]
        dx = dres_ref[...] + r * (dxhat - xhat * jnp.mean(dxhat * xhat, axis=-1, keepdims=True))
        dx_ref[...] = dx
        dxb_ref[...] = dx.astype(BF16)

        @pl.when(pl.program_id(0) == 0)
        def _():
            dg_ref[...] = jnp.zeros_like(dg_ref)

        dg_ref[...] += _rows_to_8(dhv * xhat)

    row = pl.BlockSpec((tm, d), lambda i: (i, 0))
    return pl.pallas_call(
        body, name=name, grid=(s // tm,),
        in_specs=[row, pl.BlockSpec((1, d), lambda i: (0, 0)), row, row, *[ANY] * len(deps)],
        out_specs=[row, row, pl.BlockSpec((8, d), lambda i: (0, 0))],
        out_shape=[jax.ShapeDtypeStruct((s, d), F32), jax.ShapeDtypeStruct((s, d), BF16),
                   jax.ShapeDtypeStruct((8, d), F32)],
        compiler_params=_params("arbitrary"),
    )(x, gain.reshape(1, d), dh, dres, *deps)


def loss_head(name, x, gain, target, tm=512):
    s, d = x.shape
    tm = min(tm, s)

    def body(x_ref, g_ref, t_ref, loss_ref, dx_ref, dxb_ref, dg_ref):
        xv = x_ref[...]
        g = g_ref[...]
        r = lax.rsqrt(jnp.mean(xv * xv, axis=-1, keepdims=True) + RMS_EPS)
        xhat = xv * r
        err = xhat * g - t_ref[...]
        dy = err * (1.0 / d)
        dxhat = dy * g
        dx = r * (dxhat - xhat * jnp.mean(dxhat * xhat, axis=-1, keepdims=True))
        dx_ref[...] = dx
        dxb_ref[...] = dx.astype(BF16)

        @pl.when(pl.program_id(0) == 0)
        def _():
            dg_ref[...] = jnp.zeros_like(dg_ref)
            loss_ref[...] = jnp.zeros_like(loss_ref)

        dg_ref[...] += _rows_to_8(dy * xhat)
        loss_ref[...] += _rows_to_8(err * err) * (0.5 / d)

    row = pl.BlockSpec((tm, d), lambda i: (i, 0))
    vec = pl.BlockSpec((8, d), lambda i: (0, 0))
    return pl.pallas_call(
        body, name=name, grid=(s // tm,),
        in_specs=[row, pl.BlockSpec((1, d), lambda i: (0, 0)), row],
        out_specs=[vec, row, row, vec],
        out_shape=[jax.ShapeDtypeStruct((8, d), F32), jax.ShapeDtypeStruct((s, d), F32),
                   jax.ShapeDtypeStruct((s, d), BF16), jax.ShapeDtypeStruct((8, d), F32)],
        compiler_params=_params("arbitrary"),
    )(x, gain.reshape(1, d), target)


def _half_masks(dtype):
    lane = lax.broadcasted_iota(jnp.int32, (1, LANES), 1)
    lo = (lane < HEAD_DIM).astype(dtype)
    return lo, (1 - lo).astype(dtype)


def _swap_halves(v):
    return pltpu.roll(v, HEAD_DIM, axis=1)


SWA_BLOCKS = 8


def _swa_probs(s_raw, sink, slope, first):
    t = s_raw.shape[0]
    row = lax.broadcasted_iota(jnp.int32, (t, 2 * WINDOW), 0)
    col = lax.broadcasted_iota(jnp.int32, (t, 2 * WINDOW), 1)
    dist = row + WINDOW - col
    valid = (dist >= 0) & (dist < WINDOW)
    if first is not None:
        valid = valid & ((col >= WINDOW) | jnp.logical_not(first))
    s = jnp.where(valid, s_raw - slope * dist.astype(F32), NEG_BIG)
    m = jnp.maximum(jnp.max(s, axis=-1, keepdims=True), sink)
    e = jnp.exp(s - m)
    e_sink = jnp.exp(sink - m)
    inv = 1.0 / (jnp.sum(e, axis=-1, keepdims=True) + e_sink)
    return e * inv, e_sink * inv


def _swa_band(kp_ref, kc_ref, vp_ref, vc_ref, kv_half):
    lo_b, hi_b = _half_masks(F32)
    sel = jnp.where(kv_half == 0, lo_b, hi_b)
    k = jnp.concatenate([kp_ref[...], kc_ref[...]], axis=0).astype(F32) * sel
    v = jnp.concatenate([vp_ref[...], vc_ref[...]], axis=0).astype(F32) * sel
    k = (k + _swap_halves(k)).astype(BF16)
    v = (v + _swap_halves(v)).astype(BF16)
    return k, v, sel


def _swa_specs(nq):
    t = WINDOW
    q_spec = pl.BlockSpec((nq * t, LANES), lambda j, n, *_: (n, j))
    q_blocks = N_HEADS // 2

    def prev(off):
        return pl.BlockSpec((t, LANES), lambda j, n, *_: (jnp.maximum(nq * n - 1, 0), q_blocks + off + j // 8))

    def cur(off):
        return pl.BlockSpec((nq * t, LANES), lambda j, n, *_: (n, q_blocks + off + j // 8))

    kv_blocks = N_KV_A // 2
    return q_spec, [prev(0), cur(0), prev(kv_blocks), cur(kv_blocks)]


def _rows(a, b, n=1):
    return a[b * WINDOW:(b + n) * WINDOW]


def swa_fwd(name, qkv, sinks, slopes):
    s = qkv.shape[0]
    nq = min(SWA_BLOCKS, s // WINDOW)

    def body(sink_ref, slope_ref, q_ref, kp_ref, kc_ref, vp_ref, vc_ref, o_ref):
        j, n = pl.program_id(0), pl.program_id(1)
        k_all, v_all, _ = _swa_band(kp_ref, kc_ref, vp_ref, vc_ref, (j // 4) % 2)
        q = q_ref[...] * ATTN_SCALE
        masks = _half_masks(BF16)
        chains = [(b, hq) for b in range(nq) for hq in range(2)]
        s_raw = {(b, hq): _dot(_rows(q, b) * masks[hq], _rows(k_all, b, 2), "nt") for b, hq in chains}
        p = {(b, hq): _swa_probs(s_raw[(b, hq)], sink_ref[2 * j + hq], slope_ref[2 * j + hq],
                                 (n == 0) if b == 0 else None)[0] for b, hq in chains}
        outs = [sum(_dot(p[(b, hq)].astype(BF16), _rows(v_all, b, 2) * masks[hq]) for hq in range(2))
                for b in range(nq)]
        o_ref[...] = jnp.concatenate(outs, axis=0).astype(BF16)

    q_spec, kv_specs = _swa_specs(nq)
    return pl.pallas_call(
        body, name=name,
        grid_spec=pltpu.PrefetchScalarGridSpec(
            num_scalar_prefetch=2, grid=(N_HEADS // 2, s // (nq * WINDOW)),
            in_specs=[q_spec, *kv_specs], out_specs=q_spec),
        out_shape=jax.ShapeDtypeStruct((s, N_HEADS * HEAD_DIM), BF16),
        compiler_params=_params("parallel", "parallel"),
    )(sinks, slopes, qkv, qkv, qkv, qkv, qkv)


def swa_bwd(name, qkv, do, sinks, slopes):
    s = qkv.shape[0]
    t = WINDOW
    nq = min(SWA_BLOCKS, s // t)

    def body(sink_ref, slope_ref, q_ref, kp_ref, kc_ref, vp_ref, vc_ref, do_ref, dq_ref, dk_ref, dv_ref, ds_ref):
        j, n = pl.program_id(0), pl.program_id(1)
        k_all, v_all, sel = _swa_band(kp_ref, kc_ref, vp_ref, vc_ref, (j // 4) % 2)
        q = q_ref[...] * ATTN_SCALE
        do_v = do_ref[...]
        masks = _half_masks(BF16)

        @pl.when((j % 8 == 0) & (n == 0))
        def _():
            dk_ref[...] = jnp.zeros_like(dk_ref)
            dv_ref[...] = jnp.zeros_like(dv_ref)

        @pl.when(n == 0)
        def _():
            ds_ref[...] = jnp.zeros_like(ds_ref)

        chains = [(b, hq) for b in range(nq) for hq in range(2)]
        qm = {(b, hq): _rows(q, b) * masks[hq] for b, hq in chains}
        dom = {(b, hq): _rows(do_v, b) * masks[hq] for b, hq in chains}
        s_raw = {ch: _dot(qm[ch], _rows(k_all, ch[0], 2), "nt") for ch in chains}
        dp = {ch: _dot(dom[ch], _rows(v_all, ch[0], 2), "nt") for ch in chains}
        pb, dsc = {}, {}
        dsink = [jnp.zeros((), F32), jnp.zeros((), F32)]
        for ch in chains:
            b, hq = ch
            p, p_sink = _swa_probs(s_raw[ch], sink_ref[2 * j + hq], slope_ref[2 * j + hq], (n == 0) if b == 0 else None)
            delta = jnp.sum(p * dp[ch], axis=-1, keepdims=True)
            dsc[ch] = (p * (dp[ch] - delta)).astype(BF16)
            pb[ch] = p.astype(BF16)
            dsink[hq] = dsink[hq] - jnp.sum(p_sink * delta)
        for hq in range(2):
            ds_ref[hq:hq + 1, :] += jnp.zeros((1, LANES), F32) + dsink[hq]
        dq, dk, dv = [], [], []
        for b in range(nq):
            dq.append(sum(_dot(dsc[(b, hq)], _rows(k_all, b, 2) * masks[hq]) for hq in range(2)))
            dk_b = sum(_dot(dsc[(b, hq)], qm[(b, hq)], "tn") for hq in range(2))
            dv_b = sum(_dot(pb[(b, hq)], dom[(b, hq)], "tn") for hq in range(2))
            dk.append((dk_b + _swap_halves(dk_b)) * sel)
            dv.append((dv_b + _swap_halves(dv_b)) * sel)
        dq_ref[...] = (jnp.concatenate(dq, axis=0) * ATTN_SCALE).astype(BF16)

        @pl.when(n == 0)
        def _():
            dk_ref[pl.ds(0, t), :] += dk[0][t:]
            dv_ref[pl.ds(0, t), :] += dv[0][t:]

        @pl.when(n > 0)
        def _():
            start = pl.multiple_of((nq * n - 1) * t, t)
            dk_ref[pl.ds(start, 2 * t), :] += dk[0]
            dv_ref[pl.ds(start, 2 * t), :] += dv[0]

        for b in range(1, nq):
            start = pl.multiple_of((nq * n + b - 1) * t, t)
            dk_ref[pl.ds(start, 2 * t), :] += dk[b]
            dv_ref[pl.ds(start, 2 * t), :] += dv[b]

    q_spec, kv_specs = _swa_specs(nq)
    kv_out = pl.BlockSpec((s, LANES), lambda j, n, *_: (0, j // 8))
    return pl.pallas_call(
        body, name=name,
        grid_spec=pltpu.PrefetchScalarGridSpec(
            num_scalar_prefetch=2, grid=(N_HEADS // 2, s // (nq * t)),
            in_specs=[q_spec, *kv_specs, q_spec],
            out_specs=[q_spec, kv_out, kv_out, pl.BlockSpec((None, 8, LANES), lambda j, n, *_: (j, 0, 0))]),
        out_shape=[jax.ShapeDtypeStruct((s, N_HEADS * HEAD_DIM), BF16),
                   jax.ShapeDtypeStruct((s, N_KV_A * HEAD_DIM), F32),
                   jax.ShapeDtypeStruct((s, N_KV_A * HEAD_DIM), F32),
                   jax.ShapeDtypeStruct((N_HEADS // 2, 8, LANES), F32)],
        compiler_params=_params("arbitrary", "arbitrary"),
    )(sinks, slopes, qkv, qkv, qkv, qkv, qkv, do)


SB_TILE = 256


def _split_k(v):
    hi = v.astype(BF16)
    lo = (v - hi.astype(F32)).astype(BF16)
    return jnp.concatenate([hi, lo], axis=1)


def _tri2(t, inclusive):
    r = lax.broadcasted_iota(jnp.int32, (2 * t, t), 0)
    c = lax.broadcasted_iota(jnp.int32, (2 * t, t), 1)
    r = jnp.where(r >= t, r - t, r)
    return ((r >= c) if inclusive else (r > c)).astype(BF16)


def _sb_logs(z, before):
    neg_abs = lax.bitcast_convert_type(lax.bitcast_convert_type(z, jnp.uint32) | jnp.uint32(0x80000000), F32)
    l = jnp.log(1.0 + jnp.exp(neg_abs))
    lb = jnp.minimum(z, 0.0) - l
    lm = lb - z
    if before is not None:
        lm = jnp.where(before, lm, 0.0)
    return lb, lm


def _sb_weights(lb, sfx, c_lm, before):
    a = jnp.exp(lb + sfx + c_lm)
    if before is not None:
        a = jnp.where(before, a, 0.0)
    return a


def _sb_sweep(tile, i, init, diag):
    carry = lax.cond(i > 0, lambda: tile([i, i - 1], init, diag), lambda: tile([i], init, diag))
    rest = jnp.maximum(i - 1, 0)
    odd = rest % 2
    carry = lax.cond(odd == 1, lambda: tile([i - 2], carry, None), lambda: carry)
    base = i - 2 - odd
    return lax.fori_loop(0, rest // 2, lambda n, cr: tile([base - 2 * n, base - 2 * n - 1], cr, None), carry)


def _sb_specs(s, t):
    hp = N_HEADS // 2
    q_spec = pl.BlockSpec((t, LANES), lambda h, i: (i, h))
    k_spec = pl.BlockSpec((s, LANES), lambda h, i: (0, hp + h))
    v_spec = pl.BlockSpec((s, LANES), lambda h, i: (0, 2 * hp + h))
    return q_spec, k_spec, v_spec


def sb_fwd(name, qkv, t=SB_TILE):
    s = qkv.shape[0]
    t = min(t, s)

    def body(q_ref, k_ref, v_ref, ob_ref, of_ref):
        i = pl.program_id(1)
        masks = _half_masks(BF16)
        q = q_ref[...] * ATTN_SCALE
        qm = [q * masks[0], q * masks[1]]
        tri_x = _tri2(t, False)
        r = lax.broadcasted_iota(jnp.int32, (t, t), 0)
        c = lax.broadcasted_iota(jnp.int32, (t, t), 1)
        diag = c < r

        def tile(js, carry, before):
            c0, c1, acc = carry
            cs = [c0, c1]
            kj = [k_ref[pl.ds(pl.multiple_of(j * t, t), t), :] for j in js]
            vj = [v_ref[pl.ds(pl.multiple_of(j * t, t), t), :] for j in js]
            chains = [(h, b) for b in range(len(js)) for h in range(2)]
            z = {(h, b): _dot(qm[h], kj[b], "nt") for h, b in chains}
            lb, sfx, c_at = {}, {}, {}
            for ch in chains:
                h, b = ch
                lb[ch], lm = _sb_logs(z[ch], before if b == 0 else None)
                sfx[ch] = _dot(_split_k(lm), tri_x)
                c_at[ch] = cs[h]
                cs[h] = cs[h] + jnp.sum(lm, axis=-1, keepdims=True)
            for ch in chains:
                h, b = ch
                a = _sb_weights(lb[ch], sfx[ch], c_at[ch], before if b == 0 else None)
                acc = acc + _dot(a.astype(BF16), vj[b] * masks[h])
            return cs[0], cs[1], acc

        zero = jnp.zeros((t, 1), F32)
        carry = _sb_sweep(tile, i, (zero, zero, jnp.zeros((t, LANES), F32)), diag)
        ob_ref[...] = carry[2].astype(BF16)
        of_ref[...] = carry[2]

    q_spec, k_spec, v_spec = _sb_specs(s, t)
    return pl.pallas_call(
        body, name=name, grid=(N_HEADS // 2, s // t),
        in_specs=[q_spec, k_spec, v_spec], out_specs=[q_spec, q_spec],
        out_shape=[jax.ShapeDtypeStruct((s, N_HEADS * HEAD_DIM), BF16),
                   jax.ShapeDtypeStruct((s, N_HEADS * HEAD_DIM), F32)],
        compiler_params=_params("parallel", "parallel"),
    )(qkv, qkv, qkv)


def sb_bwd(name, qkv, o_f32, do, t=SB_TILE):
    s = qkv.shape[0]
    t = min(t, s)

    def body(q_ref, k_ref, v_ref, o_ref, do_ref, dq_ref, dk_ref, dv_ref):
        i = pl.program_id(1)
        masks = _half_masks(BF16)
        fmasks = _half_masks(F32)
        q = q_ref[...] * ATTN_SCALE
        do_v = do_ref[...]
        qm = [q * masks[0], q * masks[1]]
        dom = [do_v * masks[0], do_v * masks[1]]
        prod = do_v.astype(F32) * o_ref[...]
        delta = [jnp.sum(prod * fmasks[h], axis=-1, keepdims=True) for h in range(2)]
        tri_x = _tri2(t, False)
        tri_i = _tri2(t, True)
        r = lax.broadcasted_iota(jnp.int32, (t, t), 0)
        c = lax.broadcasted_iota(jnp.int32, (t, t), 1)
        diag = c < r

        @pl.when(i == 0)
        def _():
            dk_ref[...] = jnp.zeros_like(dk_ref)
            dv_ref[...] = jnp.zeros_like(dv_ref)

        def tile(js, carry, before):
            cl0, cl1, cd0, cd1, dq = carry
            cl, cd = [cl0, cl1], [cd0, cd1]
            starts = [pl.multiple_of(j * t, t) for j in js]
            kj = [k_ref[pl.ds(st, t), :] for st in starts]
            vj = [v_ref[pl.ds(st, t), :] for st in starts]
            chains = [(h, b) for b in range(len(js)) for h in range(2)]
            z = {(h, b): _dot(qm[h], kj[b], "nt") for h, b in chains}
            da = {(h, b): _dot(dom[h], vj[b], "nt") for h, b in chains}
            lb, sfx, cl_at, cd_at, ab, de, dsfx = {}, {}, {}, {}, {}, {}, {}
            for ch in chains:
                h, b = ch
                lb[ch], lm = _sb_logs(z[ch], before if b == 0 else None)
                sfx[ch] = _dot(_split_k(lm), tri_x)
                cl_at[ch] = cl[h]
                cl[h] = cl[h] + jnp.sum(lm, axis=-1, keepdims=True)
            dv = [jnp.zeros((t, LANES), F32) for _ in js]
            dk = [jnp.zeros((t, LANES), F32) for _ in js]
            for ch in chains:
                h, b = ch
                a = _sb_weights(lb[ch], sfx[ch], cl_at[ch], before if b == 0 else None)
                ab[ch] = a.astype(BF16)
                de[ch] = da[ch] * ab[ch].astype(F32)
                dsfx[ch] = _dot(_split_k(de[ch]), tri_i)
                dv[b] = dv[b] + _dot(ab[ch], dom[h], "tn")
                cd_at[ch] = cd[h]
                cd[h] = cd[h] + jnp.sum(de[ch], axis=-1, keepdims=True)
            for ch in chains:
                h, b = ch
                farther = delta[h] - cd_at[ch] - dsfx[ch]
                sig = jnp.exp(lb[ch])
                dz = de[ch] * (1.0 - sig) - farther * sig
                if before is not None and b == 0:
                    dz = jnp.where(before, dz, 0.0)
                dzb = dz.astype(BF16)
                dq = dq + _dot(dzb, kj[b] * masks[h])
                dk[b] = dk[b] + _dot(dzb, qm[h], "tn")
            for b, st in enumerate(starts):
                dk_ref[pl.ds(st, t), :] += dk[b]
                dv_ref[pl.ds(st, t), :] += dv[b]
            return cl[0], cl[1], cd[0], cd[1], dq

        zero = jnp.zeros((t, 1), F32)
        carry = _sb_sweep(tile, i, (zero, zero, zero, zero, jnp.zeros((t, LANES), F32)), diag)
        dq_ref[...] = (carry[4] * ATTN_SCALE).astype(BF16)

    q_spec, k_spec, v_spec = _sb_specs(s, t)
    kv_out = pl.BlockSpec((s, LANES), lambda h, i: (0, h))
    width = N_HEADS * HEAD_DIM
    return pl.pallas_call(
        body, name=name, grid=(N_HEADS // 2, s // t),
        in_specs=[q_spec, k_spec, v_spec, q_spec, q_spec], out_specs=[q_spec, kv_out, kv_out],
        out_shape=[jax.ShapeDtypeStruct((s, width), BF16), jax.ShapeDtypeStruct((s, width), F32),
                   jax.ShapeDtypeStruct((s, width), F32)],
        compiler_params=_params("parallel", "arbitrary"),
    )(qkv, qkv, qkv, o_f32, do)


def adamw(name, w, g, m, v, tm=256):
    shape = w.shape
    c = shape[-1]
    rows = math.prod(shape[:-1])
    tm = min(tm, rows)

    def body(w_ref, g_ref, m_ref, v_ref, d_ref, mo_ref, vo_ref):
        gv = g_ref[...]
        m2 = ADAM_B1 * m_ref[...] + (1.0 - ADAM_B1) * gv
        v2 = ADAM_B2 * v_ref[...] + (1.0 - ADAM_B2) * (gv * gv)
        m_hat = m2 / (1.0 - ADAM_B1 ** ADAM_STEP)
        v_hat = v2 / (1.0 - ADAM_B2 ** ADAM_STEP)
        d_ref[...] = -ADAM_LR * (m_hat / (jnp.sqrt(v_hat) + ADAM_EPS) + ADAM_WD * w_ref[...])
        mo_ref[...] = m2
        vo_ref[...] = v2

    blk = pl.BlockSpec((tm, c), lambda i: (i, 0))
    outs = pl.pallas_call(
        body, name=name, grid=(rows // tm,), in_specs=[blk] * 4, out_specs=[blk] * 3,
        out_shape=[jax.ShapeDtypeStruct((rows, c), F32)] * 3, compiler_params=_params("parallel"),
    )(*[t.reshape(rows, c) for t in (w, g, m, v)])
    return [o.reshape(shape) for o in outs]


HBM = pl.BlockSpec(memory_space=pltpu.HBM)


def _place():
    x, y, c = lax.axis_index("x"), lax.axis_index("y"), lax.axis_index("c")
    return x, y, c, [(1 - x, y), (x, 1 - y), (1 - x, 1 - y)]


def _remote(src, dst, send, recv, dev):
    return pltpu.make_async_remote_copy(src_ref=src, dst_ref=dst, send_sem=send, recv_sem=recv, device_id=dev,
                                        device_id_type=MESH)


SEM = pl.BlockSpec(memory_space=pltpu.SEMAPHORE)
ANY = pl.BlockSpec(memory_space=pl.ANY)
DATAFLOW = pltpu.SideEffectType.DATAFLOW_SIDE_EFFECTING


def _in_hbm(v):
    return pltpu.with_memory_space_constraint(v, pltpu.HBM)


def split_start(name, bufs, n_copies, sends, after=()):
    nb, na = len(bufs), len(after)

    def body(*refs):
        send, recv = refs[nb + na], refs[nb + na + 1]
        for cp in sends(refs[:nb], send, recv):
            cp.start()
        refs[-1][...] = jnp.zeros_like(refs[-1])

    outs = pl.pallas_call(
        body, name=name,
        in_specs=[HBM] * nb + [ANY] * na,
        out_shape=(pltpu.SemaphoreType.DMA((n_copies,)), pltpu.SemaphoreType.DMA((n_copies,)),
                   *[pltpu.HBM(b.shape, b.dtype) for b in bufs], jax.ShapeDtypeStruct((8, LANES), F32)),
        out_specs=(SEM, SEM, *[HBM] * nb, pl.BlockSpec(memory_space=pltpu.VMEM)),
        input_output_aliases={i: 2 + i for i in range(nb)},
        compiler_params=pltpu.CompilerParams(has_side_effects=DATAFLOW),
    )(*[_in_hbm(b) for b in bufs], *after)
    return outs[0], outs[1], list(outs[2:2 + nb]), outs[-1]


def split_wait(name, send_sems, recv_sems, bufs, sends, arrivals, after):
    nb, na = len(bufs), len(after)

    def body(*refs):
        send, recv = refs[nb], refs[nb + 1]
        for cp in sends(refs[:nb], send, recv):
            cp.wait_send()
        for cp in arrivals(refs[:nb], send, recv):
            cp.wait_recv()

    outs = pl.pallas_call(
        body, name=name,
        in_specs=[HBM] * nb + [SEM, SEM] + [ANY] * na,
        out_shape=tuple(pltpu.HBM(b.shape, b.dtype) for b in bufs), out_specs=tuple([HBM] * nb),
        input_output_aliases={i: i for i in range(nb)},
        compiler_params=pltpu.CompilerParams(has_side_effects=DATAFLOW),
    )(*bufs, send_sems, recv_sems, *after)
    return list(outs)


def _gather_plan(n):
    def sends(refs, send, recv):
        x, y, c, chips = _place()
        me = 2 * x + y
        return [_remote(refs[t].at[me], refs[t].at[me], send.at[3 * t + k], recv.at[3 * t + k], (px, py, c))
                for t in range(n) for k, (px, py) in enumerate(chips)]

    def arrivals(refs, send, recv):
        x, y, c, chips = _place()
        return [_remote(refs[t].at[2 * px + py], refs[t].at[2 * px + py], send.at[3 * t + k], recv.at[3 * t + k],
                        (px, py, c)) for t in range(n) for k, (px, py) in enumerate(chips)]

    return 3 * n, sends, arrivals


def _swap_plan(n):
    def copies(refs, send, recv):
        x, y, c, _ = _place()
        return [_remote(refs[t].at[s, 1 - c], refs[n + t].at[s], send.at[N_CHIPS * t + s], recv.at[N_CHIPS * t + s],
                        (x, y, 1 - c)) for t in range(n) for s in range(N_CHIPS)]

    return N_CHIPS * n, copies, copies


def _scatter_plan(n):
    def sends(refs, send, recv):
        x, y, c, chips = _place()
        me = 2 * x + y
        return [_remote(refs[t].at[2 * px + py], refs[n + t].at[me], send.at[3 * t + k], recv.at[3 * t + k],
                        (px, py, c)) for t in range(n) for k, (px, py) in enumerate(chips)]

    def arrivals(refs, send, recv):
        x, y, c, chips = _place()
        return [_remote(refs[t].at[2 * px + py], refs[n + t].at[2 * px + py], send.at[3 * t + k], recv.at[3 * t + k],
                        (px, py, c)) for t in range(n) for k, (px, py) in enumerate(chips)]

    return 3 * n, sends, arrivals


def cast_into_slot(name, w, chip, tm=256):
    r, c = w.shape
    tm = min(tm, r)

    def body(chip_ref, w_ref, o_ref):
        o_ref[...] = w_ref[...].astype(BF16)

    return pl.pallas_call(
        body, name=name,
        grid_spec=pltpu.PrefetchScalarGridSpec(
            num_scalar_prefetch=1, grid=(r // tm,),
            in_specs=[pl.BlockSpec((tm, c), lambda i, chip_ref: (i, 0))],
            out_specs=pl.BlockSpec((None, tm, c), lambda i, chip_ref: (chip_ref[0], i, 0))),
        out_shape=jax.ShapeDtypeStruct((N_CHIPS, r, c), BF16), compiler_params=_params("parallel"),
    )(chip, w)


def join_halves(name, f):
    n = f.shape[0]

    def body(f_ref, o_ref, send, recv):
        x, y, c, _ = _place()
        sib = (x, y, 1 - c)
        sends = [_remote(f_ref.at[l, c], o_ref.at[l, c], send.at[l], recv.at[l], sib) for l in range(n)]
        for cp in sends:
            cp.start()
        for l in range(n):
            _remote(f_ref.at[l, 1 - c], o_ref.at[l, 1 - c], send.at[l], recv.at[l], sib).wait_recv()
        for cp in sends:
            cp.wait_send()

    return pl.pallas_call(
        body, name=name, in_specs=[HBM], out_specs=HBM, out_shape=jax.ShapeDtypeStruct(f.shape, f.dtype),
        input_output_aliases={0: 0},
        scratch_shapes=[pltpu.SemaphoreType.DMA((n,)), pltpu.SemaphoreType.DMA((n,))],
    )(f)


def add_sibling_half(name, g, a, core, tm=256):
    _, _, rh, c = g.shape
    tm = min(tm, rh)

    def body(core_ref, g_ref, a_ref, o_ref):
        o_ref[...] = (g_ref[...].astype(F32) + a_ref[...].astype(F32)).astype(BF16)

    return pl.pallas_call(
        body, name=name,
        grid_spec=pltpu.PrefetchScalarGridSpec(
            num_scalar_prefetch=1, grid=(N_CHIPS, rh // tm),
            in_specs=[pl.BlockSpec((None, None, tm, c), lambda s, i, core_ref: (s, core_ref[0], i, 0)),
                      pl.BlockSpec((None, tm, c), lambda s, i, core_ref: (s, i, 0))],
            out_specs=pl.BlockSpec((None, tm, c), lambda s, i, core_ref: (s, i, 0))),
        out_shape=jax.ShapeDtypeStruct(a.shape, BF16), compiler_params=_params("parallel", "parallel"),
    )(core, g, a)


def sum_chips_into(name, p, b, f, layer, chip, core, tm=256):
    _, rh, c = b.shape
    tm = min(tm, rh)

    def body(chip_ref, core_ref, p_ref, b_ref, f_ref, o_ref):
        acc = jnp.zeros((tm, c), F32)
        for s in range(N_CHIPS):
            acc = acc + jnp.where(chip_ref[0] == s, p_ref[s].astype(F32), b_ref[s].astype(F32))
        o_ref[...] = acc

    slots = pl.BlockSpec((N_CHIPS, tm, c), lambda i, chip_ref, core_ref: (0, i, 0))
    return pl.pallas_call(
        body, name=name,
        grid_spec=pltpu.PrefetchScalarGridSpec(
            num_scalar_prefetch=2, grid=(rh // tm,), in_specs=[slots, slots, ANY],
            out_specs=pl.BlockSpec((None, None, tm, c), lambda i, chip_ref, core_ref: (layer, core_ref[0], i, 0))),
        out_shape=jax.ShapeDtypeStruct(f.shape, F32), input_output_aliases={4: 0},
        compiler_params=_params("parallel"),
    )(chip, core, p, b, f)


N_DEV = 8


def allreduce_small(name, parts):
    p, _, d = parts.shape
    m_per = p * 8

    def body(x_ref, out_ref, all_ref, send_sems, recv_sems, local_sem):
        x, y, c, chips = _place()
        me, sibling = (x, y, c), (x, y, 1 - c)

        def rows(px, py, pc):
            return all_ref.at[pl.ds((4 * px + 2 * py + pc) * m_per, m_per), :]

        def copy(k, block, to, src=None):
            return _remote(rows(*block) if src is None else src, rows(*block), send_sems.at[k], recv_sems.at[k], to)

        mine = pltpu.make_async_copy(x_ref, rows(*me), local_sem)
        mine.start()
        first = [copy(0, me, sibling, src=x_ref)]
        first += [copy(1 + j, me, (*chip, c), src=x_ref) for j, chip in enumerate(chips)]
        for cp in first:
            cp.start()
        passed = [copy(4 + j, (*chip, c), sibling) for j, chip in enumerate(chips)]
        for j, chip in enumerate(chips):
            copy(1 + j, (*chip, c), me).wait_recv()
            passed[j].start()
        copy(0, sibling, me).wait_recv()
        for j, chip in enumerate(chips):
            copy(4 + j, (*chip, 1 - c), me).wait_recv()
        for cp in first + passed:
            cp.wait_send()
        mine.wait()
        acc = all_ref[pl.ds(0, m_per), :]
        for dev in range(1, N_DEV):
            acc = acc + all_ref[pl.ds(dev * m_per, m_per), :]
        out_ref[...] = jnp.sum(acc.reshape(p, 8, d), axis=1)

    vmem = pl.BlockSpec(memory_space=pltpu.VMEM)
    return pl.pallas_call(
        body, name=name, in_specs=[vmem], out_specs=vmem,
        out_shape=jax.ShapeDtypeStruct((p, d), F32),
        scratch_shapes=[pltpu.VMEM((N_DEV * m_per, d), F32), pltpu.SemaphoreType.DMA((7,)),
                        pltpu.SemaphoreType.DMA((7,)), pltpu.SemaphoreType.DMA],
        compiler_params=pltpu.CompilerParams(vmem_limit_bytes=VMEM_LIMIT),
    )(parts.reshape(m_per, d))


def _empty(shape, dtype):
    return _in_hbm(lax.empty(shape, dtype))


class _GradExchange:
    def __init__(self, layer, kinds, grads, chip, core):
        self.layer, self.kinds, self.chip, self.core = layer, kinds, chip, core
        self.g4 = [g.reshape(N_CHIPS, 2, g.shape[1] // 2, g.shape[2]) for g in grads]
        self.n = len(grads)

    def start_swap(self, after):
        n_copies, self.swap_sends, self.swap_arrivals = _swap_plan(self.n)
        lands = [_empty((N_CHIPS,) + g.shape[2:], BF16) for g in self.g4]
        self.swap = split_start(f"swap_start_l{self.layer}", self.g4 + lands, n_copies, self.swap_sends, after)
        return self.swap[3]

    def swap_to_scatter(self, after):
        send, recv, bufs, _ = self.swap
        bufs = split_wait(f"swap_wait_l{self.layer}", send, recv, bufs, self.swap_sends, self.swap_arrivals, after)
        g4, lands = bufs[:self.n], bufs[self.n:]
        self.p = [add_sibling_half(f"add_l{self.layer}_{k}", g4[t], lands[t], self.core)
                  for t, (k, _) in enumerate(self.kinds)]
        n_copies, self.sc_sends, self.sc_arrivals = _scatter_plan(self.n)
        lands = [_empty(p.shape, BF16) for p in self.p]
        self.scatter = split_start(f"scatter_start_l{self.layer}", self.p + lands, n_copies, self.sc_sends)
        return self.scatter[3]

    def finish(self, f, after):
        send, recv, bufs, _ = self.scatter
        bufs = split_wait(f"scatter_wait_l{self.layer}", send, recv, bufs, self.sc_sends, self.sc_arrivals, after)
        p, lands = bufs[:self.n], bufs[self.n:]
        for t, (kind, l) in enumerate(self.kinds):
            f[kind] = sum_chips_into(f"sum_l{self.layer}_{kind}", p[t], lands[t], f[kind], l, self.chip, self.core)


def _relu2(acc):
    r = jnp.maximum(acc, 0.0)
    return acc, r * r


def _relu2_bwd(acc, u):
    return (acc * (2.0 * jnp.maximum(u.astype(F32), 0.0)),)


def _same(acc):
    return (acc,)


def kernel(x, a_w_qkv, a_w_o, a_sinks, b_w_qkv, b_w_o, norm_mix, norm_mlp, mlp_w_in, mlp_w_out, final_norm, loss_target, m_a_w_qkv, m_a_w_o, m_a_sinks, m_b_w_qkv, m_b_w_o, m_norm_mix, m_norm_mlp, m_mlp_w_in, m_mlp_w_out, m_final_norm, v_a_w_qkv, v_a_w_o, v_a_sinks, v_b_w_qkv, v_b_w_o, v_norm_mix, v_norm_mlp, v_mlp_w_in, v_mlp_w_out, v_final_norm):
    _, s, d = x.shape
    depth = norm_mix.shape[0]
    width = N_HEADS * HEAD_DIM
    core = lax.axis_index("c").astype(jnp.int32).reshape(1)
    chip = (2 * lax.axis_index("x") + lax.axis_index("y")).astype(jnp.int32).reshape(1)
    slopes = jnp.power(2.0, -8.0 * (jnp.arange(N_HEADS, dtype=F32) + 1.0) / N_HEADS)
    qkv_of = {0: ("a_qkv", a_w_qkv), 1: ("b_qkv", b_w_qkv)}
    o_of = {0: ("a_o", a_w_o), 1: ("b_o", b_w_o)}

    def layer_kinds(i):
        return [(qkv_of[i % 2][0], i // 2), (o_of[i % 2][0], i // 2), ("mlp_in", i), ("mlp_out", i)]

    shards = [[qkv_of[i % 2][1][i // 2], o_of[i % 2][1][i // 2], mlp_w_in[i], mlp_w_out[i]] for i in range(depth)]
    slots = [[cast_into_slot(f"cast_l{i}_{k}", w, chip) for w, (k, _) in zip(shards[i], layer_kinds(i))]
             for i in range(depth)]
    n_gather, gather_sends, gather_arrivals = _gather_plan(2)

    def start_gather(tag, bufs, after):
        return split_start(f"gather_start_{tag}", bufs, n_gather, gather_sends, after)

    def wait_gather(tag, flight, after):
        return split_wait(f"gather_wait_{tag}", flight[0], flight[1], flight[2], gather_sends, gather_arrivals, after)

    saved, weights = [], []
    xc = x[0]
    flight_a = start_gather("l0a", slots[0][:2], ())
    flight_m = start_gather("l0m", slots[0][2:], (flight_a[3],))
    started = (flight_m[3],)
    for i in range(depth):
        mixer, j = i % 2, i // 2
        w_qkv, w_o = wait_gather(f"l{i}a", flight_a, (xc,))
        w_o = w_o.reshape(width, d)
        h = rms_fwd(f"l{i}_norm_mix", xc, norm_mix[i], deps=started)
        qkv = mm_cols(f"l{i}_qkv", h, w_qkv, _same, (BF16,))[0]
        if mixer == 0:
            attn, attn_f32 = swa_fwd(f"l{i}_swa", qkv, a_sinks[j], slopes), None
        else:
            attn, attn_f32 = sb_fwd(f"l{i}_sb", qkv)
        xm = mm_res(f"l{i}_o", attn, w_o, xc)
        w_in, w_out = wait_gather(f"l{i}m", flight_m, (xm,))
        w_out = w_out.reshape(-1, d)
        weights.append((w_qkv, w_o, w_in, w_out))
        started = ()
        if i + 1 < depth:
            flight_a = start_gather(f"l{i + 1}a", slots[i + 1][:2], (w_in,))
            flight_m = start_gather(f"l{i + 1}m", slots[i + 1][2:], (flight_a[3],))
            started = (flight_m[3],)
        h2 = rms_fwd(f"l{i}_norm_mlp", xm, norm_mlp[i], deps=started)
        u, hh = mm_cols(f"l{i}_in", h2, w_in, _relu2, (BF16, BF16))
        xn = mm_res(f"l{i}_out", hh, w_out, xm)
        saved.append((xc, h, qkv, attn, attn_f32, xm, h2, u, hh))
        xc = xn
    loss_rows, dx, dxb, dg_final = loss_head("loss_head", xc, final_norm, loss_target[0])
    loss = lax.psum(jnp.sum(loss_rows), ("x", "y", "c"))

    big = {"a_qkv": (a_w_qkv, m_a_w_qkv, v_a_w_qkv), "a_o": (a_w_o, m_a_w_o, v_a_w_o),
           "b_qkv": (b_w_qkv, m_b_w_qkv, v_b_w_qkv), "b_o": (b_w_o, m_b_w_o, v_b_w_o),
           "mlp_in": (mlp_w_in, m_mlp_w_in, v_mlp_w_in), "mlp_out": (mlp_w_out, m_mlp_w_out, v_mlp_w_out)}
    f = {k: _empty((w.shape[0], 2, w.shape[1] // 2, w.shape[2]), F32) for k, (w, _, _) in big.items()}
    dg_mix, dg_mlp, dsinks = [], [], []
    prev_a, started = None, ()
    for i in reversed(range(depth)):
        mixer, j = i % 2, i // 2
        xin, h, qkv, attn, attn_f32, xm, h2, u, hh = saved[i]
        w_qkv, w_o, w_in, w_out = weights[i]
        kinds = layer_kinds(i)
        du = mm_nt(f"l{i}_d_hidden", dxb, w_out, _relu2_bwd, (u,), deps=started)
        g_out = mm_tn(f"l{i}_g_out", hh, dxb).reshape(N_CHIPS, -1, d)
        started = (g_out,) if prev_a is None else (g_out, prev_a.swap_to_scatter((g_out,)))
        dh2 = mm_nt_cols(f"l{i}_d_h2", du, w_in, deps=started)
        g_in = mm_tn_cols(f"l{i}_g_in", h2, du)
        cur_m = _GradExchange(f"{i}m", kinds[2:], [g_in, g_out], chip, core)
        started = (g_in, cur_m.start_swap((g_in,)))
        dxm, dxmb, dg = rms_bwd(f"l{i}_norm_mlp_bwd", xm, norm_mlp[i], dh2, dx, deps=started)
        dg_mlp.append(dg)
        g_o = mm_tn(f"l{i}_g_o", attn, dxmb).reshape(N_CHIPS, -1, d)
        dattn = mm_nt(f"l{i}_d_attn", dxmb, w_o, _same, deps=(g_o, cur_m.swap_to_scatter((g_o,))))
        if mixer == 0:
            dq, dk, dv, dsk = swa_bwd(f"l{i}_swa_bwd", qkv, dattn, a_sinks[j], slopes)
            dsinks.append(dsk[:, :2, 0].reshape(N_HEADS))
        else:
            dq, dk, dv = sb_bwd(f"l{i}_sb_bwd", qkv, attn_f32, dattn)
        dqkv = jnp.concatenate([dq, dk.astype(BF16), dv.astype(BF16)], axis=1)
        dh = mm_nt_cols(f"l{i}_d_h", dqkv, w_qkv)
        g_qkv = mm_tn_cols(f"l{i}_g_qkv", h, dqkv)
        dx, dxb, dg = rms_bwd(f"l{i}_norm_mix_bwd", xin, norm_mix[i], dh, dxm, deps=(g_qkv,))
        dg_mix.append(dg)
        if prev_a is not None:
            prev_a.finish(f, (dx,))
        cur_m.finish(f, (dx,))
        prev_a = _GradExchange(f"{i}a", kinds[:2], [g_qkv, g_o], chip, core)
        started = (prev_a.start_swap((dx,)),)
    prev_a.swap_to_scatter((dx,))
    prev_a.finish(f, (dx,))
    for lst in (dg_mix, dg_mlp, dsinks):
        lst.reverse()

    res = {}
    for kind, (w, m, v) in big.items():
        g = join_halves(f"join_{kind}", f[kind]).reshape(w.shape)
        res[kind] = (g, *adamw(f"adamw_{kind}", w, g, m, v))

    n_sink = a_sinks.size
    sink_rows = jnp.zeros((1, 8, d), F32).at[0, 0, :n_sink].set(jnp.concatenate(dsinks))
    parts = jnp.concatenate([jnp.stack(dg_mix), jnp.stack(dg_mlp), dg_final[None], sink_rows], axis=0)
    g_small = allreduce_small("allreduce_small", parts)

    def pack(mix, mlp, fin, snk):
        snk_row = jnp.zeros((1, d), F32).at[0, :n_sink].set(snk.reshape(-1))
        return jnp.concatenate([mix, mlp, fin[None], snk_row], axis=0)

    def unpack(t):
        return t[:depth], t[depth:2 * depth], t[2 * depth], t[2 * depth + 1, :n_sink].reshape(a_sinks.shape)

    small = adamw("adamw_small", pack(norm_mix, norm_mlp, final_norm, a_sinks), g_small,
                  pack(m_norm_mix, m_norm_mlp, m_final_norm, m_a_sinks),
                  pack(v_norm_mix, v_norm_mlp, v_final_norm, v_a_sinks))
    outs = []
    for idx in range(4):
        mix, mlp, fin, snk = unpack(g_small if idx == 0 else small[idx - 1])
        outs += [res["a_qkv"][idx], res["a_o"][idx], snk, res["b_qkv"][idx], res["b_o"][idx], mix, mlp,
                 res["mlp_in"][idx], res["mlp_out"][idx], fin]
    return (loss, dx.reshape(x.shape), *outs)
```

```python
import functools
import math

import jax
import jax.numpy as jnp
from jax import lax
from jax.experimental import pallas as pl
from jax.experimental.pallas import tpu as pltpu

F32 = jnp.float32
BF16 = jnp.bfloat16
MESH = pl.DeviceIdType.MESH

N_CHIPS = 4
HEAD_DIM = 64
LANES = 128
N_HEADS = 32
N_KV_A = 4
WINDOW = 128
RMS_EPS = 1e-5
ATTN_SCALE = 1.0 / math.sqrt(HEAD_DIM)
ADAM_LR, ADAM_B1, ADAM_B2, ADAM_EPS, ADAM_WD, ADAM_STEP = 0.001, 0.9, 0.999, 1e-08, 0.01, 10
NEG_BIG = -1e30
VMEM_LIMIT = 56 * 1024 * 1024

_DN = {"nn": (((1,), (0,)), ((), ())), "nt": (((1,), (1,)), ((), ())), "tn": (((0,), (0,)), ((), ()))}


def _dot(a, b, mode="nn"):
    return lax.dot_general(a, b, _DN[mode], preferred_element_type=F32)


def _params(*sem):
    return pltpu.CompilerParams(dimension_semantics=sem, vmem_limit_bytes=VMEM_LIMIT)


def _pick(n, prefs):
    for t in prefs:
        if n % t == 0:
            return t
    return n


def _mm(name, mode, a, b, *, grid, a_spec, b_spec, extras=(), extra_specs=(), out_shapes, out_specs, nk,
        acc_shape, epilogue, deps=()):
    n_ex, n_out = len(extras), len(out_shapes)
    first_out = 2 + n_ex + len(deps)

    def body(*refs):
        a_ref, b_ref = refs[0], refs[1]
        ex = refs[2:2 + n_ex]
        outs = refs[first_out:first_out + n_out]
        part = _dot(a_ref[...], b_ref[...], mode)

        def finish(acc):
            res = epilogue(acc, *[e[...] for e in ex])
            for o, r in zip(outs, res):
                o[...] = r.astype(o.dtype)

        if nk == 1:
            finish(part)
        else:
            acc_ref = refs[-1]
            k = pl.program_id(len(grid) - 1)

            @pl.when(k == 0)
            def _():
                acc_ref[...] = part

            @pl.when(k > 0)
            def _():
                acc_ref[...] += part

            @pl.when(k == nk - 1)
            def _():
                finish(acc_ref[...])

    sem = ("parallel",) * (len(grid) - 1) + ("arbitrary" if nk > 1 else "parallel",)
    return pl.pallas_call(
        body, name=name, grid=grid,
        in_specs=[a_spec, b_spec, *extra_specs, *[ANY] * len(deps)],
        out_specs=list(out_specs), out_shape=list(out_shapes),
        scratch_shapes=[] if nk == 1 else [pltpu.VMEM(acc_shape, F32)],
        compiler_params=_params(*sem),
    )(a, b, *extras, *deps)


def mm_cols(name, a, wg, epilogue, out_dtypes, tm=2048):
    m, k = a.shape
    c = wg.shape[2]
    tm = min(tm, m)
    tn = _pick(c, (512, 640, 256, 128))
    nj = c // tn
    o_spec = pl.BlockSpec((tm, tn), lambda i, j: (i, j))
    return _mm(name, "nn", a, wg, grid=(m // tm, N_CHIPS * nj),
               a_spec=pl.BlockSpec((tm, k), lambda i, j: (i, 0)),
               b_spec=pl.BlockSpec((None, k, tn), lambda i, j: (j // nj, 0, j % nj)),
               out_shapes=[jax.ShapeDtypeStruct((m, N_CHIPS * c), d) for d in out_dtypes],
               out_specs=[o_spec] * len(out_dtypes), nk=1, acc_shape=None, epilogue=epilogue)


def mm_res(name, a, w, res, tm=1024, tn=1024, tk=2048):
    m, k = a.shape
    n = w.shape[1]
    tm, tn, tk = min(tm, m), min(tn, n), min(tk, k)
    nk = k // tk
    return _mm(name, "nn", a, w, grid=(m // tm, n // tn, nk),
               a_spec=pl.BlockSpec((tm, tk), lambda i, j, kk: (i, kk)),
               b_spec=pl.BlockSpec((tk, tn), lambda i, j, kk: (kk, j)),
               extras=(res,), extra_specs=(pl.BlockSpec((tm, tn), lambda i, j, kk: (i, j)),),
               out_shapes=[jax.ShapeDtypeStruct((m, n), F32)],
               out_specs=[pl.BlockSpec((tm, tn), lambda i, j, kk: (i, j))], nk=nk, acc_shape=(tm, tn),
               epilogue=lambda acc, r: (acc + r,))[0]


def mm_nt(name, a, w, epilogue, extras=(), deps=(), tm=2048, tn=512):
    m, k = a.shape
    n = w.shape[0]
    tm, tn = min(tm, m), min(tn, n)
    o_spec = pl.BlockSpec((tm, tn), lambda i, j: (i, j))
    return _mm(name, "nt", a, w, grid=(m // tm, n // tn),
               a_spec=pl.BlockSpec((tm, k), lambda i, j: (i, 0)),
               b_spec=pl.BlockSpec((tn, k), lambda i, j: (j, 0)),
               extras=tuple(extras), extra_specs=(o_spec,) * len(extras),
               out_shapes=[jax.ShapeDtypeStruct((m, n), BF16)], out_specs=[o_spec], nk=1, acc_shape=None,
               epilogue=epilogue, deps=deps)[0]


def mm_nt_cols(name, dy, wg, deps=(), tm=1024, tn=2048):
    m = dy.shape[0]
    _, d, c = wg.shape
    tm, tn = min(tm, m), min(tn, d)
    tk = _pick(c, (1024, 768, 640, 512, 128))
    nkk = c // tk
    nk = N_CHIPS * nkk
    return _mm(name, "nt", dy, wg, grid=(m // tm, d // tn, nk),
               a_spec=pl.BlockSpec((tm, tk), lambda i, j, kk: (i, kk)),
               b_spec=pl.BlockSpec((None, tn, tk), lambda i, j, kk: (kk // nkk, j, kk % nkk)),
               out_shapes=[jax.ShapeDtypeStruct((m, d), F32)],
               out_specs=[pl.BlockSpec((tm, tn), lambda i, j, kk: (i, j))], nk=nk, acc_shape=(tm, tn),
               epilogue=lambda acc: (acc,), deps=deps)[0]


def mm_tn(name, a, b, tm=1024, tn=2048, tk=1024):
    m, p = a.shape
    q = b.shape[1]
    tm, tn, tk = min(tm, p), min(tn, q), min(tk, m)
    nk = m // tk
    return _mm(name, "tn", a, b, grid=(p // tm, q // tn, nk),
               a_spec=pl.BlockSpec((tk, tm), lambda i, j, kk: (kk, i)),
               b_spec=pl.BlockSpec((tk, tn), lambda i, j, kk: (kk, j)),
               out_shapes=[jax.ShapeDtypeStruct((p, q), BF16)],
               out_specs=[pl.BlockSpec((tm, tn), lambda i, j, kk: (i, j))], nk=nk, acc_shape=(tm, tn),
               epilogue=lambda acc: (acc,))[0]


def mm_tn_cols(name, a, dy, tm=2048, tk=1024):
    m, d = a.shape
    c = dy.shape[1] // N_CHIPS
    tm, tk = min(tm, d), min(tk, m)
    tn = _pick(c, (1024, 768, 640, 512, 128))
    nj = c // tn
    nk = m // tk
    return _mm(name, "tn", a, dy, grid=(d // tm, N_CHIPS * nj, nk),
               a_spec=pl.BlockSpec((tk, tm), lambda i, j, kk: (kk, i)),
               b_spec=pl.BlockSpec((tk, tn), lambda i, j, kk: (kk, j)),
               out_shapes=[jax.ShapeDtypeStruct((N_CHIPS, d, c), BF16)],
               out_specs=[pl.BlockSpec((None, tm, tn), lambda i, j, kk: (j // nj, i, j % nj))], nk=nk,
               acc_shape=(tm, tn), epilogue=lambda acc: (acc,))[0]


def _rows_to_8(v):
    tm, d = v.shape
    return jnp.sum(v.reshape(tm // 8, 8, d), axis=0)


def rms_fwd(name, x, gain, deps=(), tm=512):
    s, d = x.shape
    tm = min(tm, s)

    def body(x_ref, g_ref, *rest):
        h_ref = rest[-1]
        xv = x_ref[...]
        r = lax.rsqrt(jnp.mean(xv * xv, axis=-1, keepdims=True) + RMS_EPS)
        h_ref[...] = (xv * r * g_ref[...]).astype(BF16)

    row = pl.BlockSpec((tm, d), lambda i: (i, 0))
    return pl.pallas_call(
        body, name=name, grid=(s // tm,),
        in_specs=[row, pl.BlockSpec((1, d), lambda i: (0, 0)), *[ANY] * len(deps)], out_specs=row,
        out_shape=jax.ShapeDtypeStruct((s, d), BF16), compiler_params=_params("parallel"),
    )(x, gain.reshape(1, d), *deps)


def rms_bwd(name, x, gain, dh, dres, deps=(), tm=512):
    s, d = x.shape
    tm = min(tm, s)

    def body(x_ref, g_ref, dh_ref, dres_ref, *rest):
        dx_ref, dxb_ref, dg_ref = rest[-3:]
        xv = x_ref[...]
        r = lax.rsqrt(jnp.mean(xv * xv, axis=-1, keepdims=True) + RMS_EPS)
        xhat = xv * r
        dhv = dh_ref[...]
        dxhat = dhv * g_ref[...]
        dx = dres_ref[...] + r * (dxhat - xhat * jnp.mean(dxhat * xhat, axis=-1, keepdims=True))
        dx_ref[...] = dx
        dxb_ref[...] = dx.astype(BF16)

        @pl.when(pl.program_id(0) == 0)
        def _():
            dg_ref[...] = jnp.zeros_like(dg_ref)

        dg_ref[...] += _rows_to_8(dhv * xhat)

    row = pl.BlockSpec((tm, d), lambda i: (i, 0))
    return pl.pallas_call(
        body, name=name, grid=(s // tm,),
        in_specs=[row, pl.BlockSpec((1, d), lambda i: (0, 0)), row, row, *[ANY] * len(deps)],
        out_specs=[row, row, pl.BlockSpec((8, d), lambda i: (0, 0))],
        out_shape=[jax.ShapeDtypeStruct((s, d), F32), jax.ShapeDtypeStruct((s, d), BF16),
                   jax.ShapeDtypeStruct((8, d), F32)],
        compiler_params=_params("arbitrary"),
    )(x, gain.reshape(1, d), dh, dres, *deps)


def loss_head(name, x, gain, target, tm=512):
    s, d = x.shape
    tm = min(tm, s)

    def body(x_ref, g_ref, t_ref, loss_ref, dx_ref, dxb_ref, dg_ref):
        xv = x_ref[...]
        g = g_ref[...]
        r = lax.rsqrt(jnp.mean(xv * xv, axis=-1, keepdims=True) + RMS_EPS)
        xhat = xv * r
        err = xhat * g - t_ref[...]
        dy = err * (1.0 / d)
        dxhat = dy * g
        dx = r * (dxhat - xhat * jnp.mean(dxhat * xhat, axis=-1, keepdims=True))
        dx_ref[...] = dx
        dxb_ref[...] = dx.astype(BF16)

        @pl.when(pl.program_id(0) == 0)
        def _():
            dg_ref[...] = jnp.zeros_like(dg_ref)
            loss_ref[...] = jnp.zeros_like(loss_ref)

        dg_ref[...] += _rows_to_8(dy * xhat)
        loss_ref[...] += _rows_to_8(err * err) * (0.5 / d)

    row = pl.BlockSpec((tm, d), lambda i: (i, 0))
    vec = pl.BlockSpec((8, d), lambda i: (0, 0))
    return pl.pallas_call(
        body, name=name, grid=(s // tm,),
        in_specs=[row, pl.BlockSpec((1, d), lambda i: (0, 0)), row],
        out_specs=[vec, row, row, vec],
        out_shape=[jax.ShapeDtypeStruct((8, d), F32), jax.ShapeDtypeStruct((s, d), F32),
                   jax.ShapeDtypeStruct((s, d), BF16), jax.ShapeDtypeStruct((8, d), F32)],
        compiler_params=_params("arbitrary"),
    )(x, gain.reshape(1, d), target)


def _half_masks(dtype):
    lane = lax.broadcasted_iota(jnp.int32, (1, LANES), 1)
    lo = (lane < HEAD_DIM).astype(dtype)
    return lo, (1 - lo).astype(dtype)


def _swap_halves(v):
    return pltpu.roll(v, HEAD_DIM, axis=1)


SWA_BLOCKS = 8


def _swa_probs(s_raw, sink, slope, first):
    t = s_raw.shape[0]
    row = lax.broadcasted_iota(jnp.int32, (t, 2 * WINDOW), 0)
    col = lax.broadcasted_iota(jnp.int32, (t, 2 * WINDOW), 1)
    dist = row + WINDOW - col
    valid = (dist >= 0) & (dist < WINDOW)
    if first is not None:
        valid = valid & ((col >= WINDOW) | jnp.logical_not(first))
    s = jnp.where(valid, s_raw - slope * dist.astype(F32), NEG_BIG)
    m = jnp.maximum(jnp.max(s, axis=-1, keepdims=True), sink)
    e = jnp.exp(s - m)
    e_sink = jnp.exp(sink - m)
    inv = 1.0 / (jnp.sum(e, axis=-1, keepdims=True) + e_sink)
    return e * inv, e_sink * inv


def _swa_band(kp_ref, kc_ref, vp_ref, vc_ref, kv_half):
    lo_b, hi_b = _half_masks(F32)
    sel = jnp.where(kv_half == 0, lo_b, hi_b)
    k = jnp.concatenate([kp_ref[...], kc_ref[...]], axis=0).astype(F32) * sel
    v = jnp.concatenate([vp_ref[...], vc_ref[...]], axis=0).astype(F32) * sel
    k = (k + _swap_halves(k)).astype(BF16)
    v = (v + _swap_halves(v)).astype(BF16)
    return k, v, sel


def _swa_specs(nq):
    t = WINDOW
    q_spec = pl.BlockSpec((nq * t, LANES), lambda j, n, *_: (n, j))
    q_blocks = N_HEADS // 2

    def prev(off):
        return pl.BlockSpec((t, LANES), lambda j, n, *_: (jnp.maximum(nq * n - 1, 0), q_blocks + off + j // 8))

    def cur(off):
        return pl.BlockSpec((nq * t, LANES), lambda j, n, *_: (n, q_blocks + off + j // 8))

    kv_blocks = N_KV_A // 2
    return q_spec, [prev(0), cur(0), prev(kv_blocks), cur(kv_blocks)]


def _rows(a, b, n=1):
    return a[b * WINDOW:(b + n) * WINDOW]


def swa_fwd(name, qkv, sinks, slopes):
    s = qkv.shape[0]
    nq = min(SWA_BLOCKS, s // WINDOW)

    def body(sink_ref, slope_ref, q_ref, kp_ref, kc_ref, vp_ref, vc_ref, o_ref):
        j, n = pl.program_id(0), pl.program_id(1)
        k_all, v_all, _ = _swa_band(kp_ref, kc_ref, vp_ref, vc_ref, (j // 4) % 2)
        q = q_ref[...] * ATTN_SCALE
        masks = _half_masks(BF16)
        chains = [(b, hq) for b in range(nq) for hq in range(2)]
        s_raw = {(b, hq): _dot(_rows(q, b) * masks[hq], _rows(k_all, b, 2), "nt") for b, hq in chains}
        p = {(b, hq): _swa_probs(s_raw[(b, hq)], sink_ref[2 * j + hq], slope_ref[2 * j + hq],
                                 (n == 0) if b == 0 else None)[0] for b, hq in chains}
        outs = [sum(_dot(p[(b, hq)].astype(BF16), _rows(v_all, b, 2) * masks[hq]) for hq in range(2))
                for b in range(nq)]
        o_ref[...] = jnp.concatenate(outs, axis=0).astype(BF16)

    q_spec, kv_specs = _swa_specs(nq)
    return pl.pallas_call(
        body, name=name,
        grid_spec=pltpu.PrefetchScalarGridSpec(
            num_scalar_prefetch=2, grid=(N_HEADS // 2, s // (nq * WINDOW)),
            in_specs=[q_spec, *kv_specs], out_specs=q_spec),
        out_shape=jax.ShapeDtypeStruct((s, N_HEADS * HEAD_DIM), BF16),
        compiler_params=_params("parallel", "parallel"),
    )(sinks, slopes, qkv, qkv, qkv, qkv, qkv)


def swa_bwd(name, qkv, do, sinks, slopes):
    s = qkv.shape[0]
    t = WINDOW
    nq = min(SWA_BLOCKS, s // t)

    def body(sink_ref, slope_ref, q_ref, kp_ref, kc_ref, vp_ref, vc_ref, do_ref, dq_ref, dk_ref, dv_ref, ds_ref):
        j, n = pl.program_id(0), pl.program_id(1)
        k_all, v_all, sel = _swa_band(kp_ref, kc_ref, vp_ref, vc_ref, (j // 4) % 2)
        q = q_ref[...] * ATTN_SCALE
        do_v = do_ref[...]
        masks = _half_masks(BF16)

        @pl.when((j % 8 == 0) & (n == 0))
        def _():
            dk_ref[...] = jnp.zeros_like(dk_ref)
            dv_ref[...] = jnp.zeros_like(dv_ref)

        @pl.when(n == 0)
        def _():
            ds_ref[...] = jnp.zeros_like(ds_ref)

        chains = [(b, hq) for b in range(nq) for hq in range(2)]
        qm = {(b, hq): _rows(q, b) * masks[hq] for b, hq in chains}
        dom = {(b, hq): _rows(do_v, b) * masks[hq] for b, hq in chains}
        s_raw = {ch: _dot(qm[ch], _rows(k_all, ch[0], 2), "nt") for ch in chains}
        dp = {ch: _dot(dom[ch], _rows(v_all, ch[0], 2), "nt") for ch in chains}
        pb, dsc = {}, {}
        dsink = [jnp.zeros((), F32), jnp.zeros((), F32)]
        for ch in chains:
            b, hq = ch
            p, p_sink = _swa_probs(s_raw[ch], sink_ref[2 * j + hq], slope_ref[2 * j + hq], (n == 0) if b == 0 else None)
            delta = jnp.sum(p * dp[ch], axis=-1, keepdims=True)
            dsc[ch] = (p * (dp[ch] - delta)).astype(BF16)
            pb[ch] = p.astype(BF16)
            dsink[hq] = dsink[hq] - jnp.sum(p_sink * delta)
        for hq in range(2):
            ds_ref[hq:hq + 1, :] += jnp.zeros((1, LANES), F32) + dsink[hq]
        dq, dk, dv = [], [], []
        for b in range(nq):
            dq.append(sum(_dot(dsc[(b, hq)], _rows(k_all, b, 2) * masks[hq]) for hq in range(2)))
            dk_b = sum(_dot(dsc[(b, hq)], qm[(b, hq)], "tn") for hq in range(2))
            dv_b = sum(_dot(pb[(b, hq)], dom[(b, hq)], "tn") for hq in range(2))
            dk.append((dk_b + _swap_halves(dk_b)) * sel)
            dv.append((dv_b + _swap_halves(dv_b)) * sel)
        dq_ref[...] = (jnp.concatenate(dq, axis=0) * ATTN_SCALE).astype(BF16)

        @pl.when(n == 0)
        def _():
            dk_ref[pl.ds(0, t), :] += dk[0][t:]
            dv_ref[pl.ds(0, t), :] += dv[0][t:]

        @pl.when(n > 0)
        def _():
            start = pl.multiple_of((nq * n - 1) * t, t)
            dk_ref[pl.ds(start, 2 * t), :] += dk[0]
            dv_ref[pl.ds(start, 2 * t), :] += dv[0]

        for b in range(1, nq):
            start = pl.multiple_of((nq * n + b - 1) * t, t)
            dk_ref[pl.ds(start, 2 * t), :] += dk[b]
            dv_ref[pl.ds(start, 2 * t), :] += dv[b]

    q_spec, kv_specs = _swa_specs(nq)
    kv_out = pl.BlockSpec((s, LANES), lambda j, n, *_: (0, j // 8))
    return pl.pallas_call(
        body, name=name,
        grid_spec=pltpu.PrefetchScalarGridSpec(
            num_scalar_prefetch=2, grid=(N_HEADS // 2, s // (nq * t)),
            in_specs=[q_spec, *kv_specs, q_spec],
            out_specs=[q_spec, kv_out, kv_out, pl.BlockSpec((None, 8, LANES), lambda j, n, *_: (j, 0, 0))]),
        out_shape=[jax.ShapeDtypeStruct((s, N_HEADS * HEAD_DIM), BF16),
                   jax.ShapeDtypeStruct((s, N_KV_A * HEAD_DIM), F32),
                   jax.ShapeDtypeStruct((s, N_KV_A * HEAD_DIM), F32),
                   jax.ShapeDtypeStruct((N_HEADS // 2, 8, LANES), F32)],
        compiler_params=_params("arbitrary", "arbitrary"),
    )(sinks, slopes, qkv, qkv, qkv, qkv, qkv, do)


SB_TILE = 256


def _split_k(v):
    hi = v.astype(BF16)
    lo = (v - hi.astype(F32)).astype(BF16)
    return jnp.concatenate([hi, lo], axis=1)


def _tri2(t, inclusive):
    r = lax.broadcasted_iota(jnp.int32, (2 * t, t), 0)
    c = lax.broadcasted_iota(jnp.int32, (2 * t, t), 1)
    r = jnp.where(r >= t, r - t, r)
    return ((r >= c) if inclusive else (r > c)).astype(BF16)


def _sb_logs(z, before):
    neg_abs = lax.bitcast_convert_type(lax.bitcast_convert_type(z, jnp.uint32) | jnp.uint32(0x80000000), F32)
    l = jnp.log(1.0 + jnp.exp(neg_abs))
    lb = jnp.minimum(z, 0.0) - l
    lm = lb - z
    if before is not None:
        lm = jnp.where(before, lm, 0.0)
    return lb, lm


def _sb_weights(lb, sfx, c_lm, before):
    a = jnp.exp(lb + sfx + c_lm)
    if before is not None:
        a = jnp.where(before, a, 0.0)
    return a


def _sb_sweep(tile, i, init, diag):
    carry = lax.cond(i > 0, lambda: tile([i, i - 1], init, diag), lambda: tile([i], init, diag))
    rest = jnp.maximum(i - 1, 0)
    odd = rest % 2
    carry = lax.cond(odd == 1, lambda: tile([i - 2], carry, None), lambda: carry)
    base = i - 2 - odd
    return lax.fori_loop(0, rest // 2, lambda n, cr: tile([base - 2 * n, base - 2 * n - 1], cr, None), carry)


def _sb_specs(s, t):
    hp = N_HEADS // 2
    q_spec = pl.BlockSpec((t, LANES), lambda h, i: (i, h))
    k_spec = pl.BlockSpec((s, LANES), lambda h, i: (0, hp + h))
    v_spec = pl.BlockSpec((s, LANES), lambda h, i: (0, 2 * hp + h))
    return q_spec, k_spec, v_spec


def sb_fwd(name, qkv, t=SB_TILE):
    s = qkv.shape[0]
    t = min(t, s)

    def body(q_ref, k_ref, v_ref, ob_ref, of_ref):
        i = pl.program_id(1)
        masks = _half_masks(BF16)
        q = q_ref[...] * ATTN_SCALE
        qm = [q * masks[0], q * masks[1]]
        tri_x = _tri2(t, False)
        r = lax.broadcasted_iota(jnp.int32, (t, t), 0)
        c = lax.broadcasted_iota(jnp.int32, (t, t), 1)
        diag = c < r

        def tile(js, carry, before):
            c0, c1, acc = carry
            cs = [c0, c1]
            kj = [k_ref[pl.ds(pl.multiple_of(j * t, t), t), :] for j in js]
            vj = [v_ref[pl.ds(pl.multiple_of(j * t, t), t), :] for j in js]
            chains = [(h, b) for b in range(len(js)) for h in range(2)]
            z = {(h, b): _dot(qm[h], kj[b], "nt") for h, b in chains}
            lb, sfx, c_at = {}, {}, {}
            for ch in chains:
                h, b = ch
                lb[ch], lm = _sb_logs(z[ch], before if b == 0 else None)
                sfx[ch] = _dot(_split_k(lm), tri_x)
                c_at[ch] = cs[h]
                cs[h] = cs[h] + jnp.sum(lm, axis=-1, keepdims=True)
            for ch in chains:
                h, b = ch
                a = _sb_weights(lb[ch], sfx[ch], c_at[ch], before if b == 0 else None)
                acc = acc + _dot(a.astype(BF16), vj[b] * masks[h])
            return cs[0], cs[1], acc

        zero = jnp.zeros((t, 1), F32)
        carry = _sb_sweep(tile, i, (zero, zero, jnp.zeros((t, LANES), F32)), diag)
        ob_ref[...] = carry[2].astype(BF16)
        of_ref[...] = carry[2]

    q_spec, k_spec, v_spec = _sb_specs(s, t)
    return pl.pallas_call(
        body, name=name, grid=(N_HEADS // 2, s // t),
        in_specs=[q_spec, k_spec, v_spec], out_specs=[q_spec, q_spec],
        out_shape=[jax.ShapeDtypeStruct((s, N_HEADS * HEAD_DIM), BF16),
                   jax.ShapeDtypeStruct((s, N_HEADS * HEAD_DIM), F32)],
        compiler_params=_params("parallel", "parallel"),
    )(qkv, qkv, qkv)


def sb_bwd(name, qkv, o_f32, do, t=SB_TILE):
    s = qkv.shape[0]
    t = min(t, s)

    def body(q_ref, k_ref, v_ref, o_ref, do_ref, dq_ref, dk_ref, dv_ref):
        i = pl.program_id(1)
        masks = _half_masks(BF16)
        fmasks = _half_masks(F32)
        q = q_ref[...] * ATTN_SCALE
        do_v = do_ref[...]
        qm = [q * masks[0], q * masks[1]]
        dom = [do_v * masks[0], do_v * masks[1]]
        prod = do_v.astype(F32) * o_ref[...]
        delta = [jnp.sum(prod * fmasks[h], axis=-1, keepdims=True) for h in range(2)]
        tri_x = _tri2(t, False)
        tri_i = _tri2(t, True)
        r = lax.broadcasted_iota(jnp.int32, (t, t), 0)
        c = lax.broadcasted_iota(jnp.int32, (t, t), 1)
        diag = c < r

        @pl.when(i == 0)
        def _():
            dk_ref[...] = jnp.zeros_like(dk_ref)
            dv_ref[...] = jnp.zeros_like(dv_ref)

        def tile(js, carry, before):
            cl0, cl1, cd0, cd1, dq = carry
            cl, cd = [cl0, cl1], [cd0, cd1]
            starts = [pl.multiple_of(j * t, t) for j in js]
            kj = [k_ref[pl.ds(st, t), :] for st in starts]
            vj = [v_ref[pl.ds(st, t), :] for st in starts]
            chains = [(h, b) for b in range(len(js)) for h in range(2)]
            z = {(h, b): _dot(qm[h], kj[b], "nt") for h, b in chains}
            da = {(h, b): _dot(dom[h], vj[b], "nt") for h, b in chains}
            lb, sfx, cl_at, cd_at, ab, de, dsfx = {}, {}, {}, {}, {}, {}, {}
            for ch in chains:
                h, b = ch
                lb[ch], lm = _sb_logs(z[ch], before if b == 0 else None)
                sfx[ch] = _dot(_split_k(lm), tri_x)
                cl_at[ch] = cl[h]
                cl[h] = cl[h] + jnp.sum(lm, axis=-1, keepdims=True)
            dv = [jnp.zeros((t, LANES), F32) for _ in js]
            dk = [jnp.zeros((t, LANES), F32) for _ in js]
            for ch in chains:
                h, b = ch
                a = _sb_weights(lb[ch], sfx[ch], cl_at[ch], before if b == 0 else None)
                ab[ch] = a.astype(BF16)
                de[ch] = da[ch] * ab[ch].astype(F32)
                dsfx[ch] = _dot(_split_k(de[ch]), tri_i)
                dv[b] = dv[b] + _dot(ab[ch], dom[h], "tn")
                cd_at[ch] = cd[h]
                cd[h] = cd[h] + jnp.sum(de[ch], axis=-1, keepdims=True)
            for ch in chains:
                h, b = ch
                farther = delta[h] - cd_at[ch] - dsfx[ch]
                sig = jnp.exp(lb[ch])
                dz = de[ch] - sig * (de[ch] + farther)
                if before is not None and b == 0:
                    dz = jnp.where(before, dz, 0.0)
                dzb = dz.astype(BF16)
                dq = dq + _dot(dzb, kj[b] * masks[h])
                dk[b] = dk[b] + _dot(dzb, qm[h], "tn")
            for b, st in enumerate(starts):
                dk_ref[pl.ds(st, t), :] += dk[b]
                dv_ref[pl.ds(st, t), :] += dv[b]
            return cl[0], cl[1], cd[0], cd[1], dq

        zero = jnp.zeros((t, 1), F32)
        carry = _sb_sweep(tile, i, (zero, zero, zero, zero, jnp.zeros((t, LANES), F32)), diag)
        dq_ref[...] = (carry[4] * ATTN_SCALE).astype(BF16)

    q_spec, k_spec, v_spec = _sb_specs(s, t)
    kv_out = pl.BlockSpec((s, LANES), lambda h, i: (0, h))
    width = N_HEADS * HEAD_DIM
    return pl.pallas_call(
        body, name=name, grid=(N_HEADS // 2, s // t),
        in_specs=[q_spec, k_spec, v_spec, q_spec, q_spec], out_specs=[q_spec, kv_out, kv_out],
        out_shape=[jax.ShapeDtypeStruct((s, width), BF16), jax.ShapeDtypeStruct((s, width), F32),
                   jax.ShapeDtypeStruct((s, width), F32)],
        compiler_params=_params("parallel", "arbitrary"),
    )(qkv, qkv, qkv, o_f32, do)


def adamw(name, w, g, m, v, tm=256):
    shape = w.shape
    c = shape[-1]
    rows = math.prod(shape[:-1])
    tm = min(tm, rows)

    def body(w_ref, g_ref, m_ref, v_ref, d_ref, mo_ref, vo_ref):
        gv = g_ref[...]
        m2 = ADAM_B1 * m_ref[...] + (1.0 - ADAM_B1) * gv
        v2 = ADAM_B2 * v_ref[...] + (1.0 - ADAM_B2) * (gv * gv)
        m_hat = m2 / (1.0 - ADAM_B1 ** ADAM_STEP)
        v_hat = v2 / (1.0 - ADAM_B2 ** ADAM_STEP)
        d_ref[...] = -ADAM_LR * (m_hat / (jnp.sqrt(v_hat) + ADAM_EPS) + ADAM_WD * w_ref[...])
        mo_ref[...] = m2
        vo_ref[...] = v2

    blk = pl.BlockSpec((tm, c), lambda i: (i, 0))
    outs = pl.pallas_call(
        body, name=name, grid=(rows // tm,), in_specs=[blk] * 4, out_specs=[blk] * 3,
        out_shape=[jax.ShapeDtypeStruct((rows, c), F32)] * 3, compiler_params=_params("parallel"),
    )(*[t.reshape(rows, c) for t in (w, g, m, v)])
    return [o.reshape(shape) for o in outs]


HBM = pl.BlockSpec(memory_space=pltpu.HBM)


def _place():
    x, y, c = lax.axis_index("x"), lax.axis_index("y"), lax.axis_index("c")
    return x, y, c, [(1 - x, y), (x, 1 - y), (1 - x, 1 - y)]


def _remote(src, dst, send, recv, dev):
    return pltpu.make_async_remote_copy(src_ref=src, dst_ref=dst, send_sem=send, recv_sem=recv, device_id=dev,
                                        device_id_type=MESH)


SEM = pl.BlockSpec(memory_space=pltpu.SEMAPHORE)
ANY = pl.BlockSpec(memory_space=pl.ANY)
DATAFLOW = pltpu.SideEffectType.DATAFLOW_SIDE_EFFECTING


def _in_hbm(v):
    return pltpu.with_memory_space_constraint(v, pltpu.HBM)


def split_start(name, bufs, n_copies, sends, after=()):
    nb, na = len(bufs), len(after)

    def body(*refs):
        send, recv = refs[nb + na], refs[nb + na + 1]
        for cp in sends(refs[:nb], send, recv):
            cp.start()
        refs[-1][...] = jnp.zeros_like(refs[-1])

    outs = pl.pallas_call(
        body, name=name,
        in_specs=[HBM] * nb + [ANY] * na,
        out_shape=(pltpu.SemaphoreType.DMA((n_copies,)), pltpu.SemaphoreType.DMA((n_copies,)),
                   *[pltpu.HBM(b.shape, b.dtype) for b in bufs], jax.ShapeDtypeStruct((8, LANES), F32)),
        out_specs=(SEM, SEM, *[HBM] * nb, pl.BlockSpec(memory_space=pltpu.VMEM)),
        input_output_aliases={i: 2 + i for i in range(nb)},
        compiler_params=pltpu.CompilerParams(has_side_effects=DATAFLOW),
    )(*[_in_hbm(b) for b in bufs], *after)
    return outs[0], outs[1], list(outs[2:2 + nb]), outs[-1]


def split_wait(name, send_sems, recv_sems, bufs, sends, arrivals, after):
    nb, na = len(bufs), len(after)

    def body(*refs):
        send, recv = refs[nb], refs[nb + 1]
        for cp in sends(refs[:nb], send, recv):
            cp.wait_send()
        for cp in arrivals(refs[:nb], send, recv):
            cp.wait_recv()

    outs = pl.pallas_call(
        body, name=name,
        in_specs=[HBM] * nb + [SEM, SEM] + [ANY] * na,
        out_shape=tuple(pltpu.HBM(b.shape, b.dtype) for b in bufs), out_specs=tuple([HBM] * nb),
        input_output_aliases={i: i for i in range(nb)},
        compiler_params=pltpu.CompilerParams(has_side_effects=DATAFLOW),
    )(*bufs, send_sems, recv_sems, *after)
    return list(outs)


def _gather_plan(n):
    def sends(refs, send, recv):
        x, y, c, chips = _place()
        me = 2 * x + y
        return [_remote(refs[t].at[me], refs[t].at[me], send.at[3 * t + k], recv.at[3 * t + k], (px, py, c))
                for t in range(n) for k, (px, py) in enumerate(chips)]

    def arrivals(refs, send, recv):
        x, y, c, chips = _place()
        return [_remote(refs[t].at[2 * px + py], refs[t].at[2 * px + py], send.at[3 * t + k], recv.at[3 * t + k],
                        (px, py, c)) for t in range(n) for k, (px, py) in enumerate(chips)]

    return 3 * n, sends, arrivals


def _swap_plan(n):
    def copies(refs, send, recv):
        x, y, c, _ = _place()
        return [_remote(refs[t].at[s, 1 - c], refs[n + t].at[s], send.at[N_CHIPS * t + s], recv.at[N_CHIPS * t + s],
                        (x, y, 1 - c)) for t in range(n) for s in range(N_CHIPS)]

    return N_CHIPS * n, copies, copies


def _scatter_plan(n):
    def sends(refs, send, recv):
        x, y, c, chips = _place()
        me = 2 * x + y
        return [_remote(refs[t].at[2 * px + py], refs[n + t].at[me], send.at[3 * t + k], recv.at[3 * t + k],
                        (px, py, c)) for t in range(n) for k, (px, py) in enumerate(chips)]

    def arrivals(refs, send, recv):
        x, y, c, chips = _place()
        return [_remote(refs[t].at[2 * px + py], refs[n + t].at[2 * px + py], send.at[3 * t + k], recv.at[3 * t + k],
                        (px, py, c)) for t in range(n) for k, (px, py) in enumerate(chips)]

    return 3 * n, sends, arrivals


def cast_into_slot(name, w, layer, chip, tm=256):
    _, r, c = w.shape
    tm = min(tm, r)

    def body(chip_ref, w_ref, o_ref):
        o_ref[...] = w_ref[...].astype(BF16)

    return pl.pallas_call(
        body, name=name,
        grid_spec=pltpu.PrefetchScalarGridSpec(
            num_scalar_prefetch=1, grid=(r // tm,),
            in_specs=[pl.BlockSpec((None, tm, c), lambda i, chip_ref: (layer, i, 0))],
            out_specs=pl.BlockSpec((None, tm, c), lambda i, chip_ref: (chip_ref[0], i, 0))),
        out_shape=jax.ShapeDtypeStruct((N_CHIPS, r, c), BF16), compiler_params=_params("parallel"),
    )(chip, w)


def join_halves(name, f):
    n = f.shape[0]

    def body(f_ref, o_ref, send, recv):
        x, y, c, _ = _place()
        sib = (x, y, 1 - c)
        sends = [_remote(f_ref.at[l, c], o_ref.at[l, c], send.at[l], recv.at[l], sib) for l in range(n)]
        for cp in sends:
            cp.start()
        for l in range(n):
            _remote(f_ref.at[l, 1 - c], o_ref.at[l, 1 - c], send.at[l], recv.at[l], sib).wait_recv()
        for cp in sends:
            cp.wait_send()

    return pl.pallas_call(
        body, name=name, in_specs=[HBM], out_specs=HBM, out_shape=jax.ShapeDtypeStruct(f.shape, f.dtype),
        input_output_aliases={0: 0},
        scratch_shapes=[pltpu.SemaphoreType.DMA((n,)), pltpu.SemaphoreType.DMA((n,))],
    )(f)


def add_sibling_half(name, g, a, core, tm=256):
    _, _, rh, c = g.shape
    tm = min(tm, rh)

    def body(core_ref, g_ref, a_ref, o_ref):
        o_ref[...] = (g_ref[...].astype(F32) + a_ref[...].astype(F32)).astype(BF16)

    return pl.pallas_call(
        body, name=name,
        grid_spec=pltpu.PrefetchScalarGridSpec(
            num_scalar_prefetch=1, grid=(N_CHIPS, rh // tm),
            in_specs=[pl.BlockSpec((None, None, tm, c), lambda s, i, core_ref: (s, core_ref[0], i, 0)),
                      pl.BlockSpec((None, tm, c), lambda s, i, core_ref: (s, i, 0))],
            out_specs=pl.BlockSpec((None, tm, c), lambda s, i, core_ref: (s, i, 0))),
        out_shape=jax.ShapeDtypeStruct(a.shape, BF16), compiler_params=_params("parallel", "parallel"),
    )(core, g, a)


def sum_chips_into(name, p, b, f, layer, chip, core, tm=256):
    _, rh, c = b.shape
    tm = min(tm, rh)

    def body(chip_ref, core_ref, p_ref, b_ref, f_ref, o_ref):
        acc = jnp.zeros((tm, c), F32)
        for s in range(N_CHIPS):
            acc = acc + jnp.where(chip_ref[0] == s, p_ref[s].astype(F32), b_ref[s].astype(F32))
        o_ref[...] = acc

    slots = pl.BlockSpec((N_CHIPS, tm, c), lambda i, chip_ref, core_ref: (0, i, 0))
    return pl.pallas_call(
        body, name=name,
        grid_spec=pltpu.PrefetchScalarGridSpec(
            num_scalar_prefetch=2, grid=(rh // tm,), in_specs=[slots, slots, ANY],
            out_specs=pl.BlockSpec((None, None, tm, c), lambda i, chip_ref, core_ref: (layer, core_ref[0], i, 0))),
        out_shape=jax.ShapeDtypeStruct(f.shape, F32), input_output_aliases={4: 0},
        compiler_params=_params("parallel"),
    )(chip, core, p, b, f)


N_DEV = 8


def allreduce_small(name, parts):
    p, _, d = parts.shape
    m_per = p * 8

    def body(x_ref, out_ref, all_ref, send_sems, recv_sems, local_sem):
        x, y, c, chips = _place()
        me, sibling = (x, y, c), (x, y, 1 - c)

        def rows(px, py, pc):
            return all_ref.at[pl.ds((4 * px + 2 * py + pc) * m_per, m_per), :]

        def copy(k, block, to, src=None):
            return _remote(rows(*block) if src is None else src, rows(*block), send_sems.at[k], recv_sems.at[k], to)

        mine = pltpu.make_async_copy(x_ref, rows(*me), local_sem)
        mine.start()
        first = [copy(0, me, sibling, src=x_ref)]
        first += [copy(1 + j, me, (*chip, c), src=x_ref) for j, chip in enumerate(chips)]
        for cp in first:
            cp.start()
        passed = [copy(4 + j, (*chip, c), sibling) for j, chip in enumerate(chips)]
        for j, chip in enumerate(chips):
            copy(1 + j, (*chip, c), me).wait_recv()
            passed[j].start()
        copy(0, sibling, me).wait_recv()
        for j, chip in enumerate(chips):
            copy(4 + j, (*chip, 1 - c), me).wait_recv()
        for cp in first + passed:
            cp.wait_send()
        mine.wait()
        acc = all_ref[pl.ds(0, m_per), :]
        for dev in range(1, N_DEV):
            acc = acc + all_ref[pl.ds(dev * m_per, m_per), :]
        out_ref[...] = jnp.sum(acc.reshape(p, 8, d), axis=1)

    vmem = pl.BlockSpec(memory_space=pltpu.VMEM)
    return pl.pallas_call(
        body, name=name, in_specs=[vmem], out_specs=vmem,
        out_shape=jax.ShapeDtypeStruct((p, d), F32),
        scratch_shapes=[pltpu.VMEM((N_DEV * m_per, d), F32), pltpu.SemaphoreType.DMA((7,)),
                        pltpu.SemaphoreType.DMA((7,)), pltpu.SemaphoreType.DMA],
        compiler_params=pltpu.CompilerParams(vmem_limit_bytes=VMEM_LIMIT),
    )(parts.reshape(m_per, d))


def _empty(shape, dtype):
    return _in_hbm(lax.empty(shape, dtype))


class _GradExchange:
    def __init__(self, layer, kinds, grads, chip, core):
        self.layer, self.kinds, self.chip, self.core = layer, kinds, chip, core
        self.g4 = [g.reshape(N_CHIPS, 2, g.shape[1] // 2, g.shape[2]) for g in grads]
        self.n = len(grads)

    def start_swap(self, after):
        n_copies, self.swap_sends, self.swap_arrivals = _swap_plan(self.n)
        lands = [_empty((N_CHIPS,) + g.shape[2:], BF16) for g in self.g4]
        self.swap = split_start(f"swap_start_l{self.layer}", self.g4 + lands, n_copies, self.swap_sends, after)
        return self.swap[3]

    def swap_to_scatter(self, after):
        send, recv, bufs, _ = self.swap
        bufs = split_wait(f"swap_wait_l{self.layer}", send, recv, bufs, self.swap_sends, self.swap_arrivals, after)
        g4, lands = bufs[:self.n], bufs[self.n:]
        self.p = [add_sibling_half(f"add_l{self.layer}_{k}", g4[t], lands[t], self.core)
                  for t, (k, _) in enumerate(self.kinds)]
        n_copies, self.sc_sends, self.sc_arrivals = _scatter_plan(self.n)
        lands = [_empty(p.shape, BF16) for p in self.p]
        self.scatter = split_start(f"scatter_start_l{self.layer}", self.p + lands, n_copies, self.sc_sends)
        return self.scatter[3]

    def finish(self, f, after):
        send, recv, bufs, _ = self.scatter
        bufs = split_wait(f"scatter_wait_l{self.layer}", send, recv, bufs, self.sc_sends, self.sc_arrivals, after)
        p, lands = bufs[:self.n], bufs[self.n:]
        for t, (kind, l) in enumerate(self.kinds):
            f[kind] = sum_chips_into(f"sum_l{self.layer}_{kind}", p[t], lands[t], f[kind], l, self.chip, self.core)


def _relu2(acc):
    r = jnp.maximum(acc, 0.0)
    return acc, r * r


def _relu2_bwd(acc, u):
    return (acc * (2.0 * jnp.maximum(u.astype(F32), 0.0)),)


def _same(acc):
    return (acc,)


def kernel(x, a_w_qkv, a_w_o, a_sinks, b_w_qkv, b_w_o, norm_mix, norm_mlp, mlp_w_in, mlp_w_out, final_norm, loss_target, m_a_w_qkv, m_a_w_o, m_a_sinks, m_b_w_qkv, m_b_w_o, m_norm_mix, m_norm_mlp, m_mlp_w_in, m_mlp_w_out, m_final_norm, v_a_w_qkv, v_a_w_o, v_a_sinks, v_b_w_qkv, v_b_w_o, v_norm_mix, v_norm_mlp, v_mlp_w_in, v_mlp_w_out, v_final_norm):
    _, s, d = x.shape
    depth = norm_mix.shape[0]
    width = N_HEADS * HEAD_DIM
    core = lax.axis_index("c").astype(jnp.int32).reshape(1)
    chip = (2 * lax.axis_index("x") + lax.axis_index("y")).astype(jnp.int32).reshape(1)
    slopes = jnp.power(2.0, -8.0 * (jnp.arange(N_HEADS, dtype=F32) + 1.0) / N_HEADS)
    qkv_of = {0: ("a_qkv", a_w_qkv), 1: ("b_qkv", b_w_qkv)}
    o_of = {0: ("a_o", a_w_o), 1: ("b_o", b_w_o)}

    def layer_kinds(i):
        return [(qkv_of[i % 2][0], i // 2), (o_of[i % 2][0], i // 2), ("mlp_in", i), ("mlp_out", i)]

    stacks = {"a_qkv": a_w_qkv, "a_o": a_w_o, "b_qkv": b_w_qkv, "b_o": b_w_o, "mlp_in": mlp_w_in, "mlp_out": mlp_w_out}
    slots = [[cast_into_slot(f"cast_l{i}_{k}", stacks[k], l, chip) for k, l in layer_kinds(i)] for i in range(depth)]
    n_gather, gather_sends, gather_arrivals = _gather_plan(2)

    def start_gather(tag, bufs, after):
        return split_start(f"gather_start_{tag}", bufs, n_gather, gather_sends, after)

    def wait_gather(tag, flight, after):
        return split_wait(f"gather_wait_{tag}", flight[0], flight[1], flight[2], gather_sends, gather_arrivals, after)

    saved, weights = [], []
    xc = x[0]
    flight_a = start_gather("l0a", slots[0][:2], ())
    flight_m = start_gather("l0m", slots[0][2:], (flight_a[3],))
    for i in range(depth):
        mixer, j = i % 2, i // 2
        w_qkv, w_o = wait_gather(f"l{i}a", flight_a, (xc,))
        w_o = w_o.reshape(width, d)
        started, this_m = (flight_m[3],), flight_m
        if i + 1 < depth:
            flight_a = start_gather(f"l{i + 1}a", slots[i + 1][:2], (w_qkv, flight_m[3]))
            flight_m = start_gather(f"l{i + 1}m", slots[i + 1][2:], (flight_a[3],))
            started = (flight_m[3],)
        h = rms_fwd(f"l{i}_norm_mix", xc, norm_mix[i], deps=started)
        qkv = mm_cols(f"l{i}_qkv", h, w_qkv, _same, (BF16,))[0]
        if mixer == 0:
            attn, attn_f32 = swa_fwd(f"l{i}_swa", qkv, a_sinks[j], slopes), None
        else:
            attn, attn_f32 = sb_fwd(f"l{i}_sb", qkv)
        xm = mm_res(f"l{i}_o", attn, w_o, xc)
        w_in, w_out = wait_gather(f"l{i}m", this_m, (xm,))
        w_out = w_out.reshape(-1, d)
        weights.append((w_qkv, w_o, w_in, w_out))
        h2 = rms_fwd(f"l{i}_norm_mlp", xm, norm_mlp[i])
        u, hh = mm_cols(f"l{i}_in", h2, w_in, _relu2, (BF16, BF16))
        xn = mm_res(f"l{i}_out", hh, w_out, xm)
        saved.append((xc, h, qkv, attn, attn_f32, xm, h2, u, hh))
        xc = xn
    loss_rows, dx, dxb, dg_final = loss_head("loss_head", xc, final_norm, loss_target[0])
    loss = lax.psum(jnp.sum(loss_rows), ("x", "y", "c"))

    big = {"a_qkv": (a_w_qkv, m_a_w_qkv, v_a_w_qkv), "a_o": (a_w_o, m_a_w_o, v_a_w_o),
           "b_qkv": (b_w_qkv, m_b_w_qkv, v_b_w_qkv), "b_o": (b_w_o, m_b_w_o, v_b_w_o),
           "mlp_in": (mlp_w_in, m_mlp_w_in, v_mlp_w_in), "mlp_out": (mlp_w_out, m_mlp_w_out, v_mlp_w_out)}
    f = {k: _empty((w.shape[0], 2, w.shape[1] // 2, w.shape[2]), F32) for k, (w, _, _) in big.items()}
    dg_mix, dg_mlp, dsinks = [], [], []
    prev_a, started = None, ()
    for i in reversed(range(depth)):
        mixer, j = i % 2, i // 2
        xin, h, qkv, attn, attn_f32, xm, h2, u, hh = saved[i]
        w_qkv, w_o, w_in, w_out = weights[i]
        kinds = layer_kinds(i)
        du = mm_nt(f"l{i}_d_hidden", dxb, w_out, _relu2_bwd, (u,), deps=started)
        g_out = mm_tn(f"l{i}_g_out", hh, dxb).reshape(N_CHIPS, -1, d)
        started = (g_out,) if prev_a is None else (g_out, prev_a.swap_to_scatter((g_out,)))
        dh2 = mm_nt_cols(f"l{i}_d_h2", du, w_in, deps=started)
        g_in = mm_tn_cols(f"l{i}_g_in", h2, du)
        cur_m = _GradExchange(f"{i}m", kinds[2:], [g_in, g_out], chip, core)
        started = (g_in, cur_m.start_swap((g_in,)))
        dxm, dxmb, dg = rms_bwd(f"l{i}_norm_mlp_bwd", xm, norm_mlp[i], dh2, dx, deps=started)
        dg_mlp.append(dg)
        g_o = mm_tn(f"l{i}_g_o", attn, dxmb).reshape(N_CHIPS, -1, d)
        dattn = mm_nt(f"l{i}_d_attn", dxmb, w_o, _same, deps=(g_o, cur_m.swap_to_scatter((g_o,))))
        if mixer == 0:
            dq, dk, dv, dsk = swa_bwd(f"l{i}_swa_bwd", qkv, dattn, a_sinks[j], slopes)
            dsinks.append(dsk[:, :2, 0].reshape(N_HEADS))
        else:
            dq, dk, dv = sb_bwd(f"l{i}_sb_bwd", qkv, attn_f32, dattn)
        dqkv = jnp.concatenate([dq, dk.astype(BF16), dv.astype(BF16)], axis=1)
        dh = mm_nt_cols(f"l{i}_d_h", dqkv, w_qkv)
        g_qkv = mm_tn_cols(f"l{i}_g_qkv", h, dqkv)
        dx, dxb, dg = rms_bwd(f"l{i}_norm_mix_bwd", xin, norm_mix[i], dh, dxm, deps=(g_qkv,))
        dg_mix.append(dg)
        if prev_a is not None:
            prev_a.finish(f, (dx,))
        cur_m.finish(f, (dx,))
        prev_a = _GradExchange(f"{i}a", kinds[:2], [g_qkv, g_o], chip, core)
        started = (prev_a.start_swap((dx,)),)
    prev_a.swap_to_scatter((dx,))
    prev_a.finish(f, (dx,))
    for lst in (dg_mix, dg_mlp, dsinks):
        lst.reverse()

    res = {}
    for kind, (w, m, v) in big.items():
        g = join_halves(f"join_{kind}", f[kind]).reshape(w.shape)
        res[kind] = (g, *adamw(f"adamw_{kind}", w, g, m, v))

    n_sink = a_sinks.size
    sink_rows = jnp.zeros((1, 8, d), F32).at[0, 0, :n_sink].set(jnp.concatenate(dsinks))
    parts = jnp.concatenate([jnp.stack(dg_mix), jnp.stack(dg_mlp), dg_final[None], sink_rows], axis=0)
    g_small = allreduce_small("allreduce_small", parts)

    def pack(mix, mlp, fin, snk):
        snk_row = jnp.zeros((1, d), F32).at[0, :n_sink].set(snk.reshape(-1))
        return jnp.concatenate([mix, mlp, fin[None], snk_row], axis=0)

    def unpack(t):
        return t[:depth], t[depth:2 * depth], t[2 * depth], t[2 * depth + 1, :n_sink].reshape(a_sinks.shape)

    small = adamw("adamw_small", pack(norm_mix, norm_mlp, final_norm, a_sinks), g_small,
                  pack(m_norm_mix, m_norm_mlp, m_final_norm, m_a_sinks),
                  pack(v_norm_mix, v_norm_mlp, v_final_norm, v_a_sinks))
    outs = []
    for idx in range(4):
        mix, mlp, fin, snk = unpack(g_small if idx == 0 else small[idx - 1])
        outs += [res["a_qkv"][idx], res["a_o"][idx], snk, res["b_qkv"][idx], res["b_o"][idx], mix, mlp,
                 res["mlp_in"][idx], res["mlp_out"][idx], fin]
    return (loss, dx.reshape(x.shape), *outs)
```

```python
import functools
import math

import jax
import jax.numpy as jnp
from jax import lax
from jax.experimental import pallas as pl
from jax.experimental.pallas import tpu as pltpu

F32 = jnp.float32
BF16 = jnp.bfloat16
MESH = pl.DeviceIdType.MESH

N_CHIPS = 4
HEAD_DIM = 64
LANES = 128
N_HEADS = 32
N_KV_A = 4
WINDOW = 128
RMS_EPS = 1e-5
ATTN_SCALE = 1.0 / math.sqrt(HEAD_DIM)
ADAM_LR, ADAM_B1, ADAM_B2, ADAM_EPS, ADAM_WD, ADAM_STEP = 0.001, 0.9, 0.999, 1e-08, 0.01, 10
NEG_BIG = -1e30
VMEM_LIMIT = 56 * 1024 * 1024

_DN = {"nn": (((1,), (0,)), ((), ())), "nt": (((1,), (1,)), ((), ())), "tn": (((0,), (0,)), ((), ()))}


def _dot(a, b, mode="nn"):
    return lax.dot_general(a, b, _DN[mode], preferred_element_type=F32)


def _params(*sem):
    return pltpu.CompilerParams(dimension_semantics=sem, vmem_limit_bytes=VMEM_LIMIT)


def _pick(n, prefs):
    for t in prefs:
        if n % t == 0:
            return t
    return n


def _mm(name, mode, a, b, *, grid, a_spec, b_spec, extras=(), extra_specs=(), out_shapes, out_specs, nk,
        acc_shape, epilogue, deps=()):
    n_ex, n_out = len(extras), len(out_shapes)
    first_out = 2 + n_ex + len(deps)

    def body(*refs):
        a_ref, b_ref = refs[0], refs[1]
        ex = refs[2:2 + n_ex]
        outs = refs[first_out:first_out + n_out]
        part = _dot(a_ref[...], b_ref[...], mode)

        def finish(acc):
            res = epilogue(acc, *[e[...] for e in ex])
            for o, r in zip(outs, res):
                o[...] = r.astype(o.dtype)

        if nk == 1:
            finish(part)
        else:
            acc_ref = refs[-1]
            k = pl.program_id(len(grid) - 1)

            @pl.when(k == 0)
            def _():
                acc_ref[...] = part

            @pl.when(k > 0)
            def _():
                acc_ref[...] += part

            @pl.when(k == nk - 1)
            def _():
                finish(acc_ref[...])

    sem = ("parallel",) * (len(grid) - 1) + ("arbitrary" if nk > 1 else "parallel",)
    return pl.pallas_call(
        body, name=name, grid=grid,
        in_specs=[a_spec, b_spec, *extra_specs, *[ANY] * len(deps)],
        out_specs=list(out_specs), out_shape=list(out_shapes),
        scratch_shapes=[] if nk == 1 else [pltpu.VMEM(acc_shape, F32)],
        compiler_params=_params(*sem),
    )(a, b, *extras, *deps)


def mm_cols(name, a, wg, epilogue, out_dtypes, tm=2048):
    m, k = a.shape
    c = wg.shape[2]
    tm = min(tm, m)
    tn = _pick(c, (512, 640, 256, 128))
    nj = c // tn
    o_spec = pl.BlockSpec((tm, tn), lambda i, j: (i, j))
    return _mm(name, "nn", a, wg, grid=(m // tm, N_CHIPS * nj),
               a_spec=pl.BlockSpec((tm, k), lambda i, j: (i, 0)),
               b_spec=pl.BlockSpec((None, k, tn), lambda i, j: (j // nj, 0, j % nj)),
               out_shapes=[jax.ShapeDtypeStruct((m, N_CHIPS * c), d) for d in out_dtypes],
               out_specs=[o_spec] * len(out_dtypes), nk=1, acc_shape=None, epilogue=epilogue)


def mm_res(name, a, w, res, tm=1024, tn=1024, tk=2048):
    m, k = a.shape
    n = w.shape[1]
    tm, tn, tk = min(tm, m), min(tn, n), min(tk, k)
    nk = k // tk
    return _mm(name, "nn", a, w, grid=(m // tm, n // tn, nk),
               a_spec=pl.BlockSpec((tm, tk), lambda i, j, kk: (i, kk)),
               b_spec=pl.BlockSpec((tk, tn), lambda i, j, kk: (kk, j)),
               extras=(res,), extra_specs=(pl.BlockSpec((tm, tn), lambda i, j, kk: (i, j)),),
               out_shapes=[jax.ShapeDtypeStruct((m, n), F32)],
               out_specs=[pl.BlockSpec((tm, tn), lambda i, j, kk: (i, j))], nk=nk, acc_shape=(tm, tn),
               epilogue=lambda acc, r: (acc + r,))[0]


def mm_nt(name, a, w, epilogue, extras=(), deps=(), tm=2048, tn=512):
    m, k = a.shape
    n = w.shape[0]
    tm, tn = min(tm, m), min(tn, n)
    o_spec = pl.BlockSpec((tm, tn), lambda i, j: (i, j))
    return _mm(name, "nt", a, w, grid=(m // tm, n // tn),
               a_spec=pl.BlockSpec((tm, k), lambda i, j: (i, 0)),
               b_spec=pl.BlockSpec((tn, k), lambda i, j: (j, 0)),
               extras=tuple(extras), extra_specs=(o_spec,) * len(extras),
               out_shapes=[jax.ShapeDtypeStruct((m, n), BF16)], out_specs=[o_spec], nk=1, acc_shape=None,
               epilogue=epilogue, deps=deps)[0]


def mm_nt_cols(name, dy, wg, deps=(), tm=1024, tn=2048):
    m = dy.shape[0]
    _, d, c = wg.shape
    tm, tn = min(tm, m), min(tn, d)
    tk = _pick(c, (1024, 768, 640, 512, 128))
    nkk = c // tk
    nk = N_CHIPS * nkk
    return _mm(name, "nt", dy, wg, grid=(m // tm, d // tn, nk),
               a_spec=pl.BlockSpec((tm, tk), lambda i, j, kk: (i, kk)),
               b_spec=pl.BlockSpec((None, tn, tk), lambda i, j, kk: (kk // nkk, j, kk % nkk)),
               out_shapes=[jax.ShapeDtypeStruct((m, d), F32)],
               out_specs=[pl.BlockSpec((tm, tn), lambda i, j, kk: (i, j))], nk=nk, acc_shape=(tm, tn),
               epilogue=lambda acc: (acc,), deps=deps)[0]


def mm_tn(name, a, b, tm=1024, tn=2048, tk=1024):
    m, p = a.shape
    q = b.shape[1]
    tm, tn, tk = min(tm, p), min(tn, q), min(tk, m)
    nk = m // tk
    return _mm(name, "tn", a, b, grid=(p // tm, q // tn, nk),
               a_spec=pl.BlockSpec((tk, tm), lambda i, j, kk: (kk, i)),
               b_spec=pl.BlockSpec((tk, tn), lambda i, j, kk: (kk, j)),
               out_shapes=[jax.ShapeDtypeStruct((p, q), BF16)],
               out_specs=[pl.BlockSpec((tm, tn), lambda i, j, kk: (i, j))], nk=nk, acc_shape=(tm, tn),
               epilogue=lambda acc: (acc,))[0]


def mm_tn_cols(name, a, dy, tm=2048, tk=1024):
    m, d = a.shape
    c = dy.shape[1] // N_CHIPS
    tm, tk = min(tm, d), min(tk, m)
    tn = _pick(c, (1024, 768, 640, 512, 128))
    nj = c // tn
    nk = m // tk
    return _mm(name, "tn", a, dy, grid=(d // tm, N_CHIPS * nj, nk),
               a_spec=pl.BlockSpec((tk, tm), lambda i, j, kk: (kk, i)),
               b_spec=pl.BlockSpec((tk, tn), lambda i, j, kk: (kk, j)),
               out_shapes=[jax.ShapeDtypeStruct((N_CHIPS, d, c), BF16)],
               out_specs=[pl.BlockSpec((None, tm, tn), lambda i, j, kk: (j // nj, i, j % nj))], nk=nk,
               acc_shape=(tm, tn), epilogue=lambda acc: (acc,))[0]


def _rows_to_8(v):
    tm, d = v.shape
    return jnp.sum(v.reshape(tm // 8, 8, d), axis=0)


def rms_fwd(name, x, gain, deps=(), tm=512):
    s, d = x.shape
    tm = min(tm, s)

    def body(x_ref, g_ref, *rest):
        h_ref = rest[-1]
        xv = x_ref[...]
        r = lax.rsqrt(jnp.mean(xv * xv, axis=-1, keepdims=True) + RMS_EPS)
        h_ref[...] = (xv * r * g_ref[...]).astype(BF16)

    row = pl.BlockSpec((tm, d), lambda i: (i, 0))
    return pl.pallas_call(
        body, name=name, grid=(s // tm,),
        in_specs=[row, pl.BlockSpec((1, d), lambda i: (0, 0)), *[ANY] * len(deps)], out_specs=row,
        out_shape=jax.ShapeDtypeStruct((s, d), BF16), compiler_params=_params("parallel"),
    )(x, gain.reshape(1, d), *deps)


def rms_bwd(name, x, gain, dh, dres, deps=(), tm=512):
    s, d = x.shape
    tm = min(tm, s)

    def body(x_ref, g_ref, dh_ref, dres_ref, *rest):
        dx_ref, dxb_ref, dg_ref = rest[-3:]
        xv = x_ref[...]
        r = lax.rsqrt(jnp.mean(xv * xv, axis=-1, keepdims=True) + RMS_EPS)
        xhat = xv * r
        dhv = dh_ref[...]
        dxhat = dhv * g_ref[...]
        dx = dres_ref[...] + r * (dxhat - xhat * jnp.mean(dxhat * xhat, axis=-1, keepdims=True))
        dx_ref[...] = dx
        dxb_ref[...] = dx.astype(BF16)

        @pl.when(pl.program_id(0) == 0)
        def _():
            dg_ref[...] = jnp.zeros_like(dg_ref)

        dg_ref[...] += _rows_to_8(dhv * xhat)

    row = pl.BlockSpec((tm, d), lambda i: (i, 0))
    return pl.pallas_call(
        body, name=name, grid=(s // tm,),
        in_specs=[row, pl.BlockSpec((1, d), lambda i: (0, 0)), row, row, *[ANY] * len(deps)],
        out_specs=[row, row, pl.BlockSpec((8, d), lambda i: (0, 0))],
        out_shape=[jax.ShapeDtypeStruct((s, d), F32), jax.ShapeDtypeStruct((s, d), BF16),
                   jax.ShapeDtypeStruct((8, d), F32)],
        compiler_params=_params("arbitrary"),
    )(x, gain.reshape(1, d), dh, dres, *deps)


def loss_head(name, x, gain, target, tm=512):
    s, d = x.shape
    tm = min(tm, s)

    def body(x_ref, g_ref, t_ref, loss_ref, dx_ref, dxb_ref, dg_ref):
        xv = x_ref[...]
        g = g_ref[...]
        r = lax.rsqrt(jnp.mean(xv * xv, axis=-1, keepdims=True) + RMS_EPS)
        xhat = xv * r
        err = xhat * g - t_ref[...]
        dy = err * (1.0 / d)
        dxhat = dy * g
        dx = r * (dxhat - xhat * jnp.mean(dxhat * xhat, axis=-1, keepdims=True))
        dx_ref[...] = dx
        dxb_ref[...] = dx.astype(BF16)

        @pl.when(pl.program_id(0) == 0)
        def _():
            dg_ref[...] = jnp.zeros_like(dg_ref)
            loss_ref[...] = jnp.zeros_like(loss_ref)

        dg_ref[...] += _rows_to_8(dy * xhat)
        loss_ref[...] += _rows_to_8(err * err) * (0.5 / d)

    row = pl.BlockSpec((tm, d), lambda i: (i, 0))
    vec = pl.BlockSpec((8, d), lambda i: (0, 0))
    return pl.pallas_call(
        body, name=name, grid=(s // tm,),
        in_specs=[row, pl.BlockSpec((1, d), lambda i: (0, 0)), row],
        out_specs=[vec, row, row, vec],
        out_shape=[jax.ShapeDtypeStruct((8, d), F32), jax.ShapeDtypeStruct((s, d), F32),
                   jax.ShapeDtypeStruct((s, d), BF16), jax.ShapeDtypeStruct((8, d), F32)],
        compiler_params=_params("arbitrary"),
    )(x, gain.reshape(1, d), target)


def _half_masks(dtype):
    lane = lax.broadcasted_iota(jnp.int32, (1, LANES), 1)
    lo = (lane < HEAD_DIM).astype(dtype)
    return lo, (1 - lo).astype(dtype)


def _swap_halves(v):
    return pltpu.roll(v, HEAD_DIM, axis=1)


SWA_BLOCKS = 8


def _swa_probs(s_raw, sink, slope, first):
    t = s_raw.shape[0]
    row = lax.broadcasted_iota(jnp.int32, (t, 2 * WINDOW), 0)
    col = lax.broadcasted_iota(jnp.int32, (t, 2 * WINDOW), 1)
    dist = row + WINDOW - col
    valid = (dist >= 0) & (dist < WINDOW)
    if first is not None:
        valid = valid & ((col >= WINDOW) | jnp.logical_not(first))
    s = jnp.where(valid, s_raw - slope * dist.astype(F32), NEG_BIG)
    m = jnp.maximum(jnp.max(s, axis=-1, keepdims=True), sink)
    e = jnp.exp(s - m)
    e_sink = jnp.exp(sink - m)
    inv = 1.0 / (jnp.sum(e, axis=-1, keepdims=True) + e_sink)
    return e * inv, e_sink * inv


def _swa_band(kp_ref, kc_ref, vp_ref, vc_ref, kv_half):
    lo_b, hi_b = _half_masks(F32)
    sel = jnp.where(kv_half == 0, lo_b, hi_b)
    k = jnp.concatenate([kp_ref[...], kc_ref[...]], axis=0).astype(F32) * sel
    v = jnp.concatenate([vp_ref[...], vc_ref[...]], axis=0).astype(F32) * sel
    k = (k + _swap_halves(k)).astype(BF16)
    v = (v + _swap_halves(v)).astype(BF16)
    return k, v, sel


def _swa_specs(nq):
    t = WINDOW
    q_spec = pl.BlockSpec((nq * t, LANES), lambda j, n, *_: (n, j))
    q_blocks = N_HEADS // 2

    def prev(off):
        return pl.BlockSpec((t, LANES), lambda j, n, *_: (jnp.maximum(nq * n - 1, 0), q_blocks + off + j // 8))

    def cur(off):
        return pl.BlockSpec((nq * t, LANES), lambda j, n, *_: (n, q_blocks + off + j // 8))

    kv_blocks = N_KV_A // 2
    return q_spec, [prev(0), cur(0), prev(kv_blocks), cur(kv_blocks)]


def _rows(a, b, n=1):
    return a[b * WINDOW:(b + n) * WINDOW]


def swa_fwd(name, qkv, sinks, slopes):
    s = qkv.shape[0]
    nq = min(SWA_BLOCKS, s // WINDOW)

    def body(sink_ref, slope_ref, q_ref, kp_ref, kc_ref, vp_ref, vc_ref, o_ref):
        j, n = pl.program_id(0), pl.program_id(1)
        k_all, v_all, _ = _swa_band(kp_ref, kc_ref, vp_ref, vc_ref, (j // 4) % 2)
        q = q_ref[...] * ATTN_SCALE
        masks = _half_masks(BF16)
        chains = [(b, hq) for b in range(nq) for hq in range(2)]
        s_raw = {(b, hq): _dot(_rows(q, b) * masks[hq], _rows(k_all, b, 2), "nt") for b, hq in chains}
        p = {(b, hq): _swa_probs(s_raw[(b, hq)], sink_ref[2 * j + hq], slope_ref[2 * j + hq],
                                 (n == 0) if b == 0 else None)[0] for b, hq in chains}
        outs = [sum(_dot(p[(b, hq)].astype(BF16), _rows(v_all, b, 2) * masks[hq]) for hq in range(2))
                for b in range(nq)]
        o_ref[...] = jnp.concatenate(outs, axis=0).astype(BF16)

    q_spec, kv_specs = _swa_specs(nq)
    return pl.pallas_call(
        body, name=name,
        grid_spec=pltpu.PrefetchScalarGridSpec(
            num_scalar_prefetch=2, grid=(N_HEADS // 2, s // (nq * WINDOW)),
            in_specs=[q_spec, *kv_specs], out_specs=q_spec),
        out_shape=jax.ShapeDtypeStruct((s, N_HEADS * HEAD_DIM), BF16),
        compiler_params=_params("parallel", "parallel"),
    )(sinks, slopes, qkv, qkv, qkv, qkv, qkv)


def swa_bwd(name, qkv, do, sinks, slopes):
    s = qkv.shape[0]
    t = WINDOW
    nq = min(SWA_BLOCKS, s // t)

    def body(sink_ref, slope_ref, q_ref, kp_ref, kc_ref, vp_ref, vc_ref, do_ref, dq_ref, dk_ref, dv_ref, ds_ref):
        j, n = pl.program_id(0), pl.program_id(1)
        k_all, v_all, sel = _swa_band(kp_ref, kc_ref, vp_ref, vc_ref, (j // 4) % 2)
        q = q_ref[...] * ATTN_SCALE
        do_v = do_ref[...]
        masks = _half_masks(BF16)

        @pl.when((j % 8 == 0) & (n == 0))
        def _():
            dk_ref[...] = jnp.zeros_like(dk_ref)
            dv_ref[...] = jnp.zeros_like(dv_ref)

        @pl.when(n == 0)
        def _():
            ds_ref[...] = jnp.zeros_like(ds_ref)

        chains = [(b, hq) for b in range(nq) for hq in range(2)]
        qm = {(b, hq): _rows(q, b) * masks[hq] for b, hq in chains}
        dom = {(b, hq): _rows(do_v, b) * masks[hq] for b, hq in chains}
        s_raw = {ch: _dot(qm[ch], _rows(k_all, ch[0], 2), "nt") for ch in chains}
        dp = {ch: _dot(dom[ch], _rows(v_all, ch[0], 2), "nt") for ch in chains}
        pb, dsc = {}, {}
        dsink = [jnp.zeros((), F32), jnp.zeros((), F32)]
        for ch in chains:
            b, hq = ch
            p, p_sink = _swa_probs(s_raw[ch], sink_ref[2 * j + hq], slope_ref[2 * j + hq], (n == 0) if b == 0 else None)
            delta = jnp.sum(p * dp[ch], axis=-1, keepdims=True)
            dsc[ch] = (p * (dp[ch] - delta)).astype(BF16)
            pb[ch] = p.astype(BF16)
            dsink[hq] = dsink[hq] - jnp.sum(p_sink * delta)
        for hq in range(2):
            ds_ref[hq:hq + 1, :] += jnp.zeros((1, LANES), F32) + dsink[hq]
        dq, dk, dv = [], [], []
        for b in range(nq):
            dq.append(sum(_dot(dsc[(b, hq)], _rows(k_all, b, 2) * masks[hq]) for hq in range(2)))
            dk_b = sum(_dot(dsc[(b, hq)], qm[(b, hq)], "tn") for hq in range(2))
            dv_b = sum(_dot(pb[(b, hq)], dom[(b, hq)], "tn") for hq in range(2))
            dk.append((dk_b + _swap_halves(dk_b)) * sel)
            dv.append((dv_b + _swap_halves(dv_b)) * sel)
        dq_ref[...] = (jnp.concatenate(dq, axis=0) * ATTN_SCALE).astype(BF16)

        @pl.when(n == 0)
        def _():
            dk_ref[pl.ds(0, t), :] += dk[0][t:]
            dv_ref[pl.ds(0, t), :] += dv[0][t:]

        @pl.when(n > 0)
        def _():
            start = pl.multiple_of((nq * n - 1) * t, t)
            dk_ref[pl.ds(start, 2 * t), :] += dk[0]
            dv_ref[pl.ds(start, 2 * t), :] += dv[0]

        for b in range(1, nq):
            start = pl.multiple_of((nq * n + b - 1) * t, t)
            dk_ref[pl.ds(start, 2 * t), :] += dk[b]
            dv_ref[pl.ds(start, 2 * t), :] += dv[b]

    q_spec, kv_specs = _swa_specs(nq)
    kv_out = pl.BlockSpec((s, LANES), lambda j, n, *_: (0, j // 8))
    return pl.pallas_call(
        body, name=name,
        grid_spec=pltpu.PrefetchScalarGridSpec(
            num_scalar_prefetch=2, grid=(N_HEADS // 2, s // (nq * t)),
            in_specs=[q_spec, *kv_specs, q_spec],
            out_specs=[q_spec, kv_out, kv_out, pl.BlockSpec((None, 8, LANES), lambda j, n, *_: (j, 0, 0))]),
        out_shape=[jax.ShapeDtypeStruct((s, N_HEADS * HEAD_DIM), BF16),
                   jax.ShapeDtypeStruct((s, N_KV_A * HEAD_DIM), F32),
                   jax.ShapeDtypeStruct((s, N_KV_A * HEAD_DIM), F32),
                   jax.ShapeDtypeStruct((N_HEADS // 2, 8, LANES), F32)],
        compiler_params=_params("arbitrary", "arbitrary"),
    )(sinks, slopes, qkv, qkv, qkv, qkv, qkv, do)


SB_TILE = 256


def _split_k(v):
    hi = v.astype(BF16)
    lo = (v - hi.astype(F32)).astype(BF16)
    return jnp.concatenate([hi, lo], axis=1)


def _tri2(t, inclusive):
    r = lax.broadcasted_iota(jnp.int32, (2 * t, t), 0)
    c = lax.broadcasted_iota(jnp.int32, (2 * t, t), 1)
    r = jnp.where(r >= t, r - t, r)
    return ((r >= c) if inclusive else (r > c)).astype(BF16)


def _sb_logs(z, before):
    neg_abs = lax.bitcast_convert_type(lax.bitcast_convert_type(z, jnp.uint32) | jnp.uint32(0x80000000), F32)
    l = jnp.log(1.0 + jnp.exp(neg_abs))
    lb = jnp.minimum(z, 0.0) - l
    lm = lb - z
    if before is not None:
        lm = jnp.where(before, lm, 0.0)
    return lb, lm


def _sb_weights(lb, sfx, c_lm, before):
    a = jnp.exp(lb + sfx + c_lm)
    if before is not None:
        a = jnp.where(before, a, 0.0)
    return a


def _sb_steps_table(n_blocks):
    steps = [(i, g) for i in range(n_blocks) for g in range(i // 2 + 1)]
    return (jnp.asarray([i for i, _ in steps], jnp.int32), jnp.asarray([g for _, g in steps], jnp.int32))


def _sb_specs(s, t):
    hp = N_HEADS // 2

    def pair(st, i_tab, g_tab):
        return i_tab[st] // 2 - g_tab[st]

    q_spec = pl.BlockSpec((t, LANES), lambda h, st, i_tab, g_tab: (i_tab[st], h))
    k_spec = pl.BlockSpec((2 * t, LANES), lambda h, st, i_tab, g_tab: (pair(st, i_tab, g_tab), hp + h))
    v_spec = pl.BlockSpec((2 * t, LANES), lambda h, st, i_tab, g_tab: (pair(st, i_tab, g_tab), 2 * hp + h))
    saved = pl.BlockSpec((None, 2, t, 2 * t), lambda h, st, i_tab, g_tab: (h, 0, i_tab[st], pair(st, i_tab, g_tab)))
    return q_spec, k_spec, v_spec, saved


def _sb_steps(tile, i, g, diag):
    @pl.when((g == 0) & (i % 2 == 1))
    def _():
        tile([1, 0], diag)

    @pl.when((g == 0) & (i % 2 == 0))
    def _():
        tile([0], diag)

    @pl.when(g > 0)
    def _():
        tile([1, 0], None)


def sb_fwd(name, qkv, t=SB_TILE):
    s = qkv.shape[0]
    t = min(t, s // 2)
    i_tab, g_tab = _sb_steps_table(s // t)

    def body(i_ref, g_ref, q_ref, k_ref, v_ref, ob_ref, of_ref, a_ref, sg_ref, c_ref, acc_ref):
        i, g = i_ref[pl.program_id(1)], g_ref[pl.program_id(1)]
        masks = _half_masks(BF16)
        q = q_ref[...] * ATTN_SCALE
        qm = [q * masks[0], q * masks[1]]
        tri_x = _tri2(t, False)
        r = lax.broadcasted_iota(jnp.int32, (t, t), 0)
        c = lax.broadcasted_iota(jnp.int32, (t, t), 1)
        diag = c < r

        @pl.when(g == 0)
        def _():
            c_ref[...] = jnp.zeros_like(c_ref)
            acc_ref[...] = jnp.zeros_like(acc_ref)

        def tile(halves, before):
            cs = [c_ref[0], c_ref[1]]
            acc = acc_ref[...]
            kj = [k_ref[hf * t:(hf + 1) * t, :] for hf in halves]
            vj = [v_ref[hf * t:(hf + 1) * t, :] for hf in halves]
            chains = [(h, b) for b in range(len(halves)) for h in range(2)]
            z = {(h, b): _dot(qm[h], kj[b], "nt") for h, b in chains}
            lb, sfx, c_at = {}, {}, {}
            for ch in chains:
                h, b = ch
                lb[ch], lm = _sb_logs(z[ch], before if b == 0 else None)
                sfx[ch] = _dot(_split_k(lm), tri_x)
                c_at[ch] = cs[h]
                cs[h] = cs[h] + jnp.sum(lm, axis=-1, keepdims=True)
            for ch in chains:
                h, b = ch
                cols = slice(halves[b] * t, (halves[b] + 1) * t)
                ab = _sb_weights(lb[ch], sfx[ch], c_at[ch], before if b == 0 else None).astype(BF16)
                a_ref[h, :, cols] = ab
                sg_ref[h, :, cols] = jnp.exp(lb[ch]).astype(BF16)
                acc = acc + _dot(ab, vj[b] * masks[h])
            if len(halves) == 1:
                a_ref[:, :, t:] = jnp.zeros((2, t, t), BF16)
                sg_ref[:, :, t:] = jnp.zeros((2, t, t), BF16)
            c_ref[0], c_ref[1] = cs[0], cs[1]
            acc_ref[...] = acc

        _sb_steps(tile, i, g, diag)

        @pl.when(g == i // 2)
        def _():
            ob_ref[...] = acc_ref[...].astype(BF16)
            of_ref[...] = acc_ref[...]

    q_spec, k_spec, v_spec, saved = _sb_specs(s, t)
    width = N_HEADS * HEAD_DIM
    keep = jax.ShapeDtypeStruct((N_HEADS // 2, 2, s, s), BF16)
    return pl.pallas_call(
        body, name=name,
        grid_spec=pltpu.PrefetchScalarGridSpec(
            num_scalar_prefetch=2, grid=(N_HEADS // 2, i_tab.shape[0]),
            in_specs=[q_spec, k_spec, v_spec], out_specs=[q_spec, q_spec, saved, saved],
            scratch_shapes=[pltpu.VMEM((2, t, 1), F32), pltpu.VMEM((t, LANES), F32)]),
        out_shape=[jax.ShapeDtypeStruct((s, width), BF16), jax.ShapeDtypeStruct((s, width), F32), keep, keep],
        compiler_params=_params("parallel", "arbitrary"),
    )(i_tab, g_tab, qkv, qkv, qkv)


def sb_bwd(name, qkv, o_f32, do, a_all, sg_all, t=SB_TILE):
    s = qkv.shape[0]
    t = min(t, s // 2)
    i_tab, g_tab = _sb_steps_table(s // t)

    def body(i_ref, g_ref, q_ref, k_ref, v_ref, o_ref, do_ref, a_ref, sg_ref, dq_ref, dk_ref, dv_ref, cd_ref, delta_ref,
             acc_ref):
        i, g = i_ref[pl.program_id(1)], g_ref[pl.program_id(1)]
        p = i // 2 - g
        masks = _half_masks(BF16)
        fmasks = _half_masks(F32)
        q = q_ref[...] * ATTN_SCALE
        do_v = do_ref[...]
        qm = [q * masks[0], q * masks[1]]
        dom = [do_v * masks[0], do_v * masks[1]]
        tri_i = _tri2(t, True)
        r = lax.broadcasted_iota(jnp.int32, (t, t), 0)
        c = lax.broadcasted_iota(jnp.int32, (t, t), 1)
        diag = c < r

        @pl.when(pl.program_id(1) == 0)
        def _():
            dk_ref[...] = jnp.zeros_like(dk_ref)
            dv_ref[...] = jnp.zeros_like(dv_ref)

        @pl.when(g == 0)
        def _():
            prod = do_v.astype(F32) * o_ref[...]
            for h in range(2):
                delta_ref[h] = jnp.sum(prod * fmasks[h], axis=-1, keepdims=True)
            cd_ref[...] = jnp.zeros_like(cd_ref)
            acc_ref[...] = jnp.zeros_like(acc_ref)

        def tile(halves, before):
            cd = [cd_ref[0], cd_ref[1]]
            dq = acc_ref[...]
            kj = [k_ref[hf * t:(hf + 1) * t, :] for hf in halves]
            vj = [v_ref[hf * t:(hf + 1) * t, :] for hf in halves]
            chains = [(h, b) for b in range(len(halves)) for h in range(2)]
            cols = {b: slice(halves[b] * t, (halves[b] + 1) * t) for b in range(len(halves))}
            da = {(h, b): _dot(dom[h], vj[b], "nt") for h, b in chains}
            ab, de, dsfx, cd_at = {}, {}, {}, {}
            dv = [jnp.zeros((t, LANES), F32) for _ in halves]
            dk = [jnp.zeros((t, LANES), F32) for _ in halves]
            for ch in chains:
                h, b = ch
                ab[ch] = a_ref[h, :, cols[b]]
                de[ch] = da[ch] * ab[ch].astype(F32)
                dsfx[ch] = _dot(_split_k(de[ch]), tri_i)
                dv[b] = dv[b] + _dot(ab[ch], dom[h], "tn")
                cd_at[ch] = cd[h]
                cd[h] = cd[h] + jnp.sum(de[ch], axis=-1, keepdims=True)
            for ch in chains:
                h, b = ch
                farther = delta_ref[h] - cd_at[ch] - dsfx[ch]
                dz = de[ch] - sg_ref[h, :, cols[b]].astype(F32) * (de[ch] + farther)
                if before is not None and b == 0:
                    dz = jnp.where(before, dz, 0.0)
                dzb = dz.astype(BF16)
                dq = dq + _dot(dzb, kj[b] * masks[h])
                dk[b] = dk[b] + _dot(dzb, qm[h], "tn")
            for b, hf in enumerate(halves):
                start = pl.multiple_of(p * 2 * t + hf * t, t)
                dk_ref[pl.ds(start, t), :] += dk[b]
                dv_ref[pl.ds(start, t), :] += dv[b]
            cd_ref[0], cd_ref[1] = cd[0], cd[1]
            acc_ref[...] = dq

        _sb_steps(tile, i, g, diag)

        @pl.when(g == i // 2)
        def _():
            dq_ref[...] = (acc_ref[...] * ATTN_SCALE).astype(BF16)

    q_spec, k_spec, v_spec, saved = _sb_specs(s, t)
    kv_out = pl.BlockSpec((s, LANES), lambda h, st, i_tab, g_tab: (0, h))
    width = N_HEADS * HEAD_DIM
    return pl.pallas_call(
        body, name=name,
        grid_spec=pltpu.PrefetchScalarGridSpec(
            num_scalar_prefetch=2, grid=(N_HEADS // 2, i_tab.shape[0]),
            in_specs=[q_spec, k_spec, v_spec, q_spec, q_spec, saved, saved], out_specs=[q_spec, kv_out, kv_out],
            scratch_shapes=[pltpu.VMEM((2, t, 1), F32), pltpu.VMEM((2, t, 1), F32), pltpu.VMEM((t, LANES), F32)]),
        out_shape=[jax.ShapeDtypeStruct((s, width), BF16), jax.ShapeDtypeStruct((s, width), F32),
                   jax.ShapeDtypeStruct((s, width), F32)],
        compiler_params=_params("parallel", "arbitrary"),
    )(i_tab, g_tab, qkv, qkv, qkv, o_f32, do, a_all, sg_all)


def adamw(name, w, g, m, v, tm=256):
    shape = w.shape
    c = shape[-1]
    rows = math.prod(shape[:-1])
    tm = min(tm, rows)

    def body(w_ref, g_ref, m_ref, v_ref, d_ref, mo_ref, vo_ref):
        gv = g_ref[...]
        m2 = ADAM_B1 * m_ref[...] + (1.0 - ADAM_B1) * gv
        v2 = ADAM_B2 * v_ref[...] + (1.0 - ADAM_B2) * (gv * gv)
        m_hat = m2 / (1.0 - ADAM_B1 ** ADAM_STEP)
        v_hat = v2 / (1.0 - ADAM_B2 ** ADAM_STEP)
        d_ref[...] = -ADAM_LR * (m_hat / (jnp.sqrt(v_hat) + ADAM_EPS) + ADAM_WD * w_ref[...])
        mo_ref[...] = m2
        vo_ref[...] = v2

    blk = pl.BlockSpec((tm, c), lambda i: (i, 0))
    outs = pl.pallas_call(
        body, name=name, grid=(rows // tm,), in_specs=[blk] * 4, out_specs=[blk] * 3,
        out_shape=[jax.ShapeDtypeStruct((rows, c), F32)] * 3, compiler_params=_params("parallel"),
    )(*[t.reshape(rows, c) for t in (w, g, m, v)])
    return [o.reshape(shape) for o in outs]


HBM = pl.BlockSpec(memory_space=pltpu.HBM)


def _place():
    x, y, c = lax.axis_index("x"), lax.axis_index("y"), lax.axis_index("c")
    return x, y, c, [(1 - x, y), (x, 1 - y), (1 - x, 1 - y)]


def _remote(src, dst, send, recv, dev):
    return pltpu.make_async_remote_copy(src_ref=src, dst_ref=dst, send_sem=send, recv_sem=recv, device_id=dev,
                                        device_id_type=MESH)


SEM = pl.BlockSpec(memory_space=pltpu.SEMAPHORE)
ANY = pl.BlockSpec(memory_space=pl.ANY)
DATAFLOW = pltpu.SideEffectType.DATAFLOW_SIDE_EFFECTING


def _in_hbm(v):
    return pltpu.with_memory_space_constraint(v, pltpu.HBM)


def split_start(name, bufs, n_copies, sends, after=()):
    nb, na = len(bufs), len(after)

    def body(*refs):
        send, recv = refs[nb + na], refs[nb + na + 1]
        for cp in sends(refs[:nb], send, recv):
            cp.start()
        refs[-1][...] = jnp.zeros_like(refs[-1])

    outs = pl.pallas_call(
        body, name=name,
        in_specs=[HBM] * nb + [ANY] * na,
        out_shape=(pltpu.SemaphoreType.DMA((n_copies,)), pltpu.SemaphoreType.DMA((n_copies,)),
                   *[pltpu.HBM(b.shape, b.dtype) for b in bufs], jax.ShapeDtypeStruct((8, LANES), F32)),
        out_specs=(SEM, SEM, *[HBM] * nb, pl.BlockSpec(memory_space=pltpu.VMEM)),
        input_output_aliases={i: 2 + i for i in range(nb)},
        compiler_params=pltpu.CompilerParams(has_side_effects=DATAFLOW),
    )(*[_in_hbm(b) for b in bufs], *after)
    return outs[0], outs[1], list(outs[2:2 + nb]), outs[-1]


def split_wait(name, send_sems, recv_sems, bufs, sends, arrivals, after):
    nb, na = len(bufs), len(after)

    def body(*refs):
        send, recv = refs[nb], refs[nb + 1]
        for cp in sends(refs[:nb], send, recv):
            cp.wait_send()
        for cp in arrivals(refs[:nb], send, recv):
            cp.wait_recv()

    outs = pl.pallas_call(
        body, name=name,
        in_specs=[HBM] * nb + [SEM, SEM] + [ANY] * na,
        out_shape=tuple(pltpu.HBM(b.shape, b.dtype) for b in bufs), out_specs=tuple([HBM] * nb),
        input_output_aliases={i: i for i in range(nb)},
        compiler_params=pltpu.CompilerParams(has_side_effects=DATAFLOW),
    )(*bufs, send_sems, recv_sems, *after)
    return list(outs)


def _gather_plan(n):
    def sends(refs, send, recv):
        x, y, c, chips = _place()
        me = 2 * x + y
        return [_remote(refs[t].at[me], refs[t].at[me], send.at[3 * t + k], recv.at[3 * t + k], (px, py, c))
                for t in range(n) for k, (px, py) in enumerate(chips)]

    def arrivals(refs, send, recv):
        x, y, c, chips = _place()
        return [_remote(refs[t].at[2 * px + py], refs[t].at[2 * px + py], send.at[3 * t + k], recv.at[3 * t + k],
                        (px, py, c)) for t in range(n) for k, (px, py) in enumerate(chips)]

    return 3 * n, sends, arrivals


def _swap_plan(n):
    def copies(refs, send, recv):
        x, y, c, _ = _place()
        return [_remote(refs[t].at[s, 1 - c], refs[n + t].at[s], send.at[N_CHIPS * t + s], recv.at[N_CHIPS * t + s],
                        (x, y, 1 - c)) for t in range(n) for s in range(N_CHIPS)]

    return N_CHIPS * n, copies, copies


def _scatter_plan(n):
    def sends(refs, send, recv):
        x, y, c, chips = _place()
        me = 2 * x + y
        return [_remote(refs[t].at[2 * px + py], refs[n + t].at[me], send.at[3 * t + k], recv.at[3 * t + k],
                        (px, py, c)) for t in range(n) for k, (px, py) in enumerate(chips)]

    def arrivals(refs, send, recv):
        x, y, c, chips = _place()
        return [_remote(refs[t].at[2 * px + py], refs[n + t].at[2 * px + py], send.at[3 * t + k], recv.at[3 * t + k],
                        (px, py, c)) for t in range(n) for k, (px, py) in enumerate(chips)]

    return 3 * n, sends, arrivals


def cast_into_slot(name, w, layer, chip, tm=256):
    _, r, c = w.shape
    tm = min(tm, r)

    def body(chip_ref, w_ref, o_ref):
        o_ref[...] = w_ref[...].astype(BF16)

    return pl.pallas_call(
        body, name=name,
        grid_spec=pltpu.PrefetchScalarGridSpec(
            num_scalar_prefetch=1, grid=(r // tm,),
            in_specs=[pl.BlockSpec((None, tm, c), lambda i, chip_ref: (layer, i, 0))],
            out_specs=pl.BlockSpec((None, tm, c), lambda i, chip_ref: (chip_ref[0], i, 0))),
        out_shape=jax.ShapeDtypeStruct((N_CHIPS, r, c), BF16), compiler_params=_params("parallel"),
    )(chip, w)


def join_halves(name, f):
    n = f.shape[0]

    def body(f_ref, o_ref, send, recv):
        x, y, c, _ = _place()
        sib = (x, y, 1 - c)
        sends = [_remote(f_ref.at[l, c], o_ref.at[l, c], send.at[l], recv.at[l], sib) for l in range(n)]
        for cp in sends:
            cp.start()
        for l in range(n):
            _remote(f_ref.at[l, 1 - c], o_ref.at[l, 1 - c], send.at[l], recv.at[l], sib).wait_recv()
        for cp in sends:
            cp.wait_send()

    return pl.pallas_call(
        body, name=name, in_specs=[HBM], out_specs=HBM, out_shape=jax.ShapeDtypeStruct(f.shape, f.dtype),
        input_output_aliases={0: 0},
        scratch_shapes=[pltpu.SemaphoreType.DMA((n,)), pltpu.SemaphoreType.DMA((n,))],
    )(f)


def add_sibling_half(name, g, a, core, tm=256):
    _, _, rh, c = g.shape
    tm = min(tm, rh)

    def body(core_ref, g_ref, a_ref, o_ref):
        o_ref[...] = (g_ref[...].astype(F32) + a_ref[...].astype(F32)).astype(BF16)

    return pl.pallas_call(
        body, name=name,
        grid_spec=pltpu.PrefetchScalarGridSpec(
            num_scalar_prefetch=1, grid=(N_CHIPS, rh // tm),
            in_specs=[pl.BlockSpec((None, None, tm, c), lambda s, i, core_ref: (s, core_ref[0], i, 0)),
                      pl.BlockSpec((None, tm, c), lambda s, i, core_ref: (s, i, 0))],
            out_specs=pl.BlockSpec((None, tm, c), lambda s, i, core_ref: (s, i, 0))),
        out_shape=jax.ShapeDtypeStruct(a.shape, BF16), compiler_params=_params("parallel", "parallel"),
    )(core, g, a)


def sum_chips_into(name, p, b, f, layer, chip, core, tm=256):
    _, rh, c = b.shape
    tm = min(tm, rh)

    def body(chip_ref, core_ref, p_ref, b_ref, f_ref, o_ref):
        acc = jnp.zeros((tm, c), F32)
        for s in range(N_CHIPS):
            acc = acc + jnp.where(chip_ref[0] == s, p_ref[s].astype(F32), b_ref[s].astype(F32))
        o_ref[...] = acc

    slots = pl.BlockSpec((N_CHIPS, tm, c), lambda i, chip_ref, core_ref: (0, i, 0))
    return pl.pallas_call(
        body, name=name,
        grid_spec=pltpu.PrefetchScalarGridSpec(
            num_scalar_prefetch=2, grid=(rh // tm,), in_specs=[slots, slots, ANY],
            out_specs=pl.BlockSpec((None, None, tm, c), lambda i, chip_ref, core_ref: (layer, core_ref[0], i, 0))),
        out_shape=jax.ShapeDtypeStruct(f.shape, F32), input_output_aliases={4: 0},
        compiler_params=_params("parallel"),
    )(chip, core, p, b, f)


N_DEV = 8


def allreduce_small(name, parts):
    p, _, d = parts.shape
    m_per = p * 8

    def body(x_ref, out_ref, all_ref, send_sems, recv_sems, local_sem):
        x, y, c, chips = _place()
        me, sibling = (x, y, c), (x, y, 1 - c)

        def rows(px, py, pc):
            return all_ref.at[pl.ds((4 * px + 2 * py + pc) * m_per, m_per), :]

        def copy(k, block, to, src=None):
            return _remote(rows(*block) if src is None else src, rows(*block), send_sems.at[k], recv_sems.at[k], to)

        mine = pltpu.make_async_copy(x_ref, rows(*me), local_sem)
        mine.start()
        first = [copy(0, me, sibling, src=x_ref)]
        first += [copy(1 + j, me, (*chip, c), src=x_ref) for j, chip in enumerate(chips)]
        for cp in first:
            cp.start()
        passed = [copy(4 + j, (*chip, c), sibling) for j, chip in enumerate(chips)]
        for j, chip in enumerate(chips):
            copy(1 + j, (*chip, c), me).wait_recv()
            passed[j].start()
        copy(0, sibling, me).wait_recv()
        for j, chip in enumerate(chips):
            copy(4 + j, (*chip, 1 - c), me).wait_recv()
        for cp in first + passed:
            cp.wait_send()
        mine.wait()
        acc = all_ref[pl.ds(0, m_per), :]
        for dev in range(1, N_DEV):
            acc = acc + all_ref[pl.ds(dev * m_per, m_per), :]
        out_ref[...] = jnp.sum(acc.reshape(p, 8, d), axis=1)

    vmem = pl.BlockSpec(memory_space=pltpu.VMEM)
    return pl.pallas_call(
        body, name=name, in_specs=[vmem], out_specs=vmem,
        out_shape=jax.ShapeDtypeStruct((p, d), F32),
        scratch_shapes=[pltpu.VMEM((N_DEV * m_per, d), F32), pltpu.SemaphoreType.DMA((7,)),
                        pltpu.SemaphoreType.DMA((7,)), pltpu.SemaphoreType.DMA],
        compiler_params=pltpu.CompilerParams(vmem_limit_bytes=VMEM_LIMIT),
    )(parts.reshape(m_per, d))


def _empty(shape, dtype):
    return _in_hbm(lax.empty(shape, dtype))


class _GradExchange:
    def __init__(self, layer, kinds, grads, chip, core):
        self.layer, self.kinds, self.chip, self.core = layer, kinds, chip, core
        self.g4 = [g.reshape(N_CHIPS, 2, g.shape[1] // 2, g.shape[2]) for g in grads]
        self.n = len(grads)

    def start_swap(self, after):
        n_copies, self.swap_sends, self.swap_arrivals = _swap_plan(self.n)
        lands = [_empty((N_CHIPS,) + g.shape[2:], BF16) for g in self.g4]
        self.swap = split_start(f"swap_start_l{self.layer}", self.g4 + lands, n_copies, self.swap_sends, after)
        return self.swap[3]

    def swap_to_scatter(self, after):
        send, recv, bufs, _ = self.swap
        bufs = split_wait(f"swap_wait_l{self.layer}", send, recv, bufs, self.swap_sends, self.swap_arrivals, after)
        g4, lands = bufs[:self.n], bufs[self.n:]
        self.p = [add_sibling_half(f"add_l{self.layer}_{k}", g4[t], lands[t], self.core)
                  for t, (k, _) in enumerate(self.kinds)]
        n_copies, self.sc_sends, self.sc_arrivals = _scatter_plan(self.n)
        lands = [_empty(p.shape, BF16) for p in self.p]
        self.scatter = split_start(f"scatter_start_l{self.layer}", self.p + lands, n_copies, self.sc_sends)
        return self.scatter[3]

    def finish(self, f, after):
        send, recv, bufs, _ = self.scatter
        bufs = split_wait(f"scatter_wait_l{self.layer}", send, recv, bufs, self.sc_sends, self.sc_arrivals, after)
        p, lands = bufs[:self.n], bufs[self.n:]
        for t, (kind, l) in enumerate(self.kinds):
            f[kind] = sum_chips_into(f"sum_l{self.layer}_{kind}", p[t], lands[t], f[kind], l, self.chip, self.core)


def _relu2(acc):
    r = jnp.maximum(acc, 0.0)
    return acc, r * r


def _relu2_bwd(acc, u):
    return (acc * (2.0 * jnp.maximum(u.astype(F32), 0.0)),)


def _same(acc):
    return (acc,)


def kernel(x, a_w_qkv, a_w_o, a_sinks, b_w_qkv, b_w_o, norm_mix, norm_mlp, mlp_w_in, mlp_w_out, final_norm, loss_target, m_a_w_qkv, m_a_w_o, m_a_sinks, m_b_w_qkv, m_b_w_o, m_norm_mix, m_norm_mlp, m_mlp_w_in, m_mlp_w_out, m_final_norm, v_a_w_qkv, v_a_w_o, v_a_sinks, v_b_w_qkv, v_b_w_o, v_norm_mix, v_norm_mlp, v_mlp_w_in, v_mlp_w_out, v_final_norm):
    _, s, d = x.shape
    depth = norm_mix.shape[0]
    width = N_HEADS * HEAD_DIM
    core = lax.axis_index("c").astype(jnp.int32).reshape(1)
    chip = (2 * lax.axis_index("x") + lax.axis_index("y")).astype(jnp.int32).reshape(1)
    slopes = jnp.power(2.0, -8.0 * (jnp.arange(N_HEADS, dtype=F32) + 1.0) / N_HEADS)
    qkv_of = {0: ("a_qkv", a_w_qkv), 1: ("b_qkv", b_w_qkv)}
    o_of = {0: ("a_o", a_w_o), 1: ("b_o", b_w_o)}

    def layer_kinds(i):
        return [(qkv_of[i % 2][0], i // 2), (o_of[i % 2][0], i // 2), ("mlp_in", i), ("mlp_out", i)]

    stacks = {"a_qkv": a_w_qkv, "a_o": a_w_o, "b_qkv": b_w_qkv, "b_o": b_w_o, "mlp_in": mlp_w_in, "mlp_out": mlp_w_out}
    slots = [[cast_into_slot(f"cast_l{i}_{k}", stacks[k], l, chip) for k, l in layer_kinds(i)] for i in range(depth)]
    n_gather, gather_sends, gather_arrivals = _gather_plan(2)

    def start_gather(tag, bufs, after):
        return split_start(f"gather_start_{tag}", bufs, n_gather, gather_sends, after)

    def wait_gather(tag, flight, after):
        return split_wait(f"gather_wait_{tag}", flight[0], flight[1], flight[2], gather_sends, gather_arrivals, after)

    saved, weights = [], []
    xc = x[0]
    flight_a = start_gather("l0a", slots[0][:2], ())
    flight_m = start_gather("l0m", slots[0][2:], (flight_a[3],))
    for i in range(depth):
        mixer, j = i % 2, i // 2
        w_qkv, w_o = wait_gather(f"l{i}a", flight_a, (xc,))
        w_o = w_o.reshape(width, d)
        started, this_m = (flight_m[3],), flight_m
        if i + 1 < depth:
            flight_a = start_gather(f"l{i + 1}a", slots[i + 1][:2], (w_qkv, flight_m[3]))
            flight_m = start_gather(f"l{i + 1}m", slots[i + 1][2:], (flight_a[3],))
            started = (flight_m[3],)
        h = rms_fwd(f"l{i}_norm_mix", xc, norm_mix[i], deps=started)
        qkv = mm_cols(f"l{i}_qkv", h, w_qkv, _same, (BF16,))[0]
        if mixer == 0:
            attn, kept = swa_fwd(f"l{i}_swa", qkv, a_sinks[j], slopes), ()
        else:
            attn, *kept = sb_fwd(f"l{i}_sb", qkv)
        xm = mm_res(f"l{i}_o", attn, w_o, xc)
        w_in, w_out = wait_gather(f"l{i}m", this_m, (xm,))
        w_out = w_out.reshape(-1, d)
        weights.append((w_qkv, w_o, w_in, w_out))
        h2 = rms_fwd(f"l{i}_norm_mlp", xm, norm_mlp[i])
        u, hh = mm_cols(f"l{i}_in", h2, w_in, _relu2, (BF16, BF16))
        xn = mm_res(f"l{i}_out", hh, w_out, xm)
        saved.append((xc, h, qkv, attn, kept, xm, h2, u, hh))
        xc = xn
    loss_rows, dx, dxb, dg_final = loss_head("loss_head", xc, final_norm, loss_target[0])
    loss = lax.psum(jnp.sum(loss_rows), ("x", "y", "c"))

    big = {"a_qkv": (a_w_qkv, m_a_w_qkv, v_a_w_qkv), "a_o": (a_w_o, m_a_w_o, v_a_w_o),
           "b_qkv": (b_w_qkv, m_b_w_qkv, v_b_w_qkv), "b_o": (b_w_o, m_b_w_o, v_b_w_o),
           "mlp_in": (mlp_w_in, m_mlp_w_in, v_mlp_w_in), "mlp_out": (mlp_w_out, m_mlp_w_out, v_mlp_w_out)}
    f = {k: _empty((w.shape[0], 2, w.shape[1] // 2, w.shape[2]), F32) for k, (w, _, _) in big.items()}
    dg_mix, dg_mlp, dsinks = [], [], []
    prev_a, started = None, ()
    for i in reversed(range(depth)):
        mixer, j = i % 2, i // 2
        xin, h, qkv, attn, kept, xm, h2, u, hh = saved[i]
        w_qkv, w_o, w_in, w_out = weights[i]
        kinds = layer_kinds(i)
        du = mm_nt(f"l{i}_d_hidden", dxb, w_out, _relu2_bwd, (u,), deps=started)
        g_out = mm_tn(f"l{i}_g_out", hh, dxb).reshape(N_CHIPS, -1, d)
        started = (g_out,) if prev_a is None else (g_out, prev_a.swap_to_scatter((g_out,)))
        dh2 = mm_nt_cols(f"l{i}_d_h2", du, w_in, deps=started)
        g_in = mm_tn_cols(f"l{i}_g_in", h2, du)
        cur_m = _GradExchange(f"{i}m", kinds[2:], [g_in, g_out], chip, core)
        started = (g_in, cur_m.start_swap((g_in,)))
        dxm, dxmb, dg = rms_bwd(f"l{i}_norm_mlp_bwd", xm, norm_mlp[i], dh2, dx, deps=started)
        dg_mlp.append(dg)
        g_o = mm_tn(f"l{i}_g_o", attn, dxmb).reshape(N_CHIPS, -1, d)
        dattn = mm_nt(f"l{i}_d_attn", dxmb, w_o, _same, deps=(g_o, cur_m.swap_to_scatter((g_o,))))
        if mixer == 0:
            dq, dk, dv, dsk = swa_bwd(f"l{i}_swa_bwd", qkv, dattn, a_sinks[j], slopes)
            dsinks.append(dsk[:, :2, 0].reshape(N_HEADS))
        else:
            attn_f32, weights_a, sigmoids = kept
            dq, dk, dv = sb_bwd(f"l{i}_sb_bwd", qkv, attn_f32, dattn, weights_a, sigmoids)
        dqkv = jnp.concatenate([dq, dk.astype(BF16), dv.astype(BF16)], axis=1)
        dh = mm_nt_cols(f"l{i}_d_h", dqkv, w_qkv)
        g_qkv = mm_tn_cols(f"l{i}_g_qkv", h, dqkv)
        dx, dxb, dg = rms_bwd(f"l{i}_norm_mix_bwd", xin, norm_mix[i], dh, dxm, deps=(g_qkv,))
        dg_mix.append(dg)
        if prev_a is not None:
            prev_a.finish(f, (dx,))
        cur_m.finish(f, (dx,))
        prev_a = _GradExchange(f"{i}a", kinds[:2], [g_qkv, g_o], chip, core)
        started = (prev_a.start_swap((dx,)),)
    prev_a.swap_to_scatter((dx,))
    prev_a.finish(f, (dx,))
    for lst in (dg_mix, dg_mlp, dsinks):
        lst.reverse()

    res = {}
    for kind, (w, m, v) in big.items():
        g = join_halves(f"join_{kind}", f[kind]).reshape(w.shape)
        res[kind] = (g, *adamw(f"adamw_{kind}", w, g, m, v))

    n_sink = a_sinks.size
    sink_rows = jnp.zeros((1, 8, d), F32).at[0, 0, :n_sink].set(jnp.concatenate(dsinks))
    parts = jnp.concatenate([jnp.stack(dg_mix), jnp.stack(dg_mlp), dg_final[None], sink_rows], axis=0)
    g_small = allreduce_small("allreduce_small", parts)

    def pack(mix, mlp, fin, snk):
        snk_row = jnp.zeros((1, d), F32).at[0, :n_sink].set(snk.reshape(-1))
        return jnp.concatenate([mix, mlp, fin[None], snk_row], axis=0)

    def unpack(t):
        return t[:depth], t[depth:2 * depth], t[2 * depth], t[2 * depth + 1, :n_sink].reshape(a_sinks.shape)

    small = adamw("adamw_small", pack(norm_mix, norm_mlp, final_norm, a_sinks), g_small,
                  pack(m_norm_mix, m_norm_mlp, m_final_norm, m_a_sinks),
                  pack(v_norm_mix, v_norm_mlp, v_final_norm, v_a_sinks))
    outs = []
    for idx in range(4):
        mix, mlp, fin, snk = unpack(g_small if idx == 0 else small[idx - 1])
        outs += [res["a_qkv"][idx], res["a_o"][idx], snk, res["b_qkv"][idx], res["b_o"][idx], mix, mlp,
                 res["mlp_in"][idx], res["mlp_out"][idx], fin]
    return (loss, dx.reshape(x.shape), *outs)
```

```python
import functools
import math

import jax
import jax.numpy as jnp
from jax import lax
from jax.experimental import pallas as pl
from jax.experimental.pallas import tpu as pltpu

F32 = jnp.float32
BF16 = jnp.bfloat16
MESH = pl.DeviceIdType.MESH

N_CHIPS = 4
HEAD_DIM = 64
LANES = 128
N_HEADS = 32
N_KV_A = 4
WINDOW = 128
RMS_EPS = 1e-5
ATTN_SCALE = 1.0 / math.sqrt(HEAD_DIM)
ADAM_LR, ADAM_B1, ADAM_B2, ADAM_EPS, ADAM_WD, ADAM_STEP = 0.001, 0.9, 0.999, 1e-08, 0.01, 10
NEG_BIG = -1e30
VMEM_LIMIT = 56 * 1024 * 1024

_DN = {"nn": (((1,), (0,)), ((), ())), "nt": (((1,), (1,)), ((), ())), "tn": (((0,), (0,)), ((), ()))}


def _dot(a, b, mode="nn"):
    return lax.dot_general(a, b, _DN[mode], preferred_element_type=F32)


def _params(*sem):
    return pltpu.CompilerParams(dimension_semantics=sem, vmem_limit_bytes=VMEM_LIMIT)


def _pick(n, prefs):
    for t in prefs:
        if n % t == 0:
            return t
    return n


def _mm(name, mode, a, b, *, grid, a_spec, b_spec, extras=(), extra_specs=(), out_shapes, out_specs, nk,
        acc_shape, epilogue, deps=()):
    n_ex, n_out = len(extras), len(out_shapes)
    first_out = 2 + n_ex + len(deps)

    def body(*refs):
        a_ref, b_ref = refs[0], refs[1]
        ex = refs[2:2 + n_ex]
        outs = refs[first_out:first_out + n_out]
        part = _dot(a_ref[...], b_ref[...], mode)

        def finish(acc):
            res = epilogue(acc, *[e[...] for e in ex])
            for o, r in zip(outs, res):
                o[...] = r.astype(o.dtype)

        if nk == 1:
            finish(part)
        else:
            acc_ref = refs[-1]
            k = pl.program_id(len(grid) - 1)

            @pl.when(k == 0)
            def _():
                acc_ref[...] = part

            @pl.when(k > 0)
            def _():
                acc_ref[...] += part

            @pl.when(k == nk - 1)
            def _():
                finish(acc_ref[...])

    sem = ("parallel",) * (len(grid) - 1) + ("arbitrary" if nk > 1 else "parallel",)
    return pl.pallas_call(
        body, name=name, grid=grid,
        in_specs=[a_spec, b_spec, *extra_specs, *[ANY] * len(deps)],
        out_specs=list(out_specs), out_shape=list(out_shapes),
        scratch_shapes=[] if nk == 1 else [pltpu.VMEM(acc_shape, F32)],
        compiler_params=_params(*sem),
    )(a, b, *extras, *deps)


def mm_cols(name, a, wg, epilogue, out_dtypes, tm=2048):
    m, k = a.shape
    c = wg.shape[2]
    tm = min(tm, m)
    tn = _pick(c, (512, 640, 256, 128))
    nj = c // tn
    o_spec = pl.BlockSpec((tm, tn), lambda i, j: (i, j))
    return _mm(name, "nn", a, wg, grid=(m // tm, N_CHIPS * nj),
               a_spec=pl.BlockSpec((tm, k), lambda i, j: (i, 0)),
               b_spec=pl.BlockSpec((None, k, tn), lambda i, j: (j // nj, 0, j % nj)),
               out_shapes=[jax.ShapeDtypeStruct((m, N_CHIPS * c), d) for d in out_dtypes],
               out_specs=[o_spec] * len(out_dtypes), nk=1, acc_shape=None, epilogue=epilogue)


def mm_res(name, a, w, res, tm=1024, tn=1024, tk=2048):
    m, k = a.shape
    n = w.shape[1]
    tm, tn, tk = min(tm, m), min(tn, n), min(tk, k)
    nk = k // tk
    return _mm(name, "nn", a, w, grid=(m // tm, n // tn, nk),
               a_spec=pl.BlockSpec((tm, tk), lambda i, j, kk: (i, kk)),
               b_spec=pl.BlockSpec((tk, tn), lambda i, j, kk: (kk, j)),
               extras=(res,), extra_specs=(pl.BlockSpec((tm, tn), lambda i, j, kk: (i, j)),),
               out_shapes=[jax.ShapeDtypeStruct((m, n), F32)],
               out_specs=[pl.BlockSpec((tm, tn), lambda i, j, kk: (i, j))], nk=nk, acc_shape=(tm, tn),
               epilogue=lambda acc, r: (acc + r,))[0]


def mm_nt(name, a, w, epilogue, extras=(), deps=(), tm=2048, tn=512):
    m, k = a.shape
    n = w.shape[0]
    tm, tn = min(tm, m), min(tn, n)
    o_spec = pl.BlockSpec((tm, tn), lambda i, j: (i, j))
    return _mm(name, "nt", a, w, grid=(m // tm, n // tn),
               a_spec=pl.BlockSpec((tm, k), lambda i, j: (i, 0)),
               b_spec=pl.BlockSpec((tn, k), lambda i, j: (j, 0)),
               extras=tuple(extras), extra_specs=(o_spec,) * len(extras),
               out_shapes=[jax.ShapeDtypeStruct((m, n), BF16)], out_specs=[o_spec], nk=1, acc_shape=None,
               epilogue=epilogue, deps=deps)[0]


def mm_nt_cols(name, dy, wg, deps=(), tm=1024, tn=2048):
    m = dy.shape[0]
    _, d, c = wg.shape
    tm, tn = min(tm, m), min(tn, d)
    tk = _pick(c, (1024, 768, 640, 512, 128))
    nkk = c // tk
    nk = N_CHIPS * nkk
    return _mm(name, "nt", dy, wg, grid=(m // tm, d // tn, nk),
               a_spec=pl.BlockSpec((tm, tk), lambda i, j, kk: (i, kk)),
               b_spec=pl.BlockSpec((None, tn, tk), lambda i, j, kk: (kk // nkk, j, kk % nkk)),
               out_shapes=[jax.ShapeDtypeStruct((m, d), F32)],
               out_specs=[pl.BlockSpec((tm, tn), lambda i, j, kk: (i, j))], nk=nk, acc_shape=(tm, tn),
               epilogue=lambda acc: (acc,), deps=deps)[0]


def mm_tn(name, a, b, tm=1024, tn=2048, tk=1024):
    m, p = a.shape
    q = b.shape[1]
    tm, tn, tk = min(tm, p), min(tn, q), min(tk, m)
    nk = m // tk
    return _mm(name, "tn", a, b, grid=(p // tm, q // tn, nk),
               a_spec=pl.BlockSpec((tk, tm), lambda i, j, kk: (kk, i)),
               b_spec=pl.BlockSpec((tk, tn), lambda i, j, kk: (kk, j)),
               out_shapes=[jax.ShapeDtypeStruct((p, q), BF16)],
               out_specs=[pl.BlockSpec((tm, tn), lambda i, j, kk: (i, j))], nk=nk, acc_shape=(tm, tn),
               epilogue=lambda acc: (acc,))[0]


def mm_tn_cols(name, a, dy, tm=2048, tk=1024):
    m, d = a.shape
    c = dy.shape[1] // N_CHIPS
    tm, tk = min(tm, d), min(tk, m)
    tn = _pick(c, (1024, 768, 640, 512, 128))
    nj = c // tn
    nk = m // tk
    return _mm(name, "tn", a, dy, grid=(d // tm, N_CHIPS * nj, nk),
               a_spec=pl.BlockSpec((tk, tm), lambda i, j, kk: (kk, i)),
               b_spec=pl.BlockSpec((tk, tn), lambda i, j, kk: (kk, j)),
               out_shapes=[jax.ShapeDtypeStruct((N_CHIPS, d, c), BF16)],
               out_specs=[pl.BlockSpec((None, tm, tn), lambda i, j, kk: (j // nj, i, j % nj))], nk=nk,
               acc_shape=(tm, tn), epilogue=lambda acc: (acc,))[0]


def _rows_to_8(v):
    tm, d = v.shape
    return jnp.sum(v.reshape(tm // 8, 8, d), axis=0)


def rms_fwd(name, x, gain, deps=(), tm=512):
    s, d = x.shape
    tm = min(tm, s)

    def body(x_ref, g_ref, *rest):
        h_ref = rest[-1]
        xv = x_ref[...]
        r = lax.rsqrt(jnp.mean(xv * xv, axis=-1, keepdims=True) + RMS_EPS)
        h_ref[...] = (xv * r * g_ref[...]).astype(BF16)

    row = pl.BlockSpec((tm, d), lambda i: (i, 0))
    return pl.pallas_call(
        body, name=name, grid=(s // tm,),
        in_specs=[row, pl.BlockSpec((1, d), lambda i: (0, 0)), *[ANY] * len(deps)], out_specs=row,
        out_shape=jax.ShapeDtypeStruct((s, d), BF16), compiler_params=_params("parallel"),
    )(x, gain.reshape(1, d), *deps)


def rms_bwd(name, x, gain, dh, dres, deps=(), tm=512):
    s, d = x.shape
    tm = min(tm, s)

    def body(x_ref, g_ref, dh_ref, dres_ref, *rest):
        dx_ref, dxb_ref, dg_ref = rest[-3:]
        xv = x_ref[...]
        r = lax.rsqrt(jnp.mean(xv * xv, axis=-1, keepdims=True) + RMS_EPS)
        xhat = xv * r
        dhv = dh_ref[...]
        dxhat = dhv * g_ref[...]
        dx = dres_ref[...] + r * (dxhat - xhat * jnp.mean(dxhat * xhat, axis=-1, keepdims=True))
        dx_ref[...] = dx
        dxb_ref[...] = dx.astype(BF16)

        @pl.when(pl.program_id(0) == 0)
        def _():
            dg_ref[...] = jnp.zeros_like(dg_ref)

        dg_ref[...] += _rows_to_8(dhv * xhat)

    row = pl.BlockSpec((tm, d), lambda i: (i, 0))
    return pl.pallas_call(
        body, name=name, grid=(s // tm,),
        in_specs=[row, pl.BlockSpec((1, d), lambda i: (0, 0)), row, row, *[ANY] * len(deps)],
        out_specs=[row, row, pl.BlockSpec((8, d), lambda i: (0, 0))],
        out_shape=[jax.ShapeDtypeStruct((s, d), F32), jax.ShapeDtypeStruct((s, d), BF16),
                   jax.ShapeDtypeStruct((8, d), F32)],
        compiler_params=_params("arbitrary"),
    )(x, gain.reshape(1, d), dh, dres, *deps)


def loss_head(name, x, gain, target, tm=512):
    s, d = x.shape
    tm = min(tm, s)

    def body(x_ref, g_ref, t_ref, loss_ref, dx_ref, dxb_ref, dg_ref):
        xv = x_ref[...]
        g = g_ref[...]
        r = lax.rsqrt(jnp.mean(xv * xv, axis=-1, keepdims=True) + RMS_EPS)
        xhat = xv * r
        err = xhat * g - t_ref[...]
        dy = err * (1.0 / d)
        dxhat = dy * g
        dx = r * (dxhat - xhat * jnp.mean(dxhat * xhat, axis=-1, keepdims=True))
        dx_ref[...] = dx
        dxb_ref[...] = dx.astype(BF16)

        @pl.when(pl.program_id(0) == 0)
        def _():
            dg_ref[...] = jnp.zeros_like(dg_ref)
            loss_ref[...] = jnp.zeros_like(loss_ref)

        dg_ref[...] += _rows_to_8(dy * xhat)
        loss_ref[...] += _rows_to_8(err * err) * (0.5 / d)

    row = pl.BlockSpec((tm, d), lambda i: (i, 0))
    vec = pl.BlockSpec((8, d), lambda i: (0, 0))
    return pl.pallas_call(
        body, name=name, grid=(s // tm,),
        in_specs=[row, pl.BlockSpec((1, d), lambda i: (0, 0)), row],
        out_specs=[vec, row, row, vec],
        out_shape=[jax.ShapeDtypeStruct((8, d), F32), jax.ShapeDtypeStruct((s, d), F32),
                   jax.ShapeDtypeStruct((s, d), BF16), jax.ShapeDtypeStruct((8, d), F32)],
        compiler_params=_params("arbitrary"),
    )(x, gain.reshape(1, d), target)


def _half_masks(dtype):
    lane = lax.broadcasted_iota(jnp.int32, (1, LANES), 1)
    lo = (lane < HEAD_DIM).astype(dtype)
    return lo, (1 - lo).astype(dtype)


def _swap_halves(v):
    return pltpu.roll(v, HEAD_DIM, axis=1)


SWA_BLOCKS = 8


def _swa_probs(s_raw, sink, slope, first):
    t = s_raw.shape[0]
    row = lax.broadcasted_iota(jnp.int32, (t, 2 * WINDOW), 0)
    col = lax.broadcasted_iota(jnp.int32, (t, 2 * WINDOW), 1)
    dist = row + WINDOW - col
    valid = (dist >= 0) & (dist < WINDOW)
    if first is not None:
        valid = valid & ((col >= WINDOW) | jnp.logical_not(first))
    s = jnp.where(valid, s_raw - slope * dist.astype(F32), NEG_BIG)
    m = jnp.maximum(jnp.max(s, axis=-1, keepdims=True), sink)
    e = jnp.exp(s - m)
    e_sink = jnp.exp(sink - m)
    inv = 1.0 / (jnp.sum(e, axis=-1, keepdims=True) + e_sink)
    return e * inv, e_sink * inv


def _swa_band(kp_ref, kc_ref, vp_ref, vc_ref, kv_half):
    lo_b, hi_b = _half_masks(F32)
    sel = jnp.where(kv_half == 0, lo_b, hi_b)
    k = jnp.concatenate([kp_ref[...], kc_ref[...]], axis=0).astype(F32) * sel
    v = jnp.concatenate([vp_ref[...], vc_ref[...]], axis=0).astype(F32) * sel
    k = (k + _swap_halves(k)).astype(BF16)
    v = (v + _swap_halves(v)).astype(BF16)
    return k, v, sel


def _swa_specs(nq):
    t = WINDOW
    q_spec = pl.BlockSpec((nq * t, LANES), lambda j, n, *_: (n, j))
    q_blocks = N_HEADS // 2

    def prev(off):
        return pl.BlockSpec((t, LANES), lambda j, n, *_: (jnp.maximum(nq * n - 1, 0), q_blocks + off + j // 8))

    def cur(off):
        return pl.BlockSpec((nq * t, LANES), lambda j, n, *_: (n, q_blocks + off + j // 8))

    kv_blocks = N_KV_A // 2
    return q_spec, [prev(0), cur(0), prev(kv_blocks), cur(kv_blocks)]


def _rows(a, b, n=1):
    return a[b * WINDOW:(b + n) * WINDOW]


def swa_fwd(name, qkv, sinks, slopes):
    s = qkv.shape[0]
    nq = min(SWA_BLOCKS, s // WINDOW)

    def body(sink_ref, slope_ref, q_ref, kp_ref, kc_ref, vp_ref, vc_ref, o_ref):
        j, n = pl.program_id(0), pl.program_id(1)
        k_all, v_all, _ = _swa_band(kp_ref, kc_ref, vp_ref, vc_ref, (j // 4) % 2)
        q = q_ref[...] * ATTN_SCALE
        masks = _half_masks(BF16)
        chains = [(b, hq) for b in range(nq) for hq in range(2)]
        s_raw = {(b, hq): _dot(_rows(q, b) * masks[hq], _rows(k_all, b, 2), "nt") for b, hq in chains}
        p = {(b, hq): _swa_probs(s_raw[(b, hq)], sink_ref[2 * j + hq], slope_ref[2 * j + hq],
                                 (n == 0) if b == 0 else None)[0] for b, hq in chains}
        outs = [sum(_dot(p[(b, hq)].astype(BF16), _rows(v_all, b, 2) * masks[hq]) for hq in range(2))
                for b in range(nq)]
        o_ref[...] = jnp.concatenate(outs, axis=0).astype(BF16)

    q_spec, kv_specs = _swa_specs(nq)
    return pl.pallas_call(
        body, name=name,
        grid_spec=pltpu.PrefetchScalarGridSpec(
            num_scalar_prefetch=2, grid=(N_HEADS // 2, s // (nq * WINDOW)),
            in_specs=[q_spec, *kv_specs], out_specs=q_spec),
        out_shape=jax.ShapeDtypeStruct((s, N_HEADS * HEAD_DIM), BF16),
        compiler_params=_params("parallel", "parallel"),
    )(sinks, slopes, qkv, qkv, qkv, qkv, qkv)


def swa_bwd(name, qkv, do, sinks, slopes):
    s = qkv.shape[0]
    t = WINDOW
    nq = min(SWA_BLOCKS, s // t)

    def body(sink_ref, slope_ref, q_ref, kp_ref, kc_ref, vp_ref, vc_ref, do_ref, dq_ref, dk_ref, dv_ref, ds_ref):
        j, n = pl.program_id(0), pl.program_id(1)
        k_all, v_all, sel = _swa_band(kp_ref, kc_ref, vp_ref, vc_ref, (j // 4) % 2)
        q = q_ref[...] * ATTN_SCALE
        do_v = do_ref[...]
        masks = _half_masks(BF16)

        @pl.when((j % 8 == 0) & (n == 0))
        def _():
            dk_ref[...] = jnp.zeros_like(dk_ref)
            dv_ref[...] = jnp.zeros_like(dv_ref)

        @pl.when(n == 0)
        def _():
            ds_ref[...] = jnp.zeros_like(ds_ref)

        chains = [(b, hq) for b in range(nq) for hq in range(2)]
        qm = {(b, hq): _rows(q, b) * masks[hq] for b, hq in chains}
        dom = {(b, hq): _rows(do_v, b) * masks[hq] for b, hq in chains}
        s_raw = {ch: _dot(qm[ch], _rows(k_all, ch[0], 2), "nt") for ch in chains}
        dp = {ch: _dot(dom[ch], _rows(v_all, ch[0], 2), "nt") for ch in chains}
        pb, dsc = {}, {}
        dsink = [jnp.zeros((), F32), jnp.zeros((), F32)]
        for ch in chains:
            b, hq = ch
            p, p_sink = _swa_probs(s_raw[ch], sink_ref[2 * j + hq], slope_ref[2 * j + hq], (n == 0) if b == 0 else None)
            delta = jnp.sum(p * dp[ch], axis=-1, keepdims=True)
            dsc[ch] = (p * (dp[ch] - delta)).astype(BF16)
            pb[ch] = p.astype(BF16)
            dsink[hq] = dsink[hq] - jnp.sum(p_sink * delta)
        for hq in range(2):
            ds_ref[hq:hq + 1, :] += jnp.zeros((1, LANES), F32) + dsink[hq]
        dq, dk, dv = [], [], []
        for b in range(nq):
            dq.append(sum(_dot(dsc[(b, hq)], _rows(k_all, b, 2) * masks[hq]) for hq in range(2)))
            dk_b = sum(_dot(dsc[(b, hq)], qm[(b, hq)], "tn") for hq in range(2))
            dv_b = sum(_dot(pb[(b, hq)], dom[(b, hq)], "tn") for hq in range(2))
            dk.append((dk_b + _swap_halves(dk_b)) * sel)
            dv.append((dv_b + _swap_halves(dv_b)) * sel)
        dq_ref[...] = (jnp.concatenate(dq, axis=0) * ATTN_SCALE).astype(BF16)

        @pl.when(n == 0)
        def _():
            dk_ref[pl.ds(0, t), :] += dk[0][t:]
            dv_ref[pl.ds(0, t), :] += dv[0][t:]

        @pl.when(n > 0)
        def _():
            start = pl.multiple_of((nq * n - 1) * t, t)
            dk_ref[pl.ds(start, 2 * t), :] += dk[0]
            dv_ref[pl.ds(start, 2 * t), :] += dv[0]

        for b in range(1, nq):
            start = pl.multiple_of((nq * n + b - 1) * t, t)
            dk_ref[pl.ds(start, 2 * t), :] += dk[b]
            dv_ref[pl.ds(start, 2 * t), :] += dv[b]

    q_spec, kv_specs = _swa_specs(nq)
    kv_out = pl.BlockSpec((s, LANES), lambda j, n, *_: (0, j // 8))
    return pl.pallas_call(
        body, name=name,
        grid_spec=pltpu.PrefetchScalarGridSpec(
            num_scalar_prefetch=2, grid=(N_HEADS // 2, s // (nq * t)),
            in_specs=[q_spec, *kv_specs, q_spec],
            out_specs=[q_spec, kv_out, kv_out, pl.BlockSpec((None, 8, LANES), lambda j, n, *_: (j, 0, 0))]),
        out_shape=[jax.ShapeDtypeStruct((s, N_HEADS * HEAD_DIM), BF16),
                   jax.ShapeDtypeStruct((s, N_KV_A * HEAD_DIM), F32),
                   jax.ShapeDtypeStruct((s, N_KV_A * HEAD_DIM), F32),
                   jax.ShapeDtypeStruct((N_HEADS // 2, 8, LANES), F32)],
        compiler_params=_params("arbitrary", "arbitrary"),
    )(sinks, slopes, qkv, qkv, qkv, qkv, qkv, do)


SB_TILE = 256


def _split_k(v):
    hi = v.astype(BF16)
    lo = (v - hi.astype(F32)).astype(BF16)
    return jnp.concatenate([hi, lo], axis=1)


def _tri2(t, inclusive):
    r = lax.broadcasted_iota(jnp.int32, (2 * t, t), 0)
    c = lax.broadcasted_iota(jnp.int32, (2 * t, t), 1)
    r = jnp.where(r >= t, r - t, r)
    return ((r >= c) if inclusive else (r > c)).astype(BF16)


def _sb_logs(z, before):
    neg_abs = lax.bitcast_convert_type(lax.bitcast_convert_type(z, jnp.uint32) | jnp.uint32(0x80000000), F32)
    l = jnp.log(1.0 + jnp.exp(neg_abs))
    lb = jnp.minimum(z, 0.0) - l
    lm = lb - z
    if before is not None:
        lm = jnp.where(before, lm, 0.0)
    return lb, lm


def _sb_weights(lb, sfx, c_lm, before):
    a = jnp.exp(lb + sfx + c_lm)
    if before is not None:
        a = jnp.where(before, a, 0.0)
    return a


def _sb_sweep(tile, i, init, diag):
    carry = lax.cond(i > 0, lambda: tile([i, i - 1], init, diag), lambda: tile([i], init, diag))
    rest = jnp.maximum(i - 1, 0)
    odd = rest % 2
    carry = lax.cond(odd == 1, lambda: tile([i - 2], carry, None), lambda: carry)
    base = i - 2 - odd
    return lax.fori_loop(0, rest // 2, lambda n, cr: tile([base - 2 * n, base - 2 * n - 1], cr, None), carry)


def _sb_steps_table(n_blocks):
    steps = [(i, g) for i in range(n_blocks) for g in range(i // 2 + 1)]
    return (jnp.asarray([i for i, _ in steps], jnp.int32), jnp.asarray([g for _, g in steps], jnp.int32))


def _sb_specs(s, t):
    hp = N_HEADS // 2

    def pair(st, i_tab, g_tab):
        return i_tab[st] // 2 - g_tab[st]

    q_spec = pl.BlockSpec((t, LANES), lambda h, st, i_tab, g_tab: (i_tab[st], h))
    k_spec = pl.BlockSpec((2 * t, LANES), lambda h, st, i_tab, g_tab: (pair(st, i_tab, g_tab), hp + h))
    v_spec = pl.BlockSpec((2 * t, LANES), lambda h, st, i_tab, g_tab: (pair(st, i_tab, g_tab), 2 * hp + h))
    saved = pl.BlockSpec((None, 2, None, 2, t, t),
                         lambda h, st, i_tab, g_tab: (h, 0, i_tab[st], pair(st, i_tab, g_tab), 0, 0))
    return q_spec, k_spec, v_spec, saved


def _sb_steps(tile, i, g, diag):
    @pl.when((g == 0) & (i % 2 == 1))
    def _():
        tile([1, 0], diag)

    @pl.when((g == 0) & (i % 2 == 0))
    def _():
        tile([0], diag)

    @pl.when(g > 0)
    def _():
        tile([1, 0], None)


def sb_fwd(name, qkv, t=SB_TILE):
    s = qkv.shape[0]
    t = min(t, s // 2)
    nb = s // t

    def body(q_ref, k_ref, v_ref, ob_ref, of_ref, a_ref, sg_ref):
        i = pl.program_id(1)
        masks = _half_masks(BF16)
        q = q_ref[...] * ATTN_SCALE
        qm = [q * masks[0], q * masks[1]]
        tri_x = _tri2(t, False)
        r = lax.broadcasted_iota(jnp.int32, (t, t), 0)
        c = lax.broadcasted_iota(jnp.int32, (t, t), 1)
        diag = c < r

        def tile(js, carry, before):
            c0, c1, acc = carry
            cs = [c0, c1]
            kj = [k_ref[pl.ds(pl.multiple_of(j * t, t), t), :] for j in js]
            vj = [v_ref[pl.ds(pl.multiple_of(j * t, t), t), :] for j in js]
            chains = [(h, b) for b in range(len(js)) for h in range(2)]
            z = {(h, b): _dot(qm[h], kj[b], "nt") for h, b in chains}
            lb, sfx, c_at = {}, {}, {}
            for ch in chains:
                h, b = ch
                lb[ch], lm = _sb_logs(z[ch], before if b == 0 else None)
                sfx[ch] = _dot(_split_k(lm), tri_x)
                c_at[ch] = cs[h]
                cs[h] = cs[h] + jnp.sum(lm, axis=-1, keepdims=True)
            for ch in chains:
                h, b = ch
                ab = _sb_weights(lb[ch], sfx[ch], c_at[ch], before if b == 0 else None).astype(BF16)
                a_ref[h, js[b]] = ab
                sg_ref[h, js[b]] = jnp.exp(lb[ch]).astype(BF16)
                acc = acc + _dot(ab, vj[b] * masks[h])
            return cs[0], cs[1], acc

        zero = jnp.zeros((t, 1), F32)
        carry = _sb_sweep(tile, i, (zero, zero, jnp.zeros((t, LANES), F32)), diag)
        ob_ref[...] = carry[2].astype(BF16)
        of_ref[...] = carry[2]

    hp = N_HEADS // 2
    q_spec = pl.BlockSpec((t, LANES), lambda h, i: (i, h))
    k_spec = pl.BlockSpec((s, LANES), lambda h, i: (0, hp + h))
    v_spec = pl.BlockSpec((s, LANES), lambda h, i: (0, 2 * hp + h))
    saved = pl.BlockSpec((None, 2, None, nb, t, t), lambda h, i: (h, 0, i, 0, 0, 0))
    width = N_HEADS * HEAD_DIM
    keep = jax.ShapeDtypeStruct((hp, 2, nb, nb, t, t), BF16)
    return pl.pallas_call(
        body, name=name, grid=(hp, nb),
        in_specs=[q_spec, k_spec, v_spec], out_specs=[q_spec, q_spec, saved, saved],
        out_shape=[jax.ShapeDtypeStruct((s, width), BF16), jax.ShapeDtypeStruct((s, width), F32), keep, keep],
        compiler_params=_params("parallel", "parallel"),
    )(qkv, qkv, qkv)


def sb_bwd(name, qkv, o_f32, do, a_all, sg_all, t=SB_TILE):
    s = qkv.shape[0]
    t = min(t, s // 2)
    i_tab, g_tab = _sb_steps_table(s // t)

    def body(i_ref, g_ref, q_ref, k_ref, v_ref, o_ref, do_ref, a_ref, sg_ref, dq_ref, dk_ref, dv_ref, cd_ref, delta_ref,
             acc_ref):
        i, g = i_ref[pl.program_id(1)], g_ref[pl.program_id(1)]
        p = i // 2 - g
        masks = _half_masks(BF16)
        fmasks = _half_masks(F32)
        q = q_ref[...] * ATTN_SCALE
        do_v = do_ref[...]
        qm = [q * masks[0], q * masks[1]]
        dom = [do_v * masks[0], do_v * masks[1]]
        tri_i = _tri2(t, True)
        r = lax.broadcasted_iota(jnp.int32, (t, t), 0)
        c = lax.broadcasted_iota(jnp.int32, (t, t), 1)
        diag = c < r

        @pl.when(pl.program_id(1) == 0)
        def _():
            dk_ref[...] = jnp.zeros_like(dk_ref)
            dv_ref[...] = jnp.zeros_like(dv_ref)

        @pl.when(g == 0)
        def _():
            prod = do_v.astype(F32) * o_ref[...]
            for h in range(2):
                delta_ref[h] = jnp.sum(prod * fmasks[h], axis=-1, keepdims=True)
            cd_ref[...] = jnp.zeros_like(cd_ref)
            acc_ref[...] = jnp.zeros_like(acc_ref)

        def tile(halves, before):
            cd = [cd_ref[0], cd_ref[1]]
            dq = acc_ref[...]
            kj = [k_ref[hf * t:(hf + 1) * t, :] for hf in halves]
            vj = [v_ref[hf * t:(hf + 1) * t, :] for hf in halves]
            chains = [(h, b) for b in range(len(halves)) for h in range(2)]
            da = {(h, b): _dot(dom[h], vj[b], "nt") for h, b in chains}
            ab, de, dsfx, cd_at = {}, {}, {}, {}
            dv = [jnp.zeros((t, LANES), F32) for _ in halves]
            dk = [jnp.zeros((t, LANES), F32) for _ in halves]
            for ch in chains:
                h, b = ch
                ab[ch] = a_ref[h, halves[b]]
                de[ch] = da[ch] * ab[ch].astype(F32)
                dsfx[ch] = _dot(_split_k(de[ch]), tri_i)
                dv[b] = dv[b] + _dot(ab[ch], dom[h], "tn")
                cd_at[ch] = cd[h]
                cd[h] = cd[h] + jnp.sum(de[ch], axis=-1, keepdims=True)
            for ch in chains:
                h, b = ch
                farther = delta_ref[h] - cd_at[ch] - dsfx[ch]
                dz = de[ch] - sg_ref[h, halves[b]].astype(F32) * (de[ch] + farther)
                if before is not None and b == 0:
                    dz = jnp.where(before, dz, 0.0)
                dzb = dz.astype(BF16)
                dq = dq + _dot(dzb, kj[b] * masks[h])
                dk[b] = dk[b] + _dot(dzb, qm[h], "tn")
            for b, hf in enumerate(halves):
                start = pl.multiple_of(p * 2 * t + hf * t, t)
                dk_ref[pl.ds(start, t), :] += dk[b]
                dv_ref[pl.ds(start, t), :] += dv[b]
            cd_ref[0], cd_ref[1] = cd[0], cd[1]
            acc_ref[...] = dq

        _sb_steps(tile, i, g, diag)

        @pl.when(g == i // 2)
        def _():
            dq_ref[...] = (acc_ref[...] * ATTN_SCALE).astype(BF16)

    q_spec, k_spec, v_spec, saved = _sb_specs(s, t)
    kv_out = pl.BlockSpec((s, LANES), lambda h, st, i_tab, g_tab: (0, h))
    width = N_HEADS * HEAD_DIM
    return pl.pallas_call(
        body, name=name,
        grid_spec=pltpu.PrefetchScalarGridSpec(
            num_scalar_prefetch=2, grid=(N_HEADS // 2, i_tab.shape[0]),
            in_specs=[q_spec, k_spec, v_spec, q_spec, q_spec, saved, saved], out_specs=[q_spec, kv_out, kv_out],
            scratch_shapes=[pltpu.VMEM((2, t, 1), F32), pltpu.VMEM((2, t, 1), F32), pltpu.VMEM((t, LANES), F32)]),
        out_shape=[jax.ShapeDtypeStruct((s, width), BF16), jax.ShapeDtypeStruct((s, width), F32),
                   jax.ShapeDtypeStruct((s, width), F32)],
        compiler_params=_params("parallel", "arbitrary"),
    )(i_tab, g_tab, qkv, qkv, qkv, o_f32, do, a_all, sg_all)


def adamw(name, w, g, m, v, tm=256):
    shape = w.shape
    c = shape[-1]
    rows = math.prod(shape[:-1])
    tm = min(tm, rows)

    def body(w_ref, g_ref, m_ref, v_ref, d_ref, mo_ref, vo_ref):
        gv = g_ref[...]
        m2 = ADAM_B1 * m_ref[...] + (1.0 - ADAM_B1) * gv
        v2 = ADAM_B2 * v_ref[...] + (1.0 - ADAM_B2) * (gv * gv)
        m_hat = m2 / (1.0 - ADAM_B1 ** ADAM_STEP)
        v_hat = v2 / (1.0 - ADAM_B2 ** ADAM_STEP)
        d_ref[...] = -ADAM_LR * (m_hat / (jnp.sqrt(v_hat) + ADAM_EPS) + ADAM_WD * w_ref[...])
        mo_ref[...] = m2
        vo_ref[...] = v2

    blk = pl.BlockSpec((tm, c), lambda i: (i, 0))
    outs = pl.pallas_call(
        body, name=name, grid=(rows // tm,), in_specs=[blk] * 4, out_specs=[blk] * 3,
        out_shape=[jax.ShapeDtypeStruct((rows, c), F32)] * 3, compiler_params=_params("parallel"),
    )(*[t.reshape(rows, c) for t in (w, g, m, v)])
    return [o.reshape(shape) for o in outs]


HBM = pl.BlockSpec(memory_space=pltpu.HBM)


def _place():
    x, y, c = lax.axis_index("x"), lax.axis_index("y"), lax.axis_index("c")
    return x, y, c, [(1 - x, y), (x, 1 - y), (1 - x, 1 - y)]


def _remote(src, dst, send, recv, dev):
    return pltpu.make_async_remote_copy(src_ref=src, dst_ref=dst, send_sem=send, recv_sem=recv, device_id=dev,
                                        device_id_type=MESH)


SEM = pl.BlockSpec(memory_space=pltpu.SEMAPHORE)
ANY = pl.BlockSpec(memory_space=pl.ANY)
DATAFLOW = pltpu.SideEffectType.DATAFLOW_SIDE_EFFECTING


def _in_hbm(v):
    return pltpu.with_memory_space_constraint(v, pltpu.HBM)


def split_start(name, bufs, n_copies, sends, after=()):
    nb, na = len(bufs), len(after)

    def body(*refs):
        send, recv = refs[nb + na], refs[nb + na + 1]
        for cp in sends(refs[:nb], send, recv):
            cp.start()
        refs[-1][...] = jnp.zeros_like(refs[-1])

    outs = pl.pallas_call(
        body, name=name,
        in_specs=[HBM] * nb + [ANY] * na,
        out_shape=(pltpu.SemaphoreType.DMA((n_copies,)), pltpu.SemaphoreType.DMA((n_copies,)),
                   *[pltpu.HBM(b.shape, b.dtype) for b in bufs], jax.ShapeDtypeStruct((8, LANES), F32)),
        out_specs=(SEM, SEM, *[HBM] * nb, pl.BlockSpec(memory_space=pltpu.VMEM)),
        input_output_aliases={i: 2 + i for i in range(nb)},
        compiler_params=pltpu.CompilerParams(has_side_effects=DATAFLOW),
    )(*[_in_hbm(b) for b in bufs], *after)
    return outs[0], outs[1], list(outs[2:2 + nb]), outs[-1]


def split_wait(name, send_sems, recv_sems, bufs, sends, arrivals, after):
    nb, na = len(bufs), len(after)

    def body(*refs):
        send, recv = refs[nb], refs[nb + 1]
        for cp in sends(refs[:nb], send, recv):
            cp.wait_send()
        for cp in arrivals(refs[:nb], send, recv):
            cp.wait_recv()

    outs = pl.pallas_call(
        body, name=name,
        in_specs=[HBM] * nb + [SEM, SEM] + [ANY] * na,
        out_shape=tuple(pltpu.HBM(b.shape, b.dtype) for b in bufs), out_specs=tuple([HBM] * nb),
        input_output_aliases={i: i for i in range(nb)},
        compiler_params=pltpu.CompilerParams(has_side_effects=DATAFLOW),
    )(*bufs, send_sems, recv_sems, *after)
    return list(outs)


def _gather_plan(n):
    def sends(refs, send, recv):
        x, y, c, chips = _place()
        me = 2 * x + y
        return [_remote(refs[t].at[me], refs[t].at[me], send.at[3 * t + k], recv.at[3 * t + k], (px, py, c))
                for t in range(n) for k, (px, py) in enumerate(chips)]

    def arrivals(refs, send, recv):
        x, y, c, chips = _place()
        return [_remote(refs[t].at[2 * px + py], refs[t].at[2 * px + py], send.at[3 * t + k], recv.at[3 * t + k],
                        (px, py, c)) for t in range(n) for k, (px, py) in enumerate(chips)]

    return 3 * n, sends, arrivals


def _swap_plan(n):
    def copies(refs, send, recv):
        x, y, c, _ = _place()
        return [_remote(refs[t].at[s, 1 - c], refs[n + t].at[s], send.at[N_CHIPS * t + s], recv.at[N_CHIPS * t + s],
                        (x, y, 1 - c)) for t in range(n) for s in range(N_CHIPS)]

    return N_CHIPS * n, copies, copies


def _scatter_plan(n):
    def sends(refs, send, recv):
        x, y, c, chips = _place()
        me = 2 * x + y
        return [_remote(refs[t].at[2 * px + py], refs[n + t].at[me], send.at[3 * t + k], recv.at[3 * t + k],
                        (px, py, c)) for t in range(n) for k, (px, py) in enumerate(chips)]

    def arrivals(refs, send, recv):
        x, y, c, chips = _place()
        return [_remote(refs[t].at[2 * px + py], refs[n + t].at[2 * px + py], send.at[3 * t + k], recv.at[3 * t + k],
                        (px, py, c)) for t in range(n) for k, (px, py) in enumerate(chips)]

    return 3 * n, sends, arrivals


def cast_into_slot(name, w, layer, chip, tm=256):
    _, r, c = w.shape
    tm = min(tm, r)

    def body(chip_ref, w_ref, o_ref):
        o_ref[...] = w_ref[...].astype(BF16)

    return pl.pallas_call(
        body, name=name,
        grid_spec=pltpu.PrefetchScalarGridSpec(
            num_scalar_prefetch=1, grid=(r // tm,),
            in_specs=[pl.BlockSpec((None, tm, c), lambda i, chip_ref: (layer, i, 0))],
            out_specs=pl.BlockSpec((None, tm, c), lambda i, chip_ref: (chip_ref[0], i, 0))),
        out_shape=jax.ShapeDtypeStruct((N_CHIPS, r, c), BF16), compiler_params=_params("parallel"),
    )(chip, w)


def join_halves(name, f):
    n = f.shape[0]

    def body(f_ref, o_ref, send, recv):
        x, y, c, _ = _place()
        sib = (x, y, 1 - c)
        sends = [_remote(f_ref.at[l, c], o_ref.at[l, c], send.at[l], recv.at[l], sib) for l in range(n)]
        for cp in sends:
            cp.start()
        for l in range(n):
            _remote(f_ref.at[l, 1 - c], o_ref.at[l, 1 - c], send.at[l], recv.at[l], sib).wait_recv()
        for cp in sends:
            cp.wait_send()

    return pl.pallas_call(
        body, name=name, in_specs=[HBM], out_specs=HBM, out_shape=jax.ShapeDtypeStruct(f.shape, f.dtype),
        input_output_aliases={0: 0},
        scratch_shapes=[pltpu.SemaphoreType.DMA((n,)), pltpu.SemaphoreType.DMA((n,))],
    )(f)


def add_sibling_half(name, g, a, core, tm=256):
    _, _, rh, c = g.shape
    tm = min(tm, rh)

    def body(core_ref, g_ref, a_ref, o_ref):
        o_ref[...] = (g_ref[...].astype(F32) + a_ref[...].astype(F32)).astype(BF16)

    return pl.pallas_call(
        body, name=name,
        grid_spec=pltpu.PrefetchScalarGridSpec(
            num_scalar_prefetch=1, grid=(N_CHIPS, rh // tm),
            in_specs=[pl.BlockSpec((None, None, tm, c), lambda s, i, core_ref: (s, core_ref[0], i, 0)),
                      pl.BlockSpec((None, tm, c), lambda s, i, core_ref: (s, i, 0))],
            out_specs=pl.BlockSpec((None, tm, c), lambda s, i, core_ref: (s, i, 0))),
        out_shape=jax.ShapeDtypeStruct(a.shape, BF16), compiler_params=_params("parallel", "parallel"),
    )(core, g, a)


def sum_chips_into(name, p, b, f, layer, chip, core, tm=256):
    _, rh, c = b.shape
    tm = min(tm, rh)

    def body(chip_ref, core_ref, p_ref, b_ref, f_ref, o_ref):
        acc = jnp.zeros((tm, c), F32)
        for s in range(N_CHIPS):
            acc = acc + jnp.where(chip_ref[0] == s, p_ref[s].astype(F32), b_ref[s].astype(F32))
        o_ref[...] = acc

    slots = pl.BlockSpec((N_CHIPS, tm, c), lambda i, chip_ref, core_ref: (0, i, 0))
    return pl.pallas_call(
        body, name=name,
        grid_spec=pltpu.PrefetchScalarGridSpec(
            num_scalar_prefetch=2, grid=(rh // tm,), in_specs=[slots, slots, ANY],
            out_specs=pl.BlockSpec((None, None, tm, c), lambda i, chip_ref, core_ref: (layer, core_ref[0], i, 0))),
        out_shape=jax.ShapeDtypeStruct(f.shape, F32), input_output_aliases={4: 0},
        compiler_params=_params("parallel"),
    )(chip, core, p, b, f)


N_DEV = 8


def allreduce_small(name, parts):
    p, _, d = parts.shape
    m_per = p * 8

    def body(x_ref, out_ref, all_ref, send_sems, recv_sems, local_sem):
        x, y, c, chips = _place()
        me, sibling = (x, y, c), (x, y, 1 - c)

        def rows(px, py, pc):
            return all_ref.at[pl.ds((4 * px + 2 * py + pc) * m_per, m_per), :]

        def copy(k, block, to, src=None):
            return _remote(rows(*block) if src is None else src, rows(*block), send_sems.at[k], recv_sems.at[k], to)

        mine = pltpu.make_async_copy(x_ref, rows(*me), local_sem)
        mine.start()
        first = [copy(0, me, sibling, src=x_ref)]
        first += [copy(1 + j, me, (*chip, c), src=x_ref) for j, chip in enumerate(chips)]
        for cp in first:
            cp.start()
        passed = [copy(4 + j, (*chip, c), sibling) for j, chip in enumerate(chips)]
        for j, chip in enumerate(chips):
            copy(1 + j, (*chip, c), me).wait_recv()
            passed[j].start()
        copy(0, sibling, me).wait_recv()
        for j, chip in enumerate(chips):
            copy(4 + j, (*chip, 1 - c), me).wait_recv()
        for cp in first + passed:
            cp.wait_send()
        mine.wait()
        acc = all_ref[pl.ds(0, m_per), :]
        for dev in range(1, N_DEV):
            acc = acc + all_ref[pl.ds(dev * m_per, m_per), :]
        out_ref[...] = jnp.sum(acc.reshape(p, 8, d), axis=1)

    vmem = pl.BlockSpec(memory_space=pltpu.VMEM)
    return pl.pallas_call(
        body, name=name, in_specs=[vmem], out_specs=vmem,
        out_shape=jax.ShapeDtypeStruct((p, d), F32),
        scratch_shapes=[pltpu.VMEM((N_DEV * m_per, d), F32), pltpu.SemaphoreType.DMA((7,)),
                        pltpu.SemaphoreType.DMA((7,)), pltpu.SemaphoreType.DMA],
        compiler_params=pltpu.CompilerParams(vmem_limit_bytes=VMEM_LIMIT),
    )(parts.reshape(m_per, d))


def _empty(shape, dtype):
    return _in_hbm(lax.empty(shape, dtype))


class _GradExchange:
    def __init__(self, layer, kinds, grads, chip, core):
        self.layer, self.kinds, self.chip, self.core = layer, kinds, chip, core
        self.g4 = [g.reshape(N_CHIPS, 2, g.shape[1] // 2, g.shape[2]) for g in grads]
        self.n = len(grads)

    def start_swap(self, after):
        n_copies, self.swap_sends, self.swap_arrivals = _swap_plan(self.n)
        lands = [_empty((N_CHIPS,) + g.shape[2:], BF16) for g in self.g4]
        self.swap = split_start(f"swap_start_l{self.layer}", self.g4 + lands, n_copies, self.swap_sends, after)
        return self.swap[3]

    def swap_to_scatter(self, after):
        send, recv, bufs, _ = self.swap
        bufs = split_wait(f"swap_wait_l{self.layer}", send, recv, bufs, self.swap_sends, self.swap_arrivals, after)
        g4, lands = bufs[:self.n], bufs[self.n:]
        self.p = [add_sibling_half(f"add_l{self.layer}_{k}", g4[t], lands[t], self.core)
                  for t, (k, _) in enumerate(self.kinds)]
        n_copies, self.sc_sends, self.sc_arrivals = _scatter_plan(self.n)
        lands = [_empty(p.shape, BF16) for p in self.p]
        self.scatter = split_start(f"scatter_start_l{self.layer}", self.p + lands, n_copies, self.sc_sends)
        return self.scatter[3]

    def finish(self, f, after):
        send, recv, bufs, _ = self.scatter
        bufs = split_wait(f"scatter_wait_l{self.layer}", send, recv, bufs, self.sc_sends, self.sc_arrivals, after)
        p, lands = bufs[:self.n], bufs[self.n:]
        for t, (kind, l) in enumerate(self.kinds):
            f[kind] = sum_chips_into(f"sum_l{self.layer}_{kind}", p[t], lands[t], f[kind], l, self.chip, self.core)


def _relu2(acc):
    r = jnp.maximum(acc, 0.0)
    return acc, r * r


def _relu2_bwd(acc, u):
    return (acc * (2.0 * jnp.maximum(u.astype(F32), 0.0)),)


def _same(acc):
    return (acc,)


def kernel(x, a_w_qkv, a_w_o, a_sinks, b_w_qkv, b_w_o, norm_mix, norm_mlp, mlp_w_in, mlp_w_out, final_norm, loss_target, m_a_w_qkv, m_a_w_o, m_a_sinks, m_b_w_qkv, m_b_w_o, m_norm_mix, m_norm_mlp, m_mlp_w_in, m_mlp_w_out, m_final_norm, v_a_w_qkv, v_a_w_o, v_a_sinks, v_b_w_qkv, v_b_w_o, v_norm_mix, v_norm_mlp, v_mlp_w_in, v_mlp_w_out, v_final_norm):
    _, s, d = x.shape
    depth = norm_mix.shape[0]
    width = N_HEADS * HEAD_DIM
    core = lax.axis_index("c").astype(jnp.int32).reshape(1)
    chip = (2 * lax.axis_index("x") + lax.axis_index("y")).astype(jnp.int32).reshape(1)
    slopes = jnp.power(2.0, -8.0 * (jnp.arange(N_HEADS, dtype=F32) + 1.0) / N_HEADS)
    qkv_of = {0: ("a_qkv", a_w_qkv), 1: ("b_qkv", b_w_qkv)}
    o_of = {0: ("a_o", a_w_o), 1: ("b_o", b_w_o)}

    def layer_kinds(i):
        return [(qkv_of[i % 2][0], i // 2), (o_of[i % 2][0], i // 2), ("mlp_in", i), ("mlp_out", i)]

    stacks = {"a_qkv": a_w_qkv, "a_o": a_w_o, "b_qkv": b_w_qkv, "b_o": b_w_o, "mlp_in": mlp_w_in, "mlp_out": mlp_w_out}
    slots = [[cast_into_slot(f"cast_l{i}_{k}", stacks[k], l, chip) for k, l in layer_kinds(i)] for i in range(depth)]
    n_gather, gather_sends, gather_arrivals = _gather_plan(2)

    def start_gather(tag, bufs, after):
        return split_start(f"gather_start_{tag}", bufs, n_gather, gather_sends, after)

    def wait_gather(tag, flight, after):
        return split_wait(f"gather_wait_{tag}", flight[0], flight[1], flight[2], gather_sends, gather_arrivals, after)

    saved, weights = [], []
    xc = x[0]
    flight_a = start_gather("l0a", slots[0][:2], ())
    flight_m = start_gather("l0m", slots[0][2:], (flight_a[3],))
    for i in range(depth):
        mixer, j = i % 2, i // 2
        w_qkv, w_o = wait_gather(f"l{i}a", flight_a, (xc,))
        w_o = w_o.reshape(width, d)
        started, this_m = (flight_m[3],), flight_m
        if i + 1 < depth:
            flight_a = start_gather(f"l{i + 1}a", slots[i + 1][:2], (w_qkv, flight_m[3]))
            flight_m = start_gather(f"l{i + 1}m", slots[i + 1][2:], (flight_a[3],))
            started = (flight_m[3],)
        h = rms_fwd(f"l{i}_norm_mix", xc, norm_mix[i], deps=started)
        qkv = mm_cols(f"l{i}_qkv", h, w_qkv, _same, (BF16,))[0]
        if mixer == 0:
            attn, kept = swa_fwd(f"l{i}_swa", qkv, a_sinks[j], slopes), ()
        else:
            attn, *kept = sb_fwd(f"l{i}_sb", qkv)
        xm = mm_res(f"l{i}_o", attn, w_o, xc)
        w_in, w_out = wait_gather(f"l{i}m", this_m, (xm,))
        w_out = w_out.reshape(-1, d)
        weights.append((w_qkv, w_o, w_in, w_out))
        h2 = rms_fwd(f"l{i}_norm_mlp", xm, norm_mlp[i])
        u, hh = mm_cols(f"l{i}_in", h2, w_in, _relu2, (BF16, BF16))
        xn = mm_res(f"l{i}_out", hh, w_out, xm)
        saved.append((xc, h, qkv, attn, kept, xm, h2, u, hh))
        xc = xn
    loss_rows, dx, dxb, dg_final = loss_head("loss_head", xc, final_norm, loss_target[0])
    loss = lax.psum(jnp.sum(loss_rows), ("x", "y", "c"))

    big = {"a_qkv": (a_w_qkv, m_a_w_qkv, v_a_w_qkv), "a_o": (a_w_o, m_a_w_o, v_a_w_o),
           "b_qkv": (b_w_qkv, m_b_w_qkv, v_b_w_qkv), "b_o": (b_w_o, m_b_w_o, v_b_w_o),
           "mlp_in": (mlp_w_in, m_mlp_w_in, v_mlp_w_in), "mlp_out": (mlp_w_out, m_mlp_w_out, v_mlp_w_out)}
    f = {k: _empty((w.shape[0], 2, w.shape[1] // 2, w.shape[2]), F32) for k, (w, _, _) in big.items()}
    dg_mix, dg_mlp, dsinks = [], [], []
    prev_a, started = None, ()
    for i in reversed(range(depth)):
        mixer, j = i % 2, i // 2
        xin, h, qkv, attn, kept, xm, h2, u, hh = saved[i]
        w_qkv, w_o, w_in, w_out = weights[i]
        kinds = layer_kinds(i)
        du = mm_nt(f"l{i}_d_hidden", dxb, w_out, _relu2_bwd, (u,), deps=started)
        g_out = mm_tn(f"l{i}_g_out", hh, dxb).reshape(N_CHIPS, -1, d)
        started = (g_out,) if prev_a is None else (g_out, prev_a.swap_to_scatter((g_out,)))
        dh2 = mm_nt_cols(f"l{i}_d_h2", du, w_in, deps=started)
        g_in = mm_tn_cols(f"l{i}_g_in", h2, du)
        cur_m = _GradExchange(f"{i}m", kinds[2:], [g_in, g_out], chip, core)
        started = (g_in, cur_m.start_swap((g_in,)))
        dxm, dxmb, dg = rms_bwd(f"l{i}_norm_mlp_bwd", xm, norm_mlp[i], dh2, dx, deps=started)
        dg_mlp.append(dg)
        g_o = mm_tn(f"l{i}_g_o", attn, dxmb).reshape(N_CHIPS, -1, d)
        dattn = mm_nt(f"l{i}_d_attn", dxmb, w_o, _same, deps=(g_o, cur_m.swap_to_scatter((g_o,))))
        if mixer == 0:
            dq, dk, dv, dsk = swa_bwd(f"l{i}_swa_bwd", qkv, dattn, a_sinks[j], slopes)
            dsinks.append(dsk[:, :2, 0].reshape(N_HEADS))
        else:
            attn_f32, weights_a, sigmoids = kept
            dq, dk, dv = sb_bwd(f"l{i}_sb_bwd", qkv, attn_f32, dattn, weights_a, sigmoids)
        dqkv = jnp.concatenate([dq, dk.astype(BF16), dv.astype(BF16)], axis=1)
        dh = mm_nt_cols(f"l{i}_d_h", dqkv, w_qkv)
        g_qkv = mm_tn_cols(f"l{i}_g_qkv", h, dqkv)
        dx, dxb, dg = rms_bwd(f"l{i}_norm_mix_bwd", xin, norm_mix[i], dh, dxm, deps=(g_qkv,))
        dg_mix.append(dg)
        if prev_a is not None:
            prev_a.finish(f, (dx,))
        cur_m.finish(f, (dx,))
        prev_a = _GradExchange(f"{i}a", kinds[:2], [g_qkv, g_o], chip, core)
        started = (prev_a.start_swap((dx,)),)
    prev_a.swap_to_scatter((dx,))
    prev_a.finish(f, (dx,))
    for lst in (dg_mix, dg_mlp, dsinks):
        lst.reverse()

    res = {}
    for kind, (w, m, v) in big.items():
        g = join_halves(f"join_{kind}", f[kind]).reshape(w.shape)
        res[kind] = (g, *adamw(f"adamw_{kind}", w, g, m, v))

    n_sink = a_sinks.size
    sink_rows = jnp.zeros((1, 8, d), F32).at[0, 0, :n_sink].set(jnp.concatenate(dsinks))
    parts = jnp.concatenate([jnp.stack(dg_mix), jnp.stack(dg_mlp), dg_final[None], sink_rows], axis=0)
    g_small = allreduce_small("allreduce_small", parts)

    def pack(mix, mlp, fin, snk):
        snk_row = jnp.zeros((1, d), F32).at[0, :n_sink].set(snk.reshape(-1))
        return jnp.concatenate([mix, mlp, fin[None], snk_row], axis=0)

    def unpack(t):
        return t[:depth], t[depth:2 * depth], t[2 * depth], t[2 * depth + 1, :n_sink].reshape(a_sinks.shape)

    small = adamw("adamw_small", pack(norm_mix, norm_mlp, final_norm, a_sinks), g_small,
                  pack(m_norm_mix, m_norm_mlp, m_final_norm, m_a_sinks),
                  pack(v_norm_mix, v_norm_mlp, v_final_norm, v_a_sinks))
    outs = []
    for idx in range(4):
        mix, mlp, fin, snk = unpack(g_small if idx == 0 else small[idx - 1])
        outs += [res["a_qkv"][idx], res["a_o"][idx], snk, res["b_qkv"][idx], res["b_o"][idx], mix, mlp,
                 res["mlp_in"][idx], res["mlp_out"][idx], fin]
    return (loss, dx.reshape(x.shape), *outs)
```

```python
import functools
import math

import jax
import jax.numpy as jnp
from jax import lax
from jax.experimental import pallas as pl
from jax.experimental.pallas import tpu as pltpu

F32 = jnp.float32
BF16 = jnp.bfloat16
MESH = pl.DeviceIdType.MESH

N_CHIPS = 4
HEAD_DIM = 64
LANES = 128
N_HEADS = 32
N_KV_A = 4
WINDOW = 128
RMS_EPS = 1e-5
ATTN_SCALE = 1.0 / math.sqrt(HEAD_DIM)
ADAM_LR, ADAM_B1, ADAM_B2, ADAM_EPS, ADAM_WD, ADAM_STEP = 0.001, 0.9, 0.999, 1e-08, 0.01, 10
NEG_BIG = -1e30
VMEM_LIMIT = 56 * 1024 * 1024

_DN = {"nn": (((1,), (0,)), ((), ())), "nt": (((1,), (1,)), ((), ())), "tn": (((0,), (0,)), ((), ()))}


def _dot(a, b, mode="nn"):
    return lax.dot_general(a, b, _DN[mode], preferred_element_type=F32)


def _params(*sem):
    return pltpu.CompilerParams(dimension_semantics=sem, vmem_limit_bytes=VMEM_LIMIT)


def _pick(n, prefs):
    for t in prefs:
        if n % t == 0:
            return t
    return n


def _mm(name, mode, a, b, *, grid, a_spec, b_spec, extras=(), extra_specs=(), out_shapes, out_specs, nk,
        acc_shape, epilogue, deps=()):
    n_ex, n_out = len(extras), len(out_shapes)
    first_out = 2 + n_ex + len(deps)

    def body(*refs):
        a_ref, b_ref = refs[0], refs[1]
        ex = refs[2:2 + n_ex]
        outs = refs[first_out:first_out + n_out]
        part = _dot(a_ref[...], b_ref[...], mode)

        def finish(acc):
            res = epilogue(acc, *[e[...] for e in ex])
            for o, r in zip(outs, res):
                o[...] = r.astype(o.dtype)

        if nk == 1:
            finish(part)
        else:
            acc_ref = refs[-1]
            k = pl.program_id(len(grid) - 1)

            @pl.when(k == 0)
            def _():
                acc_ref[...] = part

            @pl.when(k > 0)
            def _():
                acc_ref[...] += part

            @pl.when(k == nk - 1)
            def _():
                finish(acc_ref[...])

    sem = ("parallel",) * (len(grid) - 1) + ("arbitrary" if nk > 1 else "parallel",)
    return pl.pallas_call(
        body, name=name, grid=grid,
        in_specs=[a_spec, b_spec, *extra_specs, *[ANY] * len(deps)],
        out_specs=list(out_specs), out_shape=list(out_shapes),
        scratch_shapes=[] if nk == 1 else [pltpu.VMEM(acc_shape, F32)],
        compiler_params=_params(*sem),
    )(a, b, *extras, *deps)


def mm_cols(name, a, wg, epilogue, out_dtypes, tm=2048):
    m, k = a.shape
    c = wg.shape[2]
    tm = min(tm, m)
    tn = _pick(c, (512, 640, 256, 128))
    nj = c // tn
    o_spec = pl.BlockSpec((tm, tn), lambda i, j: (i, j))
    return _mm(name, "nn", a, wg, grid=(m // tm, N_CHIPS * nj),
               a_spec=pl.BlockSpec((tm, k), lambda i, j: (i, 0)),
               b_spec=pl.BlockSpec((None, k, tn), lambda i, j: (j // nj, 0, j % nj)),
               out_shapes=[jax.ShapeDtypeStruct((m, N_CHIPS * c), d) for d in out_dtypes],
               out_specs=[o_spec] * len(out_dtypes), nk=1, acc_shape=None, epilogue=epilogue)


def mm_res(name, a, w, res, tm=1024, tn=1024, tk=2048):
    m, k = a.shape
    n = w.shape[1]
    tm, tn, tk = min(tm, m), min(tn, n), min(tk, k)
    nk = k // tk
    return _mm(name, "nn", a, w, grid=(m // tm, n // tn, nk),
               a_spec=pl.BlockSpec((tm, tk), lambda i, j, kk: (i, kk)),
               b_spec=pl.BlockSpec((tk, tn), lambda i, j, kk: (kk, j)),
               extras=(res,), extra_specs=(pl.BlockSpec((tm, tn), lambda i, j, kk: (i, j)),),
               out_shapes=[jax.ShapeDtypeStruct((m, n), F32)],
               out_specs=[pl.BlockSpec((tm, tn), lambda i, j, kk: (i, j))], nk=nk, acc_shape=(tm, tn),
               epilogue=lambda acc, r: (acc + r,))[0]


def mm_nt(name, a, w, epilogue, extras=(), deps=(), tm=2048, tn=512):
    m, k = a.shape
    n = w.shape[0]
    tm, tn = min(tm, m), min(tn, n)
    o_spec = pl.BlockSpec((tm, tn), lambda i, j: (i, j))
    return _mm(name, "nt", a, w, grid=(m // tm, n // tn),
               a_spec=pl.BlockSpec((tm, k), lambda i, j: (i, 0)),
               b_spec=pl.BlockSpec((tn, k), lambda i, j: (j, 0)),
               extras=tuple(extras), extra_specs=(o_spec,) * len(extras),
               out_shapes=[jax.ShapeDtypeStruct((m, n), BF16)], out_specs=[o_spec], nk=1, acc_shape=None,
               epilogue=epilogue, deps=deps)[0]


def mm_nt_cols(name, dy, wg, deps=(), tm=1024, tn=2048):
    m = dy.shape[0]
    _, d, c = wg.shape
    tm, tn = min(tm, m), min(tn, d)
    tk = _pick(c, (1024, 768, 640, 512, 128))
    nkk = c // tk
    nk = N_CHIPS * nkk
    return _mm(name, "nt", dy, wg, grid=(m // tm, d // tn, nk),
               a_spec=pl.BlockSpec((tm, tk), lambda i, j, kk: (i, kk)),
               b_spec=pl.BlockSpec((None, tn, tk), lambda i, j, kk: (kk // nkk, j, kk % nkk)),
               out_shapes=[jax.ShapeDtypeStruct((m, d), F32)],
               out_specs=[pl.BlockSpec((tm, tn), lambda i, j, kk: (i, j))], nk=nk, acc_shape=(tm, tn),
               epilogue=lambda acc: (acc,), deps=deps)[0]


def mm_tn(name, a, b, tm=1024, tn=2048, tk=1024):
    m, p = a.shape
    q = b.shape[1]
    tm, tn, tk = min(tm, p), min(tn, q), min(tk, m)
    nk = m // tk
    return _mm(name, "tn", a, b, grid=(p // tm, q // tn, nk),
               a_spec=pl.BlockSpec((tk, tm), lambda i, j, kk: (kk, i)),
               b_spec=pl.BlockSpec((tk, tn), lambda i, j, kk: (kk, j)),
               out_shapes=[jax.ShapeDtypeStruct((p, q), BF16)],
               out_specs=[pl.BlockSpec((tm, tn), lambda i, j, kk: (i, j))], nk=nk, acc_shape=(tm, tn),
               epilogue=lambda acc: (acc,))[0]


def mm_tn_cols(name, a, dy, tm=2048, tk=1024):
    m, d = a.shape
    c = dy.shape[1] // N_CHIPS
    tm, tk = min(tm, d), min(tk, m)
    tn = _pick(c, (1024, 768, 640, 512, 128))
    nj = c // tn
    nk = m // tk
    return _mm(name, "tn", a, dy, grid=(d // tm, N_CHIPS * nj, nk),
               a_spec=pl.BlockSpec((tk, tm), lambda i, j, kk: (kk, i)),
               b_spec=pl.BlockSpec((tk, tn), lambda i, j, kk: (kk, j)),
               out_shapes=[jax.ShapeDtypeStruct((N_CHIPS, d, c), BF16)],
               out_specs=[pl.BlockSpec((None, tm, tn), lambda i, j, kk: (j // nj, i, j % nj))], nk=nk,
               acc_shape=(tm, tn), epilogue=lambda acc: (acc,))[0]


def _rows_to_8(v):
    tm, d = v.shape
    return jnp.sum(v.reshape(tm // 8, 8, d), axis=0)


def rms_fwd(name, x, gain, deps=(), tm=512):
    s, d = x.shape
    tm = min(tm, s)

    def body(x_ref, g_ref, *rest):
        h_ref = rest[-1]
        xv = x_ref[...]
        r = lax.rsqrt(jnp.mean(xv * xv, axis=-1, keepdims=True) + RMS_EPS)
        h_ref[...] = (xv * r * g_ref[...]).astype(BF16)

    row = pl.BlockSpec((tm, d), lambda i: (i, 0))
    return pl.pallas_call(
        body, name=name, grid=(s // tm,),
        in_specs=[row, pl.BlockSpec((1, d), lambda i: (0, 0)), *[ANY] * len(deps)], out_specs=row,
        out_shape=jax.ShapeDtypeStruct((s, d), BF16), compiler_params=_params("parallel"),
    )(x, gain.reshape(1, d), *deps)


def rms_bwd(name, x, gain, dh, dres, deps=(), tm=512):
    s, d = x.shape
    tm = min(tm, s)

    def body(x_ref, g_ref, dh_ref, dres_ref, *rest):
        dx_ref, dxb_ref, dg_ref = rest[-3:]
        xv = x_ref[...]
        r = lax.rsqrt(jnp.mean(xv * xv, axis=-1, keepdims=True) + RMS_EPS)
        xhat = xv * r
        dhv = dh_ref[...]
        dxhat = dhv * g_ref[...]
        dx = dres_ref[...] + r * (dxhat - xhat * jnp.mean(dxhat * xhat, axis=-1, keepdims=True))
        dx_ref[...] = dx
        dxb_ref[...] = dx.astype(BF16)

        @pl.when(pl.program_id(0) == 0)
        def _():
            dg_ref[...] = jnp.zeros_like(dg_ref)

        dg_ref[...] += _rows_to_8(dhv * xhat)

    row = pl.BlockSpec((tm, d), lambda i: (i, 0))
    return pl.pallas_call(
        body, name=name, grid=(s // tm,),
        in_specs=[row, pl.BlockSpec((1, d), lambda i: (0, 0)), row, row, *[ANY] * len(deps)],
        out_specs=[row, row, pl.BlockSpec((8, d), lambda i: (0, 0))],
        out_shape=[jax.ShapeDtypeStruct((s, d), F32), jax.ShapeDtypeStruct((s, d), BF16),
                   jax.ShapeDtypeStruct((8, d), F32)],
        compiler_params=_params("arbitrary"),
    )(x, gain.reshape(1, d), dh, dres, *deps)


def loss_head(name, x, gain, target, tm=512):
    s, d = x.shape
    tm = min(tm, s)

    def body(x_ref, g_ref, t_ref, loss_ref, dx_ref, dxb_ref, dg_ref):
        xv = x_ref[...]
        g = g_ref[...]
        r = lax.rsqrt(jnp.mean(xv * xv, axis=-1, keepdims=True) + RMS_EPS)
        xhat = xv * r
        err = xhat * g - t_ref[...]
        dy = err * (1.0 / d)
        dxhat = dy * g
        dx = r * (dxhat - xhat * jnp.mean(dxhat * xhat, axis=-1, keepdims=True))
        dx_ref[...] = dx
        dxb_ref[...] = dx.astype(BF16)

        @pl.when(pl.program_id(0) == 0)
        def _():
            dg_ref[...] = jnp.zeros_like(dg_ref)
            loss_ref[...] = jnp.zeros_like(loss_ref)

        dg_ref[...] += _rows_to_8(dy * xhat)
        loss_ref[...] += _rows_to_8(err * err) * (0.5 / d)

    row = pl.BlockSpec((tm, d), lambda i: (i, 0))
    vec = pl.BlockSpec((8, d), lambda i: (0, 0))
    return pl.pallas_call(
        body, name=name, grid=(s // tm,),
        in_specs=[row, pl.BlockSpec((1, d), lambda i: (0, 0)), row],
        out_specs=[vec, row, row, vec],
        out_shape=[jax.ShapeDtypeStruct((8, d), F32), jax.ShapeDtypeStruct((s, d), F32),
                   jax.ShapeDtypeStruct((s, d), BF16), jax.ShapeDtypeStruct((8, d), F32)],
        compiler_params=_params("arbitrary"),
    )(x, gain.reshape(1, d), target)


def _half_masks(dtype):
    lane = lax.broadcasted_iota(jnp.int32, (1, LANES), 1)
    lo = (lane < HEAD_DIM).astype(dtype)
    return lo, (1 - lo).astype(dtype)


def _swap_halves(v):
    return pltpu.roll(v, HEAD_DIM, axis=1)


SWA_BLOCKS = 8


def _swa_probs(s_raw, sink, slope, first):
    t = s_raw.shape[0]
    row = lax.broadcasted_iota(jnp.int32, (t, 2 * WINDOW), 0)
    col = lax.broadcasted_iota(jnp.int32, (t, 2 * WINDOW), 1)
    dist = row + WINDOW - col
    valid = (dist >= 0) & (dist < WINDOW)
    if first is not None:
        valid = valid & ((col >= WINDOW) | jnp.logical_not(first))
    s = jnp.where(valid, s_raw - slope * dist.astype(F32), NEG_BIG)
    m = jnp.maximum(jnp.max(s, axis=-1, keepdims=True), sink)
    e = jnp.exp(s - m)
    e_sink = jnp.exp(sink - m)
    inv = 1.0 / (jnp.sum(e, axis=-1, keepdims=True) + e_sink)
    return e * inv, e_sink * inv


def _swa_band(kp_ref, kc_ref, vp_ref, vc_ref, kv_half):
    lo_b, hi_b = _half_masks(F32)
    sel = jnp.where(kv_half == 0, lo_b, hi_b)
    k = jnp.concatenate([kp_ref[...], kc_ref[...]], axis=0).astype(F32) * sel
    v = jnp.concatenate([vp_ref[...], vc_ref[...]], axis=0).astype(F32) * sel
    k = (k + _swap_halves(k)).astype(BF16)
    v = (v + _swap_halves(v)).astype(BF16)
    return k, v, sel


def _swa_specs(nq):
    t = WINDOW
    q_spec = pl.BlockSpec((nq * t, LANES), lambda j, n, *_: (n, j))
    q_blocks = N_HEADS // 2

    def prev(off):
        return pl.BlockSpec((t, LANES), lambda j, n, *_: (jnp.maximum(nq * n - 1, 0), q_blocks + off + j // 8))

    def cur(off):
        return pl.BlockSpec((nq * t, LANES), lambda j, n, *_: (n, q_blocks + off + j // 8))

    kv_blocks = N_KV_A // 2
    return q_spec, [prev(0), cur(0), prev(kv_blocks), cur(kv_blocks)]


def _rows(a, b, n=1):
    return a[b * WINDOW:(b + n) * WINDOW]


def swa_fwd(name, qkv, sinks, slopes):
    s = qkv.shape[0]
    nq = min(SWA_BLOCKS, s // WINDOW)

    def body(sink_ref, slope_ref, q_ref, kp_ref, kc_ref, vp_ref, vc_ref, o_ref):
        j, n = pl.program_id(0), pl.program_id(1)
        k_all, v_all, _ = _swa_band(kp_ref, kc_ref, vp_ref, vc_ref, (j // 4) % 2)
        q = q_ref[...] * ATTN_SCALE
        masks = _half_masks(BF16)
        chains = [(b, hq) for b in range(nq) for hq in range(2)]
        s_raw = {(b, hq): _dot(_rows(q, b) * masks[hq], _rows(k_all, b, 2), "nt") for b, hq in chains}
        p = {(b, hq): _swa_probs(s_raw[(b, hq)], sink_ref[2 * j + hq], slope_ref[2 * j + hq],
                                 (n == 0) if b == 0 else None)[0] for b, hq in chains}
        outs = [sum(_dot(p[(b, hq)].astype(BF16), _rows(v_all, b, 2) * masks[hq]) for hq in range(2))
                for b in range(nq)]
        o_ref[...] = jnp.concatenate(outs, axis=0).astype(BF16)

    q_spec, kv_specs = _swa_specs(nq)
    return pl.pallas_call(
        body, name=name,
        grid_spec=pltpu.PrefetchScalarGridSpec(
            num_scalar_prefetch=2, grid=(N_HEADS // 2, s // (nq * WINDOW)),
            in_specs=[q_spec, *kv_specs], out_specs=q_spec),
        out_shape=jax.ShapeDtypeStruct((s, N_HEADS * HEAD_DIM), BF16),
        compiler_params=_params("parallel", "parallel"),
    )(sinks, slopes, qkv, qkv, qkv, qkv, qkv)


def swa_bwd(name, qkv, do, sinks, slopes):
    s = qkv.shape[0]
    t = WINDOW
    nq = min(SWA_BLOCKS, s // t)

    def body(sink_ref, slope_ref, q_ref, kp_ref, kc_ref, vp_ref, vc_ref, do_ref, dq_ref, dk_ref, dv_ref, ds_ref):
        j, n = pl.program_id(0), pl.program_id(1)
        k_all, v_all, sel = _swa_band(kp_ref, kc_ref, vp_ref, vc_ref, (j // 4) % 2)
        q = q_ref[...] * ATTN_SCALE
        do_v = do_ref[...]
        masks = _half_masks(BF16)

        @pl.when((j % 8 == 0) & (n == 0))
        def _():
            dk_ref[...] = jnp.zeros_like(dk_ref)
            dv_ref[...] = jnp.zeros_like(dv_ref)

        @pl.when(n == 0)
        def _():
            ds_ref[...] = jnp.zeros_like(ds_ref)

        chains = [(b, hq) for b in range(nq) for hq in range(2)]
        qm = {(b, hq): _rows(q, b) * masks[hq] for b, hq in chains}
        dom = {(b, hq): _rows(do_v, b) * masks[hq] for b, hq in chains}
        s_raw = {ch: _dot(qm[ch], _rows(k_all, ch[0], 2), "nt") for ch in chains}
        dp = {ch: _dot(dom[ch], _rows(v_all, ch[0], 2), "nt") for ch in chains}
        pb, dsc = {}, {}
        dsink = [jnp.zeros((), F32), jnp.zeros((), F32)]
        for ch in chains:
            b, hq = ch
            p, p_sink = _swa_probs(s_raw[ch], sink_ref[2 * j + hq], slope_ref[2 * j + hq], (n == 0) if b == 0 else None)
            delta = jnp.sum(p * dp[ch], axis=-1, keepdims=True)
            dsc[ch] = (p * (dp[ch] - delta)).astype(BF16)
            pb[ch] = p.astype(BF16)
            dsink[hq] = dsink[hq] - jnp.sum(p_sink * delta)
        for hq in range(2):
            ds_ref[hq:hq + 1, :] += jnp.zeros((1, LANES), F32) + dsink[hq]
        dq, dk, dv = [], [], []
        for b in range(nq):
            dq.append(sum(_dot(dsc[(b, hq)], _rows(k_all, b, 2) * masks[hq]) for hq in range(2)))
            dk_b = sum(_dot(dsc[(b, hq)], qm[(b, hq)], "tn") for hq in range(2))
            dv_b = sum(_dot(pb[(b, hq)], dom[(b, hq)], "tn") for hq in range(2))
            dk.append((dk_b + _swap_halves(dk_b)) * sel)
            dv.append((dv_b + _swap_halves(dv_b)) * sel)
        dq_ref[...] = (jnp.concatenate(dq, axis=0) * ATTN_SCALE).astype(BF16)

        @pl.when(n == 0)
        def _():
            dk_ref[pl.ds(0, t), :] += dk[0][t:]
            dv_ref[pl.ds(0, t), :] += dv[0][t:]

        @pl.when(n > 0)
        def _():
            start = pl.multiple_of((nq * n - 1) * t, t)
            dk_ref[pl.ds(start, 2 * t), :] += dk[0]
            dv_ref[pl.ds(start, 2 * t), :] += dv[0]

        for b in range(1, nq):
            start = pl.multiple_of((nq * n + b - 1) * t, t)
            dk_ref[pl.ds(start, 2 * t), :] += dk[b]
            dv_ref[pl.ds(start, 2 * t), :] += dv[b]

    q_spec, kv_specs = _swa_specs(nq)
    kv_out = pl.BlockSpec((s, LANES), lambda j, n, *_: (0, j // 8))
    return pl.pallas_call(
        body, name=name,
        grid_spec=pltpu.PrefetchScalarGridSpec(
            num_scalar_prefetch=2, grid=(N_HEADS // 2, s // (nq * t)),
            in_specs=[q_spec, *kv_specs, q_spec],
            out_specs=[q_spec, kv_out, kv_out, pl.BlockSpec((None, 8, LANES), lambda j, n, *_: (j, 0, 0))]),
        out_shape=[jax.ShapeDtypeStruct((s, N_HEADS * HEAD_DIM), BF16),
                   jax.ShapeDtypeStruct((s, N_KV_A * HEAD_DIM), F32),
                   jax.ShapeDtypeStruct((s, N_KV_A * HEAD_DIM), F32),
                   jax.ShapeDtypeStruct((N_HEADS // 2, 8, LANES), F32)],
        compiler_params=_params("arbitrary", "arbitrary"),
    )(sinks, slopes, qkv, qkv, qkv, qkv, qkv, do)


SB_TILE = 256


def _split_k(v):
    hi = v.astype(BF16)
    lo = (v - hi.astype(F32)).astype(BF16)
    return jnp.concatenate([hi, lo], axis=1)


def _tri2(t, inclusive):
    r = lax.broadcasted_iota(jnp.int32, (2 * t, t), 0)
    c = lax.broadcasted_iota(jnp.int32, (2 * t, t), 1)
    r = jnp.where(r >= t, r - t, r)
    return ((r >= c) if inclusive else (r > c)).astype(BF16)


def _sb_logs(z, before):
    neg_abs = lax.bitcast_convert_type(lax.bitcast_convert_type(z, jnp.uint32) | jnp.uint32(0x80000000), F32)
    l = jnp.log(1.0 + jnp.exp(neg_abs))
    lb = jnp.minimum(z, 0.0) - l
    lm = lb - z
    if before is not None:
        lm = jnp.where(before, lm, 0.0)
    return lb, lm


def _sb_weights(lb, sfx, c_lm, before):
    a = jnp.exp(lb + sfx + c_lm)
    if before is not None:
        a = jnp.where(before, a, 0.0)
    return a


def _sb_sweep(tile, i, init, diag):
    carry = lax.cond(i > 0, lambda: tile([i, i - 1], init, diag), lambda: tile([i], init, diag))
    rest = jnp.maximum(i - 1, 0)
    odd = rest % 2
    carry = lax.cond(odd == 1, lambda: tile([i - 2], carry, None), lambda: carry)
    base = i - 2 - odd
    return lax.fori_loop(0, rest // 2, lambda n, cr: tile([base - 2 * n, base - 2 * n - 1], cr, None), carry)


def _sb_specs(s, t):
    hp, nb = N_HEADS // 2, s // t
    q_spec = pl.BlockSpec((t, LANES), lambda h, i: (i, h))
    k_spec = pl.BlockSpec((s, LANES), lambda h, i: (0, hp + h))
    v_spec = pl.BlockSpec((s, LANES), lambda h, i: (0, 2 * hp + h))
    saved = pl.BlockSpec((None, 2, None, nb, t, t), lambda h, i: (h, 0, i, 0, 0, 0))
    return q_spec, k_spec, v_spec, saved


def sb_fwd(name, qkv, t=SB_TILE):
    s = qkv.shape[0]
    t = min(t, s // 2)
    nb = s // t

    def body(q_ref, k_ref, v_ref, ob_ref, of_ref, a_ref, sg_ref):
        i = pl.program_id(1)
        masks = _half_masks(BF16)
        q = q_ref[...] * ATTN_SCALE
        qm = [q * masks[0], q * masks[1]]
        tri_x = _tri2(t, False)
        r = lax.broadcasted_iota(jnp.int32, (t, t), 0)
        c = lax.broadcasted_iota(jnp.int32, (t, t), 1)
        diag = c < r

        def tile(js, carry, before):
            c0, c1, acc = carry
            cs = [c0, c1]
            kj = [k_ref[pl.ds(pl.multiple_of(j * t, t), t), :] for j in js]
            vj = [v_ref[pl.ds(pl.multiple_of(j * t, t), t), :] for j in js]
            chains = [(h, b) for b in range(len(js)) for h in range(2)]
            z = {(h, b): _dot(qm[h], kj[b], "nt") for h, b in chains}
            lb, sfx, c_at = {}, {}, {}
            for ch in chains:
                h, b = ch
                lb[ch], lm = _sb_logs(z[ch], before if b == 0 else None)
                sfx[ch] = _dot(_split_k(lm), tri_x)
                c_at[ch] = cs[h]
                cs[h] = cs[h] + jnp.sum(lm, axis=-1, keepdims=True)
            for ch in chains:
                h, b = ch
                ab = _sb_weights(lb[ch], sfx[ch], c_at[ch], before if b == 0 else None).astype(BF16)
                a_ref[h, js[b]] = ab
                sg_ref[h, js[b]] = jnp.exp(lb[ch]).astype(BF16)
                acc = acc + _dot(ab, vj[b] * masks[h])
            return cs[0], cs[1], acc

        zero = jnp.zeros((t, 1), F32)
        carry = _sb_sweep(tile, i, (zero, zero, jnp.zeros((t, LANES), F32)), diag)
        ob_ref[...] = carry[2].astype(BF16)
        of_ref[...] = carry[2]

    q_spec, k_spec, v_spec, saved = _sb_specs(s, t)
    width = N_HEADS * HEAD_DIM
    keep = jax.ShapeDtypeStruct((N_HEADS // 2, 2, nb, nb, t, t), BF16)
    return pl.pallas_call(
        body, name=name, grid=(N_HEADS // 2, nb),
        in_specs=[q_spec, k_spec, v_spec], out_specs=[q_spec, q_spec, saved, saved],
        out_shape=[jax.ShapeDtypeStruct((s, width), BF16), jax.ShapeDtypeStruct((s, width), F32), keep, keep],
        compiler_params=_params("parallel", "parallel"),
    )(qkv, qkv, qkv)


def sb_bwd(name, qkv, o_f32, do, a_all, sg_all, t=SB_TILE):
    s = qkv.shape[0]
    t = min(t, s // 2)

    def body(q_ref, k_ref, v_ref, o_ref, do_ref, a_ref, sg_ref, dq_ref, dk_ref, dv_ref):
        i = pl.program_id(1)
        masks = _half_masks(BF16)
        fmasks = _half_masks(F32)
        q = q_ref[...] * ATTN_SCALE
        do_v = do_ref[...]
        qm = [q * masks[0], q * masks[1]]
        dom = [do_v * masks[0], do_v * masks[1]]
        prod = do_v.astype(F32) * o_ref[...]
        delta = [jnp.sum(prod * fmasks[h], axis=-1, keepdims=True) for h in range(2)]
        tri_i = _tri2(t, True)
        r = lax.broadcasted_iota(jnp.int32, (t, t), 0)
        c = lax.broadcasted_iota(jnp.int32, (t, t), 1)
        diag = c < r

        @pl.when(i == 0)
        def _():
            dk_ref[...] = jnp.zeros_like(dk_ref)
            dv_ref[...] = jnp.zeros_like(dv_ref)

        def tile(js, carry, before):
            cd0, cd1, dq = carry
            cd = [cd0, cd1]
            starts = [pl.multiple_of(j * t, t) for j in js]
            kj = [k_ref[pl.ds(st, t), :] for st in starts]
            vj = [v_ref[pl.ds(st, t), :] for st in starts]
            chains = [(h, b) for b in range(len(js)) for h in range(2)]
            da = {(h, b): _dot(dom[h], vj[b], "nt") for h, b in chains}
            ab, de, dsfx, cd_at = {}, {}, {}, {}
            dv = [jnp.zeros((t, LANES), F32) for _ in js]
            dk = [jnp.zeros((t, LANES), F32) for _ in js]
            for ch in chains:
                h, b = ch
                ab[ch] = a_ref[h, js[b]]
                de[ch] = da[ch] * ab[ch].astype(F32)
                dsfx[ch] = _dot(_split_k(de[ch]), tri_i)
                dv[b] = dv[b] + _dot(ab[ch], dom[h], "tn")
                cd_at[ch] = cd[h]
                cd[h] = cd[h] + jnp.sum(de[ch], axis=-1, keepdims=True)
            for ch in chains:
                h, b = ch
                farther = delta[h] - cd_at[ch] - dsfx[ch]
                dz = de[ch] - sg_ref[h, js[b]].astype(F32) * (de[ch] + farther)
                if before is not None and b == 0:
                    dz = jnp.where(before, dz, 0.0)
                dzb = dz.astype(BF16)
                dq = dq + _dot(dzb, kj[b] * masks[h])
                dk[b] = dk[b] + _dot(dzb, qm[h], "tn")
            for b, st in enumerate(starts):
                dk_ref[pl.ds(st, t), :] += dk[b]
                dv_ref[pl.ds(st, t), :] += dv[b]
            return cd[0], cd[1], dq

        zero = jnp.zeros((t, 1), F32)
        carry = _sb_sweep(tile, i, (zero, zero, jnp.zeros((t, LANES), F32)), diag)
        dq_ref[...] = (carry[2] * ATTN_SCALE).astype(BF16)

    q_spec, k_spec, v_spec, saved = _sb_specs(s, t)
    kv_out = pl.BlockSpec((s, LANES), lambda h, i: (0, h))
    width = N_HEADS * HEAD_DIM
    return pl.pallas_call(
        body, name=name, grid=(N_HEADS // 2, s // t),
        in_specs=[q_spec, k_spec, v_spec, q_spec, q_spec, saved, saved], out_specs=[q_spec, kv_out, kv_out],
        out_shape=[jax.ShapeDtypeStruct((s, width), BF16), jax.ShapeDtypeStruct((s, width), F32),
                   jax.ShapeDtypeStruct((s, width), F32)],
        compiler_params=_params("parallel", "arbitrary"),
    )(qkv, qkv, qkv, o_f32, do, a_all, sg_all)


def adamw(name, w, g, m, v, tm=256):
    shape = w.shape
    c = shape[-1]
    rows = math.prod(shape[:-1])
    tm = min(tm, rows)

    def body(w_ref, g_ref, m_ref, v_ref, d_ref, mo_ref, vo_ref):
        gv = g_ref[...]
        m2 = ADAM_B1 * m_ref[...] + (1.0 - ADAM_B1) * gv
        v2 = ADAM_B2 * v_ref[...] + (1.0 - ADAM_B2) * (gv * gv)
        m_hat = m2 / (1.0 - ADAM_B1 ** ADAM_STEP)
        v_hat = v2 / (1.0 - ADAM_B2 ** ADAM_STEP)
        d_ref[...] = -ADAM_LR * (m_hat / (jnp.sqrt(v_hat) + ADAM_EPS) + ADAM_WD * w_ref[...])
        mo_ref[...] = m2
        vo_ref[...] = v2

    blk = pl.BlockSpec((tm, c), lambda i: (i, 0))
    outs = pl.pallas_call(
        body, name=name, grid=(rows // tm,), in_specs=[blk] * 4, out_specs=[blk] * 3,
        out_shape=[jax.ShapeDtypeStruct((rows, c), F32)] * 3, compiler_params=_params("parallel"),
    )(*[t.reshape(rows, c) for t in (w, g, m, v)])
    return [o.reshape(shape) for o in outs]


HBM = pl.BlockSpec(memory_space=pltpu.HBM)


def _place():
    x, y, c = lax.axis_index("x"), lax.axis_index("y"), lax.axis_index("c")
    return x, y, c, [(1 - x, y), (x, 1 - y), (1 - x, 1 - y)]


def _remote(src, dst, send, recv, dev):
    return pltpu.make_async_remote_copy(src_ref=src, dst_ref=dst, send_sem=send, recv_sem=recv, device_id=dev,
                                        device_id_type=MESH)


SEM = pl.BlockSpec(memory_space=pltpu.SEMAPHORE)
ANY = pl.BlockSpec(memory_space=pl.ANY)
DATAFLOW = pltpu.SideEffectType.DATAFLOW_SIDE_EFFECTING


def _in_hbm(v):
    return pltpu.with_memory_space_constraint(v, pltpu.HBM)


def split_start(name, bufs, n_copies, sends, after=()):
    nb, na = len(bufs), len(after)

    def body(*refs):
        send, recv = refs[nb + na], refs[nb + na + 1]
        for cp in sends(refs[:nb], send, recv):
            cp.start()
        refs[-1][...] = jnp.zeros_like(refs[-1])

    outs = pl.pallas_call(
        body, name=name,
        in_specs=[HBM] * nb + [ANY] * na,
        out_shape=(pltpu.SemaphoreType.DMA((n_copies,)), pltpu.SemaphoreType.DMA((n_copies,)),
                   *[pltpu.HBM(b.shape, b.dtype) for b in bufs], jax.ShapeDtypeStruct((8, LANES), F32)),
        out_specs=(SEM, SEM, *[HBM] * nb, pl.BlockSpec(memory_space=pltpu.VMEM)),
        input_output_aliases={i: 2 + i for i in range(nb)},
        compiler_params=pltpu.CompilerParams(has_side_effects=DATAFLOW),
    )(*[_in_hbm(b) for b in bufs], *after)
    return outs[0], outs[1], list(outs[2:2 + nb]), outs[-1]


def split_wait(name, send_sems, recv_sems, bufs, sends, arrivals, after):
    nb, na = len(bufs), len(after)

    def body(*refs):
        send, recv = refs[nb], refs[nb + 1]
        for cp in sends(refs[:nb], send, recv):
            cp.wait_send()
        for cp in arrivals(refs[:nb], send, recv):
            cp.wait_recv()

    outs = pl.pallas_call(
        body, name=name,
        in_specs=[HBM] * nb + [SEM, SEM] + [ANY] * na,
        out_shape=tuple(pltpu.HBM(b.shape, b.dtype) for b in bufs), out_specs=tuple([HBM] * nb),
        input_output_aliases={i: i for i in range(nb)},
        compiler_params=pltpu.CompilerParams(has_side_effects=DATAFLOW),
    )(*bufs, send_sems, recv_sems, *after)
    return list(outs)


def _gather_plan(n):
    def sends(refs, send, recv):
        x, y, c, chips = _place()
        me = 2 * x + y
        return [_remote(refs[t].at[me], refs[t].at[me], send.at[3 * t + k], recv.at[3 * t + k], (px, py, c))
                for t in range(n) for k, (px, py) in enumerate(chips)]

    def arrivals(refs, send, recv):
        x, y, c, chips = _place()
        return [_remote(refs[t].at[2 * px + py], refs[t].at[2 * px + py], send.at[3 * t + k], recv.at[3 * t + k],
                        (px, py, c)) for t in range(n) for k, (px, py) in enumerate(chips)]

    return 3 * n, sends, arrivals


def _swap_plan(n):
    def copies(refs, send, recv):
        x, y, c, _ = _place()
        return [_remote(refs[t].at[s, 1 - c], refs[n + t].at[s], send.at[N_CHIPS * t + s], recv.at[N_CHIPS * t + s],
                        (x, y, 1 - c)) for t in range(n) for s in range(N_CHIPS)]

    return N_CHIPS * n, copies, copies


def _scatter_plan(n):
    def sends(refs, send, recv):
        x, y, c, chips = _place()
        me = 2 * x + y
        return [_remote(refs[t].at[2 * px + py], refs[n + t].at[me], send.at[3 * t + k], recv.at[3 * t + k],
                        (px, py, c)) for t in range(n) for k, (px, py) in enumerate(chips)]

    def arrivals(refs, send, recv):
        x, y, c, chips = _place()
        return [_remote(refs[t].at[2 * px + py], refs[n + t].at[2 * px + py], send.at[3 * t + k], recv.at[3 * t + k],
                        (px, py, c)) for t in range(n) for k, (px, py) in enumerate(chips)]

    return 3 * n, sends, arrivals


def cast_into_slot(name, w, layer, chip, tm=256):
    _, r, c = w.shape
    tm = min(tm, r)

    def body(chip_ref, w_ref, o_ref):
        o_ref[...] = w_ref[...].astype(BF16)

    return pl.pallas_call(
        body, name=name,
        grid_spec=pltpu.PrefetchScalarGridSpec(
            num_scalar_prefetch=1, grid=(r // tm,),
            in_specs=[pl.BlockSpec((None, tm, c), lambda i, chip_ref: (layer, i, 0))],
            out_specs=pl.BlockSpec((None, tm, c), lambda i, chip_ref: (chip_ref[0], i, 0))),
        out_shape=jax.ShapeDtypeStruct((N_CHIPS, r, c), BF16), compiler_params=_params("parallel"),
    )(chip, w)


def join_halves(name, f):
    n = f.shape[0]

    def body(f_ref, o_ref, send, recv):
        x, y, c, _ = _place()
        sib = (x, y, 1 - c)
        sends = [_remote(f_ref.at[l, c], o_ref.at[l, c], send.at[l], recv.at[l], sib) for l in range(n)]
        for cp in sends:
            cp.start()
        for l in range(n):
            _remote(f_ref.at[l, 1 - c], o_ref.at[l, 1 - c], send.at[l], recv.at[l], sib).wait_recv()
        for cp in sends:
            cp.wait_send()

    return pl.pallas_call(
        body, name=name, in_specs=[HBM], out_specs=HBM, out_shape=jax.ShapeDtypeStruct(f.shape, f.dtype),
        input_output_aliases={0: 0},
        scratch_shapes=[pltpu.SemaphoreType.DMA((n,)), pltpu.SemaphoreType.DMA((n,))],
    )(f)


def add_sibling_half(name, g, a, core, tm=256):
    _, _, rh, c = g.shape
    tm = min(tm, rh)

    def body(core_ref, g_ref, a_ref, o_ref):
        o_ref[...] = (g_ref[...].astype(F32) + a_ref[...].astype(F32)).astype(BF16)

    return pl.pallas_call(
        body, name=name,
        grid_spec=pltpu.PrefetchScalarGridSpec(
            num_scalar_prefetch=1, grid=(N_CHIPS, rh // tm),
            in_specs=[pl.BlockSpec((None, None, tm, c), lambda s, i, core_ref: (s, core_ref[0], i, 0)),
                      pl.BlockSpec((None, tm, c), lambda s, i, core_ref: (s, i, 0))],
            out_specs=pl.BlockSpec((None, tm, c), lambda s, i, core_ref: (s, i, 0))),
        out_shape=jax.ShapeDtypeStruct(a.shape, BF16), compiler_params=_params("parallel", "parallel"),
    )(core, g, a)


def sum_chips_into(name, p, b, f, layer, chip, core, tm=256):
    _, rh, c = b.shape
    tm = min(tm, rh)

    def body(chip_ref, core_ref, p_ref, b_ref, f_ref, o_ref):
        acc = jnp.zeros((tm, c), F32)
        for s in range(N_CHIPS):
            acc = acc + jnp.where(chip_ref[0] == s, p_ref[s].astype(F32), b_ref[s].astype(F32))
        o_ref[...] = acc

    slots = pl.BlockSpec((N_CHIPS, tm, c), lambda i, chip_ref, core_ref: (0, i, 0))
    return pl.pallas_call(
        body, name=name,
        grid_spec=pltpu.PrefetchScalarGridSpec(
            num_scalar_prefetch=2, grid=(rh // tm,), in_specs=[slots, slots, ANY],
            out_specs=pl.BlockSpec((None, None, tm, c), lambda i, chip_ref, core_ref: (layer, core_ref[0], i, 0))),
        out_shape=jax.ShapeDtypeStruct(f.shape, F32), input_output_aliases={4: 0},
        compiler_params=_params("parallel"),
    )(chip, core, p, b, f)


N_DEV = 8


def allreduce_small(name, parts):
    p, _, d = parts.shape
    m_per = p * 8

    def body(x_ref, out_ref, all_ref, send_sems, recv_sems, local_sem):
        x, y, c, chips = _place()
        me, sibling = (x, y, c), (x, y, 1 - c)

        def rows(px, py, pc):
            return all_ref.at[pl.ds((4 * px + 2 * py + pc) * m_per, m_per), :]

        def copy(k, block, to, src=None):
            return _remote(rows(*block) if src is None else src, rows(*block), send_sems.at[k], recv_sems.at[k], to)

        mine = pltpu.make_async_copy(x_ref, rows(*me), local_sem)
        mine.start()
        first = [copy(0, me, sibling, src=x_ref)]
        first += [copy(1 + j, me, (*chip, c), src=x_ref) for j, chip in enumerate(chips)]
        for cp in first:
            cp.start()
        passed = [copy(4 + j, (*chip, c), sibling) for j, chip in enumerate(chips)]
        for j, chip in enumerate(chips):
            copy(1 + j, (*chip, c), me).wait_recv()
            passed[j].start()
        copy(0, sibling, me).wait_recv()
        for j, chip in enumerate(chips):
            copy(4 + j, (*chip, 1 - c), me).wait_recv()
        for cp in first + passed:
            cp.wait_send()
        mine.wait()
        acc = all_ref[pl.ds(0, m_per), :]
        for dev in range(1, N_DEV):
            acc = acc + all_ref[pl.ds(dev * m_per, m_per), :]
        out_ref[...] = jnp.sum(acc.reshape(p, 8, d), axis=1)

    vmem = pl.BlockSpec(memory_space=pltpu.VMEM)
    return pl.pallas_call(
        body, name=name, in_specs=[vmem], out_specs=vmem,
        out_shape=jax.ShapeDtypeStruct((p, d), F32),
        scratch_shapes=[pltpu.VMEM((N_DEV * m_per, d), F32), pltpu.SemaphoreType.DMA((7,)),
                        pltpu.SemaphoreType.DMA((7,)), pltpu.SemaphoreType.DMA],
        compiler_params=pltpu.CompilerParams(vmem_limit_bytes=VMEM_LIMIT),
    )(parts.reshape(m_per, d))


def _empty(shape, dtype):
    return _in_hbm(lax.empty(shape, dtype))


class _GradExchange:
    def __init__(self, layer, kinds, grads, chip, core):
        self.layer, self.kinds, self.chip, self.core = layer, kinds, chip, core
        self.g4 = [g.reshape(N_CHIPS, 2, g.shape[1] // 2, g.shape[2]) for g in grads]
        self.n = len(grads)

    def start_swap(self, after):
        n_copies, self.swap_sends, self.swap_arrivals = _swap_plan(self.n)
        lands = [_empty((N_CHIPS,) + g.shape[2:], BF16) for g in self.g4]
        self.swap = split_start(f"swap_start_l{self.layer}", self.g4 + lands, n_copies, self.swap_sends, after)
        return self.swap[3]

    def swap_to_scatter(self, after):
        send, recv, bufs, _ = self.swap
        bufs = split_wait(f"swap_wait_l{self.layer}", send, recv, bufs, self.swap_sends, self.swap_arrivals, after)
        g4, lands = bufs[:self.n], bufs[self.n:]
        self.p = [add_sibling_half(f"add_l{self.layer}_{k}", g4[t], lands[t], self.core)
                  for t, (k, _) in enumerate(self.kinds)]
        n_copies, self.sc_sends, self.sc_arrivals = _scatter_plan(self.n)
        lands = [_empty(p.shape, BF16) for p in self.p]
        self.scatter = split_start(f"scatter_start_l{self.layer}", self.p + lands, n_copies, self.sc_sends)
        return self.scatter[3]

    def finish(self, f, after):
        send, recv, bufs, _ = self.scatter
        bufs = split_wait(f"scatter_wait_l{self.layer}", send, recv, bufs, self.sc_sends, self.sc_arrivals, after)
        p, lands = bufs[:self.n], bufs[self.n:]
        for t, (kind, l) in enumerate(self.kinds):
            f[kind] = sum_chips_into(f"sum_l{self.layer}_{kind}", p[t], lands[t], f[kind], l, self.chip, self.core)


def _relu2(acc):
    r = jnp.maximum(acc, 0.0)
    return acc, r * r


def _relu2_bwd(acc, u):
    return (acc * (2.0 * jnp.maximum(u.astype(F32), 0.0)),)


def _same(acc):
    return (acc,)


def kernel(x, a_w_qkv, a_w_o, a_sinks, b_w_qkv, b_w_o, norm_mix, norm_mlp, mlp_w_in, mlp_w_out, final_norm, loss_target, m_a_w_qkv, m_a_w_o, m_a_sinks, m_b_w_qkv, m_b_w_o, m_norm_mix, m_norm_mlp, m_mlp_w_in, m_mlp_w_out, m_final_norm, v_a_w_qkv, v_a_w_o, v_a_sinks, v_b_w_qkv, v_b_w_o, v_norm_mix, v_norm_mlp, v_mlp_w_in, v_mlp_w_out, v_final_norm):
    _, s, d = x.shape
    depth = norm_mix.shape[0]
    width = N_HEADS * HEAD_DIM
    core = lax.axis_index("c").astype(jnp.int32).reshape(1)
    chip = (2 * lax.axis_index("x") + lax.axis_index("y")).astype(jnp.int32).reshape(1)
    slopes = jnp.power(2.0, -8.0 * (jnp.arange(N_HEADS, dtype=F32) + 1.0) / N_HEADS)
    qkv_of = {0: ("a_qkv", a_w_qkv), 1: ("b_qkv", b_w_qkv)}
    o_of = {0: ("a_o", a_w_o), 1: ("b_o", b_w_o)}

    def layer_kinds(i):
        return [(qkv_of[i % 2][0], i // 2), (o_of[i % 2][0], i // 2), ("mlp_in", i), ("mlp_out", i)]

    stacks = {"a_qkv": a_w_qkv, "a_o": a_w_o, "b_qkv": b_w_qkv, "b_o": b_w_o, "mlp_in": mlp_w_in, "mlp_out": mlp_w_out}
    slots = [[cast_into_slot(f"cast_l{i}_{k}", stacks[k], l, chip) for k, l in layer_kinds(i)] for i in range(depth)]
    n_gather, gather_sends, gather_arrivals = _gather_plan(2)

    def start_gather(tag, bufs, after):
        return split_start(f"gather_start_{tag}", bufs, n_gather, gather_sends, after)

    def wait_gather(tag, flight, after):
        return split_wait(f"gather_wait_{tag}", flight[0], flight[1], flight[2], gather_sends, gather_arrivals, after)

    saved, weights = [], []
    xc = x[0]
    flight_a = start_gather("l0a", slots[0][:2], ())
    flight_m = start_gather("l0m", slots[0][2:], (flight_a[3],))
    for i in range(depth):
        mixer, j = i % 2, i // 2
        w_qkv, w_o = wait_gather(f"l{i}a", flight_a, (xc,))
        w_o = w_o.reshape(width, d)
        started, this_m = (flight_m[3],), flight_m
        if i + 1 < depth:
            flight_a = start_gather(f"l{i + 1}a", slots[i + 1][:2], (w_qkv, flight_m[3]))
            flight_m = start_gather(f"l{i + 1}m", slots[i + 1][2:], (flight_a[3],))
            started = (flight_m[3],)
        h = rms_fwd(f"l{i}_norm_mix", xc, norm_mix[i], deps=started)
        qkv = mm_cols(f"l{i}_qkv", h, w_qkv, _same, (BF16,))[0]
        if mixer == 0:
            attn, kept = swa_fwd(f"l{i}_swa", qkv, a_sinks[j], slopes), ()
        else:
            attn, *kept = sb_fwd(f"l{i}_sb", qkv)
        xm = mm_res(f"l{i}_o", attn, w_o, xc)
        w_in, w_out = wait_gather(f"l{i}m", this_m, (xm,))
        w_out = w_out.reshape(-1, d)
        weights.append((w_qkv, w_o, w_in, w_out))
        h2 = rms_fwd(f"l{i}_norm_mlp", xm, norm_mlp[i])
        u, hh = mm_cols(f"l{i}_in", h2, w_in, _relu2, (BF16, BF16))
        xn = mm_res(f"l{i}_out", hh, w_out, xm)
        saved.append((xc, h, qkv, attn, kept, xm, h2, u, hh))
        xc = xn
    loss_rows, dx, dxb, dg_final = loss_head("loss_head", xc, final_norm, loss_target[0])
    loss = lax.psum(jnp.sum(loss_rows), ("x", "y", "c"))

    big = {"a_qkv": (a_w_qkv, m_a_w_qkv, v_a_w_qkv), "a_o": (a_w_o, m_a_w_o, v_a_w_o),
           "b_qkv": (b_w_qkv, m_b_w_qkv, v_b_w_qkv), "b_o": (b_w_o, m_b_w_o, v_b_w_o),
           "mlp_in": (mlp_w_in, m_mlp_w_in, v_mlp_w_in), "mlp_out": (mlp_w_out, m_mlp_w_out, v_mlp_w_out)}
    f = {k: _empty((w.shape[0], 2, w.shape[1] // 2, w.shape[2]), F32) for k, (w, _, _) in big.items()}
    dg_mix, dg_mlp, dsinks = [], [], []
    prev_a, started = None, ()
    for i in reversed(range(depth)):
        mixer, j = i % 2, i // 2
        xin, h, qkv, attn, kept, xm, h2, u, hh = saved[i]
        w_qkv, w_o, w_in, w_out = weights[i]
        kinds = layer_kinds(i)
        du = mm_nt(f"l{i}_d_hidden", dxb, w_out, _relu2_bwd, (u,), deps=started)
        g_out = mm_tn(f"l{i}_g_out", hh, dxb).reshape(N_CHIPS, -1, d)
        started = (g_out,) if prev_a is None else (g_out, prev_a.swap_to_scatter((g_out,)))
        dh2 = mm_nt_cols(f"l{i}_d_h2", du, w_in, deps=started)
        g_in = mm_tn_cols(f"l{i}_g_in", h2, du)
        cur_m = _GradExchange(f"{i}m", kinds[2:], [g_in, g_out], chip, core)
        started = (g_in, cur_m.start_swap((g_in,)))
        dxm, dxmb, dg = rms_bwd(f"l{i}_norm_mlp_bwd", xm, norm_mlp[i], dh2, dx, deps=started)
        dg_mlp.append(dg)
        g_o = mm_tn(f"l{i}_g_o", attn, dxmb).reshape(N_CHIPS, -1, d)
        dattn = mm_nt(f"l{i}_d_attn", dxmb, w_o, _same, deps=(g_o, cur_m.swap_to_scatter((g_o,))))
        if mixer == 0:
            dq, dk, dv, dsk = swa_bwd(f"l{i}_swa_bwd", qkv, dattn, a_sinks[j], slopes)
            dsinks.append(dsk[:, :2, 0].reshape(N_HEADS))
        else:
            attn_f32, weights_a, sigmoids = kept
            dq, dk, dv = sb_bwd(f"l{i}_sb_bwd", qkv, attn_f32, dattn, weights_a, sigmoids)
        dqkv = jnp.concatenate([dq, dk.astype(BF16), dv.astype(BF16)], axis=1)
        dh = mm_nt_cols(f"l{i}_d_h", dqkv, w_qkv)
        g_qkv = mm_tn_cols(f"l{i}_g_qkv", h, dqkv)
        dx, dxb, dg = rms_bwd(f"l{i}_norm_mix_bwd", xin, norm_mix[i], dh, dxm, deps=(g_qkv,))
        dg_mix.append(dg)
        if prev_a is not None:
            prev_a.finish(f, (dx,))
        cur_m.finish(f, (dx,))
        prev_a = _GradExchange(f"{i}a", kinds[:2], [g_qkv, g_o], chip, core)
        started = (prev_a.start_swap((dx,)),)
    prev_a.swap_to_scatter((dx,))
    prev_a.finish(f, (dx,))
    for lst in (dg_mix, dg_mlp, dsinks):
        lst.reverse()

    res = {}
    for kind, (w, m, v) in big.items():
        g = join_halves(f"join_{kind}", f[kind]).reshape(w.shape)
        res[kind] = (g, *adamw(f"adamw_{kind}", w, g, m, v))

    n_sink = a_sinks.size
    sink_rows = jnp.zeros((1, 8, d), F32).at[0, 0, :n_sink].set(jnp.concatenate(dsinks))
    parts = jnp.concatenate([jnp.stack(dg_mix), jnp.stack(dg_mlp), dg_final[None], sink_rows], axis=0)
    g_small = allreduce_small("allreduce_small", parts)

    def pack(mix, mlp, fin, snk):
        snk_row = jnp.zeros((1, d), F32).at[0, :n_sink].set(snk.reshape(-1))
        return jnp.concatenate([mix, mlp, fin[None], snk_row], axis=0)

    def unpack(t):
        return t[:depth], t[depth:2 * depth], t[2 * depth], t[2 * depth + 1, :n_sink].reshape(a_sinks.shape)

    small = adamw("adamw_small", pack(norm_mix, norm_mlp, final_norm, a_sinks), g_small,
                  pack(m_norm_mix, m_norm_mlp, m_final_norm, m_a_sinks),
                  pack(v_norm_mix, v_norm_mlp, v_final_norm, v_a_sinks))
    outs = []
    for idx in range(4):
        mix, mlp, fin, snk = unpack(g_small if idx == 0 else small[idx - 1])
        outs += [res["a_qkv"][idx], res["a_o"][idx], snk, res["b_qkv"][idx], res["b_o"][idx], mix, mlp,
                 res["mlp_in"][idx], res["mlp_out"][idx], fin]
    return (loss, dx.reshape(x.shape), *outs)
```

```python
import functools
import math

import jax
import jax.numpy as jnp
from jax import lax
from jax.experimental import pallas as pl
from jax.experimental.pallas import tpu as pltpu

F32 = jnp.float32
BF16 = jnp.bfloat16
MESH = pl.DeviceIdType.MESH

N_CHIPS = 4
HEAD_DIM = 64
LANES = 128
N_HEADS = 32
N_KV_A = 4
WINDOW = 128
RMS_EPS = 1e-5
ATTN_SCALE = 1.0 / math.sqrt(HEAD_DIM)
ADAM_LR, ADAM_B1, ADAM_B2, ADAM_EPS, ADAM_WD, ADAM_STEP = 0.001, 0.9, 0.999, 1e-08, 0.01, 10
NEG_BIG = -1e30
VMEM_LIMIT = 56 * 1024 * 1024

_DN = {"nn": (((1,), (0,)), ((), ())), "nt": (((1,), (1,)), ((), ())), "tn": (((0,), (0,)), ((), ()))}


def _dot(a, b, mode="nn"):
    return lax.dot_general(a, b, _DN[mode], preferred_element_type=F32)


def _params(*sem):
    return pltpu.CompilerParams(dimension_semantics=sem, vmem_limit_bytes=VMEM_LIMIT)


def _pick(n, prefs):
    for t in prefs:
        if n % t == 0:
            return t
    return n


def _mm(name, mode, a, b, *, grid, a_spec, b_spec, extras=(), extra_specs=(), out_shapes, out_specs, nk,
        acc_shape, epilogue, deps=()):
    n_ex, n_out = len(extras), len(out_shapes)
    first_out = 2 + n_ex + len(deps)

    def body(*refs):
        a_ref, b_ref = refs[0], refs[1]
        ex = refs[2:2 + n_ex]
        outs = refs[first_out:first_out + n_out]
        part = _dot(a_ref[...], b_ref[...], mode)

        def finish(acc):
            res = epilogue(acc, *[e[...] for e in ex])
            for o, r in zip(outs, res):
                o[...] = r.astype(o.dtype)

        if nk == 1:
            finish(part)
        else:
            acc_ref = refs[-1]
            k = pl.program_id(len(grid) - 1)

            @pl.when(k == 0)
            def _():
                acc_ref[...] = part

            @pl.when(k > 0)
            def _():
                acc_ref[...] += part

            @pl.when(k == nk - 1)
            def _():
                finish(acc_ref[...])

    sem = ("parallel",) * (len(grid) - 1) + ("arbitrary" if nk > 1 else "parallel",)
    return pl.pallas_call(
        body, name=name, grid=grid,
        in_specs=[a_spec, b_spec, *extra_specs, *[ANY] * len(deps)],
        out_specs=list(out_specs), out_shape=list(out_shapes),
        scratch_shapes=[] if nk == 1 else [pltpu.VMEM(acc_shape, F32)],
        compiler_params=_params(*sem),
    )(a, b, *extras, *deps)


def mm_cols(name, a, wg, epilogue, out_dtypes, tm=2048):
    m, k = a.shape
    c = wg.shape[2]
    tm = min(tm, m)
    tn = _pick(c, (512, 640, 256, 128))
    nj = c // tn
    o_spec = pl.BlockSpec((tm, tn), lambda i, j: (i, j))
    return _mm(name, "nn", a, wg, grid=(m // tm, N_CHIPS * nj),
               a_spec=pl.BlockSpec((tm, k), lambda i, j: (i, 0)),
               b_spec=pl.BlockSpec((None, k, tn), lambda i, j: (j // nj, 0, j % nj)),
               out_shapes=[jax.ShapeDtypeStruct((m, N_CHIPS * c), d) for d in out_dtypes],
               out_specs=[o_spec] * len(out_dtypes), nk=1, acc_shape=None, epilogue=epilogue)


def mm_res(name, a, w, res, deps=(), tm=1024, tn=1024, tk=2048):
    m, k = a.shape
    n = w.shape[1]
    tm, tn, tk = min(tm, m), min(tn, n), min(tk, k)
    nk = k // tk
    return _mm(name, "nn", a, w, grid=(m // tm, n // tn, nk),
               a_spec=pl.BlockSpec((tm, tk), lambda i, j, kk: (i, kk)),
               b_spec=pl.BlockSpec((tk, tn), lambda i, j, kk: (kk, j)),
               extras=(res,), extra_specs=(pl.BlockSpec((tm, tn), lambda i, j, kk: (i, j)),),
               out_shapes=[jax.ShapeDtypeStruct((m, n), F32)],
               out_specs=[pl.BlockSpec((tm, tn), lambda i, j, kk: (i, j))], nk=nk, acc_shape=(tm, tn),
               epilogue=lambda acc, r: (acc + r,), deps=deps)[0]


def mm_nt(name, a, w, epilogue, extras=(), deps=(), tm=2048, tn=512):
    m, k = a.shape
    n = w.shape[0]
    tm, tn = min(tm, m), min(tn, n)
    o_spec = pl.BlockSpec((tm, tn), lambda i, j: (i, j))
    return _mm(name, "nt", a, w, grid=(m // tm, n // tn),
               a_spec=pl.BlockSpec((tm, k), lambda i, j: (i, 0)),
               b_spec=pl.BlockSpec((tn, k), lambda i, j: (j, 0)),
               extras=tuple(extras), extra_specs=(o_spec,) * len(extras),
               out_shapes=[jax.ShapeDtypeStruct((m, n), BF16)], out_specs=[o_spec], nk=1, acc_shape=None,
               epilogue=epilogue, deps=deps)[0]


def mm_nt_cols(name, dy, wg, deps=(), tm=1024, tn=2048):
    m = dy.shape[0]
    _, d, c = wg.shape
    tm, tn = min(tm, m), min(tn, d)
    tk = _pick(c, (1024, 768, 640, 512, 128))
    nkk = c // tk
    nk = N_CHIPS * nkk
    return _mm(name, "nt", dy, wg, grid=(m // tm, d // tn, nk),
               a_spec=pl.BlockSpec((tm, tk), lambda i, j, kk: (i, kk)),
               b_spec=pl.BlockSpec((None, tn, tk), lambda i, j, kk: (kk // nkk, j, kk % nkk)),
               out_shapes=[jax.ShapeDtypeStruct((m, d), F32)],
               out_specs=[pl.BlockSpec((tm, tn), lambda i, j, kk: (i, j))], nk=nk, acc_shape=(tm, tn),
               epilogue=lambda acc: (acc,), deps=deps)[0]


def mm_tn(name, a, b, tm=1024, tn=2048, tk=1024):
    m, p = a.shape
    q = b.shape[1]
    tm, tn, tk = min(tm, p), min(tn, q), min(tk, m)
    nk = m // tk
    return _mm(name, "tn", a, b, grid=(p // tm, q // tn, nk),
               a_spec=pl.BlockSpec((tk, tm), lambda i, j, kk: (kk, i)),
               b_spec=pl.BlockSpec((tk, tn), lambda i, j, kk: (kk, j)),
               out_shapes=[jax.ShapeDtypeStruct((p, q), BF16)],
               out_specs=[pl.BlockSpec((tm, tn), lambda i, j, kk: (i, j))], nk=nk, acc_shape=(tm, tn),
               epilogue=lambda acc: (acc,))[0]


def mm_tn_cols(name, a, dy, tm=2048, tk=1024):
    m, d = a.shape
    c = dy.shape[1] // N_CHIPS
    tm, tk = min(tm, d), min(tk, m)
    tn = _pick(c, (1024, 768, 640, 512, 128))
    nj = c // tn
    nk = m // tk
    return _mm(name, "tn", a, dy, grid=(d // tm, N_CHIPS * nj, nk),
               a_spec=pl.BlockSpec((tk, tm), lambda i, j, kk: (kk, i)),
               b_spec=pl.BlockSpec((tk, tn), lambda i, j, kk: (kk, j)),
               out_shapes=[jax.ShapeDtypeStruct((N_CHIPS, d, c), BF16)],
               out_specs=[pl.BlockSpec((None, tm, tn), lambda i, j, kk: (j // nj, i, j % nj))], nk=nk,
               acc_shape=(tm, tn), epilogue=lambda acc: (acc,))[0]


def _rows_to_8(v):
    tm, d = v.shape
    return jnp.sum(v.reshape(tm // 8, 8, d), axis=0)


def rms_fwd(name, x, gain, deps=(), tm=512):
    s, d = x.shape
    tm = min(tm, s)

    def body(x_ref, g_ref, *rest):
        h_ref = rest[-1]
        xv = x_ref[...]
        r = lax.rsqrt(jnp.mean(xv * xv, axis=-1, keepdims=True) + RMS_EPS)
        h_ref[...] = (xv * r * g_ref[...]).astype(BF16)

    row = pl.BlockSpec((tm, d), lambda i: (i, 0))
    return pl.pallas_call(
        body, name=name, grid=(s // tm,),
        in_specs=[row, pl.BlockSpec((1, d), lambda i: (0, 0)), *[ANY] * len(deps)], out_specs=row,
        out_shape=jax.ShapeDtypeStruct((s, d), BF16), compiler_params=_params("parallel"),
    )(x, gain.reshape(1, d), *deps)


def rms_bwd(name, x, gain, dh, dres, deps=(), tm=512):
    s, d = x.shape
    tm = min(tm, s)

    def body(x_ref, g_ref, dh_ref, dres_ref, *rest):
        dx_ref, dxb_ref, dg_ref = rest[-3:]
        xv = x_ref[...]
        r = lax.rsqrt(jnp.mean(xv * xv, axis=-1, keepdims=True) + RMS_EPS)
        xhat = xv * r
        dhv = dh_ref[...]
        dxhat = dhv * g_ref[...]
        dx = dres_ref[...] + r * (dxhat - xhat * jnp.mean(dxhat * xhat, axis=-1, keepdims=True))
        dx_ref[...] = dx
        dxb_ref[...] = dx.astype(BF16)

        @pl.when(pl.program_id(0) == 0)
        def _():
            dg_ref[...] = jnp.zeros_like(dg_ref)

        dg_ref[...] += _rows_to_8(dhv * xhat)

    row = pl.BlockSpec((tm, d), lambda i: (i, 0))
    return pl.pallas_call(
        body, name=name, grid=(s // tm,),
        in_specs=[row, pl.BlockSpec((1, d), lambda i: (0, 0)), row, row, *[ANY] * len(deps)],
        out_specs=[row, row, pl.BlockSpec((8, d), lambda i: (0, 0))],
        out_shape=[jax.ShapeDtypeStruct((s, d), F32), jax.ShapeDtypeStruct((s, d), BF16),
                   jax.ShapeDtypeStruct((8, d), F32)],
        compiler_params=_params("arbitrary"),
    )(x, gain.reshape(1, d), dh, dres, *deps)


def loss_head(name, x, gain, target, tm=512):
    s, d = x.shape
    tm = min(tm, s)

    def body(x_ref, g_ref, t_ref, loss_ref, dx_ref, dxb_ref, dg_ref):
        xv = x_ref[...]
        g = g_ref[...]
        r = lax.rsqrt(jnp.mean(xv * xv, axis=-1, keepdims=True) + RMS_EPS)
        xhat = xv * r
        err = xhat * g - t_ref[...]
        dy = err * (1.0 / d)
        dxhat = dy * g
        dx = r * (dxhat - xhat * jnp.mean(dxhat * xhat, axis=-1, keepdims=True))
        dx_ref[...] = dx
        dxb_ref[...] = dx.astype(BF16)

        @pl.when(pl.program_id(0) == 0)
        def _():
            dg_ref[...] = jnp.zeros_like(dg_ref)
            loss_ref[...] = jnp.zeros_like(loss_ref)

        dg_ref[...] += _rows_to_8(dy * xhat)
        loss_ref[...] += _rows_to_8(err * err) * (0.5 / d)

    row = pl.BlockSpec((tm, d), lambda i: (i, 0))
    vec = pl.BlockSpec((8, d), lambda i: (0, 0))
    return pl.pallas_call(
        body, name=name, grid=(s // tm,),
        in_specs=[row, pl.BlockSpec((1, d), lambda i: (0, 0)), row],
        out_specs=[vec, row, row, vec],
        out_shape=[jax.ShapeDtypeStruct((8, d), F32), jax.ShapeDtypeStruct((s, d), F32),
                   jax.ShapeDtypeStruct((s, d), BF16), jax.ShapeDtypeStruct((8, d), F32)],
        compiler_params=_params("arbitrary"),
    )(x, gain.reshape(1, d), target)


def _half_masks(dtype):
    lane = lax.broadcasted_iota(jnp.int32, (1, LANES), 1)
    lo = (lane < HEAD_DIM).astype(dtype)
    return lo, (1 - lo).astype(dtype)


def _swap_halves(v):
    return pltpu.roll(v, HEAD_DIM, axis=1)


SWA_BLOCKS = 8


def _swa_probs(s_raw, sink, slope, first):
    t = s_raw.shape[0]
    row = lax.broadcasted_iota(jnp.int32, (t, 2 * WINDOW), 0)
    col = lax.broadcasted_iota(jnp.int32, (t, 2 * WINDOW), 1)
    dist = row + WINDOW - col
    valid = (dist >= 0) & (dist < WINDOW)
    if first is not None:
        valid = valid & ((col >= WINDOW) | jnp.logical_not(first))
    s = jnp.where(valid, s_raw - slope * dist.astype(F32), NEG_BIG)
    m = jnp.maximum(jnp.max(s, axis=-1, keepdims=True), sink)
    e = jnp.exp(s - m)
    e_sink = jnp.exp(sink - m)
    inv = 1.0 / (jnp.sum(e, axis=-1, keepdims=True) + e_sink)
    return e * inv, e_sink * inv


def _swa_band(kp_ref, kc_ref, vp_ref, vc_ref, kv_half):
    lo_b, hi_b = _half_masks(F32)
    sel = jnp.where(kv_half == 0, lo_b, hi_b)
    k = jnp.concatenate([kp_ref[...], kc_ref[...]], axis=0).astype(F32) * sel
    v = jnp.concatenate([vp_ref[...], vc_ref[...]], axis=0).astype(F32) * sel
    k = (k + _swap_halves(k)).astype(BF16)
    v = (v + _swap_halves(v)).astype(BF16)
    return k, v, sel


def _swa_specs(nq):
    t = WINDOW
    q_spec = pl.BlockSpec((nq * t, LANES), lambda j, n, *_: (n, j))
    q_blocks = N_HEADS // 2

    def prev(off):
        return pl.BlockSpec((t, LANES), lambda j, n, *_: (jnp.maximum(nq * n - 1, 0), q_blocks + off + j // 8))

    def cur(off):
        return pl.BlockSpec((nq * t, LANES), lambda j, n, *_: (n, q_blocks + off + j // 8))

    kv_blocks = N_KV_A // 2
    return q_spec, [prev(0), cur(0), prev(kv_blocks), cur(kv_blocks)]


def _rows(a, b, n=1):
    return a[b * WINDOW:(b + n) * WINDOW]


def swa_fwd(name, qkv, sinks, slopes):
    s = qkv.shape[0]
    nq = min(SWA_BLOCKS, s // WINDOW)

    def body(sink_ref, slope_ref, q_ref, kp_ref, kc_ref, vp_ref, vc_ref, o_ref):
        j, n = pl.program_id(0), pl.program_id(1)
        k_all, v_all, _ = _swa_band(kp_ref, kc_ref, vp_ref, vc_ref, (j // 4) % 2)
        q = q_ref[...] * ATTN_SCALE
        masks = _half_masks(BF16)
        chains = [(b, hq) for b in range(nq) for hq in range(2)]
        s_raw = {(b, hq): _dot(_rows(q, b) * masks[hq], _rows(k_all, b, 2), "nt") for b, hq in chains}
        p = {(b, hq): _swa_probs(s_raw[(b, hq)], sink_ref[2 * j + hq], slope_ref[2 * j + hq],
                                 (n == 0) if b == 0 else None)[0] for b, hq in chains}
        outs = [sum(_dot(p[(b, hq)].astype(BF16), _rows(v_all, b, 2) * masks[hq]) for hq in range(2))
                for b in range(nq)]
        o_ref[...] = jnp.concatenate(outs, axis=0).astype(BF16)

    q_spec, kv_specs = _swa_specs(nq)
    return pl.pallas_call(
        body, name=name,
        grid_spec=pltpu.PrefetchScalarGridSpec(
            num_scalar_prefetch=2, grid=(N_HEADS // 2, s // (nq * WINDOW)),
            in_specs=[q_spec, *kv_specs], out_specs=q_spec),
        out_shape=jax.ShapeDtypeStruct((s, N_HEADS * HEAD_DIM), BF16),
        compiler_params=_params("parallel", "parallel"),
    )(sinks, slopes, qkv, qkv, qkv, qkv, qkv)


def swa_bwd(name, qkv, do, sinks, slopes):
    s = qkv.shape[0]
    t = WINDOW
    nq = min(SWA_BLOCKS, s // t)

    def body(sink_ref, slope_ref, q_ref, kp_ref, kc_ref, vp_ref, vc_ref, do_ref, dq_ref, dk_ref, dv_ref, ds_ref):
        j, n = pl.program_id(0), pl.program_id(1)
        k_all, v_all, sel = _swa_band(kp_ref, kc_ref, vp_ref, vc_ref, (j // 4) % 2)
        q = q_ref[...] * ATTN_SCALE
        do_v = do_ref[...]
        masks = _half_masks(BF16)

        @pl.when((j % 8 == 0) & (n == 0))
        def _():
            dk_ref[...] = jnp.zeros_like(dk_ref)
            dv_ref[...] = jnp.zeros_like(dv_ref)

        @pl.when(n == 0)
        def _():
            ds_ref[...] = jnp.zeros_like(ds_ref)

        chains = [(b, hq) for b in range(nq) for hq in range(2)]
        qm = {(b, hq): _rows(q, b) * masks[hq] for b, hq in chains}
        dom = {(b, hq): _rows(do_v, b) * masks[hq] for b, hq in chains}
        s_raw = {ch: _dot(qm[ch], _rows(k_all, ch[0], 2), "nt") for ch in chains}
        dp = {ch: _dot(dom[ch], _rows(v_all, ch[0], 2), "nt") for ch in chains}
        pb, dsc = {}, {}
        dsink = [jnp.zeros((), F32), jnp.zeros((), F32)]
        for ch in chains:
            b, hq = ch
            p, p_sink = _swa_probs(s_raw[ch], sink_ref[2 * j + hq], slope_ref[2 * j + hq], (n == 0) if b == 0 else None)
            delta = jnp.sum(p * dp[ch], axis=-1, keepdims=True)
            dsc[ch] = (p * (dp[ch] - delta)).astype(BF16)
            pb[ch] = p.astype(BF16)
            dsink[hq] = dsink[hq] - jnp.sum(p_sink * delta)
        for hq in range(2):
            ds_ref[hq:hq + 1, :] += jnp.zeros((1, LANES), F32) + dsink[hq]
        dq, dk, dv = [], [], []
        for b in range(nq):
            dq.append(sum(_dot(dsc[(b, hq)], _rows(k_all, b, 2) * masks[hq]) for hq in range(2)))
            dk_b = sum(_dot(dsc[(b, hq)], qm[(b, hq)], "tn") for hq in range(2))
            dv_b = sum(_dot(pb[(b, hq)], dom[(b, hq)], "tn") for hq in range(2))
            dk.append((dk_b + _swap_halves(dk_b)) * sel)
            dv.append((dv_b + _swap_halves(dv_b)) * sel)
        dq_ref[...] = (jnp.concatenate(dq, axis=0) * ATTN_SCALE).astype(BF16)

        @pl.when(n == 0)
        def _():
            dk_ref[pl.ds(0, t), :] += dk[0][t:]
            dv_ref[pl.ds(0, t), :] += dv[0][t:]

        @pl.when(n > 0)
        def _():
            start = pl.multiple_of((nq * n - 1) * t, t)
            dk_ref[pl.ds(start, 2 * t), :] += dk[0]
            dv_ref[pl.ds(start, 2 * t), :] += dv[0]

        for b in range(1, nq):
            start = pl.multiple_of((nq * n + b - 1) * t, t)
            dk_ref[pl.ds(start, 2 * t), :] += dk[b]
            dv_ref[pl.ds(start, 2 * t), :] += dv[b]

    q_spec, kv_specs = _swa_specs(nq)
    kv_out = pl.BlockSpec((s, LANES), lambda j, n, *_: (0, j // 8))
    return pl.pallas_call(
        body, name=name,
        grid_spec=pltpu.PrefetchScalarGridSpec(
            num_scalar_prefetch=2, grid=(N_HEADS // 2, s // (nq * t)),
            in_specs=[q_spec, *kv_specs, q_spec],
            out_specs=[q_spec, kv_out, kv_out, pl.BlockSpec((None, 8, LANES), lambda j, n, *_: (j, 0, 0))]),
        out_shape=[jax.ShapeDtypeStruct((s, N_HEADS * HEAD_DIM), BF16),
                   jax.ShapeDtypeStruct((s, N_KV_A * HEAD_DIM), F32),
                   jax.ShapeDtypeStruct((s, N_KV_A * HEAD_DIM), F32),
                   jax.ShapeDtypeStruct((N_HEADS // 2, 8, LANES), F32)],
        compiler_params=_params("arbitrary", "arbitrary"),
    )(sinks, slopes, qkv, qkv, qkv, qkv, qkv, do)


SB_TILE = 256


def _split_k(v):
    hi = v.astype(BF16)
    lo = (v - hi.astype(F32)).astype(BF16)
    return jnp.concatenate([hi, lo], axis=1)


def _tri2(t, inclusive):
    r = lax.broadcasted_iota(jnp.int32, (2 * t, t), 0)
    c = lax.broadcasted_iota(jnp.int32, (2 * t, t), 1)
    r = jnp.where(r >= t, r - t, r)
    return ((r >= c) if inclusive else (r > c)).astype(BF16)


def _sb_logs(z, before):
    neg_abs = lax.bitcast_convert_type(lax.bitcast_convert_type(z, jnp.uint32) | jnp.uint32(0x80000000), F32)
    l = jnp.log(1.0 + jnp.exp(neg_abs))
    lb = jnp.minimum(z, 0.0) - l
    lm = lb - z
    if before is not None:
        lm = jnp.where(before, lm, 0.0)
    return lb, lm


def _sb_weights(lb, sfx, c_lm, before):
    a = jnp.exp(lb + sfx + c_lm)
    if before is not None:
        a = jnp.where(before, a, 0.0)
    return a


def _sb_sweep(tile, i, init, diag):
    carry = lax.cond(i > 0, lambda: tile([i, i - 1], init, diag), lambda: tile([i], init, diag))
    rest = jnp.maximum(i - 1, 0)
    odd = rest % 2
    carry = lax.cond(odd == 1, lambda: tile([i - 2], carry, None), lambda: carry)
    base = i - 2 - odd
    return lax.fori_loop(0, rest // 2, lambda n, cr: tile([base - 2 * n, base - 2 * n - 1], cr, None), carry)


def _sb_specs(s, t):
    hp, nb = N_HEADS // 2, s // t
    q_spec = pl.BlockSpec((t, LANES), lambda h, i: (i, h))
    k_spec = pl.BlockSpec((s, LANES), lambda h, i: (0, hp + h))
    v_spec = pl.BlockSpec((s, LANES), lambda h, i: (0, 2 * hp + h))
    saved = pl.BlockSpec((None, 2, None, nb, t, t), lambda h, i: (h, 0, i, 0, 0, 0))
    return q_spec, k_spec, v_spec, saved


def sb_fwd(name, qkv, t=SB_TILE):
    s = qkv.shape[0]
    t = min(t, s // 2)
    nb = s // t

    def body(q_ref, k_ref, v_ref, ob_ref, of_ref, a_ref, sg_ref):
        i = pl.program_id(1)
        masks = _half_masks(BF16)
        q = q_ref[...] * ATTN_SCALE
        qm = [q * masks[0], q * masks[1]]
        tri_x = _tri2(t, False)
        r = lax.broadcasted_iota(jnp.int32, (t, t), 0)
        c = lax.broadcasted_iota(jnp.int32, (t, t), 1)
        diag = c < r

        def tile(js, carry, before):
            c0, c1, acc = carry
            cs = [c0, c1]
            kj = [k_ref[pl.ds(pl.multiple_of(j * t, t), t), :] for j in js]
            vj = [v_ref[pl.ds(pl.multiple_of(j * t, t), t), :] for j in js]
            chains = [(h, b) for b in range(len(js)) for h in range(2)]
            z = {(h, b): _dot(qm[h], kj[b], "nt") for h, b in chains}
            lb, sfx, c_at = {}, {}, {}
            for ch in chains:
                h, b = ch
                lb[ch], lm = _sb_logs(z[ch], before if b == 0 else None)
                sfx[ch] = _dot(_split_k(lm), tri_x)
                c_at[ch] = cs[h]
                cs[h] = cs[h] + jnp.sum(lm, axis=-1, keepdims=True)
            for ch in chains:
                h, b = ch
                ab = _sb_weights(lb[ch], sfx[ch], c_at[ch], before if b == 0 else None).astype(BF16)
                a_ref[h, js[b]] = ab
                sg_ref[h, js[b]] = jnp.exp(lb[ch]).astype(BF16)
                acc = acc + _dot(ab, vj[b] * masks[h])
            return cs[0], cs[1], acc

        zero = jnp.zeros((t, 1), F32)
        carry = _sb_sweep(tile, i, (zero, zero, jnp.zeros((t, LANES), F32)), diag)
        ob_ref[...] = carry[2].astype(BF16)
        of_ref[...] = carry[2]

    q_spec, k_spec, v_spec, saved = _sb_specs(s, t)
    width = N_HEADS * HEAD_DIM
    keep = jax.ShapeDtypeStruct((N_HEADS // 2, 2, nb, nb, t, t), BF16)
    return pl.pallas_call(
        body, name=name, grid=(N_HEADS // 2, nb),
        in_specs=[q_spec, k_spec, v_spec], out_specs=[q_spec, q_spec, saved, saved],
        out_shape=[jax.ShapeDtypeStruct((s, width), BF16), jax.ShapeDtypeStruct((s, width), F32), keep, keep],
        compiler_params=_params("parallel", "parallel"),
    )(qkv, qkv, qkv)


def sb_bwd(name, qkv, o_f32, do, a_all, sg_all, t=SB_TILE):
    s = qkv.shape[0]
    t = min(t, s // 2)

    def body(q_ref, k_ref, v_ref, o_ref, do_ref, a_ref, sg_ref, dq_ref, dk_ref, dv_ref):
        i = pl.program_id(1)
        masks = _half_masks(BF16)
        fmasks = _half_masks(F32)
        q = q_ref[...] * ATTN_SCALE
        do_v = do_ref[...]
        qm = [q * masks[0], q * masks[1]]
        dom = [do_v * masks[0], do_v * masks[1]]
        prod = do_v.astype(F32) * o_ref[...]
        delta = [jnp.sum(prod * fmasks[h], axis=-1, keepdims=True) for h in range(2)]
        tri_i = _tri2(t, True)
        r = lax.broadcasted_iota(jnp.int32, (t, t), 0)
        c = lax.broadcasted_iota(jnp.int32, (t, t), 1)
        diag = c < r

        @pl.when(i == 0)
        def _():
            dk_ref[...] = jnp.zeros_like(dk_ref)
            dv_ref[...] = jnp.zeros_like(dv_ref)

        def tile(js, carry, before):
            cd0, cd1, dq = carry
            cd = [cd0, cd1]
            starts = [pl.multiple_of(j * t, t) for j in js]
            kj = [k_ref[pl.ds(st, t), :] for st in starts]
            vj = [v_ref[pl.ds(st, t), :] for st in starts]
            chains = [(h, b) for b in range(len(js)) for h in range(2)]
            da = {(h, b): _dot(dom[h], vj[b], "nt") for h, b in chains}
            ab, de, dsfx, cd_at = {}, {}, {}, {}
            dv = [jnp.zeros((t, LANES), F32) for _ in js]
            dk = [jnp.zeros((t, LANES), F32) for _ in js]
            for ch in chains:
                h, b = ch
                ab[ch] = a_ref[h, js[b]]
                de[ch] = da[ch] * ab[ch].astype(F32)
                dsfx[ch] = _dot(_split_k(de[ch]), tri_i)
                dv[b] = dv[b] + _dot(ab[ch], dom[h], "tn")
                cd_at[ch] = cd[h]
                cd[h] = cd[h] + jnp.sum(de[ch], axis=-1, keepdims=True)
            for ch in chains:
                h, b = ch
                farther = delta[h] - cd_at[ch] - dsfx[ch]
                dz = de[ch] - sg_ref[h, js[b]].astype(F32) * (de[ch] + farther)
                if before is not None and b == 0:
                    dz = jnp.where(before, dz, 0.0)
                dzb = dz.astype(BF16)
                dq = dq + _dot(dzb, kj[b] * masks[h])
                dk[b] = dk[b] + _dot(dzb, qm[h], "tn")
            for b, st in enumerate(starts):
                dk_ref[pl.ds(st, t), :] += dk[b]
                dv_ref[pl.ds(st, t), :] += dv[b]
            return cd[0], cd[1], dq

        zero = jnp.zeros((t, 1), F32)
        carry = _sb_sweep(tile, i, (zero, zero, jnp.zeros((t, LANES), F32)), diag)
        dq_ref[...] = (carry[2] * ATTN_SCALE).astype(BF16)

    q_spec, k_spec, v_spec, saved = _sb_specs(s, t)
    kv_out = pl.BlockSpec((s, LANES), lambda h, i: (0, h))
    width = N_HEADS * HEAD_DIM
    return pl.pallas_call(
        body, name=name, grid=(N_HEADS // 2, s // t),
        in_specs=[q_spec, k_spec, v_spec, q_spec, q_spec, saved, saved], out_specs=[q_spec, kv_out, kv_out],
        out_shape=[jax.ShapeDtypeStruct((s, width), BF16), jax.ShapeDtypeStruct((s, width), F32),
                   jax.ShapeDtypeStruct((s, width), F32)],
        compiler_params=_params("parallel", "arbitrary"),
    )(qkv, qkv, qkv, o_f32, do, a_all, sg_all)


def adamw(name, w, g, m, v, tm=256):
    shape = w.shape
    c = shape[-1]
    rows = math.prod(shape[:-1])
    tm = min(tm, rows)

    def body(w_ref, g_ref, m_ref, v_ref, d_ref, mo_ref, vo_ref):
        gv = g_ref[...]
        m2 = ADAM_B1 * m_ref[...] + (1.0 - ADAM_B1) * gv
        v2 = ADAM_B2 * v_ref[...] + (1.0 - ADAM_B2) * (gv * gv)
        m_hat = m2 / (1.0 - ADAM_B1 ** ADAM_STEP)
        v_hat = v2 / (1.0 - ADAM_B2 ** ADAM_STEP)
        d_ref[...] = -ADAM_LR * (m_hat / (jnp.sqrt(v_hat) + ADAM_EPS) + ADAM_WD * w_ref[...])
        mo_ref[...] = m2
        vo_ref[...] = v2

    blk = pl.BlockSpec((tm, c), lambda i: (i, 0))
    outs = pl.pallas_call(
        body, name=name, grid=(rows // tm,), in_specs=[blk] * 4, out_specs=[blk] * 3,
        out_shape=[jax.ShapeDtypeStruct((rows, c), F32)] * 3, compiler_params=_params("parallel"),
    )(*[t.reshape(rows, c) for t in (w, g, m, v)])
    return [o.reshape(shape) for o in outs]


HBM = pl.BlockSpec(memory_space=pltpu.HBM)


def _place():
    x, y, c = lax.axis_index("x"), lax.axis_index("y"), lax.axis_index("c")
    return x, y, c, [(1 - x, y), (x, 1 - y), (1 - x, 1 - y)]


def _remote(src, dst, send, recv, dev):
    return pltpu.make_async_remote_copy(src_ref=src, dst_ref=dst, send_sem=send, recv_sem=recv, device_id=dev,
                                        device_id_type=MESH)


SEM = pl.BlockSpec(memory_space=pltpu.SEMAPHORE)
ANY = pl.BlockSpec(memory_space=pl.ANY)
DATAFLOW = pltpu.SideEffectType.DATAFLOW_SIDE_EFFECTING


def _in_hbm(v):
    return pltpu.with_memory_space_constraint(v, pltpu.HBM)


def split_start(name, bufs, n_copies, sends, after=()):
    nb, na = len(bufs), len(after)

    def body(*refs):
        send, recv = refs[nb + na], refs[nb + na + 1]
        for cp in sends(refs[:nb], send, recv):
            cp.start()
        refs[-1][...] = jnp.zeros_like(refs[-1])

    outs = pl.pallas_call(
        body, name=name,
        in_specs=[HBM] * nb + [ANY] * na,
        out_shape=(pltpu.SemaphoreType.DMA((n_copies,)), pltpu.SemaphoreType.DMA((n_copies,)),
                   *[pltpu.HBM(b.shape, b.dtype) for b in bufs], jax.ShapeDtypeStruct((8, LANES), F32)),
        out_specs=(SEM, SEM, *[HBM] * nb, pl.BlockSpec(memory_space=pltpu.VMEM)),
        input_output_aliases={i: 2 + i for i in range(nb)},
        compiler_params=pltpu.CompilerParams(has_side_effects=DATAFLOW),
    )(*[_in_hbm(b) for b in bufs], *after)
    return outs[0], outs[1], list(outs[2:2 + nb]), outs[-1]


def split_wait(name, send_sems, recv_sems, bufs, sends, arrivals, after):
    nb, na = len(bufs), len(after)

    def body(*refs):
        send, recv = refs[nb], refs[nb + 1]
        for cp in sends(refs[:nb], send, recv):
            cp.wait_send()
        for cp in arrivals(refs[:nb], send, recv):
            cp.wait_recv()

    outs = pl.pallas_call(
        body, name=name,
        in_specs=[HBM] * nb + [SEM, SEM] + [ANY] * na,
        out_shape=tuple(pltpu.HBM(b.shape, b.dtype) for b in bufs), out_specs=tuple([HBM] * nb),
        input_output_aliases={i: i for i in range(nb)},
        compiler_params=pltpu.CompilerParams(has_side_effects=DATAFLOW),
    )(*bufs, send_sems, recv_sems, *after)
    return list(outs)


def _gather_plan(n):
    def sends(refs, send, recv):
        x, y, c, chips = _place()
        me = 2 * x + y
        return [_remote(refs[t].at[me, c], refs[t].at[me, c], send.at[3 * t + k], recv.at[3 * t + k], (px, py, c))
                for t in range(n) for k, (px, py) in enumerate(chips)]

    def arrivals(refs, send, recv):
        x, y, c, chips = _place()
        return [_remote(refs[t].at[2 * px + py, c], refs[t].at[2 * px + py, c], send.at[3 * t + k], recv.at[3 * t + k],
                        (px, py, c)) for t in range(n) for k, (px, py) in enumerate(chips)]

    return 3 * n, sends, arrivals


def _relay_plan(n):
    def sends(refs, send, recv):
        x, y, c, chips = _place()
        return [_remote(refs[t].at[2 * px + py, c], refs[t].at[2 * px + py, c], send.at[3 * t + k], recv.at[3 * t + k],
                        (x, y, 1 - c)) for t in range(n) for k, (px, py) in enumerate(chips)]

    def arrivals(refs, send, recv):
        x, y, c, chips = _place()
        return [_remote(refs[t].at[2 * px + py, 1 - c], refs[t].at[2 * px + py, 1 - c], send.at[3 * t + k],
                        recv.at[3 * t + k], (x, y, 1 - c)) for t in range(n) for k, (px, py) in enumerate(chips)]

    return 3 * n, sends, arrivals


def _swap_plan(n):
    def copies(refs, send, recv):
        x, y, c, _ = _place()
        return [_remote(refs[t].at[s, 1 - c], refs[n + t].at[s], send.at[N_CHIPS * t + s], recv.at[N_CHIPS * t + s],
                        (x, y, 1 - c)) for t in range(n) for s in range(N_CHIPS)]

    return N_CHIPS * n, copies, copies


def _scatter_plan(n):
    def sends(refs, send, recv):
        x, y, c, chips = _place()
        me = 2 * x + y
        return [_remote(refs[t].at[2 * px + py], refs[n + t].at[me], send.at[3 * t + k], recv.at[3 * t + k],
                        (px, py, c)) for t in range(n) for k, (px, py) in enumerate(chips)]

    def arrivals(refs, send, recv):
        x, y, c, chips = _place()
        return [_remote(refs[t].at[2 * px + py], refs[n + t].at[2 * px + py], send.at[3 * t + k], recv.at[3 * t + k],
                        (px, py, c)) for t in range(n) for k, (px, py) in enumerate(chips)]

    return 3 * n, sends, arrivals


def cast_into_slot(name, w, layer, chip, deps=(), tm=256):
    _, r, c = w.shape
    tm = min(tm, r)

    def body(chip_ref, w_ref, *rest):
        rest[-1][...] = w_ref[...].astype(BF16)

    return pl.pallas_call(
        body, name=name,
        grid_spec=pltpu.PrefetchScalarGridSpec(
            num_scalar_prefetch=1, grid=(r // tm,),
            in_specs=[pl.BlockSpec((None, tm, c), lambda i, chip_ref: (layer, i, 0)), *[ANY] * len(deps)],
            out_specs=pl.BlockSpec((None, tm, c), lambda i, chip_ref: (chip_ref[0], i, 0))),
        out_shape=jax.ShapeDtypeStruct((N_CHIPS, r, c), BF16), compiler_params=_params("parallel"),
    )(chip, w, *deps)


def join_halves(name, f):
    n = f.shape[0]

    def body(f_ref, o_ref, send, recv):
        x, y, c, _ = _place()
        sib = (x, y, 1 - c)
        sends = [_remote(f_ref.at[l, c], o_ref.at[l, c], send.at[l], recv.at[l], sib) for l in range(n)]
        for cp in sends:
            cp.start()
        for l in range(n):
            _remote(f_ref.at[l, 1 - c], o_ref.at[l, 1 - c], send.at[l], recv.at[l], sib).wait_recv()
        for cp in sends:
            cp.wait_send()

    return pl.pallas_call(
        body, name=name, in_specs=[HBM], out_specs=HBM, out_shape=jax.ShapeDtypeStruct(f.shape, f.dtype),
        input_output_aliases={0: 0},
        scratch_shapes=[pltpu.SemaphoreType.DMA((n,)), pltpu.SemaphoreType.DMA((n,))],
    )(f)


def add_sibling_half(name, g, a, core, tm=256):
    _, _, rh, c = g.shape
    tm = min(tm, rh)

    def body(core_ref, g_ref, a_ref, o_ref):
        o_ref[...] = (g_ref[...].astype(F32) + a_ref[...].astype(F32)).astype(BF16)

    return pl.pallas_call(
        body, name=name,
        grid_spec=pltpu.PrefetchScalarGridSpec(
            num_scalar_prefetch=1, grid=(N_CHIPS, rh // tm),
            in_specs=[pl.BlockSpec((None, None, tm, c), lambda s, i, core_ref: (s, core_ref[0], i, 0)),
                      pl.BlockSpec((None, tm, c), lambda s, i, core_ref: (s, i, 0))],
            out_specs=pl.BlockSpec((None, tm, c), lambda s, i, core_ref: (s, i, 0))),
        out_shape=jax.ShapeDtypeStruct(a.shape, BF16), compiler_params=_params("parallel", "parallel"),
    )(core, g, a)


def sum_chips_into(name, p, b, f, layer, chip, core, tm=256):
    _, rh, c = b.shape
    tm = min(tm, rh)

    def body(chip_ref, core_ref, p_ref, b_ref, f_ref, o_ref):
        acc = jnp.zeros((tm, c), F32)
        for s in range(N_CHIPS):
            acc = acc + jnp.where(chip_ref[0] == s, p_ref[s].astype(F32), b_ref[s].astype(F32))
        o_ref[...] = acc

    slots = pl.BlockSpec((N_CHIPS, tm, c), lambda i, chip_ref, core_ref: (0, i, 0))
    return pl.pallas_call(
        body, name=name,
        grid_spec=pltpu.PrefetchScalarGridSpec(
            num_scalar_prefetch=2, grid=(rh // tm,), in_specs=[slots, slots, ANY],
            out_specs=pl.BlockSpec((None, None, tm, c), lambda i, chip_ref, core_ref: (layer, core_ref[0], i, 0))),
        out_shape=jax.ShapeDtypeStruct(f.shape, F32), input_output_aliases={4: 0},
        compiler_params=_params("parallel"),
    )(chip, core, p, b, f)


N_DEV = 8


def allreduce_small(name, parts):
    p, _, d = parts.shape
    m_per = p * 8

    def body(x_ref, out_ref, all_ref, send_sems, recv_sems, local_sem):
        x, y, c, chips = _place()
        me, sibling = (x, y, c), (x, y, 1 - c)

        def rows(px, py, pc):
            return all_ref.at[pl.ds((4 * px + 2 * py + pc) * m_per, m_per), :]

        def copy(k, block, to, src=None):
            return _remote(rows(*block) if src is None else src, rows(*block), send_sems.at[k], recv_sems.at[k], to)

        mine = pltpu.make_async_copy(x_ref, rows(*me), local_sem)
        mine.start()
        first = [copy(0, me, sibling, src=x_ref)]
        first += [copy(1 + j, me, (*chip, c), src=x_ref) for j, chip in enumerate(chips)]
        for cp in first:
            cp.start()
        passed = [copy(4 + j, (*chip, c), sibling) for j, chip in enumerate(chips)]
        for j, chip in enumerate(chips):
            copy(1 + j, (*chip, c), me).wait_recv()
            passed[j].start()
        copy(0, sibling, me).wait_recv()
        for j, chip in enumerate(chips):
            copy(4 + j, (*chip, 1 - c), me).wait_recv()
        for cp in first + passed:
            cp.wait_send()
        mine.wait()
        acc = all_ref[pl.ds(0, m_per), :]
        for dev in range(1, N_DEV):
            acc = acc + all_ref[pl.ds(dev * m_per, m_per), :]
        out_ref[...] = jnp.sum(acc.reshape(p, 8, d), axis=1)

    vmem = pl.BlockSpec(memory_space=pltpu.VMEM)
    return pl.pallas_call(
        body, name=name, in_specs=[vmem], out_specs=vmem,
        out_shape=jax.ShapeDtypeStruct((p, d), F32),
        scratch_shapes=[pltpu.VMEM((N_DEV * m_per, d), F32), pltpu.SemaphoreType.DMA((7,)),
                        pltpu.SemaphoreType.DMA((7,)), pltpu.SemaphoreType.DMA],
        compiler_params=pltpu.CompilerParams(vmem_limit_bytes=VMEM_LIMIT),
    )(parts.reshape(m_per, d))


def _empty(shape, dtype):
    return _in_hbm(lax.empty(shape, dtype))


class _GradExchange:
    def __init__(self, layer, kinds, grads, chip, core):
        self.layer, self.kinds, self.chip, self.core = layer, kinds, chip, core
        self.g4 = [g.reshape(N_CHIPS, 2, g.shape[1] // 2, g.shape[2]) for g in grads]
        self.n = len(grads)

    def start_swap(self, after):
        n_copies, self.swap_sends, self.swap_arrivals = _swap_plan(self.n)
        lands = [_empty((N_CHIPS,) + g.shape[2:], BF16) for g in self.g4]
        self.swap = split_start(f"swap_start_l{self.layer}", self.g4 + lands, n_copies, self.swap_sends, after)
        return self.swap[3]

    def swap_to_scatter(self, after):
        send, recv, bufs, _ = self.swap
        bufs = split_wait(f"swap_wait_l{self.layer}", send, recv, bufs, self.swap_sends, self.swap_arrivals, after)
        g4, lands = bufs[:self.n], bufs[self.n:]
        self.p = [add_sibling_half(f"add_l{self.layer}_{k}", g4[t], lands[t], self.core)
                  for t, (k, _) in enumerate(self.kinds)]
        n_copies, self.sc_sends, self.sc_arrivals = _scatter_plan(self.n)
        lands = [_empty(p.shape, BF16) for p in self.p]
        self.scatter = split_start(f"scatter_start_l{self.layer}", self.p + lands, n_copies, self.sc_sends)
        return self.scatter[3]

    def finish(self, f, after):
        send, recv, bufs, _ = self.scatter
        bufs = split_wait(f"scatter_wait_l{self.layer}", send, recv, bufs, self.sc_sends, self.sc_arrivals, after)
        p, lands = bufs[:self.n], bufs[self.n:]
        for t, (kind, l) in enumerate(self.kinds):
            f[kind] = sum_chips_into(f"sum_l{self.layer}_{kind}", p[t], lands[t], f[kind], l, self.chip, self.core)


def _relu2(acc):
    r = jnp.maximum(acc, 0.0)
    return acc, r * r


def _relu2_bwd(acc, u):
    return (acc * (2.0 * jnp.maximum(u.astype(F32), 0.0)),)


def _same(acc):
    return (acc,)


def kernel(x, a_w_qkv, a_w_o, a_sinks, b_w_qkv, b_w_o, norm_mix, norm_mlp, mlp_w_in, mlp_w_out, final_norm, loss_target, m_a_w_qkv, m_a_w_o, m_a_sinks, m_b_w_qkv, m_b_w_o, m_norm_mix, m_norm_mlp, m_mlp_w_in, m_mlp_w_out, m_final_norm, v_a_w_qkv, v_a_w_o, v_a_sinks, v_b_w_qkv, v_b_w_o, v_norm_mix, v_norm_mlp, v_mlp_w_in, v_mlp_w_out, v_final_norm):
    _, s, d = x.shape
    depth = norm_mix.shape[0]
    width = N_HEADS * HEAD_DIM
    core = lax.axis_index("c").astype(jnp.int32).reshape(1)
    chip = (2 * lax.axis_index("x") + lax.axis_index("y")).astype(jnp.int32).reshape(1)
    slopes = jnp.power(2.0, -8.0 * (jnp.arange(N_HEADS, dtype=F32) + 1.0) / N_HEADS)
    qkv_of = {0: ("a_qkv", a_w_qkv), 1: ("b_qkv", b_w_qkv)}
    o_of = {0: ("a_o", a_w_o), 1: ("b_o", b_w_o)}

    def layer_kinds(i):
        return [(qkv_of[i % 2][0], i // 2), (o_of[i % 2][0], i // 2), ("mlp_in", i), ("mlp_out", i)]

    stacks = {"a_qkv": a_w_qkv, "a_o": a_w_o, "b_qkv": b_w_qkv, "b_o": b_w_o, "mlp_in": mlp_w_in, "mlp_out": mlp_w_out}

    def slots(i, deps=()):
        return [cast_into_slot(f"cast_l{i}_{k}", stacks[k], l, chip, deps) for k, l in layer_kinds(i)]

    n_gather, gather_sends, gather_arrivals = _gather_plan(2)
    _, relay_sends, relay_arrivals = _relay_plan(2)

    class Flight:
        def __init__(self, tag, bufs, after):
            self.tag = tag
            halves = [b.reshape(N_CHIPS, 2, b.shape[1] // 2, b.shape[2]) for b in bufs]
            self.shapes = [b.shape for b in bufs]
            self.state = split_start(f"gather_start_{tag}", halves, n_gather, gather_sends, after)
            self.token = self.state[3]

        def relay(self, after):
            send, recv, bufs, _ = self.state
            bufs = split_wait(f"gather_wait_{self.tag}", send, recv, bufs, gather_sends, gather_arrivals, after)
            self.state = split_start(f"relay_start_{self.tag}", bufs, n_gather, relay_sends)
            return self.state[3]

        def land(self, after):
            send, recv, bufs, _ = self.state
            bufs = split_wait(f"relay_wait_{self.tag}", send, recv, bufs, relay_sends, relay_arrivals, after)
            return [b.reshape(shape) for b, shape in zip(bufs, self.shapes)]

    saved, weights = [], []
    xc = x[0]
    first = slots(0)
    flight_a = Flight("l0a", first[:2], ())
    flight_m = Flight("l0m", first[2:], (flight_a.token,))
    flight_a.relay((xc, flight_m.token))
    for i in range(depth):
        mixer, j = i % 2, i // 2
        w_qkv, w_o = flight_a.land((xc,))
        w_o = w_o.reshape(width, d)
        started, this_m = (flight_m.token,), flight_m
        if i + 1 < depth:
            following = slots(i + 1, (flight_m.token,))
            flight_a = Flight(f"l{i + 1}a", following[:2], (w_qkv,))
            flight_m = Flight(f"l{i + 1}m", following[2:], (flight_a.token,))
            started = (flight_m.token,)
        h = rms_fwd(f"l{i}_norm_mix", xc, norm_mix[i], deps=started)
        qkv = mm_cols(f"l{i}_qkv", h, w_qkv, _same, (BF16,))[0]
        if mixer == 0:
            attn, kept = swa_fwd(f"l{i}_swa", qkv, a_sinks[j], slopes), ()
        else:
            attn, *kept = sb_fwd(f"l{i}_sb", qkv)
        xm = mm_res(f"l{i}_o", attn, w_o, xc, deps=(this_m.relay((attn,)),))
        w_in, w_out = this_m.land((xm,))
        w_out = w_out.reshape(-1, d)
        weights.append((w_qkv, w_o, w_in, w_out))
        h2 = rms_fwd(f"l{i}_norm_mlp", xm, norm_mlp[i])
        u, hh = mm_cols(f"l{i}_in", h2, w_in, _relu2, (BF16, BF16))
        relayed = (flight_a.relay((u,)),) if i + 1 < depth else ()
        xn = mm_res(f"l{i}_out", hh, w_out, xm, deps=relayed)
        saved.append((xc, h, qkv, attn, kept, xm, h2, u, hh))
        xc = xn
    loss_rows, dx, dxb, dg_final = loss_head("loss_head", xc, final_norm, loss_target[0])
    loss = lax.psum(jnp.sum(loss_rows), ("x", "y", "c"))

    big = {"a_qkv": (a_w_qkv, m_a_w_qkv, v_a_w_qkv), "a_o": (a_w_o, m_a_w_o, v_a_w_o),
           "b_qkv": (b_w_qkv, m_b_w_qkv, v_b_w_qkv), "b_o": (b_w_o, m_b_w_o, v_b_w_o),
           "mlp_in": (mlp_w_in, m_mlp_w_in, v_mlp_w_in), "mlp_out": (mlp_w_out, m_mlp_w_out, v_mlp_w_out)}
    f = {k: _empty((w.shape[0], 2, w.shape[1] // 2, w.shape[2]), F32) for k, (w, _, _) in big.items()}
    dg_mix, dg_mlp, dsinks = [], [], []
    prev_a, started = None, ()
    for i in reversed(range(depth)):
        mixer, j = i % 2, i // 2
        xin, h, qkv, attn, kept, xm, h2, u, hh = saved[i]
        w_qkv, w_o, w_in, w_out = weights[i]
        kinds = layer_kinds(i)
        du = mm_nt(f"l{i}_d_hidden", dxb, w_out, _relu2_bwd, (u,), deps=started)
        g_out = mm_tn(f"l{i}_g_out", hh, dxb).reshape(N_CHIPS, -1, d)
        started = (g_out,) if prev_a is None else (g_out, prev_a.swap_to_scatter((g_out,)))
        dh2 = mm_nt_cols(f"l{i}_d_h2", du, w_in, deps=started)
        g_in = mm_tn_cols(f"l{i}_g_in", h2, du)
        cur_m = _GradExchange(f"{i}m", kinds[2:], [g_in, g_out], chip, core)
        started = (g_in, cur_m.start_swap((g_in,)))
        dxm, dxmb, dg = rms_bwd(f"l{i}_norm_mlp_bwd", xm, norm_mlp[i], dh2, dx, deps=started)
        dg_mlp.append(dg)
        g_o = mm_tn(f"l{i}_g_o", attn, dxmb).reshape(N_CHIPS, -1, d)
        dattn = mm_nt(f"l{i}_d_attn", dxmb, w_o, _same, deps=(g_o, cur_m.swap_to_scatter((g_o,))))
        if mixer == 0:
            dq, dk, dv, dsk = swa_bwd(f"l{i}_swa_bwd", qkv, dattn, a_sinks[j], slopes)
            dsinks.append(dsk[:, :2, 0].reshape(N_HEADS))
        else:
            attn_f32, weights_a, sigmoids = kept
            dq, dk, dv = sb_bwd(f"l{i}_sb_bwd", qkv, attn_f32, dattn, weights_a, sigmoids)
        dqkv = jnp.concatenate([dq, dk.astype(BF16), dv.astype(BF16)], axis=1)
        dh = mm_nt_cols(f"l{i}_d_h", dqkv, w_qkv)
        g_qkv = mm_tn_cols(f"l{i}_g_qkv", h, dqkv)
        dx, dxb, dg = rms_bwd(f"l{i}_norm_mix_bwd", xin, norm_mix[i], dh, dxm, deps=(g_qkv,))
        dg_mix.append(dg)
        if prev_a is not None:
            prev_a.finish(f, (dx,))
        cur_m.finish(f, (dx,))
        prev_a = _GradExchange(f"{i}a", kinds[:2], [g_qkv, g_o], chip, core)
        started = (prev_a.start_swap((dx,)),)
    prev_a.swap_to_scatter((dx,))
    prev_a.finish(f, (dx,))
    for lst in (dg_mix, dg_mlp, dsinks):
        lst.reverse()

    res = {}
    for kind, (w, m, v) in big.items():
        g = join_halves(f"join_{kind}", f[kind]).reshape(w.shape)
        res[kind] = (g, *adamw(f"adamw_{kind}", w, g, m, v))

    n_sink = a_sinks.size
    sink_rows = jnp.zeros((1, 8, d), F32).at[0, 0, :n_sink].set(jnp.concatenate(dsinks))
    parts = jnp.concatenate([jnp.stack(dg_mix), jnp.stack(dg_mlp), dg_final[None], sink_rows], axis=0)
    g_small = allreduce_small("allreduce_small", parts)

    def pack(mix, mlp, fin, snk):
        snk_row = jnp.zeros((1, d), F32).at[0, :n_sink].set(snk.reshape(-1))
        return jnp.concatenate([mix, mlp, fin[None], snk_row], axis=0)

    def unpack(t):
        return t[:depth], t[depth:2 * depth], t[2 * depth], t[2 * depth + 1, :n_sink].reshape(a_sinks.shape)

    small = adamw("adamw_small", pack(norm_mix, norm_mlp, final_norm, a_sinks), g_small,
                  pack(m_norm_mix, m_norm_mlp, m_final_norm, m_a_sinks),
                  pack(v_norm_mix, v_norm_mlp, v_final_norm, v_a_sinks))
    outs = []
    for idx in range(4):
        mix, mlp, fin, snk = unpack(g_small if idx == 0 else small[idx - 1])
        outs += [res["a_qkv"][idx], res["a_o"][idx], snk, res["b_qkv"][idx], res["b_o"][idx], mix, mlp,
                 res["mlp_in"][idx], res["mlp_out"][idx], fin]
    return (loss, dx.reshape(x.shape), *outs)
```

```python
import functools
import math

import jax
import jax.numpy as jnp
from jax import lax
from jax.experimental import pallas as pl
from jax.experimental.pallas import tpu as pltpu

F32 = jnp.float32
BF16 = jnp.bfloat16
MESH = pl.DeviceIdType.MESH

N_CHIPS = 4
HEAD_DIM = 64
LANES = 128
N_HEADS = 32
N_KV_A = 4
WINDOW = 128
RMS_EPS = 1e-5
ATTN_SCALE = 1.0 / math.sqrt(HEAD_DIM)
ADAM_LR, ADAM_B1, ADAM_B2, ADAM_EPS, ADAM_WD, ADAM_STEP = 0.001, 0.9, 0.999, 1e-08, 0.01, 10
NEG_BIG = -1e30
VMEM_LIMIT = 56 * 1024 * 1024

_DN = {"nn": (((1,), (0,)), ((), ())), "nt": (((1,), (1,)), ((), ())), "tn": (((0,), (0,)), ((), ()))}


def _dot(a, b, mode="nn"):
    return lax.dot_general(a, b, _DN[mode], preferred_element_type=F32)


def _params(*sem):
    return pltpu.CompilerParams(dimension_semantics=sem, vmem_limit_bytes=VMEM_LIMIT)


def _pick(n, prefs):
    for t in prefs:
        if n % t == 0:
            return t
    return n


def _mm(name, mode, a, b, *, grid, a_spec, b_spec, extras=(), extra_specs=(), out_shapes, out_specs, nk,
        acc_shape, epilogue, deps=()):
    n_ex, n_out = len(extras), len(out_shapes)
    first_out = 2 + n_ex + len(deps)

    def body(*refs):
        a_ref, b_ref = refs[0], refs[1]
        ex = refs[2:2 + n_ex]
        outs = refs[first_out:first_out + n_out]
        part = _dot(a_ref[...], b_ref[...], mode)

        def finish(acc):
            res = epilogue(acc, *[e[...] for e in ex])
            for o, r in zip(outs, res):
                o[...] = r.astype(o.dtype)

        if nk == 1:
            finish(part)
        else:
            acc_ref = refs[-1]
            k = pl.program_id(len(grid) - 1)

            @pl.when(k == 0)
            def _():
                acc_ref[...] = part

            @pl.when(k > 0)
            def _():
                acc_ref[...] += part

            @pl.when(k == nk - 1)
            def _():
                finish(acc_ref[...])

    sem = ("parallel",) * (len(grid) - 1) + ("arbitrary" if nk > 1 else "parallel",)
    return pl.pallas_call(
        body, name=name, grid=grid,
        in_specs=[a_spec, b_spec, *extra_specs, *[ANY] * len(deps)],
        out_specs=list(out_specs), out_shape=list(out_shapes),
        scratch_shapes=[] if nk == 1 else [pltpu.VMEM(acc_shape, F32)],
        compiler_params=_params(*sem),
    )(a, b, *extras, *deps)


def mm_cols(name, a, wg, epilogue, out_dtypes, tm=2048):
    m, k = a.shape
    c = wg.shape[2]
    tm = min(tm, m)
    tn = _pick(c, (512, 640, 256, 128))
    nj = c // tn
    o_spec = pl.BlockSpec((tm, tn), lambda i, j: (i, j))
    return _mm(name, "nn", a, wg, grid=(m // tm, N_CHIPS * nj),
               a_spec=pl.BlockSpec((tm, k), lambda i, j: (i, 0)),
               b_spec=pl.BlockSpec((None, k, tn), lambda i, j: (j // nj, 0, j % nj)),
               out_shapes=[jax.ShapeDtypeStruct((m, N_CHIPS * c), d) for d in out_dtypes],
               out_specs=[o_spec] * len(out_dtypes), nk=1, acc_shape=None, epilogue=epilogue)


def mm_res(name, a, w, res, deps=(), tm=1024, tn=1024, tk=2048):
    m, k = a.shape
    n = w.shape[1]
    tm, tn, tk = min(tm, m), min(tn, n), min(tk, k)
    nk = k // tk
    return _mm(name, "nn", a, w, grid=(m // tm, n // tn, nk),
               a_spec=pl.BlockSpec((tm, tk), lambda i, j, kk: (i, kk)),
               b_spec=pl.BlockSpec((tk, tn), lambda i, j, kk: (kk, j)),
               extras=(res,), extra_specs=(pl.BlockSpec((tm, tn), lambda i, j, kk: (i, j)),),
               out_shapes=[jax.ShapeDtypeStruct((m, n), F32)],
               out_specs=[pl.BlockSpec((tm, tn), lambda i, j, kk: (i, j))], nk=nk, acc_shape=(tm, tn),
               epilogue=lambda acc, r: (acc + r,), deps=deps)[0]


def mm_nt(name, a, w, epilogue, extras=(), deps=(), tm=2048, tn=512):
    m, k = a.shape
    n = w.shape[0]
    tm, tn = min(tm, m), min(tn, n)
    o_spec = pl.BlockSpec((tm, tn), lambda i, j: (i, j))
    return _mm(name, "nt", a, w, grid=(m // tm, n // tn),
               a_spec=pl.BlockSpec((tm, k), lambda i, j: (i, 0)),
               b_spec=pl.BlockSpec((tn, k), lambda i, j: (j, 0)),
               extras=tuple(extras), extra_specs=(o_spec,) * len(extras),
               out_shapes=[jax.ShapeDtypeStruct((m, n), BF16)], out_specs=[o_spec], nk=1, acc_shape=None,
               epilogue=epilogue, deps=deps)[0]


def mm_nt_cols(name, dy, wg, deps=(), tm=1024, tn=1024):
    m = dy.shape[0]
    _, d, c = wg.shape
    tm, tn = min(tm, m), min(tn, d)
    tk = _pick(c, (2048, 1536, 640, 512, 128))
    nkk = c // tk
    nk = N_CHIPS * nkk
    return _mm(name, "nt", dy, wg, grid=(m // tm, d // tn, nk),
               a_spec=pl.BlockSpec((tm, tk), lambda i, j, kk: (i, kk)),
               b_spec=pl.BlockSpec((None, tn, tk), lambda i, j, kk: (kk // nkk, j, kk % nkk)),
               out_shapes=[jax.ShapeDtypeStruct((m, d), F32)],
               out_specs=[pl.BlockSpec((tm, tn), lambda i, j, kk: (i, j))], nk=nk, acc_shape=(tm, tn),
               epilogue=lambda acc: (acc,), deps=deps)[0]


def mm_tn(name, a, b, tm=1024, tn=1024, tk=2048):
    m, p = a.shape
    q = b.shape[1]
    tm, tn, tk = min(tm, p), min(tn, q), min(tk, m)
    nk = m // tk
    return _mm(name, "tn", a, b, grid=(p // tm, q // tn, nk),
               a_spec=pl.BlockSpec((tk, tm), lambda i, j, kk: (kk, i)),
               b_spec=pl.BlockSpec((tk, tn), lambda i, j, kk: (kk, j)),
               out_shapes=[jax.ShapeDtypeStruct((p, q), BF16)],
               out_specs=[pl.BlockSpec((tm, tn), lambda i, j, kk: (i, j))], nk=nk, acc_shape=(tm, tn),
               epilogue=lambda acc: (acc,))[0]


def mm_tn_cols(name, a, dy, tm=1024, tk=2048):
    m, d = a.shape
    c = dy.shape[1] // N_CHIPS
    tm, tk = min(tm, d), min(tk, m)
    tn = _pick(c, (1024, 768, 640, 512, 128))
    nj = c // tn
    nk = m // tk
    return _mm(name, "tn", a, dy, grid=(d // tm, N_CHIPS * nj, nk),
               a_spec=pl.BlockSpec((tk, tm), lambda i, j, kk: (kk, i)),
               b_spec=pl.BlockSpec((tk, tn), lambda i, j, kk: (kk, j)),
               out_shapes=[jax.ShapeDtypeStruct((N_CHIPS, d, c), BF16)],
               out_specs=[pl.BlockSpec((None, tm, tn), lambda i, j, kk: (j // nj, i, j % nj))], nk=nk,
               acc_shape=(tm, tn), epilogue=lambda acc: (acc,))[0]


def _rows_to_8(v):
    tm, d = v.shape
    return jnp.sum(v.reshape(tm // 8, 8, d), axis=0)


def rms_fwd(name, x, gain, deps=(), tm=512):
    s, d = x.shape
    tm = min(tm, s)

    def body(x_ref, g_ref, *rest):
        h_ref = rest[-1]
        xv = x_ref[...]
        r = lax.rsqrt(jnp.mean(xv * xv, axis=-1, keepdims=True) + RMS_EPS)
        h_ref[...] = (xv * r * g_ref[...]).astype(BF16)

    row = pl.BlockSpec((tm, d), lambda i: (i, 0))
    return pl.pallas_call(
        body, name=name, grid=(s // tm,),
        in_specs=[row, pl.BlockSpec((1, d), lambda i: (0, 0)), *[ANY] * len(deps)], out_specs=row,
        out_shape=jax.ShapeDtypeStruct((s, d), BF16), compiler_params=_params("parallel"),
    )(x, gain.reshape(1, d), *deps)


def rms_bwd(name, x, gain, dh, dres, deps=(), tm=512):
    s, d = x.shape
    tm = min(tm, s)

    def body(x_ref, g_ref, dh_ref, dres_ref, *rest):
        dx_ref, dxb_ref, dg_ref = rest[-3:]
        xv = x_ref[...]
        r = lax.rsqrt(jnp.mean(xv * xv, axis=-1, keepdims=True) + RMS_EPS)
        xhat = xv * r
        dhv = dh_ref[...]
        dxhat = dhv * g_ref[...]
        dx = dres_ref[...] + r * (dxhat - xhat * jnp.mean(dxhat * xhat, axis=-1, keepdims=True))
        dx_ref[...] = dx
        dxb_ref[...] = dx.astype(BF16)

        @pl.when(pl.program_id(0) == 0)
        def _():
            dg_ref[...] = jnp.zeros_like(dg_ref)

        dg_ref[...] += _rows_to_8(dhv * xhat)

    row = pl.BlockSpec((tm, d), lambda i: (i, 0))
    return pl.pallas_call(
        body, name=name, grid=(s // tm,),
        in_specs=[row, pl.BlockSpec((1, d), lambda i: (0, 0)), row, row, *[ANY] * len(deps)],
        out_specs=[row, row, pl.BlockSpec((8, d), lambda i: (0, 0))],
        out_shape=[jax.ShapeDtypeStruct((s, d), F32), jax.ShapeDtypeStruct((s, d), BF16),
                   jax.ShapeDtypeStruct((8, d), F32)],
        compiler_params=_params("arbitrary"),
    )(x, gain.reshape(1, d), dh, dres, *deps)


def loss_head(name, x, gain, target, tm=512):
    s, d = x.shape
    tm = min(tm, s)

    def body(x_ref, g_ref, t_ref, loss_ref, dx_ref, dxb_ref, dg_ref):
        xv = x_ref[...]
        g = g_ref[...]
        r = lax.rsqrt(jnp.mean(xv * xv, axis=-1, keepdims=True) + RMS_EPS)
        xhat = xv * r
        err = xhat * g - t_ref[...]
        dy = err * (1.0 / d)
        dxhat = dy * g
        dx = r * (dxhat - xhat * jnp.mean(dxhat * xhat, axis=-1, keepdims=True))
        dx_ref[...] = dx
        dxb_ref[...] = dx.astype(BF16)

        @pl.when(pl.program_id(0) == 0)
        def _():
            dg_ref[...] = jnp.zeros_like(dg_ref)
            loss_ref[...] = jnp.zeros_like(loss_ref)

        dg_ref[...] += _rows_to_8(dy * xhat)
        loss_ref[...] += _rows_to_8(err * err) * (0.5 / d)

    row = pl.BlockSpec((tm, d), lambda i: (i, 0))
    vec = pl.BlockSpec((8, d), lambda i: (0, 0))
    return pl.pallas_call(
        body, name=name, grid=(s // tm,),
        in_specs=[row, pl.BlockSpec((1, d), lambda i: (0, 0)), row],
        out_specs=[vec, row, row, vec],
        out_shape=[jax.ShapeDtypeStruct((8, d), F32), jax.ShapeDtypeStruct((s, d), F32),
                   jax.ShapeDtypeStruct((s, d), BF16), jax.ShapeDtypeStruct((8, d), F32)],
        compiler_params=_params("arbitrary"),
    )(x, gain.reshape(1, d), target)


def _half_masks(dtype):
    lane = lax.broadcasted_iota(jnp.int32, (1, LANES), 1)
    lo = (lane < HEAD_DIM).astype(dtype)
    return lo, (1 - lo).astype(dtype)


def _swap_halves(v):
    return pltpu.roll(v, HEAD_DIM, axis=1)


SWA_BLOCKS = 8


def _swa_probs(s_raw, sink, slope, first):
    t = s_raw.shape[0]
    row = lax.broadcasted_iota(jnp.int32, (t, 2 * WINDOW), 0)
    col = lax.broadcasted_iota(jnp.int32, (t, 2 * WINDOW), 1)
    dist = row + WINDOW - col
    valid = (dist >= 0) & (dist < WINDOW)
    if first is not None:
        valid = valid & ((col >= WINDOW) | jnp.logical_not(first))
    s = jnp.where(valid, s_raw - slope * dist.astype(F32), NEG_BIG)
    m = jnp.maximum(jnp.max(s, axis=-1, keepdims=True), sink)
    e = jnp.exp(s - m)
    e_sink = jnp.exp(sink - m)
    inv = 1.0 / (jnp.sum(e, axis=-1, keepdims=True) + e_sink)
    return e * inv, e_sink * inv


def _swa_band(kp_ref, kc_ref, vp_ref, vc_ref, kv_half):
    lo_b, hi_b = _half_masks(F32)
    sel = jnp.where(kv_half == 0, lo_b, hi_b)
    k = jnp.concatenate([kp_ref[...], kc_ref[...]], axis=0).astype(F32) * sel
    v = jnp.concatenate([vp_ref[...], vc_ref[...]], axis=0).astype(F32) * sel
    k = (k + _swap_halves(k)).astype(BF16)
    v = (v + _swap_halves(v)).astype(BF16)
    return k, v, sel


def _swa_specs(nq):
    t = WINDOW
    q_spec = pl.BlockSpec((nq * t, LANES), lambda j, n, *_: (n, j))
    q_blocks = N_HEADS // 2

    def prev(off):
        return pl.BlockSpec((t, LANES), lambda j, n, *_: (jnp.maximum(nq * n - 1, 0), q_blocks + off + j // 8))

    def cur(off):
        return pl.BlockSpec((nq * t, LANES), lambda j, n, *_: (n, q_blocks + off + j // 8))

    kv_blocks = N_KV_A // 2
    return q_spec, [prev(0), cur(0), prev(kv_blocks), cur(kv_blocks)]


def _rows(a, b, n=1):
    return a[b * WINDOW:(b + n) * WINDOW]


def swa_fwd(name, qkv, sinks, slopes):
    s = qkv.shape[0]
    nq = min(SWA_BLOCKS, s // WINDOW)

    def body(sink_ref, slope_ref, q_ref, kp_ref, kc_ref, vp_ref, vc_ref, o_ref):
        j, n = pl.program_id(0), pl.program_id(1)
        k_all, v_all, _ = _swa_band(kp_ref, kc_ref, vp_ref, vc_ref, (j // 4) % 2)
        q = q_ref[...] * ATTN_SCALE
        masks = _half_masks(BF16)
        chains = [(b, hq) for b in range(nq) for hq in range(2)]
        s_raw = {(b, hq): _dot(_rows(q, b) * masks[hq], _rows(k_all, b, 2), "nt") for b, hq in chains}
        p = {(b, hq): _swa_probs(s_raw[(b, hq)], sink_ref[2 * j + hq], slope_ref[2 * j + hq],
                                 (n == 0) if b == 0 else None)[0] for b, hq in chains}
        outs = [sum(_dot(p[(b, hq)].astype(BF16), _rows(v_all, b, 2) * masks[hq]) for hq in range(2))
                for b in range(nq)]
        o_ref[...] = jnp.concatenate(outs, axis=0).astype(BF16)

    q_spec, kv_specs = _swa_specs(nq)
    return pl.pallas_call(
        body, name=name,
        grid_spec=pltpu.PrefetchScalarGridSpec(
            num_scalar_prefetch=2, grid=(N_HEADS // 2, s // (nq * WINDOW)),
            in_specs=[q_spec, *kv_specs], out_specs=q_spec),
        out_shape=jax.ShapeDtypeStruct((s, N_HEADS * HEAD_DIM), BF16),
        compiler_params=_params("parallel", "parallel"),
    )(sinks, slopes, qkv, qkv, qkv, qkv, qkv)


def swa_bwd(name, qkv, do, sinks, slopes):
    s = qkv.shape[0]
    t = WINDOW
    nq = min(SWA_BLOCKS, s // t)

    def body(sink_ref, slope_ref, q_ref, kp_ref, kc_ref, vp_ref, vc_ref, do_ref, dq_ref, dk_ref, dv_ref, ds_ref):
        j, n = pl.program_id(0), pl.program_id(1)
        k_all, v_all, sel = _swa_band(kp_ref, kc_ref, vp_ref, vc_ref, (j // 4) % 2)
        q = q_ref[...] * ATTN_SCALE
        do_v = do_ref[...]
        masks = _half_masks(BF16)

        @pl.when((j % 8 == 0) & (n == 0))
        def _():
            dk_ref[...] = jnp.zeros_like(dk_ref)
            dv_ref[...] = jnp.zeros_like(dv_ref)

        @pl.when(n == 0)
        def _():
            ds_ref[...] = jnp.zeros_like(ds_ref)

        chains = [(b, hq) for b in range(nq) for hq in range(2)]
        qm = {(b, hq): _rows(q, b) * masks[hq] for b, hq in chains}
        dom = {(b, hq): _rows(do_v, b) * masks[hq] for b, hq in chains}
        s_raw = {ch: _dot(qm[ch], _rows(k_all, ch[0], 2), "nt") for ch in chains}
        dp = {ch: _dot(dom[ch], _rows(v_all, ch[0], 2), "nt") for ch in chains}
        pb, dsc = {}, {}
        dsink = [jnp.zeros((), F32), jnp.zeros((), F32)]
        for ch in chains:
            b, hq = ch
            p, p_sink = _swa_probs(s_raw[ch], sink_ref[2 * j + hq], slope_ref[2 * j + hq], (n == 0) if b == 0 else None)
            delta = jnp.sum(p * dp[ch], axis=-1, keepdims=True)
            dsc[ch] = (p * (dp[ch] - delta)).astype(BF16)
            pb[ch] = p.astype(BF16)
            dsink[hq] = dsink[hq] - jnp.sum(p_sink * delta)
        for hq in range(2):
            ds_ref[hq:hq + 1, :] += jnp.zeros((1, LANES), F32) + dsink[hq]
        dq, dk, dv = [], [], []
        for b in range(nq):
            dq.append(sum(_dot(dsc[(b, hq)], _rows(k_all, b, 2) * masks[hq]) for hq in range(2)))
            dk_b = sum(_dot(dsc[(b, hq)], qm[(b, hq)], "tn") for hq in range(2))
            dv_b = sum(_dot(pb[(b, hq)], dom[(b, hq)], "tn") for hq in range(2))
            dk.append((dk_b + _swap_halves(dk_b)) * sel)
            dv.append((dv_b + _swap_halves(dv_b)) * sel)
        dq_ref[...] = (jnp.concatenate(dq, axis=0) * ATTN_SCALE).astype(BF16)

        @pl.when(n == 0)
        def _():
            dk_ref[pl.ds(0, t), :] += dk[0][t:]
            dv_ref[pl.ds(0, t), :] += dv[0][t:]

        @pl.when(n > 0)
        def _():
            start = pl.multiple_of((nq * n - 1) * t, t)
            dk_ref[pl.ds(start, 2 * t), :] += dk[0]
            dv_ref[pl.ds(start, 2 * t), :] += dv[0]

        for b in range(1, nq):
            start = pl.multiple_of((nq * n + b - 1) * t, t)
            dk_ref[pl.ds(start, 2 * t), :] += dk[b]
            dv_ref[pl.ds(start, 2 * t), :] += dv[b]

    q_spec, kv_specs = _swa_specs(nq)
    kv_out = pl.BlockSpec((s, LANES), lambda j, n, *_: (0, j // 8))
    return pl.pallas_call(
        body, name=name,
        grid_spec=pltpu.PrefetchScalarGridSpec(
            num_scalar_prefetch=2, grid=(N_HEADS // 2, s // (nq * t)),
            in_specs=[q_spec, *kv_specs, q_spec],
            out_specs=[q_spec, kv_out, kv_out, pl.BlockSpec((None, 8, LANES), lambda j, n, *_: (j, 0, 0))]),
        out_shape=[jax.ShapeDtypeStruct((s, N_HEADS * HEAD_DIM), BF16),
                   jax.ShapeDtypeStruct((s, N_KV_A * HEAD_DIM), F32),
                   jax.ShapeDtypeStruct((s, N_KV_A * HEAD_DIM), F32),
                   jax.ShapeDtypeStruct((N_HEADS // 2, 8, LANES), F32)],
        compiler_params=_params("arbitrary", "arbitrary"),
    )(sinks, slopes, qkv, qkv, qkv, qkv, qkv, do)


SB_TILE = 256


def _split_k(v):
    hi = v.astype(BF16)
    lo = (v - hi.astype(F32)).astype(BF16)
    return jnp.concatenate([hi, lo], axis=1)


def _tri2(t, inclusive):
    r = lax.broadcasted_iota(jnp.int32, (2 * t, t), 0)
    c = lax.broadcasted_iota(jnp.int32, (2 * t, t), 1)
    r = jnp.where(r >= t, r - t, r)
    return ((r >= c) if inclusive else (r > c)).astype(BF16)


def _sb_logs(z, before):
    neg_abs = lax.bitcast_convert_type(lax.bitcast_convert_type(z, jnp.uint32) | jnp.uint32(0x80000000), F32)
    l = jnp.log(1.0 + jnp.exp(neg_abs))
    lb = jnp.minimum(z, 0.0) - l
    lm = lb - z
    if before is not None:
        lm = jnp.where(before, lm, 0.0)
    return lb, lm


def _sb_weights(lb, sfx, c_lm, before):
    a = jnp.exp(lb + sfx + c_lm)
    if before is not None:
        a = jnp.where(before, a, 0.0)
    return a


def _sb_sweep(tile, i, init, diag):
    carry = lax.cond(i > 0, lambda: tile([i, i - 1], init, diag), lambda: tile([i], init, diag))
    rest = jnp.maximum(i - 1, 0)
    odd = rest % 2
    carry = lax.cond(odd == 1, lambda: tile([i - 2], carry, None), lambda: carry)
    base = i - 2 - odd
    return lax.fori_loop(0, rest // 2, lambda n, cr: tile([base - 2 * n, base - 2 * n - 1], cr, None), carry)


def _sb_specs(s, t):
    hp, nb = N_HEADS // 2, s // t
    q_spec = pl.BlockSpec((t, LANES), lambda h, i: (i, h))
    k_spec = pl.BlockSpec((s, LANES), lambda h, i: (0, hp + h))
    v_spec = pl.BlockSpec((s, LANES), lambda h, i: (0, 2 * hp + h))
    saved = pl.BlockSpec((None, 2, None, nb, t, t), lambda h, i: (h, 0, i, 0, 0, 0))
    return q_spec, k_spec, v_spec, saved


def sb_fwd(name, qkv, t=SB_TILE):
    s = qkv.shape[0]
    t = min(t, s // 2)
    nb = s // t

    def body(q_ref, k_ref, v_ref, ob_ref, of_ref, a_ref, sg_ref):
        i = pl.program_id(1)
        masks = _half_masks(BF16)
        q = q_ref[...] * ATTN_SCALE
        qm = [q * masks[0], q * masks[1]]
        tri_x = _tri2(t, False)
        r = lax.broadcasted_iota(jnp.int32, (t, t), 0)
        c = lax.broadcasted_iota(jnp.int32, (t, t), 1)
        diag = c < r

        def tile(js, carry, before):
            c0, c1, acc = carry
            cs = [c0, c1]
            kj = [k_ref[pl.ds(pl.multiple_of(j * t, t), t), :] for j in js]
            vj = [v_ref[pl.ds(pl.multiple_of(j * t, t), t), :] for j in js]
            chains = [(h, b) for b in range(len(js)) for h in range(2)]
            z = {(h, b): _dot(qm[h], kj[b], "nt") for h, b in chains}
            lb, sfx, c_at = {}, {}, {}
            for ch in chains:
                h, b = ch
                lb[ch], lm = _sb_logs(z[ch], before if b == 0 else None)
                sfx[ch] = _dot(_split_k(lm), tri_x)
                c_at[ch] = cs[h]
                cs[h] = cs[h] + jnp.sum(lm, axis=-1, keepdims=True)
            for ch in chains:
                h, b = ch
                ab = _sb_weights(lb[ch], sfx[ch], c_at[ch], before if b == 0 else None).astype(BF16)
                a_ref[h, js[b]] = ab
                sg_ref[h, js[b]] = jnp.exp(lb[ch]).astype(BF16)
                acc = acc + _dot(ab, vj[b] * masks[h])
            return cs[0], cs[1], acc

        zero = jnp.zeros((t, 1), F32)
        carry = _sb_sweep(tile, i, (zero, zero, jnp.zeros((t, LANES), F32)), diag)
        ob_ref[...] = carry[2].astype(BF16)
        of_ref[...] = carry[2]

    q_spec, k_spec, v_spec, saved = _sb_specs(s, t)
    width = N_HEADS * HEAD_DIM
    keep = jax.ShapeDtypeStruct((N_HEADS // 2, 2, nb, nb, t, t), BF16)
    return pl.pallas_call(
        body, name=name, grid=(N_HEADS // 2, nb),
        in_specs=[q_spec, k_spec, v_spec], out_specs=[q_spec, q_spec, saved, saved],
        out_shape=[jax.ShapeDtypeStruct((s, width), BF16), jax.ShapeDtypeStruct((s, width), F32), keep, keep],
        compiler_params=_params("parallel", "parallel"),
    )(qkv, qkv, qkv)


def sb_bwd(name, qkv, o_f32, do, a_all, sg_all, t=SB_TILE):
    s = qkv.shape[0]
    t = min(t, s // 2)

    def body(q_ref, k_ref, v_ref, o_ref, do_ref, a_ref, sg_ref, dq_ref, dk_ref, dv_ref):
        i = pl.program_id(1)
        masks = _half_masks(BF16)
        fmasks = _half_masks(F32)
        q = q_ref[...] * ATTN_SCALE
        do_v = do_ref[...]
        qm = [q * masks[0], q * masks[1]]
        dom = [do_v * masks[0], do_v * masks[1]]
        prod = do_v.astype(F32) * o_ref[...]
        delta = [jnp.sum(prod * fmasks[h], axis=-1, keepdims=True) for h in range(2)]
        tri_i = _tri2(t, True)
        r = lax.broadcasted_iota(jnp.int32, (t, t), 0)
        c = lax.broadcasted_iota(jnp.int32, (t, t), 1)
        diag = c < r

        @pl.when(i == 0)
        def _():
            dk_ref[...] = jnp.zeros_like(dk_ref)
            dv_ref[...] = jnp.zeros_like(dv_ref)

        def tile(js, carry, before):
            cd0, cd1, dq = carry
            cd = [cd0, cd1]
            starts = [pl.multiple_of(j * t, t) for j in js]
            kj = [k_ref[pl.ds(st, t), :] for st in starts]
            vj = [v_ref[pl.ds(st, t), :] for st in starts]
            chains = [(h, b) for b in range(len(js)) for h in range(2)]
            da = {(h, b): _dot(dom[h], vj[b], "nt") for h, b in chains}
            ab, de, dsfx, cd_at = {}, {}, {}, {}
            dv = [jnp.zeros((t, LANES), F32) for _ in js]
            dk = [jnp.zeros((t, LANES), F32) for _ in js]
            for ch in chains:
                h, b = ch
                ab[ch] = a_ref[h, js[b]]
                de[ch] = da[ch] * ab[ch].astype(F32)
                dsfx[ch] = _dot(_split_k(de[ch]), tri_i)
                dv[b] = dv[b] + _dot(ab[ch], dom[h], "tn")
                cd_at[ch] = cd[h]
                cd[h] = cd[h] + jnp.sum(de[ch], axis=-1, keepdims=True)
            for ch in chains:
                h, b = ch
                farther = delta[h] - cd_at[ch] - dsfx[ch]
                dz = de[ch] - sg_ref[h, js[b]].astype(F32) * (de[ch] + farther)
                if before is not None and b == 0:
                    dz = jnp.where(before, dz, 0.0)
                dzb = dz.astype(BF16)
                dq = dq + _dot(dzb, kj[b] * masks[h])
                dk[b] = dk[b] + _dot(dzb, qm[h], "tn")
            for b, st in enumerate(starts):
                dk_ref[pl.ds(st, t), :] += dk[b]
                dv_ref[pl.ds(st, t), :] += dv[b]
            return cd[0], cd[1], dq

        zero = jnp.zeros((t, 1), F32)
        carry = _sb_sweep(tile, i, (zero, zero, jnp.zeros((t, LANES), F32)), diag)
        dq_ref[...] = (carry[2] * ATTN_SCALE).astype(BF16)

    q_spec, k_spec, v_spec, saved = _sb_specs(s, t)
    kv_out = pl.BlockSpec((s, LANES), lambda h, i: (0, h))
    width = N_HEADS * HEAD_DIM
    return pl.pallas_call(
        body, name=name, grid=(N_HEADS // 2, s // t),
        in_specs=[q_spec, k_spec, v_spec, q_spec, q_spec, saved, saved], out_specs=[q_spec, kv_out, kv_out],
        out_shape=[jax.ShapeDtypeStruct((s, width), BF16), jax.ShapeDtypeStruct((s, width), F32),
                   jax.ShapeDtypeStruct((s, width), F32)],
        compiler_params=_params("parallel", "arbitrary"),
    )(qkv, qkv, qkv, o_f32, do, a_all, sg_all)


def adamw(name, w, g, m, v, tm=256):
    shape = w.shape
    c = shape[-1]
    rows = math.prod(shape[:-1])
    tm = min(tm, rows)

    def body(w_ref, g_ref, m_ref, v_ref, d_ref, mo_ref, vo_ref):
        gv = g_ref[...]
        m2 = ADAM_B1 * m_ref[...] + (1.0 - ADAM_B1) * gv
        v2 = ADAM_B2 * v_ref[...] + (1.0 - ADAM_B2) * (gv * gv)
        m_hat = m2 / (1.0 - ADAM_B1 ** ADAM_STEP)
        v_hat = v2 / (1.0 - ADAM_B2 ** ADAM_STEP)
        d_ref[...] = -ADAM_LR * (m_hat / (jnp.sqrt(v_hat) + ADAM_EPS) + ADAM_WD * w_ref[...])
        mo_ref[...] = m2
        vo_ref[...] = v2

    blk = pl.BlockSpec((tm, c), lambda i: (i, 0))
    outs = pl.pallas_call(
        body, name=name, grid=(rows // tm,), in_specs=[blk] * 4, out_specs=[blk] * 3,
        out_shape=[jax.ShapeDtypeStruct((rows, c), F32)] * 3, compiler_params=_params("parallel"),
    )(*[t.reshape(rows, c) for t in (w, g, m, v)])
    return [o.reshape(shape) for o in outs]


HBM = pl.BlockSpec(memory_space=pltpu.HBM)


def _place():
    x, y, c = lax.axis_index("x"), lax.axis_index("y"), lax.axis_index("c")
    return x, y, c, [(1 - x, y), (x, 1 - y), (1 - x, 1 - y)]


def _remote(src, dst, send, recv, dev):
    return pltpu.make_async_remote_copy(src_ref=src, dst_ref=dst, send_sem=send, recv_sem=recv, device_id=dev,
                                        device_id_type=MESH)


SEM = pl.BlockSpec(memory_space=pltpu.SEMAPHORE)
ANY = pl.BlockSpec(memory_space=pl.ANY)
DATAFLOW = pltpu.SideEffectType.DATAFLOW_SIDE_EFFECTING


def _in_hbm(v):
    return pltpu.with_memory_space_constraint(v, pltpu.HBM)


def split_start(name, bufs, n_copies, sends, after=()):
    nb, na = len(bufs), len(after)

    def body(*refs):
        send, recv = refs[nb + na], refs[nb + na + 1]
        for cp in sends(refs[:nb], send, recv):
            cp.start()
        refs[-1][...] = jnp.zeros_like(refs[-1])

    outs = pl.pallas_call(
        body, name=name,
        in_specs=[HBM] * nb + [ANY] * na,
        out_shape=(pltpu.SemaphoreType.DMA((n_copies,)), pltpu.SemaphoreType.DMA((n_copies,)),
                   *[pltpu.HBM(b.shape, b.dtype) for b in bufs], jax.ShapeDtypeStruct((8, LANES), F32)),
        out_specs=(SEM, SEM, *[HBM] * nb, pl.BlockSpec(memory_space=pltpu.VMEM)),
        input_output_aliases={i: 2 + i for i in range(nb)},
        compiler_params=pltpu.CompilerParams(has_side_effects=DATAFLOW),
    )(*[_in_hbm(b) for b in bufs], *after)
    return outs[0], outs[1], list(outs[2:2 + nb]), outs[-1]


def split_wait(name, send_sems, recv_sems, bufs, sends, arrivals, after):
    nb, na = len(bufs), len(after)

    def body(*refs):
        send, recv = refs[nb], refs[nb + 1]
        for cp in sends(refs[:nb], send, recv):
            cp.wait_send()
        for cp in arrivals(refs[:nb], send, recv):
            cp.wait_recv()

    outs = pl.pallas_call(
        body, name=name,
        in_specs=[HBM] * nb + [SEM, SEM] + [ANY] * na,
        out_shape=tuple(pltpu.HBM(b.shape, b.dtype) for b in bufs), out_specs=tuple([HBM] * nb),
        input_output_aliases={i: i for i in range(nb)},
        compiler_params=pltpu.CompilerParams(has_side_effects=DATAFLOW),
    )(*bufs, send_sems, recv_sems, *after)
    return list(outs)


def _gather_plan(n):
    def sends(refs, send, recv):
        x, y, c, chips = _place()
        me = 2 * x + y
        return [_remote(refs[t].at[me, c], refs[t].at[me, c], send.at[3 * t + k], recv.at[3 * t + k], (px, py, c))
                for t in range(n) for k, (px, py) in enumerate(chips)]

    def arrivals(refs, send, recv):
        x, y, c, chips = _place()
        return [_remote(refs[t].at[2 * px + py, c], refs[t].at[2 * px + py, c], send.at[3 * t + k], recv.at[3 * t + k],
                        (px, py, c)) for t in range(n) for k, (px, py) in enumerate(chips)]

    return 3 * n, sends, arrivals


def _relay_plan(n):
    def sends(refs, send, recv):
        x, y, c, chips = _place()
        return [_remote(refs[t].at[2 * px + py, c], refs[t].at[2 * px + py, c], send.at[3 * t + k], recv.at[3 * t + k],
                        (x, y, 1 - c)) for t in range(n) for k, (px, py) in enumerate(chips)]

    def arrivals(refs, send, recv):
        x, y, c, chips = _place()
        return [_remote(refs[t].at[2 * px + py, 1 - c], refs[t].at[2 * px + py, 1 - c], send.at[3 * t + k],
                        recv.at[3 * t + k], (x, y, 1 - c)) for t in range(n) for k, (px, py) in enumerate(chips)]

    return 3 * n, sends, arrivals


def _swap_plan(n):
    def copies(refs, send, recv):
        x, y, c, _ = _place()
        return [_remote(refs[t].at[s, 1 - c], refs[n + t].at[s], send.at[N_CHIPS * t + s], recv.at[N_CHIPS * t + s],
                        (x, y, 1 - c)) for t in range(n) for s in range(N_CHIPS)]

    return N_CHIPS * n, copies, copies


def _scatter_plan(n):
    def sends(refs, send, recv):
        x, y, c, chips = _place()
        me = 2 * x + y
        return [_remote(refs[t].at[2 * px + py], refs[n + t].at[me], send.at[3 * t + k], recv.at[3 * t + k],
                        (px, py, c)) for t in range(n) for k, (px, py) in enumerate(chips)]

    def arrivals(refs, send, recv):
        x, y, c, chips = _place()
        return [_remote(refs[t].at[2 * px + py], refs[n + t].at[2 * px + py], send.at[3 * t + k], recv.at[3 * t + k],
                        (px, py, c)) for t in range(n) for k, (px, py) in enumerate(chips)]

    return 3 * n, sends, arrivals


def cast_into_slot(name, w, layer, chip, deps=(), tm=256):
    _, r, c = w.shape
    tm = min(tm, r)

    def body(chip_ref, w_ref, *rest):
        rest[-1][...] = w_ref[...].astype(BF16)

    return pl.pallas_call(
        body, name=name,
        grid_spec=pltpu.PrefetchScalarGridSpec(
            num_scalar_prefetch=1, grid=(r // tm,),
            in_specs=[pl.BlockSpec((None, tm, c), lambda i, chip_ref: (layer, i, 0)), *[ANY] * len(deps)],
            out_specs=pl.BlockSpec((None, tm, c), lambda i, chip_ref: (chip_ref[0], i, 0))),
        out_shape=jax.ShapeDtypeStruct((N_CHIPS, r, c), BF16), compiler_params=_params("parallel"),
    )(chip, w, *deps)


def join_halves(name, f):
    n = f.shape[0]

    def body(f_ref, o_ref, send, recv):
        x, y, c, _ = _place()
        sib = (x, y, 1 - c)
        sends = [_remote(f_ref.at[l, c], o_ref.at[l, c], send.at[l], recv.at[l], sib) for l in range(n)]
        for cp in sends:
            cp.start()
        for l in range(n):
            _remote(f_ref.at[l, 1 - c], o_ref.at[l, 1 - c], send.at[l], recv.at[l], sib).wait_recv()
        for cp in sends:
            cp.wait_send()

    return pl.pallas_call(
        body, name=name, in_specs=[HBM], out_specs=HBM, out_shape=jax.ShapeDtypeStruct(f.shape, f.dtype),
        input_output_aliases={0: 0},
        scratch_shapes=[pltpu.SemaphoreType.DMA((n,)), pltpu.SemaphoreType.DMA((n,))],
    )(f)


def add_sibling_half(name, g, a, core, tm=256):
    _, _, rh, c = g.shape
    tm = min(tm, rh)

    def body(core_ref, g_ref, a_ref, o_ref):
        o_ref[...] = (g_ref[...].astype(F32) + a_ref[...].astype(F32)).astype(BF16)

    return pl.pallas_call(
        body, name=name,
        grid_spec=pltpu.PrefetchScalarGridSpec(
            num_scalar_prefetch=1, grid=(N_CHIPS, rh // tm),
            in_specs=[pl.BlockSpec((None, None, tm, c), lambda s, i, core_ref: (s, core_ref[0], i, 0)),
                      pl.BlockSpec((None, tm, c), lambda s, i, core_ref: (s, i, 0))],
            out_specs=pl.BlockSpec((None, tm, c), lambda s, i, core_ref: (s, i, 0))),
        out_shape=jax.ShapeDtypeStruct(a.shape, BF16), compiler_params=_params("parallel", "parallel"),
    )(core, g, a)


def sum_chips_into(name, p, b, f, layer, chip, core, tm=256):
    _, rh, c = b.shape
    tm = min(tm, rh)

    def body(chip_ref, core_ref, p_ref, b_ref, f_ref, o_ref):
        acc = jnp.zeros((tm, c), F32)
        for s in range(N_CHIPS):
            acc = acc + jnp.where(chip_ref[0] == s, p_ref[s].astype(F32), b_ref[s].astype(F32))
        o_ref[...] = acc

    slots = pl.BlockSpec((N_CHIPS, tm, c), lambda i, chip_ref, core_ref: (0, i, 0))
    return pl.pallas_call(
        body, name=name,
        grid_spec=pltpu.PrefetchScalarGridSpec(
            num_scalar_prefetch=2, grid=(rh // tm,), in_specs=[slots, slots, ANY],
            out_specs=pl.BlockSpec((None, None, tm, c), lambda i, chip_ref, core_ref: (layer, core_ref[0], i, 0))),
        out_shape=jax.ShapeDtypeStruct(f.shape, F32), input_output_aliases={4: 0},
        compiler_params=_params("parallel"),
    )(chip, core, p, b, f)


N_DEV = 8


def allreduce_small(name, parts):
    p, _, d = parts.shape
    m_per = p * 8

    def body(x_ref, out_ref, all_ref, send_sems, recv_sems, local_sem):
        x, y, c, chips = _place()
        me, sibling = (x, y, c), (x, y, 1 - c)

        def rows(px, py, pc):
            return all_ref.at[pl.ds((4 * px + 2 * py + pc) * m_per, m_per), :]

        def copy(k, block, to, src=None):
            return _remote(rows(*block) if src is None else src, rows(*block), send_sems.at[k], recv_sems.at[k], to)

        mine = pltpu.make_async_copy(x_ref, rows(*me), local_sem)
        mine.start()
        first = [copy(0, me, sibling, src=x_ref)]
        first += [copy(1 + j, me, (*chip, c), src=x_ref) for j, chip in enumerate(chips)]
        for cp in first:
            cp.start()
        passed = [copy(4 + j, (*chip, c), sibling) for j, chip in enumerate(chips)]
        for j, chip in enumerate(chips):
            copy(1 + j, (*chip, c), me).wait_recv()
            passed[j].start()
        copy(0, sibling, me).wait_recv()
        for j, chip in enumerate(chips):
            copy(4 + j, (*chip, 1 - c), me).wait_recv()
        for cp in first + passed:
            cp.wait_send()
        mine.wait()
        acc = all_ref[pl.ds(0, m_per), :]
        for dev in range(1, N_DEV):
            acc = acc + all_ref[pl.ds(dev * m_per, m_per), :]
        out_ref[...] = jnp.sum(acc.reshape(p, 8, d), axis=1)

    vmem = pl.BlockSpec(memory_space=pltpu.VMEM)
    return pl.pallas_call(
        body, name=name, in_specs=[vmem], out_specs=vmem,
        out_shape=jax.ShapeDtypeStruct((p, d), F32),
        scratch_shapes=[pltpu.VMEM((N_DEV * m_per, d), F32), pltpu.SemaphoreType.DMA((7,)),
                        pltpu.SemaphoreType.DMA((7,)), pltpu.SemaphoreType.DMA],
        compiler_params=pltpu.CompilerParams(vmem_limit_bytes=VMEM_LIMIT),
    )(parts.reshape(m_per, d))


def _empty(shape, dtype):
    return _in_hbm(lax.empty(shape, dtype))


class _GradExchange:
    def __init__(self, layer, kinds, grads, chip, core):
        self.layer, self.kinds, self.chip, self.core = layer, kinds, chip, core
        self.g4 = [g.reshape(N_CHIPS, 2, g.shape[1] // 2, g.shape[2]) for g in grads]
        self.n = len(grads)

    def start_swap(self, after):
        n_copies, self.swap_sends, self.swap_arrivals = _swap_plan(self.n)
        lands = [_empty((N_CHIPS,) + g.shape[2:], BF16) for g in self.g4]
        self.swap = split_start(f"swap_start_l{self.layer}", self.g4 + lands, n_copies, self.swap_sends, after)
        return self.swap[3]

    def swap_to_scatter(self, after):
        send, recv, bufs, _ = self.swap
        bufs = split_wait(f"swap_wait_l{self.layer}", send, recv, bufs, self.swap_sends, self.swap_arrivals, after)
        g4, lands = bufs[:self.n], bufs[self.n:]
        self.p = [add_sibling_half(f"add_l{self.layer}_{k}", g4[t], lands[t], self.core)
                  for t, (k, _) in enumerate(self.kinds)]
        n_copies, self.sc_sends, self.sc_arrivals = _scatter_plan(self.n)
        lands = [_empty(p.shape, BF16) for p in self.p]
        self.scatter = split_start(f"scatter_start_l{self.layer}", self.p + lands, n_copies, self.sc_sends)
        return self.scatter[3]

    def finish(self, f, after):
        send, recv, bufs, _ = self.scatter
        bufs = split_wait(f"scatter_wait_l{self.layer}", send, recv, bufs, self.sc_sends, self.sc_arrivals, after)
        p, lands = bufs[:self.n], bufs[self.n:]
        for t, (kind, l) in enumerate(self.kinds):
            f[kind] = sum_chips_into(f"sum_l{self.layer}_{kind}", p[t], lands[t], f[kind], l, self.chip, self.core)


def _relu2(acc):
    r = jnp.maximum(acc, 0.0)
    return acc, r * r


def _relu2_bwd(acc, u):
    return (acc * (2.0 * jnp.maximum(u.astype(F32), 0.0)),)


def _same(acc):
    return (acc,)


def kernel(x, a_w_qkv, a_w_o, a_sinks, b_w_qkv, b_w_o, norm_mix, norm_mlp, mlp_w_in, mlp_w_out, final_norm, loss_target, m_a_w_qkv, m_a_w_o, m_a_sinks, m_b_w_qkv, m_b_w_o, m_norm_mix, m_norm_mlp, m_mlp_w_in, m_mlp_w_out, m_final_norm, v_a_w_qkv, v_a_w_o, v_a_sinks, v_b_w_qkv, v_b_w_o, v_norm_mix, v_norm_mlp, v_mlp_w_in, v_mlp_w_out, v_final_norm):
    _, s, d = x.shape
    depth = norm_mix.shape[0]
    width = N_HEADS * HEAD_DIM
    core = lax.axis_index("c").astype(jnp.int32).reshape(1)
    chip = (2 * lax.axis_index("x") + lax.axis_index("y")).astype(jnp.int32).reshape(1)
    slopes = jnp.power(2.0, -8.0 * (jnp.arange(N_HEADS, dtype=F32) + 1.0) / N_HEADS)
    qkv_of = {0: ("a_qkv", a_w_qkv), 1: ("b_qkv", b_w_qkv)}
    o_of = {0: ("a_o", a_w_o), 1: ("b_o", b_w_o)}

    def layer_kinds(i):
        return [(qkv_of[i % 2][0], i // 2), (o_of[i % 2][0], i // 2), ("mlp_in", i), ("mlp_out", i)]

    stacks = {"a_qkv": a_w_qkv, "a_o": a_w_o, "b_qkv": b_w_qkv, "b_o": b_w_o, "mlp_in": mlp_w_in, "mlp_out": mlp_w_out}

    def slots(i, deps=()):
        return [cast_into_slot(f"cast_l{i}_{k}", stacks[k], l, chip, deps) for k, l in layer_kinds(i)]

    n_gather, gather_sends, gather_arrivals = _gather_plan(2)
    _, relay_sends, relay_arrivals = _relay_plan(2)

    class Flight:
        def __init__(self, tag, bufs, after):
            self.tag = tag
            halves = [b.reshape(N_CHIPS, 2, b.shape[1] // 2, b.shape[2]) for b in bufs]
            self.shapes = [b.shape for b in bufs]
            self.state = split_start(f"gather_start_{tag}", halves, n_gather, gather_sends, after)
            self.token = self.state[3]

        def relay(self, after):
            send, recv, bufs, _ = self.state
            bufs = split_wait(f"gather_wait_{self.tag}", send, recv, bufs, gather_sends, gather_arrivals, after)
            self.state = split_start(f"relay_start_{self.tag}", bufs, n_gather, relay_sends)
            return self.state[3]

        def land(self, after):
            send, recv, bufs, _ = self.state
            bufs = split_wait(f"relay_wait_{self.tag}", send, recv, bufs, relay_sends, relay_arrivals, after)
            return [b.reshape(shape) for b, shape in zip(bufs, self.shapes)]

    saved, weights = [], []
    xc = x[0]
    first = slots(0)
    flight_a = Flight("l0a", first[:2], ())
    flight_m = Flight("l0m", first[2:], (flight_a.token,))
    flight_a.relay((xc, flight_m.token))
    for i in range(depth):
        mixer, j = i % 2, i // 2
        w_qkv, w_o = flight_a.land((xc,))
        w_o = w_o.reshape(width, d)
        started, this_m = (flight_m.token,), flight_m
        if i + 1 < depth:
            following = slots(i + 1, (flight_m.token,))
            flight_a = Flight(f"l{i + 1}a", following[:2], (w_qkv,))
            flight_m = Flight(f"l{i + 1}m", following[2:], (flight_a.token,))
            started = (flight_m.token,)
        h = rms_fwd(f"l{i}_norm_mix", xc, norm_mix[i], deps=started)
        qkv = mm_cols(f"l{i}_qkv", h, w_qkv, _same, (BF16,))[0]
        if mixer == 0:
            attn, kept = swa_fwd(f"l{i}_swa", qkv, a_sinks[j], slopes), ()
        else:
            attn, *kept = sb_fwd(f"l{i}_sb", qkv)
        xm = mm_res(f"l{i}_o", attn, w_o, xc, deps=(this_m.relay((attn,)),))
        w_in, w_out = this_m.land((xm,))
        w_out = w_out.reshape(-1, d)
        weights.append((w_qkv, w_o, w_in, w_out))
        h2 = rms_fwd(f"l{i}_norm_mlp", xm, norm_mlp[i])
        u, hh = mm_cols(f"l{i}_in", h2, w_in, _relu2, (BF16, BF16))
        relayed = (flight_a.relay((u,)),) if i + 1 < depth else ()
        xn = mm_res(f"l{i}_out", hh, w_out, xm, deps=relayed)
        saved.append((xc, h, qkv, attn, kept, xm, h2, u, hh))
        xc = xn
    loss_rows, dx, dxb, dg_final = loss_head("loss_head", xc, final_norm, loss_target[0])
    loss = lax.psum(jnp.sum(loss_rows), ("x", "y", "c"))

    big = {"a_qkv": (a_w_qkv, m_a_w_qkv, v_a_w_qkv), "a_o": (a_w_o, m_a_w_o, v_a_w_o),
           "b_qkv": (b_w_qkv, m_b_w_qkv, v_b_w_qkv), "b_o": (b_w_o, m_b_w_o, v_b_w_o),
           "mlp_in": (mlp_w_in, m_mlp_w_in, v_mlp_w_in), "mlp_out": (mlp_w_out, m_mlp_w_out, v_mlp_w_out)}
    f = {k: _empty((w.shape[0], 2, w.shape[1] // 2, w.shape[2]), F32) for k, (w, _, _) in big.items()}
    dg_mix, dg_mlp, dsinks = [], [], []
    prev_a, started = None, ()
    for i in reversed(range(depth)):
        mixer, j = i % 2, i // 2
        xin, h, qkv, attn, kept, xm, h2, u, hh = saved[i]
        w_qkv, w_o, w_in, w_out = weights[i]
        kinds = layer_kinds(i)
        du = mm_nt(f"l{i}_d_hidden", dxb, w_out, _relu2_bwd, (u,), deps=started)
        g_out = mm_tn(f"l{i}_g_out", hh, dxb).reshape(N_CHIPS, -1, d)
        started = (g_out,) if prev_a is None else (g_out, prev_a.swap_to_scatter((g_out,)))
        dh2 = mm_nt_cols(f"l{i}_d_h2", du, w_in, deps=started)
        g_in = mm_tn_cols(f"l{i}_g_in", h2, du)
        cur_m = _GradExchange(f"{i}m", kinds[2:], [g_in, g_out], chip, core)
        started = (g_in, cur_m.start_swap((g_in,)))
        dxm, dxmb, dg = rms_bwd(f"l{i}_norm_mlp_bwd", xm, norm_mlp[i], dh2, dx, deps=started)
        dg_mlp.append(dg)
        g_o = mm_tn(f"l{i}_g_o", attn, dxmb).reshape(N_CHIPS, -1, d)
        dattn = mm_nt(f"l{i}_d_attn", dxmb, w_o, _same, deps=(g_o, cur_m.swap_to_scatter((g_o,))))
        if mixer == 0:
            dq, dk, dv, dsk = swa_bwd(f"l{i}_swa_bwd", qkv, dattn, a_sinks[j], slopes)
            dsinks.append(dsk[:, :2, 0].reshape(N_HEADS))
        else:
            attn_f32, weights_a, sigmoids = kept
            dq, dk, dv = sb_bwd(f"l{i}_sb_bwd", qkv, attn_f32, dattn, weights_a, sigmoids)
        dqkv = jnp.concatenate([dq, dk.astype(BF16), dv.astype(BF16)], axis=1)
        dh = mm_nt_cols(f"l{i}_d_h", dqkv, w_qkv)
        g_qkv = mm_tn_cols(f"l{i}_g_qkv", h, dqkv)
        dx, dxb, dg = rms_bwd(f"l{i}_norm_mix_bwd", xin, norm_mix[i], dh, dxm, deps=(g_qkv,))
        dg_mix.append(dg)
        if prev_a is not None:
            prev_a.finish(f, (dx,))
        cur_m.finish(f, (dx,))
        prev_a = _GradExchange(f"{i}a", kinds[:2], [g_qkv, g_o], chip, core)
        started = (prev_a.start_swap((dx,)),)
    prev_a.swap_to_scatter((dx,))
    prev_a.finish(f, (dx,))
    for lst in (dg_mix, dg_mlp, dsinks):
        lst.reverse()

    res = {}
    for kind, (w, m, v) in big.items():
        g = join_halves(f"join_{kind}", f[kind]).reshape(w.shape)
        res[kind] = (g, *adamw(f"adamw_{kind}", w, g, m, v))

    n_sink = a_sinks.size
    sink_rows = jnp.zeros((1, 8, d), F32).at[0, 0, :n_sink].set(jnp.concatenate(dsinks))
    parts = jnp.concatenate([jnp.stack(dg_mix), jnp.stack(dg_mlp), dg_final[None], sink_rows], axis=0)
    g_small = allreduce_small("allreduce_small", parts)

    def pack(mix, mlp, fin, snk):
        snk_row = jnp.zeros((1, d), F32).at[0, :n_sink].set(snk.reshape(-1))
        return jnp.concatenate([mix, mlp, fin[None], snk_row], axis=0)

    def unpack(t):
        return t[:depth], t[depth:2 * depth], t[2 * depth], t[2 * depth + 1, :n_sink].reshape(a_sinks.shape)

    small = adamw("adamw_small", pack(norm_mix, norm_mlp, final_norm, a_sinks), g_small,
                  pack(m_norm_mix, m_norm_mlp, m_final_norm, m_a_sinks),
                  pack(v_norm_mix, v_norm_mlp, v_final_norm, v_a_sinks))
    outs = []
    for idx in range(4):
        mix, mlp, fin, snk = unpack(g_small if idx == 0 else small[idx - 1])
        outs += [res["a_qkv"][idx], res["a_o"][idx], snk, res["b_qkv"][idx], res["b_o"][idx], mix, mlp,
                 res["mlp_in"][idx], res["mlp_out"][idx], fin]
    return (loss, dx.reshape(x.shape), *outs)
```

```python
import functools
import math

import jax
import jax.numpy as jnp
from jax import lax
from jax.experimental import pallas as pl
from jax.experimental.pallas import tpu as pltpu

F32 = jnp.float32
BF16 = jnp.bfloat16
MESH = pl.DeviceIdType.MESH

N_CHIPS = 4
HEAD_DIM = 64
LANES = 128
N_HEADS = 32
N_KV_A = 4
WINDOW = 128
RMS_EPS = 1e-5
ATTN_SCALE = 1.0 / math.sqrt(HEAD_DIM)
ADAM_LR, ADAM_B1, ADAM_B2, ADAM_EPS, ADAM_WD, ADAM_STEP = 0.001, 0.9, 0.999, 1e-08, 0.01, 10
NEG_BIG = -1e30
VMEM_LIMIT = 56 * 1024 * 1024

_DN = {"nn": (((1,), (0,)), ((), ())), "nt": (((1,), (1,)), ((), ())), "tn": (((0,), (0,)), ((), ()))}


def _dot(a, b, mode="nn"):
    return lax.dot_general(a, b, _DN[mode], preferred_element_type=F32)


def _params(*sem):
    return pltpu.CompilerParams(dimension_semantics=sem, vmem_limit_bytes=VMEM_LIMIT)


def _pick(n, prefs):
    for t in prefs:
        if n % t == 0:
            return t
    return n


def _mm(name, mode, a, b, *, grid, a_spec, b_spec, extras=(), extra_specs=(), out_shapes, out_specs, nk,
        acc_shape, epilogue, deps=()):
    n_ex, n_out = len(extras), len(out_shapes)
    first_out = 2 + n_ex + len(deps)

    def body(*refs):
        a_ref, b_ref = refs[0], refs[1]
        ex = refs[2:2 + n_ex]
        outs = refs[first_out:first_out + n_out]
        part = _dot(a_ref[...], b_ref[...], mode)

        def finish(acc):
            res = epilogue(acc, *[e[...] for e in ex])
            for o, r in zip(outs, res):
                o[...] = r.astype(o.dtype)

        if nk == 1:
            finish(part)
        else:
            acc_ref = refs[-1]
            k = pl.program_id(len(grid) - 1)

            @pl.when(k == 0)
            def _():
                acc_ref[...] = part

            @pl.when(k > 0)
            def _():
                acc_ref[...] += part

            @pl.when(k == nk - 1)
            def _():
                finish(acc_ref[...])

    sem = ("parallel",) * (len(grid) - 1) + ("arbitrary" if nk > 1 else "parallel",)
    return pl.pallas_call(
        body, name=name, grid=grid,
        in_specs=[a_spec, b_spec, *extra_specs, *[ANY] * len(deps)],
        out_specs=list(out_specs), out_shape=list(out_shapes),
        scratch_shapes=[] if nk == 1 else [pltpu.VMEM(acc_shape, F32)],
        compiler_params=_params(*sem),
    )(a, b, *extras, *deps)


def mm_cols(name, a, wg, epilogue, out_dtypes, tm=2048):
    m, k = a.shape
    c = wg.shape[2]
    tm = min(tm, m)
    tn = _pick(c, (512, 640, 256, 128))
    nj = c // tn
    o_spec = pl.BlockSpec((tm, tn), lambda i, j: (i, j))
    return _mm(name, "nn", a, wg, grid=(m // tm, N_CHIPS * nj),
               a_spec=pl.BlockSpec((tm, k), lambda i, j: (i, 0)),
               b_spec=pl.BlockSpec((None, k, tn), lambda i, j: (j // nj, 0, j % nj)),
               out_shapes=[jax.ShapeDtypeStruct((m, N_CHIPS * c), d) for d in out_dtypes],
               out_specs=[o_spec] * len(out_dtypes), nk=1, acc_shape=None, epilogue=epilogue)


def mm_res(name, a, w, res, deps=(), tm=1024, tn=1024, tk=2048):
    m, k = a.shape
    n = w.shape[1]
    tm, tn, tk = min(tm, m), min(tn, n), min(tk, k)
    nk = k // tk
    return _mm(name, "nn", a, w, grid=(m // tm, n // tn, nk),
               a_spec=pl.BlockSpec((tm, tk), lambda i, j, kk: (i, kk)),
               b_spec=pl.BlockSpec((tk, tn), lambda i, j, kk: (kk, j)),
               extras=(res,), extra_specs=(pl.BlockSpec((tm, tn), lambda i, j, kk: (i, j)),),
               out_shapes=[jax.ShapeDtypeStruct((m, n), F32)],
               out_specs=[pl.BlockSpec((tm, tn), lambda i, j, kk: (i, j))], nk=nk, acc_shape=(tm, tn),
               epilogue=lambda acc, r: (acc + r,), deps=deps)[0]


def mm_nt(name, a, w, epilogue, extras=(), deps=(), tm=2048, tn=512):
    m, k = a.shape
    n = w.shape[0]
    tm, tn = min(tm, m), min(tn, n)
    o_spec = pl.BlockSpec((tm, tn), lambda i, j: (i, j))
    return _mm(name, "nt", a, w, grid=(m // tm, n // tn),
               a_spec=pl.BlockSpec((tm, k), lambda i, j: (i, 0)),
               b_spec=pl.BlockSpec((tn, k), lambda i, j: (j, 0)),
               extras=tuple(extras), extra_specs=(o_spec,) * len(extras),
               out_shapes=[jax.ShapeDtypeStruct((m, n), BF16)], out_specs=[o_spec], nk=1, acc_shape=None,
               epilogue=epilogue, deps=deps)[0]


def mm_nt_cols(name, dy, wg, deps=(), tm=1024, tn=1024):
    m = dy.shape[0]
    _, d, c = wg.shape
    tm, tn = min(tm, m), min(tn, d)
    tk = _pick(c, (2048, 1536, 640, 512, 128))
    nkk = c // tk
    nk = N_CHIPS * nkk
    return _mm(name, "nt", dy, wg, grid=(m // tm, d // tn, nk),
               a_spec=pl.BlockSpec((tm, tk), lambda i, j, kk: (i, kk)),
               b_spec=pl.BlockSpec((None, tn, tk), lambda i, j, kk: (kk // nkk, j, kk % nkk)),
               out_shapes=[jax.ShapeDtypeStruct((m, d), F32)],
               out_specs=[pl.BlockSpec((tm, tn), lambda i, j, kk: (i, j))], nk=nk, acc_shape=(tm, tn),
               epilogue=lambda acc: (acc,), deps=deps)[0]


def mm_tn(name, a, b, tm=1024, tn=1024, tk=2048):
    m, p = a.shape
    q = b.shape[1]
    tm, tn, tk = min(tm, p), min(tn, q), min(tk, m)
    nk = m // tk
    return _mm(name, "tn", a, b, grid=(p // tm, q // tn, nk),
               a_spec=pl.BlockSpec((tk, tm), lambda i, j, kk: (kk, i)),
               b_spec=pl.BlockSpec((tk, tn), lambda i, j, kk: (kk, j)),
               out_shapes=[jax.ShapeDtypeStruct((p, q), BF16)],
               out_specs=[pl.BlockSpec((tm, tn), lambda i, j, kk: (i, j))], nk=nk, acc_shape=(tm, tn),
               epilogue=lambda acc: (acc,))[0]


def mm_tn_cols(name, a, dy, tm=1024, tk=2048):
    m, d = a.shape
    c = dy.shape[1] // N_CHIPS
    tm, tk = min(tm, d), min(tk, m)
    tn = _pick(c, (1024, 768, 640, 512, 128))
    nj = c // tn
    nk = m // tk
    return _mm(name, "tn", a, dy, grid=(d // tm, N_CHIPS * nj, nk),
               a_spec=pl.BlockSpec((tk, tm), lambda i, j, kk: (kk, i)),
               b_spec=pl.BlockSpec((tk, tn), lambda i, j, kk: (kk, j)),
               out_shapes=[jax.ShapeDtypeStruct((N_CHIPS, d, c), BF16)],
               out_specs=[pl.BlockSpec((None, tm, tn), lambda i, j, kk: (j // nj, i, j % nj))], nk=nk,
               acc_shape=(tm, tn), epilogue=lambda acc: (acc,))[0]


def _rows_to_8(v):
    tm, d = v.shape
    return jnp.sum(v.reshape(tm // 8, 8, d), axis=0)


def rms_fwd(name, x, gain, deps=(), tm=512):
    s, d = x.shape
    tm = min(tm, s)

    def body(x_ref, g_ref, *rest):
        h_ref = rest[-1]
        xv = x_ref[...]
        r = lax.rsqrt(jnp.mean(xv * xv, axis=-1, keepdims=True) + RMS_EPS)
        h_ref[...] = (xv * r * g_ref[...]).astype(BF16)

    row = pl.BlockSpec((tm, d), lambda i: (i, 0))
    return pl.pallas_call(
        body, name=name, grid=(s // tm,),
        in_specs=[row, pl.BlockSpec((1, d), lambda i: (0, 0)), *[ANY] * len(deps)], out_specs=row,
        out_shape=jax.ShapeDtypeStruct((s, d), BF16), compiler_params=_params("parallel"),
    )(x, gain.reshape(1, d), *deps)


def rms_bwd(name, x, gain, dh, dres, deps=(), tm=512):
    s, d = x.shape
    tm = min(tm, s)

    def body(x_ref, g_ref, dh_ref, dres_ref, *rest):
        dx_ref, dxb_ref, dg_ref = rest[-3:]
        xv = x_ref[...]
        r = lax.rsqrt(jnp.mean(xv * xv, axis=-1, keepdims=True) + RMS_EPS)
        xhat = xv * r
        dhv = dh_ref[...]
        dxhat = dhv * g_ref[...]
        dx = dres_ref[...] + r * (dxhat - xhat * jnp.mean(dxhat * xhat, axis=-1, keepdims=True))
        dx_ref[...] = dx
        dxb_ref[...] = dx.astype(BF16)

        @pl.when(pl.program_id(0) == 0)
        def _():
            dg_ref[...] = jnp.zeros_like(dg_ref)

        dg_ref[...] += _rows_to_8(dhv * xhat)

    row = pl.BlockSpec((tm, d), lambda i: (i, 0))
    return pl.pallas_call(
        body, name=name, grid=(s // tm,),
        in_specs=[row, pl.BlockSpec((1, d), lambda i: (0, 0)), row, row, *[ANY] * len(deps)],
        out_specs=[row, row, pl.BlockSpec((8, d), lambda i: (0, 0))],
        out_shape=[jax.ShapeDtypeStruct((s, d), F32), jax.ShapeDtypeStruct((s, d), BF16),
                   jax.ShapeDtypeStruct((8, d), F32)],
        compiler_params=_params("arbitrary"),
    )(x, gain.reshape(1, d), dh, dres, *deps)


def loss_head(name, x, gain, target, tm=512):
    s, d = x.shape
    tm = min(tm, s)

    def body(x_ref, g_ref, t_ref, loss_ref, dx_ref, dxb_ref, dg_ref):
        xv = x_ref[...]
        g = g_ref[...]
        r = lax.rsqrt(jnp.mean(xv * xv, axis=-1, keepdims=True) + RMS_EPS)
        xhat = xv * r
        err = xhat * g - t_ref[...]
        dy = err * (1.0 / d)
        dxhat = dy * g
        dx = r * (dxhat - xhat * jnp.mean(dxhat * xhat, axis=-1, keepdims=True))
        dx_ref[...] = dx
        dxb_ref[...] = dx.astype(BF16)

        @pl.when(pl.program_id(0) == 0)
        def _():
            dg_ref[...] = jnp.zeros_like(dg_ref)
            loss_ref[...] = jnp.zeros_like(loss_ref)

        dg_ref[...] += _rows_to_8(dy * xhat)
        loss_ref[...] += _rows_to_8(err * err) * (0.5 / d)

    row = pl.BlockSpec((tm, d), lambda i: (i, 0))
    vec = pl.BlockSpec((8, d), lambda i: (0, 0))
    return pl.pallas_call(
        body, name=name, grid=(s // tm,),
        in_specs=[row, pl.BlockSpec((1, d), lambda i: (0, 0)), row],
        out_specs=[vec, row, row, vec],
        out_shape=[jax.ShapeDtypeStruct((8, d), F32), jax.ShapeDtypeStruct((s, d), F32),
                   jax.ShapeDtypeStruct((s, d), BF16), jax.ShapeDtypeStruct((8, d), F32)],
        compiler_params=_params("arbitrary"),
    )(x, gain.reshape(1, d), target)


def _half_masks(dtype):
    lane = lax.broadcasted_iota(jnp.int32, (1, LANES), 1)
    lo = (lane < HEAD_DIM).astype(dtype)
    return lo, (1 - lo).astype(dtype)


def _swap_halves(v):
    return pltpu.roll(v, HEAD_DIM, axis=1)


SWA_BLOCKS = 8


def _swa_probs(s_raw, sink, slope, first):
    t = s_raw.shape[0]
    row = lax.broadcasted_iota(jnp.int32, (t, 2 * WINDOW), 0)
    col = lax.broadcasted_iota(jnp.int32, (t, 2 * WINDOW), 1)
    dist = row + WINDOW - col
    valid = (dist >= 0) & (dist < WINDOW)
    if first is not None:
        valid = valid & ((col >= WINDOW) | jnp.logical_not(first))
    s = jnp.where(valid, s_raw - slope * dist.astype(F32), NEG_BIG)
    m = jnp.maximum(jnp.max(s, axis=-1, keepdims=True), sink)
    e = jnp.exp(s - m)
    e_sink = jnp.exp(sink - m)
    inv = 1.0 / (jnp.sum(e, axis=-1, keepdims=True) + e_sink)
    return e * inv, e_sink * inv


def _swa_band(kp_ref, kc_ref, vp_ref, vc_ref, kv_half):
    lo_b, hi_b = _half_masks(F32)
    sel = jnp.where(kv_half == 0, lo_b, hi_b)
    k = jnp.concatenate([kp_ref[...], kc_ref[...]], axis=0).astype(F32) * sel
    v = jnp.concatenate([vp_ref[...], vc_ref[...]], axis=0).astype(F32) * sel
    k = (k + _swap_halves(k)).astype(BF16)
    v = (v + _swap_halves(v)).astype(BF16)
    return k, v, sel


def _swa_specs(nq):
    t = WINDOW
    q_spec = pl.BlockSpec((nq * t, LANES), lambda j, n, *_: (n, j))
    q_blocks = N_HEADS // 2

    def prev(off):
        return pl.BlockSpec((t, LANES), lambda j, n, *_: (jnp.maximum(nq * n - 1, 0), q_blocks + off + j // 8))

    def cur(off):
        return pl.BlockSpec((nq * t, LANES), lambda j, n, *_: (n, q_blocks + off + j // 8))

    kv_blocks = N_KV_A // 2
    return q_spec, [prev(0), cur(0), prev(kv_blocks), cur(kv_blocks)]


def _rows(a, b, n=1):
    return a[b * WINDOW:(b + n) * WINDOW]


def swa_fwd(name, qkv, sinks, slopes):
    s = qkv.shape[0]
    nq = min(SWA_BLOCKS, s // WINDOW)

    def body(sink_ref, slope_ref, q_ref, kp_ref, kc_ref, vp_ref, vc_ref, o_ref):
        j, n = pl.program_id(0), pl.program_id(1)
        k_all, v_all, _ = _swa_band(kp_ref, kc_ref, vp_ref, vc_ref, (j // 4) % 2)
        q = q_ref[...] * ATTN_SCALE
        masks = _half_masks(BF16)
        chains = [(b, hq) for b in range(nq) for hq in range(2)]
        s_raw = {(b, hq): _dot(_rows(q, b) * masks[hq], _rows(k_all, b, 2), "nt") for b, hq in chains}
        p = {(b, hq): _swa_probs(s_raw[(b, hq)], sink_ref[2 * j + hq], slope_ref[2 * j + hq],
                                 (n == 0) if b == 0 else None)[0] for b, hq in chains}
        outs = [sum(_dot(p[(b, hq)].astype(BF16), _rows(v_all, b, 2) * masks[hq]) for hq in range(2))
                for b in range(nq)]
        o_ref[...] = jnp.concatenate(outs, axis=0).astype(BF16)

    q_spec, kv_specs = _swa_specs(nq)
    return pl.pallas_call(
        body, name=name,
        grid_spec=pltpu.PrefetchScalarGridSpec(
            num_scalar_prefetch=2, grid=(N_HEADS // 2, s // (nq * WINDOW)),
            in_specs=[q_spec, *kv_specs], out_specs=q_spec),
        out_shape=jax.ShapeDtypeStruct((s, N_HEADS * HEAD_DIM), BF16),
        compiler_params=_params("parallel", "parallel"),
    )(sinks, slopes, qkv, qkv, qkv, qkv, qkv)


def swa_bwd(name, qkv, do, sinks, slopes):
    s = qkv.shape[0]
    t = WINDOW
    nq = min(SWA_BLOCKS, s // t)

    def body(sink_ref, slope_ref, q_ref, kp_ref, kc_ref, vp_ref, vc_ref, do_ref, dq_ref, dk_ref, dv_ref, ds_ref):
        j, n = pl.program_id(0), pl.program_id(1)
        k_all, v_all, sel = _swa_band(kp_ref, kc_ref, vp_ref, vc_ref, (j // 4) % 2)
        q = q_ref[...] * ATTN_SCALE
        do_v = do_ref[...]
        masks = _half_masks(BF16)

        @pl.when((j % 8 == 0) & (n == 0))
        def _():
            dk_ref[...] = jnp.zeros_like(dk_ref)
            dv_ref[...] = jnp.zeros_like(dv_ref)

        @pl.when(n == 0)
        def _():
            ds_ref[...] = jnp.zeros_like(ds_ref)

        chains = [(b, hq) for b in range(nq) for hq in range(2)]
        qm = {(b, hq): _rows(q, b) * masks[hq] for b, hq in chains}
        dom = {(b, hq): _rows(do_v, b) * masks[hq] for b, hq in chains}
        s_raw = {ch: _dot(qm[ch], _rows(k_all, ch[0], 2), "nt") for ch in chains}
        dp = {ch: _dot(dom[ch], _rows(v_all, ch[0], 2), "nt") for ch in chains}
        pb, dsc = {}, {}
        dsink = [jnp.zeros((), F32), jnp.zeros((), F32)]
        for ch in chains:
            b, hq = ch
            p, p_sink = _swa_probs(s_raw[ch], sink_ref[2 * j + hq], slope_ref[2 * j + hq], (n == 0) if b == 0 else None)
            delta = jnp.sum(p * dp[ch], axis=-1, keepdims=True)
            dsc[ch] = (p * (dp[ch] - delta)).astype(BF16)
            pb[ch] = p.astype(BF16)
            dsink[hq] = dsink[hq] - jnp.sum(p_sink * delta)
        for hq in range(2):
            ds_ref[hq:hq + 1, :] += jnp.zeros((1, LANES), F32) + dsink[hq]
        dq, dk, dv = [], [], []
        for b in range(nq):
            dq.append(sum(_dot(dsc[(b, hq)], _rows(k_all, b, 2) * masks[hq]) for hq in range(2)))
            dk_b = sum(_dot(dsc[(b, hq)], qm[(b, hq)], "tn") for hq in range(2))
            dv_b = sum(_dot(pb[(b, hq)], dom[(b, hq)], "tn") for hq in range(2))
            dk.append((dk_b + _swap_halves(dk_b)) * sel)
            dv.append((dv_b + _swap_halves(dv_b)) * sel)
        dq_ref[...] = (jnp.concatenate(dq, axis=0) * ATTN_SCALE).astype(BF16)

        @pl.when(n == 0)
        def _():
            dk_ref[pl.ds(0, t), :] += dk[0][t:]
            dv_ref[pl.ds(0, t), :] += dv[0][t:]

        @pl.when(n > 0)
        def _():
            start = pl.multiple_of((nq * n - 1) * t, t)
            dk_ref[pl.ds(start, 2 * t), :] += dk[0]
            dv_ref[pl.ds(start, 2 * t), :] += dv[0]

        for b in range(1, nq):
            start = pl.multiple_of((nq * n + b - 1) * t, t)
            dk_ref[pl.ds(start, 2 * t), :] += dk[b]
            dv_ref[pl.ds(start, 2 * t), :] += dv[b]

    q_spec, kv_specs = _swa_specs(nq)
    kv_out = pl.BlockSpec((s, LANES), lambda j, n, *_: (0, j // 8))
    return pl.pallas_call(
        body, name=name,
        grid_spec=pltpu.PrefetchScalarGridSpec(
            num_scalar_prefetch=2, grid=(N_HEADS // 2, s // (nq * t)),
            in_specs=[q_spec, *kv_specs, q_spec],
            out_specs=[q_spec, kv_out, kv_out, pl.BlockSpec((None, 8, LANES), lambda j, n, *_: (j, 0, 0))]),
        out_shape=[jax.ShapeDtypeStruct((s, N_HEADS * HEAD_DIM), BF16),
                   jax.ShapeDtypeStruct((s, N_KV_A * HEAD_DIM), F32),
                   jax.ShapeDtypeStruct((s, N_KV_A * HEAD_DIM), F32),
                   jax.ShapeDtypeStruct((N_HEADS // 2, 8, LANES), F32)],
        compiler_params=_params("arbitrary", "arbitrary"),
    )(sinks, slopes, qkv, qkv, qkv, qkv, qkv, do)


SB_TILE = 256


def _split_k(v):
    hi = v.astype(BF16)
    lo = (v - hi.astype(F32)).astype(BF16)
    return jnp.concatenate([hi, lo], axis=1)


def _tri2(t, inclusive):
    r = lax.broadcasted_iota(jnp.int32, (2 * t, t), 0)
    c = lax.broadcasted_iota(jnp.int32, (2 * t, t), 1)
    r = jnp.where(r >= t, r - t, r)
    return ((r >= c) if inclusive else (r > c)).astype(BF16)


def _sb_logs(z, before):
    neg_abs = lax.bitcast_convert_type(lax.bitcast_convert_type(z, jnp.uint32) | jnp.uint32(0x80000000), F32)
    l = jnp.log(1.0 + jnp.exp(neg_abs))
    lb = jnp.minimum(z, 0.0) - l
    lm = lb - z
    if before is not None:
        lm = jnp.where(before, lm, 0.0)
    return lb, lm


def _sb_weights(lb, sfx, c_lm, before):
    a = jnp.exp(lb + sfx + c_lm)
    if before is not None:
        a = jnp.where(before, a, 0.0)
    return a


def _sb_sweep(tile, i, init, diag):
    carry = lax.cond(i > 0, lambda: tile([i, i - 1], init, diag), lambda: tile([i], init, diag))
    rest = jnp.maximum(i - 1, 0)
    odd = rest % 2
    carry = lax.cond(odd == 1, lambda: tile([i - 2], carry, None), lambda: carry)
    base = i - 2 - odd
    return lax.fori_loop(0, rest // 2, lambda n, cr: tile([base - 2 * n, base - 2 * n - 1], cr, None), carry)


def _sb_specs(s, t):
    hp, nb = N_HEADS // 2, s // t
    q_spec = pl.BlockSpec((t, LANES), lambda h, i: (i, h))
    k_spec = pl.BlockSpec((s, LANES), lambda h, i: (0, hp + h))
    v_spec = pl.BlockSpec((s, LANES), lambda h, i: (0, 2 * hp + h))
    saved = pl.BlockSpec((None, 2, None, nb, t, t), lambda h, i: (h, 0, i, 0, 0, 0))
    return q_spec, k_spec, v_spec, saved


def sb_fwd(name, qkv, t=SB_TILE):
    s = qkv.shape[0]
    t = min(t, s // 2)
    nb = s // t

    def body(q_ref, k_ref, v_ref, ob_ref, of_ref, a_ref, sg_ref):
        i = pl.program_id(1)
        masks = _half_masks(BF16)
        q = q_ref[...] * ATTN_SCALE
        qm = [q * masks[0], q * masks[1]]
        tri_x = _tri2(t, False)
        r = lax.broadcasted_iota(jnp.int32, (t, t), 0)
        c = lax.broadcasted_iota(jnp.int32, (t, t), 1)
        diag = c < r

        def tile(js, carry, before):
            c0, c1, acc = carry
            cs = [c0, c1]
            kj = [k_ref[pl.ds(pl.multiple_of(j * t, t), t), :] for j in js]
            vj = [v_ref[pl.ds(pl.multiple_of(j * t, t), t), :] for j in js]
            chains = [(h, b) for b in range(len(js)) for h in range(2)]
            z = {(h, b): _dot(qm[h], kj[b], "nt") for h, b in chains}
            lb, sfx, c_at = {}, {}, {}
            for ch in chains:
                h, b = ch
                lb[ch], lm = _sb_logs(z[ch], before if b == 0 else None)
                sfx[ch] = _dot(_split_k(lm), tri_x)
                c_at[ch] = cs[h]
                cs[h] = cs[h] + jnp.sum(lm, axis=-1, keepdims=True)
            for ch in chains:
                h, b = ch
                ab = _sb_weights(lb[ch], sfx[ch], c_at[ch], before if b == 0 else None).astype(BF16)
                a_ref[h, js[b]] = ab
                sg_ref[h, js[b]] = jnp.exp(lb[ch]).astype(BF16)
                acc = acc + _dot(ab, vj[b] * masks[h])
            return cs[0], cs[1], acc

        zero = jnp.zeros((t, 1), F32)
        carry = _sb_sweep(tile, i, (zero, zero, jnp.zeros((t, LANES), F32)), diag)
        ob_ref[...] = carry[2].astype(BF16)
        of_ref[...] = carry[2]

    q_spec, k_spec, v_spec, saved = _sb_specs(s, t)
    width = N_HEADS * HEAD_DIM
    keep = jax.ShapeDtypeStruct((N_HEADS // 2, 2, nb, nb, t, t), BF16)
    return pl.pallas_call(
        body, name=name, grid=(N_HEADS // 2, nb),
        in_specs=[q_spec, k_spec, v_spec], out_specs=[q_spec, q_spec, saved, saved],
        out_shape=[jax.ShapeDtypeStruct((s, width), BF16), jax.ShapeDtypeStruct((s, width), F32), keep, keep],
        compiler_params=_params("parallel", "parallel"),
    )(qkv, qkv, qkv)


def sb_bwd(name, qkv, o_f32, do, a_all, sg_all, t=SB_TILE):
    s = qkv.shape[0]
    t = min(t, s // 2)

    def body(q_ref, k_ref, v_ref, o_ref, do_ref, a_ref, sg_ref, dq_ref, dk_ref, dv_ref):
        i = pl.program_id(1)
        masks = _half_masks(BF16)
        fmasks = _half_masks(F32)
        q = q_ref[...] * ATTN_SCALE
        do_v = do_ref[...]
        qm = [q * masks[0], q * masks[1]]
        dom = [do_v * masks[0], do_v * masks[1]]
        prod = do_v.astype(F32) * o_ref[...]
        delta = [jnp.sum(prod * fmasks[h], axis=-1, keepdims=True) for h in range(2)]
        tri_i = _tri2(t, True)
        r = lax.broadcasted_iota(jnp.int32, (t, t), 0)
        c = lax.broadcasted_iota(jnp.int32, (t, t), 1)
        diag = c < r

        @pl.when(i == 0)
        def _():
            dk_ref[...] = jnp.zeros_like(dk_ref)
            dv_ref[...] = jnp.zeros_like(dv_ref)

        def tile(js, carry, before):
            cd0, cd1, dq = carry
            cd = [cd0, cd1]
            starts = [pl.multiple_of(j * t, t) for j in js]
            kj = [k_ref[pl.ds(st, t), :] for st in starts]
            vj = [v_ref[pl.ds(st, t), :] for st in starts]
            chains = [(h, b) for b in range(len(js)) for h in range(2)]
            da = {(h, b): _dot(dom[h], vj[b], "nt") for h, b in chains}
            ab, de, dsfx, cd_at = {}, {}, {}, {}
            dv = [jnp.zeros((t, LANES), F32) for _ in js]
            dk = [jnp.zeros((t, LANES), F32) for _ in js]
            for ch in chains:
                h, b = ch
                ab[ch] = a_ref[h, js[b]]
                de[ch] = da[ch] * ab[ch].astype(F32)
                dsfx[ch] = _dot(_split_k(de[ch]), tri_i)
                dv[b] = dv[b] + _dot(ab[ch], dom[h], "tn")
                cd_at[ch] = cd[h]
                cd[h] = cd[h] + jnp.sum(de[ch], axis=-1, keepdims=True)
            for ch in chains:
                h, b = ch
                farther = delta[h] - cd_at[ch] - dsfx[ch]
                dz = de[ch] - sg_ref[h, js[b]].astype(F32) * (de[ch] + farther)
                if before is not None and b == 0:
                    dz = jnp.where(before, dz, 0.0)
                dzb = dz.astype(BF16)
                dq = dq + _dot(dzb, kj[b] * masks[h])
                dk[b] = dk[b] + _dot(dzb, qm[h], "tn")
            for b, st in enumerate(starts):
                dk_ref[pl.ds(st, t), :] += dk[b]
                dv_ref[pl.ds(st, t), :] += dv[b]
            return cd[0], cd[1], dq

        zero = jnp.zeros((t, 1), F32)
        carry = _sb_sweep(tile, i, (zero, zero, jnp.zeros((t, LANES), F32)), diag)
        dq_ref[...] = (carry[2] * ATTN_SCALE).astype(BF16)

    q_spec, k_spec, v_spec, saved = _sb_specs(s, t)
    kv_out = pl.BlockSpec((s, LANES), lambda h, i: (0, h))
    width = N_HEADS * HEAD_DIM
    return pl.pallas_call(
        body, name=name, grid=(N_HEADS // 2, s // t),
        in_specs=[q_spec, k_spec, v_spec, q_spec, q_spec, saved, saved], out_specs=[q_spec, kv_out, kv_out],
        out_shape=[jax.ShapeDtypeStruct((s, width), BF16), jax.ShapeDtypeStruct((s, width), F32),
                   jax.ShapeDtypeStruct((s, width), F32)],
        compiler_params=_params("parallel", "arbitrary"),
    )(qkv, qkv, qkv, o_f32, do, a_all, sg_all)


def adamw(name, w, g, m, v, tm=256):
    shape = w.shape
    c = shape[-1]
    rows = math.prod(shape[:-1])
    tm = min(tm, rows)

    def body(w_ref, g_ref, m_ref, v_ref, d_ref, mo_ref, vo_ref):
        gv = g_ref[...]
        m2 = ADAM_B1 * m_ref[...] + (1.0 - ADAM_B1) * gv
        v2 = ADAM_B2 * v_ref[...] + (1.0 - ADAM_B2) * (gv * gv)
        m_hat = m2 / (1.0 - ADAM_B1 ** ADAM_STEP)
        v_hat = v2 / (1.0 - ADAM_B2 ** ADAM_STEP)
        d_ref[...] = -ADAM_LR * (m_hat / (jnp.sqrt(v_hat) + ADAM_EPS) + ADAM_WD * w_ref[...])
        mo_ref[...] = m2
        vo_ref[...] = v2

    blk = pl.BlockSpec((tm, c), lambda i: (i, 0))
    outs = pl.pallas_call(
        body, name=name, grid=(rows // tm,), in_specs=[blk] * 4, out_specs=[blk] * 3,
        out_shape=[jax.ShapeDtypeStruct((rows, c), F32)] * 3, compiler_params=_params("parallel"),
    )(*[t.reshape(rows, c) for t in (w, g, m, v)])
    return [o.reshape(shape) for o in outs]


HBM = pl.BlockSpec(memory_space=pltpu.HBM)


def _place():
    x, y, c = lax.axis_index("x"), lax.axis_index("y"), lax.axis_index("c")
    return x, y, c, [(1 - x, y), (x, 1 - y), (1 - x, 1 - y)]


def _remote(src, dst, send, recv, dev):
    return pltpu.make_async_remote_copy(src_ref=src, dst_ref=dst, send_sem=send, recv_sem=recv, device_id=dev,
                                        device_id_type=MESH)


SEM = pl.BlockSpec(memory_space=pltpu.SEMAPHORE)
ANY = pl.BlockSpec(memory_space=pl.ANY)
DATAFLOW = pltpu.SideEffectType.DATAFLOW_SIDE_EFFECTING


def _in_hbm(v):
    return pltpu.with_memory_space_constraint(v, pltpu.HBM)


def split_start(name, bufs, n_copies, sends, after=()):
    nb, na = len(bufs), len(after)

    def body(*refs):
        send, recv = refs[nb + na], refs[nb + na + 1]
        for cp in sends(refs[:nb], send, recv):
            cp.start()
        refs[-1][...] = jnp.zeros_like(refs[-1])

    outs = pl.pallas_call(
        body, name=name,
        in_specs=[HBM] * nb + [ANY] * na,
        out_shape=(pltpu.SemaphoreType.DMA((n_copies,)), pltpu.SemaphoreType.DMA((n_copies,)),
                   *[pltpu.HBM(b.shape, b.dtype) for b in bufs], jax.ShapeDtypeStruct((8, LANES), F32)),
        out_specs=(SEM, SEM, *[HBM] * nb, pl.BlockSpec(memory_space=pltpu.VMEM)),
        input_output_aliases={i: 2 + i for i in range(nb)},
        compiler_params=pltpu.CompilerParams(has_side_effects=DATAFLOW),
    )(*[_in_hbm(b) for b in bufs], *after)
    return outs[0], outs[1], list(outs[2:2 + nb]), outs[-1]


def split_wait(name, send_sems, recv_sems, bufs, sends, arrivals, after):
    nb, na = len(bufs), len(after)

    def body(*refs):
        send, recv = refs[nb], refs[nb + 1]
        for cp in sends(refs[:nb], send, recv):
            cp.wait_send()
        for cp in arrivals(refs[:nb], send, recv):
            cp.wait_recv()

    outs = pl.pallas_call(
        body, name=name,
        in_specs=[HBM] * nb + [SEM, SEM] + [ANY] * na,
        out_shape=tuple(pltpu.HBM(b.shape, b.dtype) for b in bufs), out_specs=tuple([HBM] * nb),
        input_output_aliases={i: i for i in range(nb)},
        compiler_params=pltpu.CompilerParams(has_side_effects=DATAFLOW),
    )(*bufs, send_sems, recv_sems, *after)
    return list(outs)


def _gather_plan(n):
    def sends(refs, send, recv):
        x, y, c, chips = _place()
        me = 2 * x + y
        return [_remote(refs[t].at[me, c], refs[t].at[me, c], send.at[3 * t + k], recv.at[3 * t + k], (px, py, c))
                for t in range(n) for k, (px, py) in enumerate(chips)]

    def arrivals(refs, send, recv):
        x, y, c, chips = _place()
        return [_remote(refs[t].at[2 * px + py, c], refs[t].at[2 * px + py, c], send.at[3 * t + k], recv.at[3 * t + k],
                        (px, py, c)) for t in range(n) for k, (px, py) in enumerate(chips)]

    return 3 * n, sends, arrivals


def _relay_plan(n):
    def sends(refs, send, recv):
        x, y, c, chips = _place()
        return [_remote(refs[t].at[2 * px + py, c], refs[t].at[2 * px + py, c], send.at[3 * t + k], recv.at[3 * t + k],
                        (x, y, 1 - c)) for t in range(n) for k, (px, py) in enumerate(chips)]

    def arrivals(refs, send, recv):
        x, y, c, chips = _place()
        return [_remote(refs[t].at[2 * px + py, 1 - c], refs[t].at[2 * px + py, 1 - c], send.at[3 * t + k],
                        recv.at[3 * t + k], (x, y, 1 - c)) for t in range(n) for k, (px, py) in enumerate(chips)]

    return 3 * n, sends, arrivals


def _swap_plan(n):
    def copies(refs, send, recv):
        x, y, c, _ = _place()
        return [_remote(refs[t].at[s, 1 - c], refs[n + t].at[s], send.at[N_CHIPS * t + s], recv.at[N_CHIPS * t + s],
                        (x, y, 1 - c)) for t in range(n) for s in range(N_CHIPS)]

    return N_CHIPS * n, copies, copies


def _scatter_plan(n):
    def sends(refs, send, recv):
        x, y, c, chips = _place()
        me = 2 * x + y
        return [_remote(refs[t].at[2 * px + py], refs[n + t].at[me], send.at[3 * t + k], recv.at[3 * t + k],
                        (px, py, c)) for t in range(n) for k, (px, py) in enumerate(chips)]

    def arrivals(refs, send, recv):
        x, y, c, chips = _place()
        return [_remote(refs[t].at[2 * px + py], refs[n + t].at[2 * px + py], send.at[3 * t + k], recv.at[3 * t + k],
                        (px, py, c)) for t in range(n) for k, (px, py) in enumerate(chips)]

    return 3 * n, sends, arrivals


def cast_into_slot(name, w, layer, chip, deps=(), tm=1024):
    _, r, c = w.shape
    tm = min(tm, r)

    def body(chip_ref, w_ref, *rest):
        rest[-1][...] = w_ref[...].astype(BF16)

    return pl.pallas_call(
        body, name=name,
        grid_spec=pltpu.PrefetchScalarGridSpec(
            num_scalar_prefetch=1, grid=(r // tm,),
            in_specs=[pl.BlockSpec((None, tm, c), lambda i, chip_ref: (layer, i, 0)), *[ANY] * len(deps)],
            out_specs=pl.BlockSpec((None, tm, c), lambda i, chip_ref: (chip_ref[0], i, 0))),
        out_shape=jax.ShapeDtypeStruct((N_CHIPS, r, c), BF16), compiler_params=_params("parallel"),
    )(chip, w, *deps)


def join_halves(name, f):
    n = f.shape[0]

    def body(f_ref, o_ref, send, recv):
        x, y, c, _ = _place()
        sib = (x, y, 1 - c)
        sends = [_remote(f_ref.at[l, c], o_ref.at[l, c], send.at[l], recv.at[l], sib) for l in range(n)]
        for cp in sends:
            cp.start()
        for l in range(n):
            _remote(f_ref.at[l, 1 - c], o_ref.at[l, 1 - c], send.at[l], recv.at[l], sib).wait_recv()
        for cp in sends:
            cp.wait_send()

    return pl.pallas_call(
        body, name=name, in_specs=[HBM], out_specs=HBM, out_shape=jax.ShapeDtypeStruct(f.shape, f.dtype),
        input_output_aliases={0: 0},
        scratch_shapes=[pltpu.SemaphoreType.DMA((n,)), pltpu.SemaphoreType.DMA((n,))],
    )(f)


def add_sibling_half(name, g, a, core, tm=1024):
    _, _, rh, c = g.shape
    tm = min(tm, rh)

    def body(core_ref, g_ref, a_ref, o_ref):
        o_ref[...] = (g_ref[...].astype(F32) + a_ref[...].astype(F32)).astype(BF16)

    return pl.pallas_call(
        body, name=name,
        grid_spec=pltpu.PrefetchScalarGridSpec(
            num_scalar_prefetch=1, grid=(N_CHIPS, rh // tm),
            in_specs=[pl.BlockSpec((None, None, tm, c), lambda s, i, core_ref: (s, core_ref[0], i, 0)),
                      pl.BlockSpec((None, tm, c), lambda s, i, core_ref: (s, i, 0))],
            out_specs=pl.BlockSpec((None, tm, c), lambda s, i, core_ref: (s, i, 0))),
        out_shape=jax.ShapeDtypeStruct(a.shape, BF16), compiler_params=_params("parallel", "parallel"),
    )(core, g, a)


def sum_chips_into(name, p, b, f, layer, chip, core, tm=512):
    _, rh, c = b.shape
    tm = min(tm, rh)

    def body(chip_ref, core_ref, p_ref, b_ref, f_ref, o_ref):
        acc = jnp.zeros((tm, c), F32)
        for s in range(N_CHIPS):
            acc = acc + jnp.where(chip_ref[0] == s, p_ref[s].astype(F32), b_ref[s].astype(F32))
        o_ref[...] = acc

    slots = pl.BlockSpec((N_CHIPS, tm, c), lambda i, chip_ref, core_ref: (0, i, 0))
    return pl.pallas_call(
        body, name=name,
        grid_spec=pltpu.PrefetchScalarGridSpec(
            num_scalar_prefetch=2, grid=(rh // tm,), in_specs=[slots, slots, ANY],
            out_specs=pl.BlockSpec((None, None, tm, c), lambda i, chip_ref, core_ref: (layer, core_ref[0], i, 0))),
        out_shape=jax.ShapeDtypeStruct(f.shape, F32), input_output_aliases={4: 0},
        compiler_params=_params("parallel"),
    )(chip, core, p, b, f)


N_DEV = 8


def allreduce_small(name, parts):
    p, _, d = parts.shape
    m_per = p * 8

    def body(x_ref, out_ref, all_ref, send_sems, recv_sems, local_sem):
        x, y, c, chips = _place()
        me, sibling = (x, y, c), (x, y, 1 - c)

        def rows(px, py, pc):
            return all_ref.at[pl.ds((4 * px + 2 * py + pc) * m_per, m_per), :]

        def copy(k, block, to, src=None):
            return _remote(rows(*block) if src is None else src, rows(*block), send_sems.at[k], recv_sems.at[k], to)

        mine = pltpu.make_async_copy(x_ref, rows(*me), local_sem)
        mine.start()
        first = [copy(0, me, sibling, src=x_ref)]
        first += [copy(1 + j, me, (*chip, c), src=x_ref) for j, chip in enumerate(chips)]
        for cp in first:
            cp.start()
        passed = [copy(4 + j, (*chip, c), sibling) for j, chip in enumerate(chips)]
        for j, chip in enumerate(chips):
            copy(1 + j, (*chip, c), me).wait_recv()
            passed[j].start()
        copy(0, sibling, me).wait_recv()
        for j, chip in enumerate(chips):
            copy(4 + j, (*chip, 1 - c), me).wait_recv()
        for cp in first + passed:
            cp.wait_send()
        mine.wait()
        acc = all_ref[pl.ds(0, m_per), :]
        for dev in range(1, N_DEV):
            acc = acc + all_ref[pl.ds(dev * m_per, m_per), :]
        out_ref[...] = jnp.sum(acc.reshape(p, 8, d), axis=1)

    vmem = pl.BlockSpec(memory_space=pltpu.VMEM)
    return pl.pallas_call(
        body, name=name, in_specs=[vmem], out_specs=vmem,
        out_shape=jax.ShapeDtypeStruct((p, d), F32),
        scratch_shapes=[pltpu.VMEM((N_DEV * m_per, d), F32), pltpu.SemaphoreType.DMA((7,)),
                        pltpu.SemaphoreType.DMA((7,)), pltpu.SemaphoreType.DMA],
        compiler_params=pltpu.CompilerParams(vmem_limit_bytes=VMEM_LIMIT),
    )(parts.reshape(m_per, d))


def _empty(shape, dtype):
    return _in_hbm(lax.empty(shape, dtype))


class _GradExchange:
    def __init__(self, layer, kinds, grads, chip, core):
        self.layer, self.kinds, self.chip, self.core = layer, kinds, chip, core
        self.g4 = [g.reshape(N_CHIPS, 2, g.shape[1] // 2, g.shape[2]) for g in grads]
        self.n = len(grads)

    def start_swap(self, after):
        n_copies, self.swap_sends, self.swap_arrivals = _swap_plan(self.n)
        lands = [_empty((N_CHIPS,) + g.shape[2:], BF16) for g in self.g4]
        self.swap = split_start(f"swap_start_l{self.layer}", self.g4 + lands, n_copies, self.swap_sends, after)
        return self.swap[3]

    def swap_to_scatter(self, after):
        send, recv, bufs, _ = self.swap
        bufs = split_wait(f"swap_wait_l{self.layer}", send, recv, bufs, self.swap_sends, self.swap_arrivals, after)
        g4, lands = bufs[:self.n], bufs[self.n:]
        self.p = [add_sibling_half(f"add_l{self.layer}_{k}", g4[t], lands[t], self.core)
                  for t, (k, _) in enumerate(self.kinds)]
        n_copies, self.sc_sends, self.sc_arrivals = _scatter_plan(self.n)
        lands = [_empty(p.shape, BF16) for p in self.p]
        self.scatter = split_start(f"scatter_start_l{self.layer}", self.p + lands, n_copies, self.sc_sends)
        return self.scatter[3]

    def finish(self, f, after):
        send, recv, bufs, _ = self.scatter
        bufs = split_wait(f"scatter_wait_l{self.layer}", send, recv, bufs, self.sc_sends, self.sc_arrivals, after)
        p, lands = bufs[:self.n], bufs[self.n:]
        for t, (kind, l) in enumerate(self.kinds):
            f[kind] = sum_chips_into(f"sum_l{self.layer}_{kind}", p[t], lands[t], f[kind], l, self.chip, self.core)


def _relu2(acc):
    r = jnp.maximum(acc, 0.0)
    return acc, r * r


def _relu2_bwd(acc, u):
    return (acc * (2.0 * jnp.maximum(u.astype(F32), 0.0)),)


def _same(acc):
    return (acc,)


def kernel(x, a_w_qkv, a_w_o, a_sinks, b_w_qkv, b_w_o, norm_mix, norm_mlp, mlp_w_in, mlp_w_out, final_norm, loss_target, m_a_w_qkv, m_a_w_o, m_a_sinks, m_b_w_qkv, m_b_w_o, m_norm_mix, m_norm_mlp, m_mlp_w_in, m_mlp_w_out, m_final_norm, v_a_w_qkv, v_a_w_o, v_a_sinks, v_b_w_qkv, v_b_w_o, v_norm_mix, v_norm_mlp, v_mlp_w_in, v_mlp_w_out, v_final_norm):
    _, s, d = x.shape
    depth = norm_mix.shape[0]
    width = N_HEADS * HEAD_DIM
    core = lax.axis_index("c").astype(jnp.int32).reshape(1)
    chip = (2 * lax.axis_index("x") + lax.axis_index("y")).astype(jnp.int32).reshape(1)
    slopes = jnp.power(2.0, -8.0 * (jnp.arange(N_HEADS, dtype=F32) + 1.0) / N_HEADS)
    qkv_of = {0: ("a_qkv", a_w_qkv), 1: ("b_qkv", b_w_qkv)}
    o_of = {0: ("a_o", a_w_o), 1: ("b_o", b_w_o)}

    def layer_kinds(i):
        return [(qkv_of[i % 2][0], i // 2), (o_of[i % 2][0], i // 2), ("mlp_in", i), ("mlp_out", i)]

    stacks = {"a_qkv": a_w_qkv, "a_o": a_w_o, "b_qkv": b_w_qkv, "b_o": b_w_o, "mlp_in": mlp_w_in, "mlp_out": mlp_w_out}

    def slots(i, deps=()):
        return [cast_into_slot(f"cast_l{i}_{k}", stacks[k], l, chip, deps) for k, l in layer_kinds(i)]

    n_gather, gather_sends, gather_arrivals = _gather_plan(2)
    _, relay_sends, relay_arrivals = _relay_plan(2)

    class Flight:
        def __init__(self, tag, bufs, after):
            self.tag = tag
            halves = [b.reshape(N_CHIPS, 2, b.shape[1] // 2, b.shape[2]) for b in bufs]
            self.shapes = [b.shape for b in bufs]
            self.state = split_start(f"gather_start_{tag}", halves, n_gather, gather_sends, after)
            self.token = self.state[3]

        def relay(self, after):
            send, recv, bufs, _ = self.state
            bufs = split_wait(f"gather_wait_{self.tag}", send, recv, bufs, gather_sends, gather_arrivals, after)
            self.state = split_start(f"relay_start_{self.tag}", bufs, n_gather, relay_sends)
            return self.state[3]

        def land(self, after):
            send, recv, bufs, _ = self.state
            bufs = split_wait(f"relay_wait_{self.tag}", send, recv, bufs, relay_sends, relay_arrivals, after)
            return [b.reshape(shape) for b, shape in zip(bufs, self.shapes)]

    saved, weights = [], []
    xc = x[0]
    first = slots(0)
    flight_a = Flight("l0a", first[:2], ())
    flight_m = Flight("l0m", first[2:], (flight_a.token,))
    flight_a.relay((xc, flight_m.token))
    for i in range(depth):
        mixer, j = i % 2, i // 2
        w_qkv, w_o = flight_a.land((xc,))
        w_o = w_o.reshape(width, d)
        started, this_m = (flight_m.token,), flight_m
        if i + 1 < depth:
            following = slots(i + 1, (flight_m.token,))
            flight_a = Flight(f"l{i + 1}a", following[:2], (w_qkv,))
            flight_m = Flight(f"l{i + 1}m", following[2:], (flight_a.token,))
            started = (flight_m.token,)
        h = rms_fwd(f"l{i}_norm_mix", xc, norm_mix[i], deps=started)
        qkv = mm_cols(f"l{i}_qkv", h, w_qkv, _same, (BF16,))[0]
        if mixer == 0:
            attn, kept = swa_fwd(f"l{i}_swa", qkv, a_sinks[j], slopes), ()
        else:
            attn, *kept = sb_fwd(f"l{i}_sb", qkv)
        xm = mm_res(f"l{i}_o", attn, w_o, xc, deps=(this_m.relay((attn,)),))
        w_in, w_out = this_m.land((xm,))
        w_out = w_out.reshape(-1, d)
        weights.append((w_qkv, w_o, w_in, w_out))
        h2 = rms_fwd(f"l{i}_norm_mlp", xm, norm_mlp[i])
        u, hh = mm_cols(f"l{i}_in", h2, w_in, _relu2, (BF16, BF16))
        relayed = (flight_a.relay((u,)),) if i + 1 < depth else ()
        xn = mm_res(f"l{i}_out", hh, w_out, xm, deps=relayed)
        saved.append((xc, h, qkv, attn, kept, xm, h2, u, hh))
        xc = xn
    loss_rows, dx, dxb, dg_final = loss_head("loss_head", xc, final_norm, loss_target[0])
    loss = lax.psum(jnp.sum(loss_rows), ("x", "y", "c"))

    big = {"a_qkv": (a_w_qkv, m_a_w_qkv, v_a_w_qkv), "a_o": (a_w_o, m_a_w_o, v_a_w_o),
           "b_qkv": (b_w_qkv, m_b_w_qkv, v_b_w_qkv), "b_o": (b_w_o, m_b_w_o, v_b_w_o),
           "mlp_in": (mlp_w_in, m_mlp_w_in, v_mlp_w_in), "mlp_out": (mlp_w_out, m_mlp_w_out, v_mlp_w_out)}
    f = {k: _empty((w.shape[0], 2, w.shape[1] // 2, w.shape[2]), F32) for k, (w, _, _) in big.items()}
    dg_mix, dg_mlp, dsinks = [], [], []
    prev_a, started = None, ()
    for i in reversed(range(depth)):
        mixer, j = i % 2, i // 2
        xin, h, qkv, attn, kept, xm, h2, u, hh = saved[i]
        w_qkv, w_o, w_in, w_out = weights[i]
        kinds = layer_kinds(i)
        du = mm_nt(f"l{i}_d_hidden", dxb, w_out, _relu2_bwd, (u,), deps=started)
        g_out = mm_tn(f"l{i}_g_out", hh, dxb).reshape(N_CHIPS, -1, d)
        started = (g_out,) if prev_a is None else (g_out, prev_a.swap_to_scatter((g_out,)))
        dh2 = mm_nt_cols(f"l{i}_d_h2", du, w_in, deps=started)
        g_in = mm_tn_cols(f"l{i}_g_in", h2, du)
        cur_m = _GradExchange(f"{i}m", kinds[2:], [g_in, g_out], chip, core)
        started = (g_in, cur_m.start_swap((g_in,)))
        dxm, dxmb, dg = rms_bwd(f"l{i}_norm_mlp_bwd", xm, norm_mlp[i], dh2, dx, deps=started)
        dg_mlp.append(dg)
        g_o = mm_tn(f"l{i}_g_o", attn, dxmb).reshape(N_CHIPS, -1, d)
        dattn = mm_nt(f"l{i}_d_attn", dxmb, w_o, _same, deps=(g_o, cur_m.swap_to_scatter((g_o,))))
        if mixer == 0:
            dq, dk, dv, dsk = swa_bwd(f"l{i}_swa_bwd", qkv, dattn, a_sinks[j], slopes)
            dsinks.append(dsk[:, :2, 0].reshape(N_HEADS))
        else:
            attn_f32, weights_a, sigmoids = kept
            dq, dk, dv = sb_bwd(f"l{i}_sb_bwd", qkv, attn_f32, dattn, weights_a, sigmoids)
        dqkv = jnp.concatenate([dq, dk.astype(BF16), dv.astype(BF16)], axis=1)
        dh = mm_nt_cols(f"l{i}_d_h", dqkv, w_qkv)
        g_qkv = mm_tn_cols(f"l{i}_g_qkv", h, dqkv)
        dx, dxb, dg = rms_bwd(f"l{i}_norm_mix_bwd", xin, norm_mix[i], dh, dxm, deps=(g_qkv,))
        dg_mix.append(dg)
        if prev_a is not None:
            prev_a.finish(f, (dx,))
        cur_m.finish(f, (dx,))
        prev_a = _GradExchange(f"{i}a", kinds[:2], [g_qkv, g_o], chip, core)
        started = (prev_a.start_swap((dx,)),)
    prev_a.swap_to_scatter((dx,))
    prev_a.finish(f, (dx,))
    for lst in (dg_mix, dg_mlp, dsinks):
        lst.reverse()

    res = {}
    for kind, (w, m, v) in big.items():
        g = join_halves(f"join_{kind}", f[kind]).reshape(w.shape)
        res[kind] = (g, *adamw(f"adamw_{kind}", w, g, m, v))

    n_sink = a_sinks.size
    sink_rows = jnp.zeros((1, 8, d), F32).at[0, 0, :n_sink].set(jnp.concatenate(dsinks))
    parts = jnp.concatenate([jnp.stack(dg_mix), jnp.stack(dg_mlp), dg_final[None], sink_rows], axis=0)
    g_small = allreduce_small("allreduce_small", parts)

    def pack(mix, mlp, fin, snk):
        snk_row = jnp.zeros((1, d), F32).at[0, :n_sink].set(snk.reshape(-1))
        return jnp.concatenate([mix, mlp, fin[None], snk_row], axis=0)

    def unpack(t):
        return t[:depth], t[depth:2 * depth], t[2 * depth], t[2 * depth + 1, :n_sink].reshape(a_sinks.shape)

    small = adamw("adamw_small", pack(norm_mix, norm_mlp, final_norm, a_sinks), g_small,
                  pack(m_norm_mix, m_norm_mlp, m_final_norm, m_a_sinks),
                  pack(v_norm_mix, v_norm_mlp, v_final_norm, v_a_sinks))
    outs = []
    for idx in range(4):
        mix, mlp, fin, snk = unpack(g_small if idx == 0 else small[idx - 1])
        outs += [res["a_qkv"][idx], res["a_o"][idx], snk, res["b_qkv"][idx], res["b_o"][idx], mix, mlp,
                 res["mlp_in"][idx], res["mlp_out"][idx], fin]
    return (loss, dx.reshape(x.shape), *outs)
```

```python
import functools
import math

import jax
import jax.numpy as jnp
from jax import lax
from jax.experimental import pallas as pl
from jax.experimental.pallas import tpu as pltpu

F32 = jnp.float32
BF16 = jnp.bfloat16
MESH = pl.DeviceIdType.MESH

N_CHIPS = 4
HEAD_DIM = 64
LANES = 128
N_HEADS = 32
N_KV_A = 4
WINDOW = 128
RMS_EPS = 1e-5
ATTN_SCALE = 1.0 / math.sqrt(HEAD_DIM)
ADAM_LR, ADAM_B1, ADAM_B2, ADAM_EPS, ADAM_WD, ADAM_STEP = 0.001, 0.9, 0.999, 1e-08, 0.01, 10
NEG_BIG = -1e30
VMEM_LIMIT = 56 * 1024 * 1024

_DN = {"nn": (((1,), (0,)), ((), ())), "nt": (((1,), (1,)), ((), ())), "tn": (((0,), (0,)), ((), ()))}


def _dot(a, b, mode="nn"):
    return lax.dot_general(a, b, _DN[mode], preferred_element_type=F32)


def _params(*sem):
    return pltpu.CompilerParams(dimension_semantics=sem, vmem_limit_bytes=VMEM_LIMIT)


def _pick(n, prefs):
    for t in prefs:
        if n % t == 0:
            return t
    return n


def _mm(name, mode, a, b, *, grid, a_spec, b_spec, extras=(), extra_specs=(), out_shapes, out_specs, nk,
        acc_shape, epilogue, deps=()):
    n_ex, n_out = len(extras), len(out_shapes)
    first_out = 2 + n_ex + len(deps)

    def body(*refs):
        a_ref, b_ref = refs[0], refs[1]
        ex = refs[2:2 + n_ex]
        outs = refs[first_out:first_out + n_out]
        part = _dot(a_ref[...], b_ref[...], mode)

        def finish(acc):
            res = epilogue(acc, *[e[...] for e in ex])
            for o, r in zip(outs, res):
                o[...] = r.astype(o.dtype)

        if nk == 1:
            finish(part)
        else:
            acc_ref = refs[-1]
            k = pl.program_id(len(grid) - 1)

            @pl.when(k == 0)
            def _():
                acc_ref[...] = part

            @pl.when(k > 0)
            def _():
                acc_ref[...] += part

            @pl.when(k == nk - 1)
            def _():
                finish(acc_ref[...])

    sem = ("parallel",) * (len(grid) - 1) + ("arbitrary" if nk > 1 else "parallel",)
    return pl.pallas_call(
        body, name=name, grid=grid,
        in_specs=[a_spec, b_spec, *extra_specs, *[ANY] * len(deps)],
        out_specs=list(out_specs), out_shape=list(out_shapes),
        scratch_shapes=[] if nk == 1 else [pltpu.VMEM(acc_shape, F32)],
        compiler_params=_params(*sem),
    )(a, b, *extras, *deps)


def mm_cols(name, a, wg, epilogue, out_dtypes, tm=2048):
    m, k = a.shape
    c = wg.shape[2]
    tm = min(tm, m)
    tn = _pick(c, (512, 640, 256, 128))
    nj = c // tn
    o_spec = pl.BlockSpec((tm, tn), lambda i, j: (i, j))
    return _mm(name, "nn", a, wg, grid=(m // tm, N_CHIPS * nj),
               a_spec=pl.BlockSpec((tm, k), lambda i, j: (i, 0)),
               b_spec=pl.BlockSpec((None, k, tn), lambda i, j: (j // nj, 0, j % nj)),
               out_shapes=[jax.ShapeDtypeStruct((m, N_CHIPS * c), d) for d in out_dtypes],
               out_specs=[o_spec] * len(out_dtypes), nk=1, acc_shape=None, epilogue=epilogue)


def mm_res(name, a, w, res, deps=(), tm=1024, tn=1024, tk=2048):
    m, k = a.shape
    n = w.shape[1]
    tm, tn, tk = min(tm, m), min(tn, n), min(tk, k)
    nk = k // tk
    return _mm(name, "nn", a, w, grid=(m // tm, n // tn, nk),
               a_spec=pl.BlockSpec((tm, tk), lambda i, j, kk: (i, kk)),
               b_spec=pl.BlockSpec((tk, tn), lambda i, j, kk: (kk, j)),
               extras=(res,), extra_specs=(pl.BlockSpec((tm, tn), lambda i, j, kk: (i, j)),),
               out_shapes=[jax.ShapeDtypeStruct((m, n), F32)],
               out_specs=[pl.BlockSpec((tm, tn), lambda i, j, kk: (i, j))], nk=nk, acc_shape=(tm, tn),
               epilogue=lambda acc, r: (acc + r,), deps=deps)[0]


def mm_nt(name, a, w, epilogue, extras=(), deps=(), tm=2048, tn=512):
    m, k = a.shape
    n = w.shape[0]
    tm, tn = min(tm, m), min(tn, n)
    o_spec = pl.BlockSpec((tm, tn), lambda i, j: (i, j))
    return _mm(name, "nt", a, w, grid=(m // tm, n // tn),
               a_spec=pl.BlockSpec((tm, k), lambda i, j: (i, 0)),
               b_spec=pl.BlockSpec((tn, k), lambda i, j: (j, 0)),
               extras=tuple(extras), extra_specs=(o_spec,) * len(extras),
               out_shapes=[jax.ShapeDtypeStruct((m, n), BF16)], out_specs=[o_spec], nk=1, acc_shape=None,
               epilogue=epilogue, deps=deps)[0]


def mm_nt_cols(name, dy, wg, deps=(), tm=1024, tn=1024):
    m = dy.shape[0]
    _, d, c = wg.shape
    tm, tn = min(tm, m), min(tn, d)
    tk = _pick(c, (2048, 1536, 640, 512, 128))
    nkk = c // tk
    nk = N_CHIPS * nkk
    return _mm(name, "nt", dy, wg, grid=(m // tm, d // tn, nk),
               a_spec=pl.BlockSpec((tm, tk), lambda i, j, kk: (i, kk)),
               b_spec=pl.BlockSpec((None, tn, tk), lambda i, j, kk: (kk // nkk, j, kk % nkk)),
               out_shapes=[jax.ShapeDtypeStruct((m, d), F32)],
               out_specs=[pl.BlockSpec((tm, tn), lambda i, j, kk: (i, j))], nk=nk, acc_shape=(tm, tn),
               epilogue=lambda acc: (acc,), deps=deps)[0]


def mm_tn(name, a, b, tm=1024, tn=1024, tk=2048):
    m, p = a.shape
    q = b.shape[1]
    tm, tn, tk = min(tm, p), min(tn, q), min(tk, m)
    nk = m // tk
    return _mm(name, "tn", a, b, grid=(p // tm, q // tn, nk),
               a_spec=pl.BlockSpec((tk, tm), lambda i, j, kk: (kk, i)),
               b_spec=pl.BlockSpec((tk, tn), lambda i, j, kk: (kk, j)),
               out_shapes=[jax.ShapeDtypeStruct((p, q), BF16)],
               out_specs=[pl.BlockSpec((tm, tn), lambda i, j, kk: (i, j))], nk=nk, acc_shape=(tm, tn),
               epilogue=lambda acc: (acc,))[0]


def mm_tn_cols(name, a, dy, tm=1024, tk=2048):
    m, d = a.shape
    c = dy.shape[1] // N_CHIPS
    tm, tk = min(tm, d), min(tk, m)
    tn = _pick(c, (1024, 768, 640, 512, 128))
    nj = c // tn
    nk = m // tk
    return _mm(name, "tn", a, dy, grid=(d // tm, N_CHIPS * nj, nk),
               a_spec=pl.BlockSpec((tk, tm), lambda i, j, kk: (kk, i)),
               b_spec=pl.BlockSpec((tk, tn), lambda i, j, kk: (kk, j)),
               out_shapes=[jax.ShapeDtypeStruct((N_CHIPS, d, c), BF16)],
               out_specs=[pl.BlockSpec((None, tm, tn), lambda i, j, kk: (j // nj, i, j % nj))], nk=nk,
               acc_shape=(tm, tn), epilogue=lambda acc: (acc,))[0]


def _rows_to_8(v):
    tm, d = v.shape
    return jnp.sum(v.reshape(tm // 8, 8, d), axis=0)


def rms_fwd(name, x, gain, deps=(), tm=512):
    s, d = x.shape
    tm = min(tm, s)

    def body(x_ref, g_ref, *rest):
        h_ref = rest[-1]
        xv = x_ref[...]
        r = lax.rsqrt(jnp.mean(xv * xv, axis=-1, keepdims=True) + RMS_EPS)
        h_ref[...] = (xv * r * g_ref[...]).astype(BF16)

    row = pl.BlockSpec((tm, d), lambda i: (i, 0))
    return pl.pallas_call(
        body, name=name, grid=(s // tm,),
        in_specs=[row, pl.BlockSpec((1, d), lambda i: (0, 0)), *[ANY] * len(deps)], out_specs=row,
        out_shape=jax.ShapeDtypeStruct((s, d), BF16), compiler_params=_params("parallel"),
    )(x, gain.reshape(1, d), *deps)


def rms_bwd(name, x, gain, dh, dres, deps=(), tm=512):
    s, d = x.shape
    tm = min(tm, s)

    def body(x_ref, g_ref, dh_ref, dres_ref, *rest):
        dx_ref, dxb_ref, dg_ref = rest[-3:]
        xv = x_ref[...]
        r = lax.rsqrt(jnp.mean(xv * xv, axis=-1, keepdims=True) + RMS_EPS)
        xhat = xv * r
        dhv = dh_ref[...]
        dxhat = dhv * g_ref[...]
        dx = dres_ref[...] + r * (dxhat - xhat * jnp.mean(dxhat * xhat, axis=-1, keepdims=True))
        dx_ref[...] = dx
        dxb_ref[...] = dx.astype(BF16)

        @pl.when(pl.program_id(0) == 0)
        def _():
            dg_ref[...] = jnp.zeros_like(dg_ref)

        dg_ref[...] += _rows_to_8(dhv * xhat)

    row = pl.BlockSpec((tm, d), lambda i: (i, 0))
    return pl.pallas_call(
        body, name=name, grid=(s // tm,),
        in_specs=[row, pl.BlockSpec((1, d), lambda i: (0, 0)), row, row, *[ANY] * len(deps)],
        out_specs=[row, row, pl.BlockSpec((8, d), lambda i: (0, 0))],
        out_shape=[jax.ShapeDtypeStruct((s, d), F32), jax.ShapeDtypeStruct((s, d), BF16),
                   jax.ShapeDtypeStruct((8, d), F32)],
        compiler_params=_params("arbitrary"),
    )(x, gain.reshape(1, d), dh, dres, *deps)


def loss_head(name, x, gain, target, tm=512):
    s, d = x.shape
    tm = min(tm, s)

    def body(x_ref, g_ref, t_ref, loss_ref, dx_ref, dxb_ref, dg_ref):
        xv = x_ref[...]
        g = g_ref[...]
        r = lax.rsqrt(jnp.mean(xv * xv, axis=-1, keepdims=True) + RMS_EPS)
        xhat = xv * r
        err = xhat * g - t_ref[...]
        dy = err * (1.0 / d)
        dxhat = dy * g
        dx = r * (dxhat - xhat * jnp.mean(dxhat * xhat, axis=-1, keepdims=True))
        dx_ref[...] = dx
        dxb_ref[...] = dx.astype(BF16)

        @pl.when(pl.program_id(0) == 0)
        def _():
            dg_ref[...] = jnp.zeros_like(dg_ref)
            loss_ref[...] = jnp.zeros_like(loss_ref)

        dg_ref[...] += _rows_to_8(dy * xhat)
        loss_ref[...] += _rows_to_8(err * err) * (0.5 / d)

    row = pl.BlockSpec((tm, d), lambda i: (i, 0))
    vec = pl.BlockSpec((8, d), lambda i: (0, 0))
    return pl.pallas_call(
        body, name=name, grid=(s // tm,),
        in_specs=[row, pl.BlockSpec((1, d), lambda i: (0, 0)), row],
        out_specs=[vec, row, row, vec],
        out_shape=[jax.ShapeDtypeStruct((8, d), F32), jax.ShapeDtypeStruct((s, d), F32),
                   jax.ShapeDtypeStruct((s, d), BF16), jax.ShapeDtypeStruct((8, d), F32)],
        compiler_params=_params("arbitrary"),
    )(x, gain.reshape(1, d), target)


def _half_masks(dtype):
    lane = lax.broadcasted_iota(jnp.int32, (1, LANES), 1)
    lo = (lane < HEAD_DIM).astype(dtype)
    return lo, (1 - lo).astype(dtype)


def _swap_halves(v):
    return pltpu.roll(v, HEAD_DIM, axis=1)


SWA_BLOCKS = 8


def _swa_probs(s_raw, sink, slope, first):
    t = s_raw.shape[0]
    row = lax.broadcasted_iota(jnp.int32, (t, 2 * WINDOW), 0)
    col = lax.broadcasted_iota(jnp.int32, (t, 2 * WINDOW), 1)
    dist = row + WINDOW - col
    valid = (dist >= 0) & (dist < WINDOW)
    if first is not None:
        valid = valid & ((col >= WINDOW) | jnp.logical_not(first))
    s = jnp.where(valid, s_raw - slope * dist.astype(F32), NEG_BIG)
    m = jnp.maximum(jnp.max(s, axis=-1, keepdims=True), sink)
    e = jnp.exp(s - m)
    e_sink = jnp.exp(sink - m)
    inv = 1.0 / (jnp.sum(e, axis=-1, keepdims=True) + e_sink)
    return e * inv, e_sink * inv


def _swa_band(kp_ref, kc_ref, vp_ref, vc_ref, kv_half):
    lo_b, hi_b = _half_masks(F32)
    sel = jnp.where(kv_half == 0, lo_b, hi_b)
    k = jnp.concatenate([kp_ref[...], kc_ref[...]], axis=0).astype(F32) * sel
    v = jnp.concatenate([vp_ref[...], vc_ref[...]], axis=0).astype(F32) * sel
    k = (k + _swap_halves(k)).astype(BF16)
    v = (v + _swap_halves(v)).astype(BF16)
    return k, v, sel


def _swa_specs(nq):
    t = WINDOW
    q_spec = pl.BlockSpec((nq * t, LANES), lambda j, n, *_: (n, j))
    q_blocks = N_HEADS // 2

    def prev(off):
        return pl.BlockSpec((t, LANES), lambda j, n, *_: (jnp.maximum(nq * n - 1, 0), q_blocks + off + j // 8))

    def cur(off):
        return pl.BlockSpec((nq * t, LANES), lambda j, n, *_: (n, q_blocks + off + j // 8))

    kv_blocks = N_KV_A // 2
    return q_spec, [prev(0), cur(0), prev(kv_blocks), cur(kv_blocks)]


def _rows(a, b, n=1):
    return a[b * WINDOW:(b + n) * WINDOW]


def swa_fwd(name, qkv, sinks, slopes):
    s = qkv.shape[0]
    nq = min(SWA_BLOCKS, s // WINDOW)

    def body(sink_ref, slope_ref, q_ref, kp_ref, kc_ref, vp_ref, vc_ref, o_ref):
        j, n = pl.program_id(0), pl.program_id(1)
        k_all, v_all, _ = _swa_band(kp_ref, kc_ref, vp_ref, vc_ref, (j // 4) % 2)
        q = q_ref[...] * ATTN_SCALE
        masks = _half_masks(BF16)
        chains = [(b, hq) for b in range(nq) for hq in range(2)]
        s_raw = {(b, hq): _dot(_rows(q, b) * masks[hq], _rows(k_all, b, 2), "nt") for b, hq in chains}
        p = {(b, hq): _swa_probs(s_raw[(b, hq)], sink_ref[2 * j + hq], slope_ref[2 * j + hq],
                                 (n == 0) if b == 0 else None)[0] for b, hq in chains}
        outs = [sum(_dot(p[(b, hq)].astype(BF16), _rows(v_all, b, 2) * masks[hq]) for hq in range(2))
                for b in range(nq)]
        o_ref[...] = jnp.concatenate(outs, axis=0).astype(BF16)

    q_spec, kv_specs = _swa_specs(nq)
    return pl.pallas_call(
        body, name=name,
        grid_spec=pltpu.PrefetchScalarGridSpec(
            num_scalar_prefetch=2, grid=(N_HEADS // 2, s // (nq * WINDOW)),
            in_specs=[q_spec, *kv_specs], out_specs=q_spec),
        out_shape=jax.ShapeDtypeStruct((s, N_HEADS * HEAD_DIM), BF16),
        compiler_params=_params("parallel", "parallel"),
    )(sinks, slopes, qkv, qkv, qkv, qkv, qkv)


def swa_bwd(name, qkv, do, sinks, slopes):
    s = qkv.shape[0]
    t = WINDOW
    nq = min(SWA_BLOCKS, s // t)

    def body(sink_ref, slope_ref, q_ref, kp_ref, kc_ref, vp_ref, vc_ref, do_ref, dq_ref, dk_ref, dv_ref, ds_ref):
        j, n = pl.program_id(0), pl.program_id(1)
        k_all, v_all, sel = _swa_band(kp_ref, kc_ref, vp_ref, vc_ref, (j // 4) % 2)
        q = q_ref[...] * ATTN_SCALE
        do_v = do_ref[...]
        masks = _half_masks(BF16)

        @pl.when((j % 8 == 0) & (n == 0))
        def _():
            dk_ref[...] = jnp.zeros_like(dk_ref)
            dv_ref[...] = jnp.zeros_like(dv_ref)

        @pl.when(n == 0)
        def _():
            ds_ref[...] = jnp.zeros_like(ds_ref)

        chains = [(b, hq) for b in range(nq) for hq in range(2)]
        qm = {(b, hq): _rows(q, b) * masks[hq] for b, hq in chains}
        dom = {(b, hq): _rows(do_v, b) * masks[hq] for b, hq in chains}
        s_raw = {ch: _dot(qm[ch], _rows(k_all, ch[0], 2), "nt") for ch in chains}
        dp = {ch: _dot(dom[ch], _rows(v_all, ch[0], 2), "nt") for ch in chains}
        pb, dsc = {}, {}
        dsink = [jnp.zeros((), F32), jnp.zeros((), F32)]
        for ch in chains:
            b, hq = ch
            p, p_sink = _swa_probs(s_raw[ch], sink_ref[2 * j + hq], slope_ref[2 * j + hq], (n == 0) if b == 0 else None)
            delta = jnp.sum(p * dp[ch], axis=-1, keepdims=True)
            dsc[ch] = (p * (dp[ch] - delta)).astype(BF16)
            pb[ch] = p.astype(BF16)
            dsink[hq] = dsink[hq] - jnp.sum(p_sink * delta)
        for hq in range(2):
            ds_ref[hq:hq + 1, :] += jnp.zeros((1, LANES), F32) + dsink[hq]
        dq, dk, dv = [], [], []
        for b in range(nq):
            dq.append(sum(_dot(dsc[(b, hq)], _rows(k_all, b, 2) * masks[hq]) for hq in range(2)))
            dk_b = sum(_dot(dsc[(b, hq)], qm[(b, hq)], "tn") for hq in range(2))
            dv_b = sum(_dot(pb[(b, hq)], dom[(b, hq)], "tn") for hq in range(2))
            dk.append((dk_b + _swap_halves(dk_b)) * sel)
            dv.append((dv_b + _swap_halves(dv_b)) * sel)
        dq_ref[...] = (jnp.concatenate(dq, axis=0) * ATTN_SCALE).astype(BF16)

        @pl.when(n == 0)
        def _():
            dk_ref[pl.ds(0, t), :] += dk[0][t:]
            dv_ref[pl.ds(0, t), :] += dv[0][t:]

        @pl.when(n > 0)
        def _():
            start = pl.multiple_of((nq * n - 1) * t, t)
            dk_ref[pl.ds(start, 2 * t), :] += dk[0]
            dv_ref[pl.ds(start, 2 * t), :] += dv[0]

        for b in range(1, nq):
            start = pl.multiple_of((nq * n + b - 1) * t, t)
            dk_ref[pl.ds(start, 2 * t), :] += dk[b]
            dv_ref[pl.ds(start, 2 * t), :] += dv[b]

    q_spec, kv_specs = _swa_specs(nq)
    kv_out = pl.BlockSpec((s, LANES), lambda j, n, *_: (0, j // 8))
    return pl.pallas_call(
        body, name=name,
        grid_spec=pltpu.PrefetchScalarGridSpec(
            num_scalar_prefetch=2, grid=(N_HEADS // 2, s // (nq * t)),
            in_specs=[q_spec, *kv_specs, q_spec],
            out_specs=[q_spec, kv_out, kv_out, pl.BlockSpec((None, 8, LANES), lambda j, n, *_: (j, 0, 0))]),
        out_shape=[jax.ShapeDtypeStruct((s, N_HEADS * HEAD_DIM), BF16),
                   jax.ShapeDtypeStruct((s, N_KV_A * HEAD_DIM), F32),
                   jax.ShapeDtypeStruct((s, N_KV_A * HEAD_DIM), F32),
                   jax.ShapeDtypeStruct((N_HEADS // 2, 8, LANES), F32)],
        compiler_params=_params("arbitrary", "arbitrary"),
    )(sinks, slopes, qkv, qkv, qkv, qkv, qkv, do)


SB_TILE = 256


def _split_k(v):
    hi = v.astype(BF16)
    lo = (v - hi.astype(F32)).astype(BF16)
    return jnp.concatenate([hi, lo], axis=1)


def _tri2(t, inclusive):
    r = lax.broadcasted_iota(jnp.int32, (2 * t, t), 0)
    c = lax.broadcasted_iota(jnp.int32, (2 * t, t), 1)
    r = jnp.where(r >= t, r - t, r)
    return ((r >= c) if inclusive else (r > c)).astype(BF16)


def _sb_logs(z, before):
    neg_abs = lax.bitcast_convert_type(lax.bitcast_convert_type(z, jnp.uint32) | jnp.uint32(0x80000000), F32)
    l = jnp.log(1.0 + jnp.exp(neg_abs))
    lb = jnp.minimum(z, 0.0) - l
    lm = lb - z
    if before is not None:
        lm = jnp.where(before, lm, 0.0)
    return lb, lm


def _sb_weights(lb, sfx, c_lm, before):
    a = jnp.exp(lb + sfx + c_lm)
    if before is not None:
        a = jnp.where(before, a, 0.0)
    return a


def _sb_sweep(tile, i, init, diag):
    carry = lax.cond(i > 0, lambda: tile([i, i - 1], init, diag), lambda: tile([i], init, diag))
    rest = jnp.maximum(i - 1, 0)
    odd = rest % 2
    carry = lax.cond(odd == 1, lambda: tile([i - 2], carry, None), lambda: carry)
    base = i - 2 - odd
    return lax.fori_loop(0, rest // 2, lambda n, cr: tile([base - 2 * n, base - 2 * n - 1], cr, None), carry)


def _sb_specs(s, t):
    hp, nb = N_HEADS // 2, s // t
    q_spec = pl.BlockSpec((t, LANES), lambda h, i: (i, h))
    k_spec = pl.BlockSpec((s, LANES), lambda h, i: (0, hp + h))
    v_spec = pl.BlockSpec((s, LANES), lambda h, i: (0, 2 * hp + h))
    saved = pl.BlockSpec((None, 2, None, nb, t, t), lambda h, i: (h, 0, i, 0, 0, 0))
    return q_spec, k_spec, v_spec, saved


def sb_fwd(name, qkv, t=SB_TILE):
    s = qkv.shape[0]
    t = min(t, s // 2)
    nb = s // t

    def body(q_ref, k_ref, v_ref, ob_ref, of_ref, a_ref, sg_ref):
        i = pl.program_id(1)
        masks = _half_masks(BF16)
        q = q_ref[...] * ATTN_SCALE
        qm = [q * masks[0], q * masks[1]]
        tri_x = _tri2(t, False)
        r = lax.broadcasted_iota(jnp.int32, (t, t), 0)
        c = lax.broadcasted_iota(jnp.int32, (t, t), 1)
        diag = c < r

        def tile(js, carry, before):
            c0, c1, acc = carry
            cs = [c0, c1]
            kj = [k_ref[pl.ds(pl.multiple_of(j * t, t), t), :] for j in js]
            vj = [v_ref[pl.ds(pl.multiple_of(j * t, t), t), :] for j in js]
            chains = [(h, b) for b in range(len(js)) for h in range(2)]
            z = {(h, b): _dot(qm[h], kj[b], "nt") for h, b in chains}
            lb, sfx, c_at = {}, {}, {}
            for ch in chains:
                h, b = ch
                lb[ch], lm = _sb_logs(z[ch], before if b == 0 else None)
                sfx[ch] = _dot(_split_k(lm), tri_x)
                c_at[ch] = cs[h]
                cs[h] = cs[h] + jnp.sum(lm, axis=-1, keepdims=True)
            for ch in chains:
                h, b = ch
                ab = _sb_weights(lb[ch], sfx[ch], c_at[ch], before if b == 0 else None).astype(BF16)
                a_ref[h, js[b]] = ab
                sg_ref[h, js[b]] = jnp.exp(lb[ch]).astype(BF16)
                acc = acc + _dot(ab, vj[b] * masks[h])
            return cs[0], cs[1], acc

        zero = jnp.zeros((t, 1), F32)
        carry = _sb_sweep(tile, i, (zero, zero, jnp.zeros((t, LANES), F32)), diag)
        ob_ref[...] = carry[2].astype(BF16)
        of_ref[...] = carry[2]

    q_spec, k_spec, v_spec, saved = _sb_specs(s, t)
    width = N_HEADS * HEAD_DIM
    keep = jax.ShapeDtypeStruct((N_HEADS // 2, 2, nb, nb, t, t), BF16)
    return pl.pallas_call(
        body, name=name, grid=(N_HEADS // 2, nb),
        in_specs=[q_spec, k_spec, v_spec], out_specs=[q_spec, q_spec, saved, saved],
        out_shape=[jax.ShapeDtypeStruct((s, width), BF16), jax.ShapeDtypeStruct((s, width), F32), keep, keep],
        compiler_params=_params("parallel", "parallel"),
    )(qkv, qkv, qkv)


def sb_bwd(name, qkv, o_f32, do, a_all, sg_all, t=SB_TILE):
    s = qkv.shape[0]
    t = min(t, s // 2)

    def body(q_ref, k_ref, v_ref, o_ref, do_ref, a_ref, sg_ref, dq_ref, dkb_ref, dvb_ref, dk_ref, dv_ref):
        i = pl.program_id(1)
        masks = _half_masks(BF16)
        fmasks = _half_masks(F32)
        q = q_ref[...] * ATTN_SCALE
        do_v = do_ref[...]
        qm = [q * masks[0], q * masks[1]]
        dom = [do_v * masks[0], do_v * masks[1]]
        prod = do_v.astype(F32) * o_ref[...]
        delta = [jnp.sum(prod * fmasks[h], axis=-1, keepdims=True) for h in range(2)]
        tri_i = _tri2(t, True)
        r = lax.broadcasted_iota(jnp.int32, (t, t), 0)
        c = lax.broadcasted_iota(jnp.int32, (t, t), 1)
        diag = c < r

        @pl.when(i == 0)
        def _():
            dk_ref[...] = jnp.zeros_like(dk_ref)
            dv_ref[...] = jnp.zeros_like(dv_ref)

        def tile(js, carry, before):
            cd0, cd1, dq = carry
            cd = [cd0, cd1]
            starts = [pl.multiple_of(j * t, t) for j in js]
            kj = [k_ref[pl.ds(st, t), :] for st in starts]
            vj = [v_ref[pl.ds(st, t), :] for st in starts]
            chains = [(h, b) for b in range(len(js)) for h in range(2)]
            da = {(h, b): _dot(dom[h], vj[b], "nt") for h, b in chains}
            ab, de, dsfx, cd_at = {}, {}, {}, {}
            dv = [jnp.zeros((t, LANES), F32) for _ in js]
            dk = [jnp.zeros((t, LANES), F32) for _ in js]
            for ch in chains:
                h, b = ch
                ab[ch] = a_ref[h, js[b]]
                de[ch] = da[ch] * ab[ch].astype(F32)
                dsfx[ch] = _dot(_split_k(de[ch]), tri_i)
                dv[b] = dv[b] + _dot(ab[ch], dom[h], "tn")
                cd_at[ch] = cd[h]
                cd[h] = cd[h] + jnp.sum(de[ch], axis=-1, keepdims=True)
            for ch in chains:
                h, b = ch
                farther = delta[h] - cd_at[ch] - dsfx[ch]
                dz = de[ch] - sg_ref[h, js[b]].astype(F32) * (de[ch] + farther)
                if before is not None and b == 0:
                    dz = jnp.where(before, dz, 0.0)
                dzb = dz.astype(BF16)
                dq = dq + _dot(dzb, kj[b] * masks[h])
                dk[b] = dk[b] + _dot(dzb, qm[h], "tn")
            for b, st in enumerate(starts):
                dk_ref[pl.ds(st, t), :] += dk[b]
                dv_ref[pl.ds(st, t), :] += dv[b]
            return cd[0], cd[1], dq

        zero = jnp.zeros((t, 1), F32)
        carry = _sb_sweep(tile, i, (zero, zero, jnp.zeros((t, LANES), F32)), diag)
        dq_ref[...] = (carry[2] * ATTN_SCALE).astype(BF16)

        @pl.when(i == s // t - 1)
        def _():
            dkb_ref[...] = dk_ref[...].astype(BF16)
            dvb_ref[...] = dv_ref[...].astype(BF16)

    q_spec, k_spec, v_spec, saved = _sb_specs(s, t)
    kv_out = pl.BlockSpec((s, LANES), lambda h, i: (0, h))
    width = N_HEADS * HEAD_DIM
    return pl.pallas_call(
        body, name=name, grid=(N_HEADS // 2, s // t),
        in_specs=[q_spec, k_spec, v_spec, q_spec, q_spec, saved, saved], out_specs=[q_spec, kv_out, kv_out],
        out_shape=[jax.ShapeDtypeStruct((s, width), BF16)] * 3,
        scratch_shapes=[pltpu.VMEM((s, LANES), F32), pltpu.VMEM((s, LANES), F32)],
        compiler_params=_params("arbitrary", "arbitrary"),
    )(qkv, qkv, qkv, o_f32, do, a_all, sg_all)


def adamw(name, w, g, m, v, tm=256):
    shape = w.shape
    c = shape[-1]
    rows = math.prod(shape[:-1])
    tm = min(tm, rows)

    def body(w_ref, g_ref, m_ref, v_ref, d_ref, mo_ref, vo_ref):
        gv = g_ref[...]
        m2 = ADAM_B1 * m_ref[...] + (1.0 - ADAM_B1) * gv
        v2 = ADAM_B2 * v_ref[...] + (1.0 - ADAM_B2) * (gv * gv)
        m_hat = m2 / (1.0 - ADAM_B1 ** ADAM_STEP)
        v_hat = v2 / (1.0 - ADAM_B2 ** ADAM_STEP)
        d_ref[...] = -ADAM_LR * (m_hat / (jnp.sqrt(v_hat) + ADAM_EPS) + ADAM_WD * w_ref[...])
        mo_ref[...] = m2
        vo_ref[...] = v2

    blk = pl.BlockSpec((tm, c), lambda i: (i, 0))
    outs = pl.pallas_call(
        body, name=name, grid=(rows // tm,), in_specs=[blk] * 4, out_specs=[blk] * 3,
        out_shape=[jax.ShapeDtypeStruct((rows, c), F32)] * 3, compiler_params=_params("parallel"),
    )(*[t.reshape(rows, c) for t in (w, g, m, v)])
    return [o.reshape(shape) for o in outs]


HBM = pl.BlockSpec(memory_space=pltpu.HBM)


def _place():
    x, y, c = lax.axis_index("x"), lax.axis_index("y"), lax.axis_index("c")
    return x, y, c, [(1 - x, y), (x, 1 - y), (1 - x, 1 - y)]


def _remote(src, dst, send, recv, dev):
    return pltpu.make_async_remote_copy(src_ref=src, dst_ref=dst, send_sem=send, recv_sem=recv, device_id=dev,
                                        device_id_type=MESH)


SEM = pl.BlockSpec(memory_space=pltpu.SEMAPHORE)
ANY = pl.BlockSpec(memory_space=pl.ANY)
DATAFLOW = pltpu.SideEffectType.DATAFLOW_SIDE_EFFECTING


def _in_hbm(v):
    return pltpu.with_memory_space_constraint(v, pltpu.HBM)


def split_start(name, bufs, n_copies, sends, after=()):
    nb, na = len(bufs), len(after)

    def body(*refs):
        send, recv = refs[nb + na], refs[nb + na + 1]
        for cp in sends(refs[:nb], send, recv):
            cp.start()
        refs[-1][...] = jnp.zeros_like(refs[-1])

    outs = pl.pallas_call(
        body, name=name,
        in_specs=[HBM] * nb + [ANY] * na,
        out_shape=(pltpu.SemaphoreType.DMA((n_copies,)), pltpu.SemaphoreType.DMA((n_copies,)),
                   *[pltpu.HBM(b.shape, b.dtype) for b in bufs], jax.ShapeDtypeStruct((8, LANES), F32)),
        out_specs=(SEM, SEM, *[HBM] * nb, pl.BlockSpec(memory_space=pltpu.VMEM)),
        input_output_aliases={i: 2 + i for i in range(nb)},
        compiler_params=pltpu.CompilerParams(has_side_effects=DATAFLOW),
    )(*[_in_hbm(b) for b in bufs], *after)
    return outs[0], outs[1], list(outs[2:2 + nb]), outs[-1]


def split_wait(name, send_sems, recv_sems, bufs, sends, arrivals, after):
    nb, na = len(bufs), len(after)

    def body(*refs):
        send, recv = refs[nb], refs[nb + 1]
        for cp in sends(refs[:nb], send, recv):
            cp.wait_send()
        for cp in arrivals(refs[:nb], send, recv):
            cp.wait_recv()

    outs = pl.pallas_call(
        body, name=name,
        in_specs=[HBM] * nb + [SEM, SEM] + [ANY] * na,
        out_shape=tuple(pltpu.HBM(b.shape, b.dtype) for b in bufs), out_specs=tuple([HBM] * nb),
        input_output_aliases={i: i for i in range(nb)},
        compiler_params=pltpu.CompilerParams(has_side_effects=DATAFLOW),
    )(*bufs, send_sems, recv_sems, *after)
    return list(outs)


def _gather_plan(n):
    def sends(refs, send, recv):
        x, y, c, chips = _place()
        me = 2 * x + y
        return [_remote(refs[t].at[me, c], refs[t].at[me, c], send.at[3 * t + k], recv.at[3 * t + k], (px, py, c))
                for t in range(n) for k, (px, py) in enumerate(chips)]

    def arrivals(refs, send, recv):
        x, y, c, chips = _place()
        return [_remote(refs[t].at[2 * px + py, c], refs[t].at[2 * px + py, c], send.at[3 * t + k], recv.at[3 * t + k],
                        (px, py, c)) for t in range(n) for k, (px, py) in enumerate(chips)]

    return 3 * n, sends, arrivals


def _relay_plan(n):
    def sends(refs, send, recv):
        x, y, c, chips = _place()
        return [_remote(refs[t].at[2 * px + py, c], refs[t].at[2 * px + py, c], send.at[3 * t + k], recv.at[3 * t + k],
                        (x, y, 1 - c)) for t in range(n) for k, (px, py) in enumerate(chips)]

    def arrivals(refs, send, recv):
        x, y, c, chips = _place()
        return [_remote(refs[t].at[2 * px + py, 1 - c], refs[t].at[2 * px + py, 1 - c], send.at[3 * t + k],
                        recv.at[3 * t + k], (x, y, 1 - c)) for t in range(n) for k, (px, py) in enumerate(chips)]

    return 3 * n, sends, arrivals


def _swap_plan(n):
    def copies(refs, send, recv):
        x, y, c, _ = _place()
        return [_remote(refs[t].at[s, 1 - c], refs[n + t].at[s], send.at[N_CHIPS * t + s], recv.at[N_CHIPS * t + s],
                        (x, y, 1 - c)) for t in range(n) for s in range(N_CHIPS)]

    return N_CHIPS * n, copies, copies


def _scatter_plan(n):
    def sends(refs, send, recv):
        x, y, c, chips = _place()
        me = 2 * x + y
        return [_remote(refs[t].at[2 * px + py], refs[n + t].at[me], send.at[3 * t + k], recv.at[3 * t + k],
                        (px, py, c)) for t in range(n) for k, (px, py) in enumerate(chips)]

    def arrivals(refs, send, recv):
        x, y, c, chips = _place()
        return [_remote(refs[t].at[2 * px + py], refs[n + t].at[2 * px + py], send.at[3 * t + k], recv.at[3 * t + k],
                        (px, py, c)) for t in range(n) for k, (px, py) in enumerate(chips)]

    return 3 * n, sends, arrivals


def cast_into_slot(name, w, layer, chip, deps=(), tm=1024):
    _, r, c = w.shape
    tm = min(tm, r)

    def body(chip_ref, w_ref, *rest):
        rest[-1][...] = w_ref[...].astype(BF16)

    return pl.pallas_call(
        body, name=name,
        grid_spec=pltpu.PrefetchScalarGridSpec(
            num_scalar_prefetch=1, grid=(r // tm,),
            in_specs=[pl.BlockSpec((None, tm, c), lambda i, chip_ref: (layer, i, 0)), *[ANY] * len(deps)],
            out_specs=pl.BlockSpec((None, tm, c), lambda i, chip_ref: (chip_ref[0], i, 0))),
        out_shape=jax.ShapeDtypeStruct((N_CHIPS, r, c), BF16), compiler_params=_params("parallel"),
    )(chip, w, *deps)


def join_halves(name, f):
    n = f.shape[0]

    def body(f_ref, o_ref, send, recv):
        x, y, c, _ = _place()
        sib = (x, y, 1 - c)
        sends = [_remote(f_ref.at[l, c], o_ref.at[l, c], send.at[l], recv.at[l], sib) for l in range(n)]
        for cp in sends:
            cp.start()
        for l in range(n):
            _remote(f_ref.at[l, 1 - c], o_ref.at[l, 1 - c], send.at[l], recv.at[l], sib).wait_recv()
        for cp in sends:
            cp.wait_send()

    return pl.pallas_call(
        body, name=name, in_specs=[HBM], out_specs=HBM, out_shape=jax.ShapeDtypeStruct(f.shape, f.dtype),
        input_output_aliases={0: 0},
        scratch_shapes=[pltpu.SemaphoreType.DMA((n,)), pltpu.SemaphoreType.DMA((n,))],
    )(f)


def add_sibling_half(name, g, a, core, tm=1024):
    _, _, rh, c = g.shape
    tm = min(tm, rh)

    def body(core_ref, g_ref, a_ref, o_ref):
        o_ref[...] = (g_ref[...].astype(F32) + a_ref[...].astype(F32)).astype(BF16)

    return pl.pallas_call(
        body, name=name,
        grid_spec=pltpu.PrefetchScalarGridSpec(
            num_scalar_prefetch=1, grid=(N_CHIPS, rh // tm),
            in_specs=[pl.BlockSpec((None, None, tm, c), lambda s, i, core_ref: (s, core_ref[0], i, 0)),
                      pl.BlockSpec((None, tm, c), lambda s, i, core_ref: (s, i, 0))],
            out_specs=pl.BlockSpec((None, tm, c), lambda s, i, core_ref: (s, i, 0))),
        out_shape=jax.ShapeDtypeStruct(a.shape, BF16), compiler_params=_params("parallel", "parallel"),
    )(core, g, a)


def sum_chips_into(name, p, b, f, layer, chip, core, tm=512):
    _, rh, c = b.shape
    tm = min(tm, rh)

    def body(chip_ref, core_ref, p_ref, b_ref, f_ref, o_ref):
        acc = jnp.zeros((tm, c), F32)
        for s in range(N_CHIPS):
            acc = acc + jnp.where(chip_ref[0] == s, p_ref[s].astype(F32), b_ref[s].astype(F32))
        o_ref[...] = acc

    slots = pl.BlockSpec((N_CHIPS, tm, c), lambda i, chip_ref, core_ref: (0, i, 0))
    return pl.pallas_call(
        body, name=name,
        grid_spec=pltpu.PrefetchScalarGridSpec(
            num_scalar_prefetch=2, grid=(rh // tm,), in_specs=[slots, slots, ANY],
            out_specs=pl.BlockSpec((None, None, tm, c), lambda i, chip_ref, core_ref: (layer, core_ref[0], i, 0))),
        out_shape=jax.ShapeDtypeStruct(f.shape, F32), input_output_aliases={4: 0},
        compiler_params=_params("parallel"),
    )(chip, core, p, b, f)


N_DEV = 8


def allreduce_small(name, parts):
    p, _, d = parts.shape
    m_per = p * 8

    def body(x_ref, out_ref, all_ref, send_sems, recv_sems, local_sem):
        x, y, c, chips = _place()
        me, sibling = (x, y, c), (x, y, 1 - c)

        def rows(px, py, pc):
            return all_ref.at[pl.ds((4 * px + 2 * py + pc) * m_per, m_per), :]

        def copy(k, block, to, src=None):
            return _remote(rows(*block) if src is None else src, rows(*block), send_sems.at[k], recv_sems.at[k], to)

        mine = pltpu.make_async_copy(x_ref, rows(*me), local_sem)
        mine.start()
        first = [copy(0, me, sibling, src=x_ref)]
        first += [copy(1 + j, me, (*chip, c), src=x_ref) for j, chip in enumerate(chips)]
        for cp in first:
            cp.start()
        passed = [copy(4 + j, (*chip, c), sibling) for j, chip in enumerate(chips)]
        for j, chip in enumerate(chips):
            copy(1 + j, (*chip, c), me).wait_recv()
            passed[j].start()
        copy(0, sibling, me).wait_recv()
        for j, chip in enumerate(chips):
            copy(4 + j, (*chip, 1 - c), me).wait_recv()
        for cp in first + passed:
            cp.wait_send()
        mine.wait()
        acc = all_ref[pl.ds(0, m_per), :]
        for dev in range(1, N_DEV):
            acc = acc + all_ref[pl.ds(dev * m_per, m_per), :]
        out_ref[...] = jnp.sum(acc.reshape(p, 8, d), axis=1)

    vmem = pl.BlockSpec(memory_space=pltpu.VMEM)
    return pl.pallas_call(
        body, name=name, in_specs=[vmem], out_specs=vmem,
        out_shape=jax.ShapeDtypeStruct((p, d), F32),
        scratch_shapes=[pltpu.VMEM((N_DEV * m_per, d), F32), pltpu.SemaphoreType.DMA((7,)),
                        pltpu.SemaphoreType.DMA((7,)), pltpu.SemaphoreType.DMA],
        compiler_params=pltpu.CompilerParams(vmem_limit_bytes=VMEM_LIMIT),
    )(parts.reshape(m_per, d))


def _empty(shape, dtype):
    return _in_hbm(lax.empty(shape, dtype))


class _GradExchange:
    def __init__(self, layer, kinds, grads, chip, core):
        self.layer, self.kinds, self.chip, self.core = layer, kinds, chip, core
        self.g4 = [g.reshape(N_CHIPS, 2, g.shape[1] // 2, g.shape[2]) for g in grads]
        self.n = len(grads)

    def start_swap(self, after):
        n_copies, self.swap_sends, self.swap_arrivals = _swap_plan(self.n)
        lands = [_empty((N_CHIPS,) + g.shape[2:], BF16) for g in self.g4]
        self.swap = split_start(f"swap_start_l{self.layer}", self.g4 + lands, n_copies, self.swap_sends, after)
        return self.swap[3]

    def swap_to_scatter(self, after):
        send, recv, bufs, _ = self.swap
        bufs = split_wait(f"swap_wait_l{self.layer}", send, recv, bufs, self.swap_sends, self.swap_arrivals, after)
        g4, lands = bufs[:self.n], bufs[self.n:]
        self.p = [add_sibling_half(f"add_l{self.layer}_{k}", g4[t], lands[t], self.core)
                  for t, (k, _) in enumerate(self.kinds)]
        n_copies, self.sc_sends, self.sc_arrivals = _scatter_plan(self.n)
        lands = [_empty(p.shape, BF16) for p in self.p]
        self.scatter = split_start(f"scatter_start_l{self.layer}", self.p + lands, n_copies, self.sc_sends)
        return self.scatter[3]

    def finish(self, f, after):
        send, recv, bufs, _ = self.scatter
        bufs = split_wait(f"scatter_wait_l{self.layer}", send, recv, bufs, self.sc_sends, self.sc_arrivals, after)
        p, lands = bufs[:self.n], bufs[self.n:]
        for t, (kind, l) in enumerate(self.kinds):
            f[kind] = sum_chips_into(f"sum_l{self.layer}_{kind}", p[t], lands[t], f[kind], l, self.chip, self.core)


def _relu2(acc):
    r = jnp.maximum(acc, 0.0)
    return acc, r * r


def _relu2_bwd(acc, u):
    return (acc * (2.0 * jnp.maximum(u.astype(F32), 0.0)),)


def _same(acc):
    return (acc,)


def kernel(x, a_w_qkv, a_w_o, a_sinks, b_w_qkv, b_w_o, norm_mix, norm_mlp, mlp_w_in, mlp_w_out, final_norm, loss_target, m_a_w_qkv, m_a_w_o, m_a_sinks, m_b_w_qkv, m_b_w_o, m_norm_mix, m_norm_mlp, m_mlp_w_in, m_mlp_w_out, m_final_norm, v_a_w_qkv, v_a_w_o, v_a_sinks, v_b_w_qkv, v_b_w_o, v_norm_mix, v_norm_mlp, v_mlp_w_in, v_mlp_w_out, v_final_norm):
    _, s, d = x.shape
    depth = norm_mix.shape[0]
    width = N_HEADS * HEAD_DIM
    core = lax.axis_index("c").astype(jnp.int32).reshape(1)
    chip = (2 * lax.axis_index("x") + lax.axis_index("y")).astype(jnp.int32).reshape(1)
    slopes = jnp.power(2.0, -8.0 * (jnp.arange(N_HEADS, dtype=F32) + 1.0) / N_HEADS)
    qkv_of = {0: ("a_qkv", a_w_qkv), 1: ("b_qkv", b_w_qkv)}
    o_of = {0: ("a_o", a_w_o), 1: ("b_o", b_w_o)}

    def layer_kinds(i):
        return [(qkv_of[i % 2][0], i // 2), (o_of[i % 2][0], i // 2), ("mlp_in", i), ("mlp_out", i)]

    stacks = {"a_qkv": a_w_qkv, "a_o": a_w_o, "b_qkv": b_w_qkv, "b_o": b_w_o, "mlp_in": mlp_w_in, "mlp_out": mlp_w_out}

    def slots(i, deps=()):
        return [cast_into_slot(f"cast_l{i}_{k}", stacks[k], l, chip, deps) for k, l in layer_kinds(i)]

    n_gather, gather_sends, gather_arrivals = _gather_plan(2)
    _, relay_sends, relay_arrivals = _relay_plan(2)

    class Flight:
        def __init__(self, tag, bufs, after):
            self.tag = tag
            halves = [b.reshape(N_CHIPS, 2, b.shape[1] // 2, b.shape[2]) for b in bufs]
            self.shapes = [b.shape for b in bufs]
            self.state = split_start(f"gather_start_{tag}", halves, n_gather, gather_sends, after)
            self.token = self.state[3]

        def relay(self, after):
            send, recv, bufs, _ = self.state
            bufs = split_wait(f"gather_wait_{self.tag}", send, recv, bufs, gather_sends, gather_arrivals, after)
            self.state = split_start(f"relay_start_{self.tag}", bufs, n_gather, relay_sends)
            return self.state[3]

        def land(self, after):
            send, recv, bufs, _ = self.state
            bufs = split_wait(f"relay_wait_{self.tag}", send, recv, bufs, relay_sends, relay_arrivals, after)
            return [b.reshape(shape) for b, shape in zip(bufs, self.shapes)]

    saved, weights = [], []
    xc = x[0]
    first = slots(0)
    flight_a = Flight("l0a", first[:2], ())
    flight_m = Flight("l0m", first[2:], (flight_a.token,))
    flight_a.relay((xc, flight_m.token))
    for i in range(depth):
        mixer, j = i % 2, i // 2
        w_qkv, w_o = flight_a.land((xc,))
        w_o = w_o.reshape(width, d)
        started, this_m = (flight_m.token,), flight_m
        if i + 1 < depth:
            following = slots(i + 1, (flight_m.token,))
            flight_a = Flight(f"l{i + 1}a", following[:2], (w_qkv,))
            flight_m = Flight(f"l{i + 1}m", following[2:], (flight_a.token,))
            started = (flight_m.token,)
        h = rms_fwd(f"l{i}_norm_mix", xc, norm_mix[i], deps=started)
        qkv = mm_cols(f"l{i}_qkv", h, w_qkv, _same, (BF16,))[0]
        if mixer == 0:
            attn, kept = swa_fwd(f"l{i}_swa", qkv, a_sinks[j], slopes), ()
        else:
            attn, *kept = sb_fwd(f"l{i}_sb", qkv)
        xm = mm_res(f"l{i}_o", attn, w_o, xc, deps=(this_m.relay((attn,)),))
        w_in, w_out = this_m.land((xm,))
        w_out = w_out.reshape(-1, d)
        weights.append((w_qkv, w_o, w_in, w_out))
        h2 = rms_fwd(f"l{i}_norm_mlp", xm, norm_mlp[i])
        u, hh = mm_cols(f"l{i}_in", h2, w_in, _relu2, (BF16, BF16))
        relayed = (flight_a.relay((u,)),) if i + 1 < depth else ()
        xn = mm_res(f"l{i}_out", hh, w_out, xm, deps=relayed)
        saved.append((xc, h, qkv, attn, kept, xm, h2, u, hh))
        xc = xn
    loss_rows, dx, dxb, dg_final = loss_head("loss_head", xc, final_norm, loss_target[0])
    loss = lax.psum(jnp.sum(loss_rows), ("x", "y", "c"))

    big = {"a_qkv": (a_w_qkv, m_a_w_qkv, v_a_w_qkv), "a_o": (a_w_o, m_a_w_o, v_a_w_o),
           "b_qkv": (b_w_qkv, m_b_w_qkv, v_b_w_qkv), "b_o": (b_w_o, m_b_w_o, v_b_w_o),
           "mlp_in": (mlp_w_in, m_mlp_w_in, v_mlp_w_in), "mlp_out": (mlp_w_out, m_mlp_w_out, v_mlp_w_out)}
    f = {k: _empty((w.shape[0], 2, w.shape[1] // 2, w.shape[2]), F32) for k, (w, _, _) in big.items()}
    dg_mix, dg_mlp, dsinks = [], [], []
    prev_a, started = None, ()
    for i in reversed(range(depth)):
        mixer, j = i % 2, i // 2
        xin, h, qkv, attn, kept, xm, h2, u, hh = saved[i]
        w_qkv, w_o, w_in, w_out = weights[i]
        kinds = layer_kinds(i)
        du = mm_nt(f"l{i}_d_hidden", dxb, w_out, _relu2_bwd, (u,), deps=started)
        g_out = mm_tn(f"l{i}_g_out", hh, dxb).reshape(N_CHIPS, -1, d)
        started = (g_out,) if prev_a is None else (g_out, prev_a.swap_to_scatter((g_out,)))
        dh2 = mm_nt_cols(f"l{i}_d_h2", du, w_in, deps=started)
        g_in = mm_tn_cols(f"l{i}_g_in", h2, du)
        cur_m = _GradExchange(f"{i}m", kinds[2:], [g_in, g_out], chip, core)
        started = (g_in, cur_m.start_swap((g_in,)))
        dxm, dxmb, dg = rms_bwd(f"l{i}_norm_mlp_bwd", xm, norm_mlp[i], dh2, dx, deps=started)
        dg_mlp.append(dg)
        g_o = mm_tn(f"l{i}_g_o", attn, dxmb).reshape(N_CHIPS, -1, d)
        dattn = mm_nt(f"l{i}_d_attn", dxmb, w_o, _same, deps=(g_o, cur_m.swap_to_scatter((g_o,))))
        if mixer == 0:
            dq, dk, dv, dsk = swa_bwd(f"l{i}_swa_bwd", qkv, dattn, a_sinks[j], slopes)
            dsinks.append(dsk[:, :2, 0].reshape(N_HEADS))
        else:
            attn_f32, weights_a, sigmoids = kept
            dq, dk, dv = sb_bwd(f"l{i}_sb_bwd", qkv, attn_f32, dattn, weights_a, sigmoids)
        dqkv = jnp.concatenate([dq, dk.astype(BF16), dv.astype(BF16)], axis=1)
        dh = mm_nt_cols(f"l{i}_d_h", dqkv, w_qkv)
        g_qkv = mm_tn_cols(f"l{i}_g_qkv", h, dqkv)
        dx, dxb, dg = rms_bwd(f"l{i}_norm_mix_bwd", xin, norm_mix[i], dh, dxm, deps=(g_qkv,))
        dg_mix.append(dg)
        if prev_a is not None:
            prev_a.finish(f, (dx,))
        cur_m.finish(f, (dx,))
        prev_a = _GradExchange(f"{i}a", kinds[:2], [g_qkv, g_o], chip, core)
        started = (prev_a.start_swap((dx,)),)
    prev_a.swap_to_scatter((dx,))
    prev_a.finish(f, (dx,))
    for lst in (dg_mix, dg_mlp, dsinks):
        lst.reverse()

    res = {}
    for kind, (w, m, v) in big.items():
        g = join_halves(f"join_{kind}", f[kind]).reshape(w.shape)
        res[kind] = (g, *adamw(f"adamw_{kind}", w, g, m, v))

    n_sink = a_sinks.size
    sink_rows = jnp.zeros((1, 8, d), F32).at[0, 0, :n_sink].set(jnp.concatenate(dsinks))
    parts = jnp.concatenate([jnp.stack(dg_mix), jnp.stack(dg_mlp), dg_final[None], sink_rows], axis=0)
    g_small = allreduce_small("allreduce_small", parts)

    def pack(mix, mlp, fin, snk):
        snk_row = jnp.zeros((1, d), F32).at[0, :n_sink].set(snk.reshape(-1))
        return jnp.concatenate([mix, mlp, fin[None], snk_row], axis=0)

    def unpack(t):
        return t[:depth], t[depth:2 * depth], t[2 * depth], t[2 * depth + 1, :n_sink].reshape(a_sinks.shape)

    small = adamw("adamw_small", pack(norm_mix, norm_mlp, final_norm, a_sinks), g_small,
                  pack(m_norm_mix, m_norm_mlp, m_final_norm, m_a_sinks),
                  pack(v_norm_mix, v_norm_mlp, v_final_norm, v_a_sinks))
    outs = []
    for idx in range(4):
        mix, mlp, fin, snk = unpack(g_small if idx == 0 else small[idx - 1])
        outs += [res["a_qkv"][idx], res["a_o"][idx], snk, res["b_qkv"][idx], res["b_o"][idx], mix, mlp,
                 res["mlp_in"][idx], res["mlp_out"][idx], fin]
    return (loss, dx.reshape(x.shape), *outs)
```
